```python
import jax, jax.numpy as jnp
from jax import lax
import numpy as np

D_MODEL = 1024
BATCH = 8
SEQ = 4096
DEPTH = 4

GRID_W = 64
CTX_LEN = 256
N_EVEN = (DEPTH + 1) // 2
N_ODD = DEPTH // 2
MOD_CHUNKS = 6
EPS = 1e-6
CHUNK = 128
GMLP_GROUP_W = 128
D_GMLP = D_MODEL // 2
GMLP_GROUPS = D_GMLP // GMLP_GROUP_W
FOURIER_GROUP_W = 128
D_FOURIER = D_MODEL // 2
FOURIER_GROUPS = D_FOURIER // FOURIER_GROUP_W
EVEN_IN = 2 * D_GMLP + D_FOURIER
EVEN_MIX = D_GMLP + D_FOURIER
HEAD_DIM = 64
N_HEADS = (D_MODEL // 2) // HEAD_DIM
N_KV_HEADS = 2
Q_PER_KV = N_HEADS // N_KV_HEADS
WINDOW = 128
ATT_BLOCK = 128
AXIS_DIM = HEAD_DIM // 2
ROPE_BASE = 10000.0
D_CONV = D_MODEL // 2
CONV_WIDTH = 3
Q_W = N_HEADS * HEAD_DIM
KV_W = N_KV_HEADS * HEAD_DIM
ODD_SPLITS = (Q_W, Q_W + KV_W, Q_W + 2 * KV_W, Q_W + 2 * KV_W + D_CONV, Q_W + 2 * KV_W + 2 * D_CONV)
ODD_IN = Q_W + 2 * KV_W + 3 * D_CONV
ODD_MIX = Q_W + D_CONV
N_EXPERTS = 64
TOP_K = 6
D_EXPERT = 256
D_SHARED = 256
ROUTED_SCALE = 2.5
EXPERT_BLOCK = 128

kernel_name = "hybrid_gmlp_fourier_swa_conv_moe_dit"


def rms_norm(x, g):
    xf = x.astype(jnp.float32)
    y = xf * lax.rsqrt(jnp.mean(xf * xf, axis=-1, keepdims=True) + EPS)
    return (y * g.astype(jnp.float32)).astype(x.dtype)


def modulation(cond, w, b):
    m = jax.nn.silu(cond) @ w + b
    return jnp.split(m[..., None, :], MOD_CHUNKS, axis=-1)


def modulate(h, shift, scale):
    return h * (1 + scale) + shift


def axial_rope_tables(n_tok):
    rows = n_tok // GRID_W
    r = jnp.repeat(jnp.arange(rows, dtype=jnp.float32), GRID_W)
    col = jnp.tile(jnp.arange(GRID_W, dtype=jnp.float32), rows)
    inv = ROPE_BASE ** (-jnp.arange(0, AXIS_DIM, 2, dtype=jnp.float32) / AXIS_DIM)
    ang = jnp.stack([r[:, None] * inv, col[:, None] * inv], axis=1)
    return jnp.cos(ang), jnp.sin(ang)


def apply_axial_rope(x, cos, sin):
    b, s, h, _ = x.shape
    xr = x.reshape(b, s, h, 2, 2, AXIS_DIM // 2)
    x1, x2 = xr[..., 0, :], xr[..., 1, :]
    cs = cos[:, None].astype(x.dtype)
    sn = sin[:, None].astype(x.dtype)
    out = jnp.stack([x1 * cs - x2 * sn, x2 * cs + x1 * sn], axis=-2)
    return out.reshape(x.shape)


def heads(t, n):
    return t.reshape(t.shape[0], t.shape[1], n, HEAD_DIM)


def gmlp_spatial_gating(u, v, w_s, b_s):
    b, L, _ = u.shape
    nc = L // CHUNK
    u = jax.nn.gelu(u).reshape(b, nc, CHUNK, GMLP_GROUPS, GMLP_GROUP_W)
    vf = jax.nn.gelu(v).astype(jnp.float32).reshape(b, nc, CHUNK, GMLP_GROUPS, GMLP_GROUP_W)
    mu = jnp.mean(vf, axis=-1, keepdims=True)
    var = jnp.mean(jnp.square(vf - mu), axis=-1, keepdims=True)
    vn = ((vf - mu) * lax.rsqrt(var + 1e-5)).astype(u.dtype)
    f = jnp.einsum("gpq,bnqgc->bnpgc", w_s, vn) + b_s.T[:, :, None]
    return (u * f).reshape(b, L, D_GMLP)


def fourier_mix(f):
    b, L, _ = f.shape
    fg = f.astype(jnp.float32).reshape(b, L, FOURIER_GROUPS, FOURIER_GROUP_W)
    y = jnp.fft.fft2(fg, axes=(1, 3), norm="ortho").real
    return y.astype(f.dtype).reshape(b, L, D_FOURIER)


def even_mixer(h, w_in, w_s, b_s, w_out):
    u, v, f = jnp.split(h @ w_in, [D_GMLP, 2 * D_GMLP], axis=-1)
    mixed = jnp.concatenate([gmlp_spatial_gating(u, v, w_s, b_s), fourier_mix(f)], axis=-1)
    return mixed @ w_out


def short_conv(g_in, g_out, z_in, w):
    z = g_in * z_in
    L = z.shape[1]
    pad = CONV_WIDTH // 2
    zp = jnp.pad(z, ((0, 0), (pad, pad), (0, 0)))
    y = zp[:, 0:L] * w[0]
    for j in range(1, CONV_WIDTH):
        y = y + zp[:, j:j + L] * w[j]
    return g_out * y


def sink_softmax(parts, sink):
    m = sink
    for s in parts:
        m = jnp.maximum(m, jnp.max(s, axis=-1, keepdims=True))
    den = jnp.exp(sink - m)
    es = []
    for s in parts:
        e = jnp.exp(s - m)
        den = den + jnp.sum(e, axis=-1, keepdims=True)
        es.append(e)
    return [e / den for e in es]


def latent_window_attention(q, k, v, kc, vc, sink):
    b, s, _, dh = q.shape
    nb = s // ATT_BLOCK
    scale = dh ** -0.5
    qb = q.reshape(b, nb, ATT_BLOCK, N_KV_HEADS, Q_PER_KV, dh)

    def band(t):
        tb = t.reshape(b, nb, ATT_BLOCK, N_KV_HEADS, dh)
        tp = jnp.pad(tb, ((0, 0), (1, 1), (0, 0), (0, 0), (0, 0)))
        return jnp.concatenate([tp[:, :-2], tp[:, 1:-1], tp[:, 2:]], axis=2)

    kb, vb = band(k), band(v)
    s_band = jnp.einsum("bnqhgd,bnkhd->bnhgqk", qb, kb).astype(jnp.float32) * scale
    blk = jnp.arange(nb)[:, None, None]
    qpos = blk * ATT_BLOCK + jnp.arange(ATT_BLOCK)[None, :, None]
    kpos = (blk - 1) * ATT_BLOCK + jnp.arange(3 * ATT_BLOCK)[None, None, :]
    valid = (jnp.abs(kpos - qpos) <= WINDOW) & (kpos >= 0) & (kpos < s)
    s_band = jnp.where(valid[None, :, None, None], s_band, -jnp.inf)
    s_ctx = jnp.einsum("bnqhgd,bkhd->bnhgqk", qb, kc).astype(jnp.float32) * scale
    snk = sink.astype(jnp.float32).reshape(1, 1, N_KV_HEADS, Q_PER_KV, 1, 1)
    p_band, p_ctx = sink_softmax([s_band, s_ctx], snk)
    o = (jnp.einsum("bnhgqk,bnkhd->bnqhgd", p_band.astype(v.dtype), vb)
         + jnp.einsum("bnhgqk,bkhd->bnqhgd", p_ctx.astype(v.dtype), vc))
    return o.reshape(b, s, N_HEADS * dh)


def context_attention(qc, kc, vc, sink):
    b, lc, _, dh = qc.shape
    qg = qc.reshape(b, lc, N_KV_HEADS, Q_PER_KV, dh)
    sc = jnp.einsum("bqhgd,bkhd->bhgqk", qg, kc).astype(jnp.float32) * dh ** -0.5
    snk = sink.astype(jnp.float32).reshape(1, N_KV_HEADS, Q_PER_KV, 1, 1)
    (p,) = sink_softmax([sc], snk)
    o = jnp.einsum("bhgqk,bkhd->bqhgd", p.astype(vc.dtype), vc)
    return o.reshape(b, lc, N_HEADS * dh)


def odd_mixer(h, hc, w_in, q_g, k_g, sink, conv_w, w_out, cos, sin, ctx_out):
    q, k, v, g_in, g_out, z = jnp.split(h @ w_in, list(ODD_SPLITS), axis=-1)
    q = apply_axial_rope(rms_norm(heads(q, N_HEADS), q_g), cos, sin)
    k = apply_axial_rope(rms_norm(heads(k, N_KV_HEADS), k_g), cos, sin)
    v = heads(v, N_KV_HEADS)
    if ctx_out:
        qc, kc, vc, gc_in, gc_out, zc = jnp.split(hc @ w_in, list(ODD_SPLITS), axis=-1)
    else:
        kc, vc = jnp.split(hc @ w_in[:, Q_W:Q_W + 2 * KV_W], 2, axis=-1)
    kc = rms_norm(heads(kc, N_KV_HEADS), k_g)
    vc = heads(vc, N_KV_HEADS)
    y = jnp.concatenate([latent_window_attention(q, k, v, kc, vc, sink),
                         short_conv(g_in, g_out, z, conv_w)], axis=-1) @ w_out
    if not ctx_out:
        return y, None
    qc = rms_norm(heads(qc, N_HEADS), q_g)
    yc = jnp.concatenate([context_attention(qc, kc, vc, sink),
                          short_conv(gc_in, gc_out, zc, conv_w)], axis=-1) @ w_out
    return y, yc


def swiglu(x, w1, w3, w2):
    return (jax.nn.silu(x @ w1) * (x @ w3)) @ w2


def routed_experts(xf, idx, gates, w1, w3, w2):
    n, d = xf.shape
    nk = n * TOP_K
    flat_e = idx.reshape(-1)
    flat_g = gates.reshape(-1)
    flat_tok = jnp.arange(nk, dtype=jnp.int32) // TOP_K
    order = jnp.argsort(flat_e)
    sorted_e = flat_e[order]
    counts = jax.ops.segment_sum(jnp.ones((nk,), jnp.int32), flat_e, num_segments=N_EXPERTS)
    padded = (counts + EXPERT_BLOCK - 1) // EXPERT_BLOCK * EXPERT_BLOCK
    pend = jnp.cumsum(padded)
    pstart = pend - padded
    start = jnp.cumsum(counts) - counts
    dest = pstart[sorted_e] + jnp.arange(nk, dtype=jnp.int32) - start[sorted_e]
    n_blocks = (nk + N_EXPERTS * (EXPERT_BLOCK - 1) + EXPERT_BLOCK - 1) // EXPERT_BLOCK
    p_rows = n_blocks * EXPERT_BLOCK
    row_tok = jnp.full((p_rows,), n, jnp.int32).at[dest].set(flat_tok[order])
    row_gate = jnp.zeros((p_rows,), xf.dtype).at[dest].set(flat_g[order])
    block_start = jnp.arange(n_blocks, dtype=jnp.int32) * EXPERT_BLOCK
    block_e = jnp.minimum(jnp.searchsorted(pend, block_start, side="right"), N_EXPERTS - 1)
    x_pad = jnp.concatenate([xf, jnp.zeros((1, d), xf.dtype)], axis=0)

    def block_fn(args):
        tok, e, g = args
        xb = x_pad[tok]
        return swiglu(xb, w1[e], w3[e], w2[e]) * g[:, None]

    yb = lax.map(block_fn, (row_tok.reshape(n_blocks, EXPERT_BLOCK), block_e,
                            row_gate.reshape(n_blocks, EXPERT_BLOCK)))
    y = jax.ops.segment_sum(yb.reshape(p_rows, d), row_tok, num_segments=n + 1)
    return y[:n]


def moe(xf, router_w, router_b, w1, w3, w2, sw1, sw3, sw2):
    scores = jax.nn.sigmoid((xf @ router_w).astype(jnp.float32))
    _, idx = lax.top_k(scores + router_b.astype(jnp.float32), TOP_K)
    g = jnp.take_along_axis(scores, idx, axis=-1)
    g = g / (jnp.sum(g, axis=-1, keepdims=True) + 1e-20) * ROUTED_SCALE
    return routed_experts(xf, idx, g.astype(xf.dtype), w1, w3, w2) + swiglu(xf, sw1, sw3, sw2)


def setup_inputs(seed: int = 0) -> dict:
    key = jax.random.key(seed)
    k = jax.random.split(key, 26)
    d = D_MODEL

    def nrm(kk, shape, scale):
        return jax.random.normal(kk, shape, jnp.float32) * scale

    return {
        "x": nrm(k[0], (BATCH, SEQ, d), 1.0),
        "c": nrm(k[1], (BATCH, d), 1.0),
        "ctx": nrm(k[2], (BATCH, CTX_LEN, d), 1.0),
        "c_ctx": nrm(k[3], (d,), 1.0),
        "ada_w": nrm(k[4], (DEPTH, d, MOD_CHUNKS * d), 0.01),
        "ada_b": nrm(k[5], (DEPTH, MOD_CHUNKS * d), 0.02),
        "norm1_g": 1.0 + nrm(k[6], (DEPTH, d), 0.02),
        "norm2_g": 1.0 + nrm(k[7], (DEPTH, d), 0.02),
        "ev_w_in": nrm(k[8], (N_EVEN, d, EVEN_IN), d ** -0.5),
        "ev_w_s": nrm(k[9], (N_EVEN, GMLP_GROUPS, CHUNK, CHUNK), CHUNK ** -0.5),
        "ev_b_s": 1.0 + nrm(k[10], (N_EVEN, GMLP_GROUPS, CHUNK), 0.02),
        "ev_w_out": nrm(k[11], (N_EVEN, EVEN_MIX, d), EVEN_MIX ** -0.5),
        "od_w_in": nrm(k[12], (N_ODD, d, ODD_IN), d ** -0.5),
        "od_q_norm_g": 1.0 + nrm(k[13], (N_ODD, HEAD_DIM), 0.02),
        "od_k_norm_g": 1.0 + nrm(k[14], (N_ODD, HEAD_DIM), 0.02),
        "od_sink": nrm(k[15], (N_ODD, N_HEADS), 0.5),
        "od_conv_w": nrm(k[16], (N_ODD, CONV_WIDTH, D_CONV), CONV_WIDTH ** -0.5),
        "od_w_out": nrm(k[17], (N_ODD, ODD_MIX, d), ODD_MIX ** -0.5),
        "router_w": nrm(k[18], (DEPTH, d, N_EXPERTS), d ** -0.5),
        "router_b": nrm(k[19], (DEPTH, N_EXPERTS), 0.01),
        "exp_w_gate": nrm(k[20], (DEPTH, N_EXPERTS, d, D_EXPERT), d ** -0.5),
        "exp_w_up": nrm(k[21], (DEPTH, N_EXPERTS, d, D_EXPERT), d ** -0.5),
        "exp_w_down": nrm(k[22], (DEPTH, N_EXPERTS, D_EXPERT, d), D_EXPERT ** -0.5),
        "sh_w_gate": nrm(k[23], (DEPTH, d, D_SHARED), d ** -0.5),
        "sh_w_up": nrm(k[24], (DEPTH, d, D_SHARED), d ** -0.5),
        "sh_w_down": nrm(k[25], (DEPTH, D_SHARED, d), D_SHARED ** -0.5),
    }


def reference(x, c, ctx, c_ctx, ada_w, ada_b, norm1_g, norm2_g,
              ev_w_in, ev_w_s, ev_b_s, ev_w_out,
              od_w_in, od_q_norm_g, od_k_norm_g, od_sink, od_conv_w, od_w_out,
              router_w, router_b, exp_w_gate, exp_w_up, exp_w_down,
              sh_w_gate, sh_w_up, sh_w_down):
    b, s, d = x.shape
    lc = ctx.shape[1]
    cos, sin = axial_rope_tables(s)
    xc = ctx
    for l in range(DEPTH):
        last = l == DEPTH - 1
        even = l % 2 == 0
        need_ctx_mix = not (last and even)
        sh1, sc1, gt1, sh2, sc2, gt2 = modulation(c, ada_w[l], ada_b[l])
        h = modulate(rms_norm(x, norm1_g[l]), sh1, sc1)
        if need_ctx_mix:
            csh1, csc1, cgt1, csh2, csc2, cgt2 = modulation(c_ctx, ada_w[l], ada_b[l])
            hc = modulate(rms_norm(xc, norm1_g[l]), csh1, csc1)
        if even:
            e = l // 2
            y = even_mixer(h, ev_w_in[e], ev_w_s[e], ev_b_s[e], ev_w_out[e])
            yc = None if last else even_mixer(hc, ev_w_in[e], ev_w_s[e], ev_b_s[e], ev_w_out[e])
        else:
            o = l // 2
            y, yc = odd_mixer(h, hc, od_w_in[o], od_q_norm_g[o], od_k_norm_g[o], od_sink[o],
                              od_conv_w[o], od_w_out[o], cos, sin, not last)
        x = x + gt1 * y
        h2 = modulate(rms_norm(x, norm2_g[l]), sh2, sc2)
        moe_args = (router_w[l], router_b[l], exp_w_gate[l], exp_w_up[l], exp_w_down[l],
                    sh_w_gate[l], sh_w_up[l], sh_w_down[l])
        if last:
            x = x + gt2 * moe(h2.reshape(b * s, d), *moe_args).reshape(b, s, d)
        else:
            xc = xc + cgt1 * yc
            h2c = modulate(rms_norm(xc, norm2_g[l]), csh2, csc2)
            tokens = jnp.concatenate([h2.reshape(b * s, d), h2c.reshape(b * lc, d)], axis=0)
            out = moe(tokens, *moe_args)
            x = x + gt2 * out[:b * s].reshape(b, s, d)
            xc = xc + cgt2 * out[b * s:].reshape(b, lc, d)
    return x
```

```python
import functools
import math

import jax
import jax.numpy as jnp
from jax import lax
from jax.experimental import pallas as pl
from jax.experimental.pallas import tpu as pltpu

F32 = jnp.float32
BF16 = jnp.bfloat16
I32 = jnp.int32

LANES = 128
VMEM_LIMIT = 48 * 2**20

EPS = 1e-6
GRID_W = 64
CHUNK = 128
HEAD_DIM = 64
N_HEADS = 8
N_KV = 2
Q_PER_KV = N_HEADS // N_KV
ATT_BLOCK = 128
ROPE_BASE = 10000.0
N_EXPERTS = 64
TOP_K = 6
ROUTED_SCALE = 2.5
EXPERT_BLOCK = 256
ROUTE_TM = 256
MOVE_TM = 128
ROW_PARTS = 8
SLAB_IDX, SLAB_RANK, SLAB_GATE = 0, 8, 16


def _params(*sem):
    return pltpu.CompilerParams(dimension_semantics=sem, vmem_limit_bytes=VMEM_LIMIT)


def _sigmoid(x):
    return 1.0 / (1.0 + jnp.exp(-x))


def _silu(x):
    return x * _sigmoid(x)


def _gelu_tanh(x):
    c = math.sqrt(2.0 / math.pi)
    return x * (0.5 * (1.0 + jnp.tanh(c * (x + 0.044715 * (x * x * x)))))


def _dot(a, b):
    return jnp.dot(a, b, preferred_element_type=F32)


def _load_rows(ref, n, d):
    parts = d // LANES
    return jnp.concatenate([ref[pl.ds(c, n, stride=parts), :] for c in range(parts)], axis=-1)


def _store_rows(ref, val):
    n, d = val.shape
    parts = d // LANES
    for c in range(parts):
        ref[pl.ds(c, n, stride=parts), :] = val[:, c * LANES:(c + 1) * LANES]


def _adaln_kernel(c_ref, w_ref, b_ref, o_ref):
    o_ref[...] = _dot(_silu(c_ref[...]), w_ref[...]) + b_ref[...]


def _adaln(cond, ada_w, ada_b):
    n_layers, d, n6 = ada_w.shape
    rows = cond.shape[0]
    tn = 768
    return pl.pallas_call(
        _adaln_kernel,
        grid=(n_layers, n6 // tn),
        in_specs=[pl.BlockSpec((rows, d), lambda l, j: (0, 0)),
                  pl.BlockSpec((None, d, tn), lambda l, j: (l, 0, j)),
                  pl.BlockSpec((None, 1, tn), lambda l, j: (l, 0, j))],
        out_specs=pl.BlockSpec((None, rows, tn), lambda l, j: (l, 0, j)),
        out_shape=jax.ShapeDtypeStruct((n_layers, rows, n6), F32),
        compiler_params=_params("parallel", "parallel"),
        name="adaln",
    )(cond, ada_w, ada_b.reshape(n_layers, 1, n6))


def _norm_mod(x, g, shift, scale):
    ms = jnp.mean(x * x, axis=-1, keepdims=True)
    h = (x * lax.rsqrt(ms + EPS)) * g
    return h * (1.0 + scale) + shift


def _nmm_kernel(x_ref, g_ref, sh_ref, sc_ref, w_ref, o_ref):
    h = _norm_mod(x_ref[...], g_ref[...], sh_ref[...], sc_ref[...])
    o_ref[...] = _dot(h.astype(BF16), w_ref[...]).astype(o_ref.dtype)


def _norm_mod_matmul(x, g, shift, scale, w, tm):
    b, s, d = x.shape
    n = w.shape[1]
    return pl.pallas_call(
        _nmm_kernel,
        grid=(b, s // tm),
        in_specs=[pl.BlockSpec((None, tm, d), lambda bi, i: (bi, i, 0)),
                  pl.BlockSpec((1, d), lambda bi, i: (0, 0)),
                  pl.BlockSpec((None, 1, d), lambda bi, i: (bi, 0, 0)),
                  pl.BlockSpec((None, 1, d), lambda bi, i: (bi, 0, 0)),
                  pl.BlockSpec((d, n), lambda bi, i: (0, 0))],
        out_specs=pl.BlockSpec((None, tm, n), lambda bi, i: (bi, i, 0)),
        out_shape=jax.ShapeDtypeStruct((b, s, n), F32),
        compiler_params=_params("parallel", "parallel"),
        name="norm_mod_matmul",
    )(x, g, shift, scale, w)


def _even_mix_kernel(u_ref, v_ref, f_ref, ws_ref, bs_ref, cs_ref, gm_ref, z_ref, *, n_chunks, n_groups):
    for c in range(n_chunks):
        rows = slice(c * CHUNK, (c + 1) * CHUNK)
        for g in range(n_groups):
            cols = slice(g * LANES, (g + 1) * LANES)
            ug = _gelu_tanh(u_ref[rows, cols])
            vg = _gelu_tanh(v_ref[rows, cols])
            mu = jnp.mean(vg, axis=-1, keepdims=True)
            dv = vg - mu
            var = jnp.mean(dv * dv, axis=-1, keepdims=True)
            vn = dv * lax.rsqrt(var + 1e-5)
            fg = _dot(ws_ref[g], vn.astype(BF16)) + bs_ref[g]
            gm_ref[rows, cols] = (ug * fg).astype(gm_ref.dtype)
    fz = _dot(f_ref[...].astype(BF16), cs_ref[...])
    half = fz.shape[1] // 2
    z_ref[0] = fz[:, :half].astype(z_ref.dtype)
    z_ref[1] = fz[:, half:].astype(z_ref.dtype)


def _even_mix(uvf, ws, bs, cs, tm):
    b, s, n3 = uvf.shape
    w = n3 // 3
    n_groups = w // LANES
    kern = functools.partial(_even_mix_kernel, n_chunks=tm // CHUNK, n_groups=n_groups)
    return pl.pallas_call(
        kern,
        grid=(b, s // tm),
        in_specs=[pl.BlockSpec((None, tm, w), lambda bi, i: (bi, i, 0)),
                  pl.BlockSpec((None, tm, w), lambda bi, i: (bi, i, 1)),
                  pl.BlockSpec((None, tm, w), lambda bi, i: (bi, i, 2)),
                  pl.BlockSpec(ws.shape, lambda bi, i: (0, 0, 0)),
                  pl.BlockSpec(bs.shape, lambda bi, i: (0, 0, 0)),
                  pl.BlockSpec(cs.shape, lambda bi, i: (0, 0))],
        out_specs=[pl.BlockSpec((None, tm, w), lambda bi, i: (bi, i, 0)),
                   pl.BlockSpec((2, tm, w), lambda bi, i: (0, i, bi))],
        out_shape=[jax.ShapeDtypeStruct((b, s, w), BF16),
                   jax.ShapeDtypeStruct((2, s, b * w), BF16)],
        compiler_params=_params("parallel", "parallel"),
        name="even_mix",
    )(uvf, uvf, uvf, ws, bs, cs)


def _mm_kernel(a_ref, b_ref, o_ref, acc_ref):
    k = pl.program_id(2)

    @pl.when(k == 0)
    def _():
        acc_ref[...] = jnp.zeros_like(acc_ref)

    acc_ref[...] += _dot(a_ref[...], b_ref[...])

    @pl.when(k == pl.num_programs(2) - 1)
    def _():
        o_ref[...] = acc_ref[...].astype(o_ref.dtype)


def _matmul(a, b, tm, tn, tk, out_dtype):
    m, kd = a.shape
    n = b.shape[1]
    return pl.pallas_call(
        _mm_kernel,
        grid=(m // tm, n // tn, kd // tk),
        in_specs=[pl.BlockSpec((tm, tk), lambda i, j, k: (i, k)),
                  pl.BlockSpec((tk, tn), lambda i, j, k: (k, j))],
        out_specs=pl.BlockSpec((tm, tn), lambda i, j, k: (i, j)),
        out_shape=jax.ShapeDtypeStruct((m, n), out_dtype),
        scratch_shapes=[pltpu.VMEM((tm, tn), F32)],
        compiler_params=_params("parallel", "parallel", "arbitrary"),
        name="dft_matmul",
    )(a, b)


def _outproj_kernel(a_ref, b_ref, w_ref, x_ref, gate_ref, o_ref):
    ab = jnp.concatenate([a_ref[...], b_ref[...]], axis=-1)
    o_ref[...] = x_ref[...] + gate_ref[...] * _dot(ab, w_ref[...])


def _outproj(a, a_spec, bsrc, b_spec, w, x, gate, tm):
    b, s, d = x.shape
    return pl.pallas_call(
        _outproj_kernel,
        grid=(b, s // tm),
        in_specs=[a_spec, b_spec,
                  pl.BlockSpec(w.shape, lambda bi, i: (0, 0)),
                  pl.BlockSpec((None, tm, d), lambda bi, i: (bi, i, 0)),
                  pl.BlockSpec((None, 1, d), lambda bi, i: (bi, 0, 0))],
        out_specs=pl.BlockSpec((None, tm, d), lambda bi, i: (bi, i, 0)),
        out_shape=jax.ShapeDtypeStruct((b, s, d), F32),
        compiler_params=_params("parallel", "parallel"),
        name="outproj",
    )(a, bsrc, w, x, gate)


def _head_rms(x, bd_ref):
    xx = x * x
    hi = xx.astype(BF16)
    lo = (xx - hi.astype(F32)).astype(BF16)
    ms = _dot(hi, bd_ref[...]) + _dot(lo, bd_ref[...])
    return x * lax.rsqrt(ms + EPS)


def _rope(x, cos, sins):
    width = x.shape[1]
    lane = lax.broadcasted_iota(I32, x.shape, 1)
    first = (lane & 31) < 16
    swapped = jnp.where(first, pltpu.roll(x, width - 16, 1), pltpu.roll(x, 16, 1))
    return x * cos + swapped * sins


def _qkv_kernel(q_ref, k_ref, v_ref, cos_ref, sin_ref, qg_ref, kg_ref, bdq_ref, bdk_ref, tile_ref,
                qo_ref, k4_ref, v4_ref, *, with_q):
    kw = k_ref.shape[1]
    if with_q:
        qn = _head_rms(q_ref[...], bdq_ref) * qg_ref[...]
        qo_ref[...] = _rope(qn, cos_ref[...], sin_ref[...]).astype(qo_ref.dtype)
    else:
        qo_ref[...] = jnp.zeros_like(qo_ref)
    kn = _head_rms(k_ref[...], bdk_ref) * kg_ref[...]
    kr = _rope(kn, cos_ref[:, :kw], sin_ref[:, :kw]).astype(BF16)
    vb = v_ref[...].astype(BF16)
    for h in range(N_KV):
        k4_ref[h] = _dot(kr, tile_ref[h]).astype(k4_ref.dtype)
        v4_ref[h] = _dot(vb, tile_ref[h]).astype(v4_ref.dtype)


def _qkv_prep(proj, cos, sin, qg, kg, bdq, bdk, tile, tm, with_q):
    b, s, _ = proj.shape
    qw = N_HEADS * HEAD_DIM
    kw = N_KV * HEAD_DIM
    rep = Q_PER_KV * HEAD_DIM
    k_blk = (4 * qw) // kw
    kern = functools.partial(_qkv_kernel, with_q=with_q)
    return pl.pallas_call(
        kern,
        grid=(b, s // tm),
        in_specs=[pl.BlockSpec((None, tm, qw), lambda bi, i: (bi, i, 0)),
                  pl.BlockSpec((None, tm, kw), lambda bi, i: (bi, i, k_blk)),
                  pl.BlockSpec((None, tm, kw), lambda bi, i: (bi, i, k_blk + 1)),
                  pl.BlockSpec((tm, qw), lambda bi, i: (i, 0)),
                  pl.BlockSpec((tm, qw), lambda bi, i: (i, 0)),
                  pl.BlockSpec((1, qw), lambda bi, i: (0, 0)),
                  pl.BlockSpec((1, kw), lambda bi, i: (0, 0)),
                  pl.BlockSpec(bdq.shape, lambda bi, i: (0, 0)),
                  pl.BlockSpec(bdk.shape, lambda bi, i: (0, 0)),
                  pl.BlockSpec(tile.shape, lambda bi, i: (0, 0, 0))],
        out_specs=[pl.BlockSpec((None, tm, qw), lambda bi, i: (bi, i, 0)),
                   pl.BlockSpec((None, N_KV, tm, rep), lambda bi, i: (bi, 0, i, 0)),
                   pl.BlockSpec((None, N_KV, tm, rep), lambda bi, i: (bi, 0, i, 0))],
        out_shape=[jax.ShapeDtypeStruct((b, s, qw), BF16),
                   jax.ShapeDtypeStruct((b, N_KV, s, rep), BF16),
                   jax.ShapeDtypeStruct((b, N_KV, s, rep), BF16)],
        compiler_params=_params("parallel", "parallel"),
        name="qkv_prep",
    )(proj, proj, proj, cos, sin, qg, kg, bdq, bdk, tile)


def _conv_kernel(gi_ref, go_ref, z_ref, w_ref, o_ref, *, rows):
    s = z_ref.shape[0]
    n = s // rows
    w0, w1, w2 = w_ref[0:1, :], w_ref[1:2, :], w_ref[2:3, :]
    ridx = lax.broadcasted_iota(I32, (rows, z_ref.shape[1]), 0)
    prev_last = jnp.zeros((1, z_ref.shape[1]), F32)
    for j in range(n):
        r0 = j * rows
        zc = gi_ref[r0:r0 + rows, :] * z_ref[r0:r0 + rows, :]
        if j + 1 < n:
            nxt = gi_ref[r0 + rows:r0 + rows + 1, :] * z_ref[r0 + rows:r0 + rows + 1, :]
        else:
            nxt = jnp.zeros_like(prev_last)
        zp = jnp.where(ridx == 0, prev_last, pltpu.roll(zc, 1, 0))
        zn = jnp.where(ridx == rows - 1, nxt, pltpu.roll(zc, rows - 1, 0))
        y = zp * w0 + zc * w1 + zn * w2
        o_ref[r0:r0 + rows, :] = (go_ref[r0:r0 + rows, :] * y).astype(o_ref.dtype)
        prev_last = zc[rows - 1:rows, :]


def _short_conv(proj, conv_w):
    b, s, _ = proj.shape
    dc = conv_w.shape[1]
    nb = dc // LANES
    base = dc // LANES
    rows = min(s, 512)
    kern = functools.partial(_conv_kernel, rows=rows)
    return pl.pallas_call(
        kern,
        grid=(b, nb),
        in_specs=[pl.BlockSpec((None, s, LANES), lambda bi, c: (bi, 0, base + c)),
                  pl.BlockSpec((None, s, LANES), lambda bi, c: (bi, 0, 2 * base + c)),
                  pl.BlockSpec((None, s, LANES), lambda bi, c: (bi, 0, 3 * base + c)),
                  pl.BlockSpec((conv_w.shape[0], LANES), lambda bi, c: (0, c))],
        out_specs=pl.BlockSpec((None, s, LANES), lambda bi, c: (bi, 0, c)),
        out_shape=jax.ShapeDtypeStruct((b, s, dc), BF16),
        compiler_params=_params("parallel", "parallel"),
        name="short_conv",
    )(proj, proj, proj, conv_w)


def _attn_kernel(sink_ref, q_ref, *refs, band, n_blocks):
    o_ref = refs[-1]
    if band:
        kp, kc_, kn, vp, vc_, vn, kx, vx = refs[:-1]
        kcat = jnp.concatenate([kp[...], kc_[...], kn[...], kx[...]], axis=0)
        vcat = jnp.concatenate([vp[...], vc_[...], vn[...], vx[...]], axis=0)
    else:
        kx, vx = refs[:-1]
        kcat, vcat = kx[...], vx[...]
    h = pl.program_id(1)
    i = pl.program_id(2)
    q = q_ref[...]
    t, w = q.shape
    lane = lax.broadcasted_iota(I32, (t, w), 1)
    masks = [(lane >= g * HEAD_DIM) & (lane < (g + 1) * HEAD_DIM) for g in range(Q_PER_KV)]
    q4 = jnp.concatenate([jnp.where(m, q, jnp.zeros_like(q)) for m in masks], axis=0)
    s = lax.dot_general(q4, kcat, (((1,), (1,)), ((), ())), preferred_element_type=F32)
    if band:
        row = lax.broadcasted_iota(I32, s.shape, 0) & (t - 1)
        col = lax.broadcasted_iota(I32, s.shape, 1)
        off_prev = jnp.where(i > 0, 0, 4 * t)
        off_next = jnp.where(i < n_blocks - 1, 0, 4 * t)
        bad_prev = (col < t) & (col < row + off_prev)
        bad_next = (col >= 2 * t) & (col < 3 * t) & (col - 2 * t > row - off_next)
        s = jnp.where(bad_prev | bad_next, -jnp.inf, s)
    sink = jnp.concatenate(
        [jnp.full((t, 1), sink_ref[h * Q_PER_KV + g], F32) for g in range(Q_PER_KV)], axis=0)
    m = jnp.maximum(jnp.max(s, axis=-1, keepdims=True), sink)
    e = jnp.exp(s - m)
    den = jnp.sum(e, axis=-1, keepdims=True) + jnp.exp(sink - m)
    r = _dot(e.astype(BF16), vcat) * (1.0 / den)
    o = jnp.zeros((t, w), F32)
    for g in range(Q_PER_KV):
        o = o + jnp.where(masks[g], r[g * t:(g + 1) * t, :], 0.0)
    o_ref[...] = o.astype(o_ref.dtype)


def _attention(q, k4, v4, kx4, vx4, sink, band):
    b, s, qw = q.shape
    rep = k4.shape[-1] if band else kx4.shape[-1]
    lc = kx4.shape[2]
    t = ATT_BLOCK
    nb = s // t
    kern = functools.partial(_attn_kernel, band=band, n_blocks=nb)

    def kv_spec(off):
        return pl.BlockSpec((None, None, t, rep),
                            lambda bi, h, i: (bi, h, jnp.clip(i + off, 0, nb - 1), 0))

    ctx_spec = pl.BlockSpec((None, None, lc, rep), lambda bi, h, i: (bi, h, 0, 0))
    in_specs = [pl.BlockSpec(memory_space=pltpu.SMEM),
                pl.BlockSpec((None, t, rep), lambda bi, h, i: (bi, i, h))]
    args = [sink, q]
    if band:
        in_specs += [kv_spec(-1), kv_spec(0), kv_spec(1), kv_spec(-1), kv_spec(0), kv_spec(1)]
        args += [k4, k4, k4, v4, v4, v4]
    in_specs += [ctx_spec, ctx_spec]
    args += [kx4, vx4]
    return pl.pallas_call(
        kern,
        grid=(b, N_KV, nb),
        in_specs=in_specs,
        out_specs=pl.BlockSpec((None, t, rep), lambda bi, h, i: (bi, i, h)),
        out_shape=jax.ShapeDtypeStruct((b, s, qw), BF16),
        compiler_params=_params("parallel", "parallel", "parallel"),
        name="attention_band" if band else "attention_ctx",
    )(*args)


def _route_kernel(x_ref, g_ref, sh_ref, sc_ref, whi_ref, wlo_ref, rb_ref, cin_ref, tri_ref,
                  h_ref, slab_ref, cnt_ref, carry_ref):
    first = (pl.program_id(0) == 0) & (pl.program_id(1) == 0)

    @pl.when(first)
    def _():
        carry_ref[...] = cin_ref[...]

    h = _norm_mod(x_ref[...], g_ref[...], sh_ref[...], sc_ref[...])
    _store_rows(h_ref, h)
    hi = h.astype(BF16)
    lo = (h - hi.astype(F32)).astype(BF16)
    logits = _dot(hi, whi_ref[...]) + _dot(lo, whi_ref[...]) + _dot(hi, wlo_ref[...])
    scores = _sigmoid(logits)
    tm, lanes = scores.shape
    lane = lax.broadcasted_iota(I32, (tm, lanes), 1).astype(F32)
    work = jnp.where(lane < N_EXPERTS, scores + rb_ref[...], -jnp.inf)
    hits, idxs, gates = [], [], []
    for _ in range(TOP_K):
        mx = jnp.max(work, axis=-1, keepdims=True)
        idx = jnp.min(jnp.where(work == mx, lane, float(lanes)), axis=-1, keepdims=True)
        hit = lane == idx
        hits.append(hit)
        idxs.append(idx)
        gates.append(jnp.sum(jnp.where(hit, scores, 0.0), axis=-1, keepdims=True))
        work = jnp.where(hit, -jnp.inf, work)
    gsum = gates[0]
    for gk in gates[1:]:
        gsum = gsum + gk
    gscale = ROUTED_SCALE / (gsum + 1e-20)
    onehot = jnp.zeros((tm, lanes), F32)
    for hit in hits:
        onehot = jnp.where(hit, 1.0, onehot)
    before = _dot(tri_ref[...], onehot.astype(BF16)) + carry_ref[...]
    slab = jnp.zeros((tm, lanes), F32)
    for k in range(TOP_K):
        rank = jnp.sum(jnp.where(hits[k], before, 0.0), axis=-1, keepdims=True)
        slab = jnp.where(lane == SLAB_IDX + k, idxs[k], slab)
        slab = jnp.where(lane == SLAB_RANK + k, rank, slab)
        slab = jnp.where(lane == SLAB_GATE + k, gates[k] * gscale, slab)
    slab_ref[...] = slab
    carry_ref[...] = carry_ref[...] + jnp.sum(onehot, axis=0, keepdims=True)
    cnt_ref[...] = carry_ref[...]


def _route(x, g, shift, scale, whi, wlo, rb, counts_in, tri, tm):
    b, s, d = x.shape
    parts = d // LANES
    return pl.pallas_call(
        _route_kernel,
        grid=(b, s // tm),
        in_specs=[pl.BlockSpec((None, tm, d), lambda bi, i: (bi, i, 0)),
                  pl.BlockSpec((1, d), lambda bi, i: (0, 0)),
                  pl.BlockSpec((None, 1, d), lambda bi, i: (bi, 0, 0)),
                  pl.BlockSpec((None, 1, d), lambda bi, i: (bi, 0, 0)),
                  pl.BlockSpec(whi.shape, lambda bi, i: (0, 0)),
                  pl.BlockSpec(wlo.shape, lambda bi, i: (0, 0)),
                  pl.BlockSpec((1, LANES), lambda bi, i: (0, 0)),
                  pl.BlockSpec((1, LANES), lambda bi, i: (0, 0)),
                  pl.BlockSpec((tm, tm), lambda bi, i: (0, 0))],
        out_specs=[pl.BlockSpec((None, tm * parts, LANES), lambda bi, i: (bi, i, 0)),
                   pl.BlockSpec((None, tm, LANES), lambda bi, i: (bi, i, 0)),
                   pl.BlockSpec((1, LANES), lambda bi, i: (0, 0))],
        out_shape=[jax.ShapeDtypeStruct((b, s * parts, LANES), F32),
                   jax.ShapeDtypeStruct((b, s, LANES), F32),
                   jax.ShapeDtypeStruct((1, LANES), F32)],
        scratch_shapes=[pltpu.VMEM((1, LANES), F32)],
        compiler_params=_params("arbitrary", "arbitrary"),
        name="route",
    )(x, g, shift, scale, whi, wlo, rb, counts_in, tri)


def _slot_kernel(slab_ref, start_ref, o_ref):
    slab = slab_ref[...]
    tm, lanes = slab.shape
    lane = lax.broadcasted_iota(I32, (tm, lanes), 1).astype(F32)
    out = jnp.zeros((tm, lanes), F32)
    for k in range(TOP_K):
        idx = jnp.sum(jnp.where(lane == SLAB_IDX + k, slab, 0.0), axis=-1, keepdims=True)
        rank = jnp.sum(jnp.where(lane == SLAB_RANK + k, slab, 0.0), axis=-1, keepdims=True)
        start = jnp.sum(jnp.where(lane == idx, start_ref[...], 0.0), axis=-1, keepdims=True)
        out = jnp.where(lane == k, start + rank, out)
    o_ref[...] = out.astype(I32)


def _slots(slab, starts, tm):
    n = slab.shape[0]
    return pl.pallas_call(
        _slot_kernel,
        grid=(n // tm,),
        in_specs=[pl.BlockSpec((tm, LANES), lambda i: (i, 0)),
                  pl.BlockSpec((1, LANES), lambda i: (0, 0))],
        out_specs=pl.BlockSpec((tm, LANES), lambda i: (i, 0)),
        out_shape=jax.ShapeDtypeStruct((n, LANES), I32),
        compiler_params=_params("parallel"),
        name="slots",
    )(slab, starts)


def _row_copy(src_ref, src_row, dst_ref, dst_row, sem):
    src = src_ref.at[pl.ds(pl.multiple_of(src_row * ROW_PARTS, ROW_PARTS), ROW_PARTS)]
    dst = dst_ref.at[pl.ds(pl.multiple_of(dst_row * ROW_PARTS, ROW_PARTS), ROW_PARTS)]
    return pltpu.make_async_copy(src, dst, sem)


def _dispatch_kernel(ends_ref, nu_ref, h_ref, pos_ref, *rest, tm, zero_fill, tile_base, n_blocks):
    if zero_fill:
        xs_ref, pos_smem, zero_ref, sem, psem = rest
    else:
        _, xs_ref, pos_smem, sem, psem = rest
        zero_ref = None
    i = pl.program_id(0)
    pos_copy = pltpu.make_async_copy(pos_ref.at[pl.ds((tile_base + i) * tm * 8, tm * 8)], pos_smem, psem)
    pos_copy.start()

    if zero_fill:
        @pl.when(i == 0)
        def _():
            zero_ref[...] = jnp.zeros_like(zero_ref)
            rows = EXPERT_BLOCK * ROW_PARTS

            def block_copy(blk):
                return pltpu.make_async_copy(
                    zero_ref, xs_ref.at[pl.ds(pl.multiple_of(blk * rows, rows), rows)], sem)

            def fill(e, carry, *, start):
                end = ends_ref[e]
                prev = jnp.where(e > 0, ends_ref[jnp.maximum(e - 1, 0)], 0)

                @pl.when(end > prev)
                def _():
                    cp = block_copy(end // EXPERT_BLOCK - 1)
                    cp.start() if start else cp.wait()
                return carry

            def tail(j, carry, *, start):
                cp = block_copy(j)
                cp.start() if start else cp.wait()
                return carry

            lax.fori_loop(0, N_EXPERTS, functools.partial(fill, start=True), 0)
            lax.fori_loop(nu_ref[0], n_blocks, functools.partial(tail, start=True), 0)
            lax.fori_loop(0, N_EXPERTS, functools.partial(fill, start=False), 0)
            lax.fori_loop(nu_ref[0], n_blocks, functools.partial(tail, start=False), 0)

    pos_copy.wait()

    def issue(t, c):
        for k in range(TOP_K):
            _row_copy(h_ref, t, xs_ref, pos_smem[t * 8 + k], sem).start()
        return c

    lax.fori_loop(0, tm, issue, 0)

    def drain(t, c):
        for k in range(TOP_K):
            _row_copy(h_ref, 0, xs_ref, 0, sem).wait()
        return c

    lax.fori_loop(0, tm, drain, 0)


def _dispatch(h2, pos_flat, ends, n_used, xs_prev, n_rows, tile_base):
    n = h2.shape[0] // ROW_PARTS
    tm = MOVE_TM
    zero_fill = xs_prev is None
    kern = functools.partial(_dispatch_kernel, tm=tm, zero_fill=zero_fill, tile_base=tile_base,
                             n_blocks=n_rows // EXPERT_BLOCK)
    in_specs = [pl.BlockSpec((tm * ROW_PARTS, LANES), lambda i, e, nu: (i, 0)),
                pl.BlockSpec(memory_space=pl.ANY)]
    args = [ends, n_used, h2, pos_flat]
    scratch = [pltpu.SMEM((tm * 8,), I32)]
    aliases = {}
    if zero_fill:
        scratch.append(pltpu.VMEM((EXPERT_BLOCK * ROW_PARTS, LANES), F32))
    else:
        in_specs.append(pl.BlockSpec(memory_space=pl.ANY))
        args.append(xs_prev)
        aliases = {4: 0}
    scratch += [pltpu.SemaphoreType.DMA, pltpu.SemaphoreType.DMA]
    return pl.pallas_call(
        kern,
        grid_spec=pltpu.PrefetchScalarGridSpec(
            num_scalar_prefetch=2,
            grid=(n // tm,),
            in_specs=in_specs,
            out_specs=pl.BlockSpec(memory_space=pl.ANY),
            scratch_shapes=scratch),
        out_shape=jax.ShapeDtypeStruct((n_rows * ROW_PARTS, LANES), F32),
        input_output_aliases=aliases,
        compiler_params=_params("arbitrary"),
        name="dispatch",
    )(*args)


def _expert_kernel(be_ref, nu_ref, x_ref, w1_ref, w3_ref, w2_ref, o_ref):
    used = pl.program_id(0) < nu_ref[0]

    @pl.when(used)
    def _():
        d = w1_ref.shape[0]
        x = _load_rows(x_ref, x_ref.shape[0] * LANES // d, d).astype(BF16)
        a = _silu(_dot(x, w1_ref[...])) * _dot(x, w3_ref[...])
        _store_rows(o_ref, _dot(a.astype(BF16), w2_ref[...]))

    @pl.when(jnp.logical_not(used))
    def _():
        o_ref[...] = jnp.zeros_like(o_ref)


def _experts(xs, block_e, n_used, w1, w3, w2):
    d, de = w1.shape[1:]
    blk = EXPERT_BLOCK
    rows = blk * ROW_PARTS
    nblk = xs.shape[0] // rows

    def row_map(i, be, nu):
        return (jnp.minimum(i, nu[0] - 1), 0)

    return pl.pallas_call(
        _expert_kernel,
        grid_spec=pltpu.PrefetchScalarGridSpec(
            num_scalar_prefetch=2,
            grid=(nblk,),
            in_specs=[pl.BlockSpec((rows, LANES), row_map),
                      pl.BlockSpec((None, d, de), lambda i, be, nu: (be[i], 0, 0)),
                      pl.BlockSpec((None, d, de), lambda i, be, nu: (be[i], 0, 0)),
                      pl.BlockSpec((None, de, d), lambda i, be, nu: (be[i], 0, 0))],
            out_specs=pl.BlockSpec((rows, LANES), lambda i, be, nu: (i, 0))),
        out_shape=jax.ShapeDtypeStruct(xs.shape, F32),
        compiler_params=_params("arbitrary"),
        name="experts",
    )(block_e, n_used, xs, w1, w3, w2)


def _combine_kernel(pos_ref, ys_ref, slab_ref, h_ref, x_ref, gate_ref, s1_ref, s3_ref, s2_ref,
                    o_ref, pos_smem, buf_ref, sem, psem, *, tm, tile_base):
    tile = tile_base + pl.program_id(0) * pl.num_programs(1) + pl.program_id(1)
    pos_copy = pltpu.make_async_copy(pos_ref.at[pl.ds(tile * tm * 8, tm * 8)], pos_smem, psem)
    pos_copy.start()
    pos_copy.wait()

    def issue(t, c):
        for k in range(TOP_K):
            _row_copy(ys_ref, pos_smem[t * 8 + k], buf_ref.at[k], t, sem).start()
        return c

    lax.fori_loop(0, tm, issue, 0)

    d = x_ref.shape[1]
    hb = _load_rows(h_ref, tm, d).astype(BF16)
    shared = _dot((_silu(_dot(hb, s1_ref[...])) * _dot(hb, s3_ref[...])).astype(BF16), s2_ref[...])

    def drain(t, c):
        for k in range(TOP_K):
            _row_copy(ys_ref, 0, buf_ref.at[k], 0, sem).wait()
        return c

    lax.fori_loop(0, tm, drain, 0)

    slab = slab_ref[...]
    acc = shared
    for k in range(TOP_K):
        acc = acc + slab[:, SLAB_GATE + k:SLAB_GATE + k + 1] * _load_rows(buf_ref.at[k], tm, d)
    o_ref[...] = x_ref[...] + gate_ref[...] * acc


def _combine(ys, pos_flat, slab, h2, x, gate, s1, s3, s2, tile_base):
    b, s, d = x.shape
    tm = MOVE_TM
    kern = functools.partial(_combine_kernel, tm=tm, tile_base=tile_base)
    return pl.pallas_call(
        kern,
        grid=(b, s // tm),
        in_specs=[pl.BlockSpec(memory_space=pl.ANY),
                  pl.BlockSpec(memory_space=pl.ANY),
                  pl.BlockSpec((None, tm, LANES), lambda bi, i: (bi, i, 0)),
                  pl.BlockSpec((None, tm * ROW_PARTS, LANES), lambda bi, i: (bi, i, 0)),
                  pl.BlockSpec((None, tm, d), lambda bi, i: (bi, i, 0)),
                  pl.BlockSpec((None, 1, d), lambda bi, i: (bi, 0, 0)),
                  pl.BlockSpec(s1.shape, lambda bi, i: (0, 0)),
                  pl.BlockSpec(s3.shape, lambda bi, i: (0, 0)),
                  pl.BlockSpec(s2.shape, lambda bi, i: (0, 0))],
        out_specs=pl.BlockSpec((None, tm, d), lambda bi, i: (bi, i, 0)),
        out_shape=jax.ShapeDtypeStruct((b, s, d), F32),
        scratch_shapes=[pltpu.SMEM((tm * 8,), I32),
                        pltpu.VMEM((TOP_K, tm * ROW_PARTS, LANES), F32),
                        pltpu.SemaphoreType.DMA,
                        pltpu.SemaphoreType.DMA],
        compiler_params=_params("arbitrary", "arbitrary"),
        name="combine",
    )(pos_flat, ys, slab, h2, x, gate, s1, s3, s2)


def _dft_tables(length, n_chan):
    scale = 1.0 / math.sqrt(length * n_chan)
    side = 1
    while side * side < length:
        side *= 2
    outer = length // side
    k = jnp.arange(length, dtype=I32)[:, None]
    a_idx = (k * jnp.arange(outer, dtype=I32)[None, :]) % outer
    b_idx = (k * jnp.arange(side, dtype=I32)[None, :]) % length
    ang_a = a_idx.astype(F32) * (2.0 * math.pi / outer)
    ang_b = b_idx.astype(F32) * (2.0 * math.pi / length)
    ca, sa = jnp.cos(ang_a)[:, :, None], jnp.sin(ang_a)[:, :, None]
    cb, sb = jnp.cos(ang_b)[:, None, :], jnp.sin(ang_b)[:, None, :]
    cos_t = (ca * cb - sa * sb).reshape(length, length)
    sin_t = (sa * cb + ca * sb).reshape(length, length)
    table = (jnp.concatenate([cos_t, -sin_t], axis=1) * scale).astype(BF16)
    return table


def _channel_table(n_chan, n_groups):
    m = jnp.arange(n_chan, dtype=I32)
    ang = ((m[:, None] * m[None, :]) % n_chan).astype(F32) * (2.0 * math.pi / n_chan)
    eye = jnp.eye(n_groups, dtype=F32)
    return jnp.concatenate([jnp.kron(eye, jnp.cos(ang)), jnp.kron(eye, jnp.sin(ang))], axis=1).astype(BF16)


def _rope_tables(n_tok):
    rows = n_tok // GRID_W
    axis_dim = HEAD_DIM // 2
    r = jnp.repeat(jnp.arange(rows, dtype=F32), GRID_W)
    col = jnp.tile(jnp.arange(GRID_W, dtype=F32), rows)
    inv = ROPE_BASE ** (-jnp.arange(0, axis_dim, 2, dtype=F32) / axis_dim)
    ar, ac = r[:, None] * inv, col[:, None] * inv
    cos = jnp.concatenate([jnp.cos(ar), jnp.cos(ar), jnp.cos(ac), jnp.cos(ac)], axis=1)
    sin = jnp.concatenate([-jnp.sin(ar), jnp.sin(ar), -jnp.sin(ac), jnp.sin(ac)], axis=1)
    return jnp.tile(cos, (1, N_HEADS)), jnp.tile(sin, (1, N_HEADS))


def _head_mean_matrix(width):
    h = jnp.arange(width) // HEAD_DIM
    return ((h[:, None] == h[None, :]).astype(F32) / HEAD_DIM).astype(BF16)


def _tile_matrices():
    src = jnp.arange(N_KV * HEAD_DIM)
    dst = jnp.arange(Q_PER_KV * HEAD_DIM)
    mats = [((src[:, None] // HEAD_DIM == h) & (src[:, None] % HEAD_DIM == dst[None, :] % HEAD_DIM))
            for h in range(N_KV)]
    return jnp.stack(mats).astype(BF16)


def _even_layer(x, xc, mod, modc, norm_g, w_in, ws, bs, w_out, tables):
    b, s, d = x.shape
    w = w_in.shape[1] // 3
    outs = []
    for stream, m, tm in ((x, mod, 512), (xc, modc, 256)):
        if stream is None:
            outs.append(None)
            continue
        length = stream.shape[1]
        tm = min(tm, length)
        uvf = _norm_mod_matmul(stream, norm_g, m[0], m[1], w_in, tm)
        gm, z = _even_mix(uvf, ws, bs, tables["chan"], tm)
        table = tables["pos"][length]
        y = _matmul(table, z.reshape(2 * length, b * w),
                    min(1024, length), min(1024, b * w), min(1024, 2 * length), BF16)
        outs.append(_outproj(
            gm, pl.BlockSpec((None, tm, w), lambda bi, i: (bi, i, 0)),
            y, pl.BlockSpec((tm, w), lambda bi, i: (i, bi)),
            w_out, stream, m[2], tm))
    return outs


def _odd_layer(x, xc, mod, modc, norm_g, w_in, qg, kg, sink, conv_w, w_out, tables, ctx_out):
    b, s, d = x.shape
    lc = xc.shape[1]
    half = N_HEADS * HEAD_DIM
    tm, tmc = 512, min(256, lc)
    proj = _norm_mod_matmul(x, norm_g, mod[0], mod[1], w_in, tm)
    projc = _norm_mod_matmul(xc, norm_g, modc[0], modc[1], w_in, tmc)
    prep = functools.partial(_qkv_prep, qg=qg, kg=kg, bdq=tables["bdq"], bdk=tables["bdk"], tile=tables["tile"])
    q, k4, v4 = prep(proj, tables["cos"], tables["sin"], tm=tm, with_q=True)
    qc, kc4, vc4 = prep(projc, tables["cos_c"], tables["sin_c"], tm=tmc, with_q=ctx_out)
    att = _attention(q, k4, v4, kc4, vc4, sink, band=True)
    conv = _short_conv(proj, conv_w)
    spec = lambda t: pl.BlockSpec((None, t, half), lambda bi, i: (bi, i, 0))
    y = _outproj(att, spec(tm), conv, spec(tm), w_out, x, mod[2], tm)
    yc = None
    if ctx_out:
        attc = _attention(qc, None, None, kc4, vc4, sink, band=False)
        convc = _short_conv(projc, conv_w)
        yc = _outproj(attc, spec(tmc), convc, spec(tmc), w_out, xc, modc[2], tmc)
    return y, yc


def _moe(x, xc, mod, modc, norm_g, rw_hi, rw_lo, rb, w1, w3, w2, s1, s3, s2, tri):
    b, s, d = x.shape
    n_lat = b * s
    counts0 = jnp.zeros((1, LANES), F32)
    h2, slab, counts = _route(x, norm_g, mod[3], mod[4], rw_hi, rw_lo, rb, counts0, tri, ROUTE_TM)
    slabs = [slab.reshape(n_lat, LANES)]
    n_tok = n_lat
    if xc is not None:
        lc = xc.shape[1]
        h2c, slabc, counts = _route(xc, norm_g, modc[3], modc[4], rw_hi, rw_lo, rb, counts, tri,
                                    min(ROUTE_TM, lc))
        slabs.append(slabc.reshape(b * lc, LANES))
        n_tok += b * lc
    cnt = counts[0, :N_EXPERTS].astype(I32)
    blk = EXPERT_BLOCK
    padded = (cnt + blk - 1) // blk * blk
    ends = jnp.cumsum(padded)
    starts = ends - padded
    n_rows = (n_tok * TOP_K + N_EXPERTS * (blk - 1) + blk - 1) // blk * blk
    n_blocks = n_rows // blk
    n_used = (ends[-1] // blk).reshape(1)
    blk_start = jnp.arange(n_blocks, dtype=I32) * blk
    be = jnp.minimum(jnp.searchsorted(ends, blk_start, side="right"), N_EXPERTS - 1).astype(I32)
    be_last = be[jnp.maximum(n_used[0] - 1, 0)]
    block_e = jnp.where(jnp.arange(n_blocks) < n_used[0], be, be_last)
    starts_row = jnp.zeros((1, LANES), F32).at[0, :N_EXPERTS].set(starts.astype(F32))
    slab_all = jnp.concatenate(slabs, axis=0) if len(slabs) > 1 else slabs[0]
    pos = _slots(slab_all, starts_row, ROUTE_TM)
    pos_flat = pos[:, :8].reshape(-1)
    ends = ends.astype(I32)
    n_used = n_used.astype(I32)
    xs = _dispatch(h2.reshape(-1, LANES), pos_flat, ends, n_used, None, n_rows, 0)
    if xc is not None:
        xs = _dispatch(h2c.reshape(-1, LANES), pos_flat, ends, n_used, xs, n_rows, n_lat // MOVE_TM)
    ys = _experts(xs, block_e, n_used, w1, w3, w2)
    x_new = _combine(ys, pos_flat, slab, h2, x, mod[5], s1, s3, s2, 0)
    xc_new = None
    if xc is not None:
        xc_new = _combine(ys, pos_flat, slabc, h2c, xc, modc[5], s1, s3, s2, n_lat // MOVE_TM)
    return x_new, xc_new


def kernel(x, c, ctx, c_ctx, ada_w, ada_b, norm1_g, norm2_g, ev_w_in, ev_w_s, ev_b_s, ev_w_out, od_w_in, od_q_norm_g, od_k_norm_g, od_sink, od_conv_w, od_w_out, router_w, router_b, exp_w_gate, exp_w_up, exp_w_down, sh_w_gate, sh_w_up, sh_w_down):
    b, s, d = x.shape
    lc = ctx.shape[1]
    depth = ada_w.shape[0]
    n_groups = ev_w_s.shape[1]
    half = d // 2

    rows = -(-(b + 1) // 8) * 8
    cond = jnp.zeros((rows, d), F32).at[:b].set(c).at[b].set(c_ctx)
    mod_all = _adaln(cond, ada_w, ada_b)

    tables = {
        "chan": _channel_table(LANES, n_groups),
        "pos": {s: _dft_tables(s, LANES), lc: _dft_tables(lc, LANES)},
        "bdq": _head_mean_matrix(N_HEADS * HEAD_DIM),
        "bdk": _head_mean_matrix(N_KV * HEAD_DIM),
        "tile": _tile_matrices(),
    }
    tables["cos"], tables["sin"] = _rope_tables(s)
    tables["cos_c"] = jnp.ones((lc, N_HEADS * HEAD_DIM), F32)
    tables["sin_c"] = jnp.zeros((lc, N_HEADS * HEAD_DIM), F32)
    tri = (jnp.arange(ROUTE_TM)[:, None] > jnp.arange(ROUTE_TM)[None, :]).astype(BF16)

    qw, kw = N_HEADS * HEAD_DIM, N_KV * HEAD_DIM
    perm = jnp.concatenate([jnp.arange(0, qw), jnp.arange(qw + 2 * kw, qw + 2 * kw + 3 * half),
                            jnp.arange(qw, qw + 2 * kw)])

    xc = ctx
    for l in range(depth):
        last = l == depth - 1
        even = l % 2 == 0
        need_ctx = not (last and even)
        pieces = [mod_all[l, :, j * d:(j + 1) * d] for j in range(6)]
        mod = [p[:b].reshape(b, 1, d) for p in pieces]
        modc = [jnp.broadcast_to(p[b].reshape(1, 1, d), (b, 1, d)) for p in pieces]
        g1 = norm1_g[l].reshape(1, d)
        g2 = norm2_g[l].reshape(1, d)
        if even:
            e = l // 2
            bs = jnp.broadcast_to(ev_b_s[e][:, :, None], (n_groups, CHUNK, LANES))
            y, yc = _even_layer(x, xc if (need_ctx and not last) else None, mod, modc, g1,
                                ev_w_in[e].astype(BF16), ev_w_s[e].astype(BF16), bs,
                                ev_w_out[e].astype(BF16), tables)
        else:
            o = l // 2
            qg = (jnp.tile(od_q_norm_g[o], N_HEADS) * (HEAD_DIM ** -0.5)).reshape(1, qw)
            kg = jnp.tile(od_k_norm_g[o], N_KV).reshape(1, kw)
            y, yc = _odd_layer(x, xc, mod, modc, g1, od_w_in[o][:, perm].astype(BF16), qg, kg,
                               od_sink[o], od_conv_w[o], od_w_out[o].astype(BF16), tables, not last)
        x = y
        if not last:
            xc = yc
        rw = jnp.zeros((d, LANES), F32).at[:, :N_EXPERTS].set(router_w[l])
        rw_hi = rw.astype(BF16)
        rw_lo = (rw - rw_hi.astype(F32)).astype(BF16)
        rb = jnp.zeros((1, LANES), F32).at[0, :N_EXPERTS].set(router_b[l])
        x, xc_new = _moe(x, None if last else xc, mod, modc, g2, rw_hi, rw_lo, rb,
                         exp_w_gate[l].astype(BF16), exp_w_up[l].astype(BF16), exp_w_down[l].astype(BF16),
                         sh_w_gate[l].astype(BF16), sh_w_up[l].astype(BF16), sh_w_down[l].astype(BF16), tri)
        if not last:
            xc = xc_new
    return x
```

```python
import functools
import math

import jax
import jax.numpy as jnp
from jax import lax
from jax.experimental import pallas as pl
from jax.experimental.pallas import tpu as pltpu

F32 = jnp.float32
BF16 = jnp.bfloat16
I32 = jnp.int32

LANES = 128
VMEM_LIMIT = 48 * 2**20

EPS = 1e-6
GRID_W = 64
CHUNK = 128
HEAD_DIM = 64
N_HEADS = 8
N_KV = 2
Q_PER_KV = N_HEADS // N_KV
ATT_BLOCK = 128
ROPE_BASE = 10000.0
N_EXPERTS = 64
TOP_K = 6
ROUTED_SCALE = 2.5
EXPERT_BLOCK = 256
MOE_TM = 256
ROW_PARTS = 8
SLAB_IDX, SLAB_LOC, SLAB_GATE = 0, 8, 16
TAB_WORDS = 1024


def _params(*sem):
    return pltpu.CompilerParams(dimension_semantics=sem, vmem_limit_bytes=VMEM_LIMIT)


def _sigmoid(x):
    return 1.0 / (1.0 + jnp.exp(-x))


def _silu(x):
    return x * _sigmoid(x)


def _gelu_tanh(x):
    c = math.sqrt(2.0 / math.pi)
    return x * (0.5 * (1.0 + jnp.tanh(c * (x + 0.044715 * (x * x * x)))))


def _dot(a, b):
    return jnp.dot(a, b, preferred_element_type=F32)


def _load_rows(ref, n, d):
    parts = d // LANES
    return jnp.concatenate([ref[pl.ds(c, n, stride=parts), :] for c in range(parts)], axis=-1)


def _store_rows(ref, val):
    n, d = val.shape
    parts = d // LANES
    for c in range(parts):
        ref[pl.ds(c, n, stride=parts), :] = val[:, c * LANES:(c + 1) * LANES]


def _adaln_kernel(c_ref, w_ref, b_ref, o_ref):
    o_ref[...] = _dot(_silu(c_ref[...]), w_ref[...]) + b_ref[...]


def _adaln(cond, ada_w, ada_b):
    n_layers, d, n6 = ada_w.shape
    rows = cond.shape[0]
    tn = 768
    return pl.pallas_call(
        _adaln_kernel,
        grid=(n_layers, n6 // tn),
        in_specs=[pl.BlockSpec((rows, d), lambda l, j: (0, 0)),
                  pl.BlockSpec((None, d, tn), lambda l, j: (l, 0, j)),
                  pl.BlockSpec((None, 1, tn), lambda l, j: (l, 0, j))],
        out_specs=pl.BlockSpec((None, rows, tn), lambda l, j: (l, 0, j)),
        out_shape=jax.ShapeDtypeStruct((n_layers, rows, n6), F32),
        compiler_params=_params("parallel", "parallel"),
        name="adaln",
    )(cond, ada_w, ada_b.reshape(n_layers, 1, n6))


def _norm_mod(x, g, shift, scale):
    ms = jnp.mean(x * x, axis=-1, keepdims=True)
    h = (x * lax.rsqrt(ms + EPS)) * g
    return h * (1.0 + scale) + shift


def _nmm_kernel(x_ref, g_ref, sh_ref, sc_ref, w_ref, o_ref):
    h = _norm_mod(x_ref[...], g_ref[...], sh_ref[...], sc_ref[...])
    o_ref[...] = _dot(h.astype(BF16), w_ref[...]).astype(o_ref.dtype)


def _norm_mod_matmul(x, g, shift, scale, w, tm):
    b, s, d = x.shape
    n = w.shape[1]
    return pl.pallas_call(
        _nmm_kernel,
        grid=(b, s // tm),
        in_specs=[pl.BlockSpec((None, tm, d), lambda bi, i: (bi, i, 0)),
                  pl.BlockSpec((1, d), lambda bi, i: (0, 0)),
                  pl.BlockSpec((None, 1, d), lambda bi, i: (bi, 0, 0)),
                  pl.BlockSpec((None, 1, d), lambda bi, i: (bi, 0, 0)),
                  pl.BlockSpec((d, n), lambda bi, i: (0, 0))],
        out_specs=pl.BlockSpec((None, tm, n), lambda bi, i: (bi, i, 0)),
        out_shape=jax.ShapeDtypeStruct((b, s, n), F32),
        compiler_params=_params("parallel", "parallel"),
        name="norm_mod_matmul",
    )(x, g, shift, scale, w)


def _even_mix_kernel(u_ref, v_ref, f_ref, ws_ref, bs_ref, cs_ref, gm_ref, z_ref, *, n_chunks, n_groups):
    for c in range(n_chunks):
        rows = slice(c * CHUNK, (c + 1) * CHUNK)
        for g in range(n_groups):
            cols = slice(g * LANES, (g + 1) * LANES)
            ug = _gelu_tanh(u_ref[rows, cols])
            vg = _gelu_tanh(v_ref[rows, cols])
            mu = jnp.mean(vg, axis=-1, keepdims=True)
            dv = vg - mu
            var = jnp.mean(dv * dv, axis=-1, keepdims=True)
            vn = dv * lax.rsqrt(var + 1e-5)
            fg = _dot(ws_ref[g], vn.astype(BF16)) + bs_ref[g]
            gm_ref[rows, cols] = (ug * fg).astype(gm_ref.dtype)
    fz = _dot(f_ref[...].astype(BF16), cs_ref[...])
    half = fz.shape[1] // 2
    z_ref[0] = fz[:, :half].astype(z_ref.dtype)
    z_ref[1] = fz[:, half:].astype(z_ref.dtype)


def _even_mix(uvf, ws, bs, cs, tm):
    b, s, n3 = uvf.shape
    w = n3 // 3
    n_groups = w // LANES
    kern = functools.partial(_even_mix_kernel, n_chunks=tm // CHUNK, n_groups=n_groups)
    return pl.pallas_call(
        kern,
        grid=(b, s // tm),
        in_specs=[pl.BlockSpec((None, tm, w), lambda bi, i: (bi, i, 0)),
                  pl.BlockSpec((None, tm, w), lambda bi, i: (bi, i, 1)),
                  pl.BlockSpec((None, tm, w), lambda bi, i: (bi, i, 2)),
                  pl.BlockSpec(ws.shape, lambda bi, i: (0, 0, 0)),
                  pl.BlockSpec(bs.shape, lambda bi, i: (0, 0, 0)),
                  pl.BlockSpec(cs.shape, lambda bi, i: (0, 0))],
        out_specs=[pl.BlockSpec((None, tm, w), lambda bi, i: (bi, i, 0)),
                   pl.BlockSpec((2, tm, w), lambda bi, i: (0, i, bi))],
        out_shape=[jax.ShapeDtypeStruct((b, s, w), BF16),
                   jax.ShapeDtypeStruct((2, s, b * w), BF16)],
        compiler_params=_params("parallel", "parallel"),
        name="even_mix",
    )(uvf, uvf, uvf, ws, bs, cs)


def _mm_kernel(a_ref, b_ref, o_ref, acc_ref):
    k = pl.program_id(2)

    @pl.when(k == 0)
    def _():
        acc_ref[...] = jnp.zeros_like(acc_ref)

    acc_ref[...] += _dot(a_ref[...], b_ref[...])

    @pl.when(k == pl.num_programs(2) - 1)
    def _():
        o_ref[...] = acc_ref[...].astype(o_ref.dtype)


def _matmul(a, b, tm, tn, tk, out_dtype):
    m, kd = a.shape
    n = b.shape[1]
    return pl.pallas_call(
        _mm_kernel,
        grid=(m // tm, n // tn, kd // tk),
        in_specs=[pl.BlockSpec((tm, tk), lambda i, j, k: (i, k)),
                  pl.BlockSpec((tk, tn), lambda i, j, k: (k, j))],
        out_specs=pl.BlockSpec((tm, tn), lambda i, j, k: (i, j)),
        out_shape=jax.ShapeDtypeStruct((m, n), out_dtype),
        scratch_shapes=[pltpu.VMEM((tm, tn), F32)],
        compiler_params=_params("parallel", "parallel", "arbitrary"),
        name="dft_matmul",
    )(a, b)


def _outproj_kernel(a_ref, b_ref, w_ref, x_ref, gate_ref, o_ref):
    ab = jnp.concatenate([a_ref[...], b_ref[...]], axis=-1)
    o_ref[...] = x_ref[...] + gate_ref[...] * _dot(ab, w_ref[...])


def _outproj(a, a_spec, bsrc, b_spec, w, x, gate, tm):
    b, s, d = x.shape
    return pl.pallas_call(
        _outproj_kernel,
        grid=(b, s // tm),
        in_specs=[a_spec, b_spec,
                  pl.BlockSpec(w.shape, lambda bi, i: (0, 0)),
                  pl.BlockSpec((None, tm, d), lambda bi, i: (bi, i, 0)),
                  pl.BlockSpec((None, 1, d), lambda bi, i: (bi, 0, 0))],
        out_specs=pl.BlockSpec((None, tm, d), lambda bi, i: (bi, i, 0)),
        out_shape=jax.ShapeDtypeStruct((b, s, d), F32),
        compiler_params=_params("parallel", "parallel"),
        name="outproj",
    )(a, bsrc, w, x, gate)


def _head_rms(x, bd_ref):
    xx = x * x
    hi = xx.astype(BF16)
    lo = (xx - hi.astype(F32)).astype(BF16)
    ms = _dot(hi, bd_ref[...]) + _dot(lo, bd_ref[...])
    return x * lax.rsqrt(ms + EPS)


def _rope(x, cos, sins):
    width = x.shape[1]
    lane = lax.broadcasted_iota(I32, x.shape, 1)
    first = (lane & 31) < 16
    swapped = jnp.where(first, pltpu.roll(x, width - 16, 1), pltpu.roll(x, 16, 1))
    return x * cos + swapped * sins


def _qkv_kernel(q_ref, k_ref, v_ref, cos_ref, sin_ref, qg_ref, kg_ref, bdq_ref, bdk_ref, tile_ref,
                qo_ref, k4_ref, v4_ref, *, with_q):
    kw = k_ref.shape[1]
    if with_q:
        qn = _head_rms(q_ref[...], bdq_ref) * qg_ref[...]
        qo_ref[...] = _rope(qn, cos_ref[...], sin_ref[...]).astype(qo_ref.dtype)
    else:
        qo_ref[...] = jnp.zeros_like(qo_ref)
    kn = _head_rms(k_ref[...], bdk_ref) * kg_ref[...]
    kr = _rope(kn, cos_ref[:, :kw], sin_ref[:, :kw]).astype(BF16)
    vb = v_ref[...].astype(BF16)
    for h in range(N_KV):
        k4_ref[h] = _dot(kr, tile_ref[h]).astype(k4_ref.dtype)
        v4_ref[h] = _dot(vb, tile_ref[h]).astype(v4_ref.dtype)


def _qkv_prep(proj, cos, sin, qg, kg, bdq, bdk, tile, tm, with_q):
    b, s, _ = proj.shape
    qw = N_HEADS * HEAD_DIM
    kw = N_KV * HEAD_DIM
    rep = Q_PER_KV * HEAD_DIM
    k_blk = (4 * qw) // kw
    kern = functools.partial(_qkv_kernel, with_q=with_q)
    return pl.pallas_call(
        kern,
        grid=(b, s // tm),
        in_specs=[pl.BlockSpec((None, tm, qw), lambda bi, i: (bi, i, 0)),
                  pl.BlockSpec((None, tm, kw), lambda bi, i: (bi, i, k_blk)),
                  pl.BlockSpec((None, tm, kw), lambda bi, i: (bi, i, k_blk + 1)),
                  pl.BlockSpec((tm, qw), lambda bi, i: (i, 0)),
                  pl.BlockSpec((tm, qw), lambda bi, i: (i, 0)),
                  pl.BlockSpec((1, qw), lambda bi, i: (0, 0)),
                  pl.BlockSpec((1, kw), lambda bi, i: (0, 0)),
                  pl.BlockSpec(bdq.shape, lambda bi, i: (0, 0)),
                  pl.BlockSpec(bdk.shape, lambda bi, i: (0, 0)),
                  pl.BlockSpec(tile.shape, lambda bi, i: (0, 0, 0))],
        out_specs=[pl.BlockSpec((None, tm, qw), lambda bi, i: (bi, i, 0)),
                   pl.BlockSpec((None, N_KV, tm, rep), lambda bi, i: (bi, 0, i, 0)),
                   pl.BlockSpec((None, N_KV, tm, rep), lambda bi, i: (bi, 0, i, 0))],
        out_shape=[jax.ShapeDtypeStruct((b, s, qw), BF16),
                   jax.ShapeDtypeStruct((b, N_KV, s, rep), BF16),
                   jax.ShapeDtypeStruct((b, N_KV, s, rep), BF16)],
        compiler_params=_params("parallel", "parallel"),
        name="qkv_prep",
    )(proj, proj, proj, cos, sin, qg, kg, bdq, bdk, tile)


def _conv_kernel(gi_ref, go_ref, z_ref, w_ref, o_ref, *, rows):
    s = z_ref.shape[0]
    n = s // rows
    w0, w1, w2 = w_ref[0:1, :], w_ref[1:2, :], w_ref[2:3, :]
    ridx = lax.broadcasted_iota(I32, (rows, z_ref.shape[1]), 0)
    prev_last = jnp.zeros((1, z_ref.shape[1]), F32)
    for j in range(n):
        r0 = j * rows
        zc = gi_ref[r0:r0 + rows, :] * z_ref[r0:r0 + rows, :]
        if j + 1 < n:
            nxt = gi_ref[r0 + rows:r0 + rows + 1, :] * z_ref[r0 + rows:r0 + rows + 1, :]
        else:
            nxt = jnp.zeros_like(prev_last)
        zp = jnp.where(ridx == 0, prev_last, pltpu.roll(zc, 1, 0))
        zn = jnp.where(ridx == rows - 1, nxt, pltpu.roll(zc, rows - 1, 0))
        y = zp * w0 + zc * w1 + zn * w2
        o_ref[r0:r0 + rows, :] = (go_ref[r0:r0 + rows, :] * y).astype(o_ref.dtype)
        prev_last = zc[rows - 1:rows, :]


def _short_conv(proj, conv_w):
    b, s, _ = proj.shape
    dc = conv_w.shape[1]
    nb = dc // LANES
    base = dc // LANES
    rows = min(s, 512)
    kern = functools.partial(_conv_kernel, rows=rows)
    return pl.pallas_call(
        kern,
        grid=(b, nb),
        in_specs=[pl.BlockSpec((None, s, LANES), lambda bi, c: (bi, 0, base + c)),
                  pl.BlockSpec((None, s, LANES), lambda bi, c: (bi, 0, 2 * base + c)),
                  pl.BlockSpec((None, s, LANES), lambda bi, c: (bi, 0, 3 * base + c)),
                  pl.BlockSpec((conv_w.shape[0], LANES), lambda bi, c: (0, c))],
        out_specs=pl.BlockSpec((None, s, LANES), lambda bi, c: (bi, 0, c)),
        out_shape=jax.ShapeDtypeStruct((b, s, dc), BF16),
        compiler_params=_params("parallel", "parallel"),
        name="short_conv",
    )(proj, proj, proj, conv_w)


def _attn_kernel(sink_ref, q_ref, *refs, band, n_blocks):
    o_ref = refs[-1]
    if band:
        kp, kc_, kn, vp, vc_, vn, kx, vx = refs[:-1]
        kcat = jnp.concatenate([kp[...], kc_[...], kn[...], kx[...]], axis=0)
        vcat = jnp.concatenate([vp[...], vc_[...], vn[...], vx[...]], axis=0)
    else:
        kx, vx = refs[:-1]
        kcat, vcat = kx[...], vx[...]
    h = pl.program_id(1)
    i = pl.program_id(2)
    q = q_ref[...]
    t, w = q.shape
    lane = lax.broadcasted_iota(I32, (t, w), 1)
    masks = [(lane >= g * HEAD_DIM) & (lane < (g + 1) * HEAD_DIM) for g in range(Q_PER_KV)]
    q4 = jnp.concatenate([jnp.where(m, q, jnp.zeros_like(q)) for m in masks], axis=0)
    s = lax.dot_general(q4, kcat, (((1,), (1,)), ((), ())), preferred_element_type=F32)
    if band:
        row = lax.broadcasted_iota(I32, s.shape, 0) & (t - 1)
        col = lax.broadcasted_iota(I32, s.shape, 1)
        off_prev = jnp.where(i > 0, 0, 4 * t)
        off_next = jnp.where(i < n_blocks - 1, 0, 4 * t)
        bad_prev = (col < t) & (col < row + off_prev)
        bad_next = (col >= 2 * t) & (col < 3 * t) & (col - 2 * t > row - off_next)
        s = jnp.where(bad_prev | bad_next, -jnp.inf, s)
    sink = jnp.concatenate(
        [jnp.full((t, 1), sink_ref[h * Q_PER_KV + g], F32) for g in range(Q_PER_KV)], axis=0)
    m = jnp.maximum(jnp.max(s, axis=-1, keepdims=True), sink)
    e = jnp.exp(s - m)
    den = jnp.sum(e, axis=-1, keepdims=True) + jnp.exp(sink - m)
    r = _dot(e.astype(BF16), vcat) * (1.0 / den)
    o = jnp.zeros((t, w), F32)
    for g in range(Q_PER_KV):
        o = o + jnp.where(masks[g], r[g * t:(g + 1) * t, :], 0.0)
    o_ref[...] = o.astype(o_ref.dtype)


def _attention(q, k4, v4, kx4, vx4, sink, band):
    b, s, qw = q.shape
    rep = k4.shape[-1] if band else kx4.shape[-1]
    lc = kx4.shape[2]
    t = ATT_BLOCK
    nb = s // t
    kern = functools.partial(_attn_kernel, band=band, n_blocks=nb)

    def kv_spec(off):
        return pl.BlockSpec((None, None, t, rep),
                            lambda bi, h, i: (bi, h, jnp.clip(i + off, 0, nb - 1), 0))

    ctx_spec = pl.BlockSpec((None, None, lc, rep), lambda bi, h, i: (bi, h, 0, 0))
    in_specs = [pl.BlockSpec(memory_space=pltpu.SMEM),
                pl.BlockSpec((None, t, rep), lambda bi, h, i: (bi, i, h))]
    args = [sink, q]
    if band:
        in_specs += [kv_spec(-1), kv_spec(0), kv_spec(1), kv_spec(-1), kv_spec(0), kv_spec(1)]
        args += [k4, k4, k4, v4, v4, v4]
    in_specs += [ctx_spec, ctx_spec]
    args += [kx4, vx4]
    return pl.pallas_call(
        kern,
        grid=(b, N_KV, nb),
        in_specs=in_specs,
        out_specs=pl.BlockSpec((None, t, rep), lambda bi, h, i: (bi, i, h)),
        out_shape=jax.ShapeDtypeStruct((b, s, qw), BF16),
        compiler_params=_params("parallel", "parallel", "parallel"),
        name="attention_band" if band else "attention_ctx",
    )(*args)


def _route_kernel(x_ref, g_ref, sh_ref, sc_ref, whi_ref, wlo_ref, rb_ref, cin_ref, tri_ref, upper_ref,
                  h_ref, slab_ref, tab_ref, cnt_ref, carry_ref):
    first = (pl.program_id(0) == 0) & (pl.program_id(1) == 0)

    @pl.when(first)
    def _():
        carry_ref[...] = cin_ref[...]

    h = _norm_mod(x_ref[...], g_ref[...], sh_ref[...], sc_ref[...])
    _store_rows(h_ref, h)
    hi = h.astype(BF16)
    lo = (h - hi.astype(F32)).astype(BF16)
    logits = _dot(hi, whi_ref[...]) + _dot(lo, whi_ref[...]) + _dot(hi, wlo_ref[...])
    scores = _sigmoid(logits)
    tm, lanes = scores.shape
    lane = lax.broadcasted_iota(I32, (tm, lanes), 1).astype(F32)
    work = jnp.where(lane < N_EXPERTS, scores + rb_ref[...], -jnp.inf)
    hits, idxs, gates = [], [], []
    for _ in range(TOP_K):
        mx = jnp.max(work, axis=-1, keepdims=True)
        idx = jnp.min(jnp.where(work == mx, lane, float(lanes)), axis=-1, keepdims=True)
        hit = lane == idx
        hits.append(hit)
        idxs.append(idx)
        gates.append(jnp.sum(jnp.where(hit, scores, 0.0), axis=-1, keepdims=True))
        work = jnp.where(hit, -jnp.inf, work)
    gsum = gates[0]
    for gk in gates[1:]:
        gsum = gsum + gk
    gscale = ROUTED_SCALE / (gsum + 1e-20)
    onehot = jnp.zeros((tm, lanes), F32)
    for hit in hits:
        onehot = jnp.where(hit, 1.0, onehot)
    cnt = jnp.sum(onehot, axis=0, keepdims=True)
    seg = _dot(jnp.broadcast_to(cnt, (8, lanes)).astype(BF16), upper_ref[...])[0:1]
    before = _dot(tri_ref[...], onehot.astype(BF16)) + seg
    slab = jnp.zeros((tm, lanes), F32)
    for k in range(TOP_K):
        loc = jnp.sum(jnp.where(hits[k], before, 0.0), axis=-1, keepdims=True)
        slab = jnp.where(lane == SLAB_IDX + k, idxs[k], slab)
        slab = jnp.where(lane == SLAB_LOC + k, loc, slab)
        slab = jnp.where(lane == SLAB_GATE + k, gates[k] * gscale, slab)
    slab_ref[...] = slab
    row = lax.broadcasted_iota(I32, (8, lanes), 0)
    tab_ref[...] = jnp.where(row == 0, cnt, jnp.where(row == 1, seg, jnp.where(row == 2, carry_ref[...], 0.0)))
    carry_ref[...] = carry_ref[...] + cnt
    cnt_ref[...] = carry_ref[...]


def _route(x, g, shift, scale, whi, wlo, rb, counts_in, tri, upper):
    b, s, d = x.shape
    parts = d // LANES
    tm = MOE_TM
    nt = s // tm
    return pl.pallas_call(
        _route_kernel,
        grid=(b, nt),
        in_specs=[pl.BlockSpec((None, tm, d), lambda bi, i: (bi, i, 0)),
                  pl.BlockSpec((1, d), lambda bi, i: (0, 0)),
                  pl.BlockSpec((None, 1, d), lambda bi, i: (bi, 0, 0)),
                  pl.BlockSpec((None, 1, d), lambda bi, i: (bi, 0, 0)),
                  pl.BlockSpec(whi.shape, lambda bi, i: (0, 0)),
                  pl.BlockSpec(wlo.shape, lambda bi, i: (0, 0)),
                  pl.BlockSpec((1, LANES), lambda bi, i: (0, 0)),
                  pl.BlockSpec((1, LANES), lambda bi, i: (0, 0)),
                  pl.BlockSpec((tm, tm), lambda bi, i: (0, 0)),
                  pl.BlockSpec((LANES, LANES), lambda bi, i: (0, 0))],
        out_specs=[pl.BlockSpec((None, tm * parts, LANES), lambda bi, i: (bi, i, 0)),
                   pl.BlockSpec((None, tm, LANES), lambda bi, i: (bi, i, 0)),
                   pl.BlockSpec((None, 8, LANES), lambda bi, i: (bi * nt + i, 0, 0)),
                   pl.BlockSpec((1, LANES), lambda bi, i: (0, 0))],
        out_shape=[jax.ShapeDtypeStruct((b, s * parts, LANES), F32),
                   jax.ShapeDtypeStruct((b, s, LANES), F32),
                   jax.ShapeDtypeStruct((b * nt, 8, LANES), F32),
                   jax.ShapeDtypeStruct((1, LANES), F32)],
        scratch_shapes=[pltpu.VMEM((1, LANES), F32)],
        compiler_params=_params("arbitrary", "arbitrary"),
        name="route",
    )(x, g, shift, scale, whi, wlo, rb, counts_in, tri, upper)


def _rows(ref, row, n):
    return ref.at[pl.ds(pl.multiple_of(row * ROW_PARTS, ROW_PARTS), n * ROW_PARTS)]


def _run_copies(tab_smem, stage_ref, far_ref, sem, to_far):
    def per_expert(e, carry):
        n = tab_smem[e]
        near = tab_smem[N_EXPERTS + e]
        far = tab_smem[2 * N_EXPERTS + e]
        for bit in range(MOE_TM.bit_length()):
            size = 1 << bit

            @pl.when((n & size) != 0)
            def _():
                done = n & (size - 1)
                a, b = _rows(stage_ref, near + done, size), _rows(far_ref, far + done, size)
                (pltpu.make_async_copy(a, b, sem) if to_far else pltpu.make_async_copy(b, a, sem)).start()
        return carry

    lax.fori_loop(0, N_EXPERTS, per_expert, 0)


def _wait_tile(stage_ref, far_ref, sem, to_far):
    n = TOP_K * MOE_TM
    a, b = _rows(stage_ref, 0, n), _rows(far_ref, 0, n)
    (pltpu.make_async_copy(a, b, sem) if to_far else pltpu.make_async_copy(b, a, sem)).wait()


def _dispatch_kernel(ends_ref, nu_ref, h_ref, loc_ref, tab_ref, *rest, zero_fill, tile_base, n_blocks):
    if zero_fill:
        xs_ref, loc_smem, tab_smem, stage_ref, zero_ref, sem, psem = rest
    else:
        _, xs_ref, loc_smem, tab_smem, stage_ref, sem, psem = rest
        zero_ref = None
    tm = MOE_TM
    i = pl.program_id(0)
    loc_copy = pltpu.make_async_copy(loc_ref.at[pl.ds((tile_base + i) * tm * 8, tm * 8)], loc_smem, psem)
    tab_copy = pltpu.make_async_copy(tab_ref.at[pl.ds((tile_base + i) * TAB_WORDS, TAB_WORDS)], tab_smem, psem)
    loc_copy.start()
    tab_copy.start()

    if zero_fill:
        @pl.when(i == 0)
        def _():
            zero_ref[...] = jnp.zeros_like(zero_ref)
            rows = EXPERT_BLOCK * ROW_PARTS

            def block_copy(blk):
                return pltpu.make_async_copy(
                    zero_ref, xs_ref.at[pl.ds(pl.multiple_of(blk * rows, rows), rows)], sem)

            def fill(e, carry, *, start):
                end = ends_ref[e]
                prev = jnp.where(e > 0, ends_ref[jnp.maximum(e - 1, 0)], 0)

                @pl.when(end > prev)
                def _():
                    cp = block_copy(end // EXPERT_BLOCK - 1)
                    cp.start() if start else cp.wait()
                return carry

            def tail(j, carry, *, start):
                cp = block_copy(j)
                cp.start() if start else cp.wait()
                return carry

            lax.fori_loop(0, N_EXPERTS, functools.partial(fill, start=True), 0)
            lax.fori_loop(nu_ref[0], n_blocks, functools.partial(tail, start=True), 0)
            lax.fori_loop(0, N_EXPERTS, functools.partial(fill, start=False), 0)
            lax.fori_loop(nu_ref[0], n_blocks, functools.partial(tail, start=False), 0)

    loc_copy.wait()
    tab_copy.wait()

    def place(t, c):
        row = h_ref[pl.ds(pl.multiple_of(t * ROW_PARTS, ROW_PARTS), ROW_PARTS), :]
        for k in range(TOP_K):
            at = loc_smem[t * 8 + k]
            stage_ref[pl.ds(pl.multiple_of(at * ROW_PARTS, ROW_PARTS), ROW_PARTS), :] = row
        return c

    lax.fori_loop(0, tm, place, 0, unroll=4)
    _run_copies(tab_smem, stage_ref, xs_ref, sem, to_far=True)
    _wait_tile(stage_ref, xs_ref, sem, to_far=True)


def _dispatch(h2, loc_flat, tab_flat, ends, n_used, xs_prev, n_rows, tile_base):
    n = h2.shape[0] // ROW_PARTS
    tm = MOE_TM
    zero_fill = xs_prev is None
    kern = functools.partial(_dispatch_kernel, zero_fill=zero_fill, tile_base=tile_base,
                             n_blocks=n_rows // EXPERT_BLOCK)
    in_specs = [pl.BlockSpec((tm * ROW_PARTS, LANES), lambda i, e, nu: (i, 0)),
                pl.BlockSpec(memory_space=pl.ANY),
                pl.BlockSpec(memory_space=pl.ANY)]
    args = [ends, n_used, h2, loc_flat, tab_flat]
    scratch = [pltpu.SMEM((tm * 8,), I32), pltpu.SMEM((TAB_WORDS,), I32),
               pltpu.VMEM((TOP_K * tm * ROW_PARTS, LANES), F32)]
    aliases = {}
    if zero_fill:
        scratch.append(pltpu.VMEM((EXPERT_BLOCK * ROW_PARTS, LANES), F32))
    else:
        in_specs.append(pl.BlockSpec(memory_space=pl.ANY))
        args.append(xs_prev)
        aliases = {5: 0}
    scratch += [pltpu.SemaphoreType.DMA, pltpu.SemaphoreType.DMA]
    return pl.pallas_call(
        kern,
        grid_spec=pltpu.PrefetchScalarGridSpec(
            num_scalar_prefetch=2,
            grid=(n // tm,),
            in_specs=in_specs,
            out_specs=pl.BlockSpec(memory_space=pl.ANY),
            scratch_shapes=scratch),
        out_shape=jax.ShapeDtypeStruct((n_rows * ROW_PARTS, LANES), F32),
        input_output_aliases=aliases,
        compiler_params=_params("arbitrary"),
        name="dispatch",
    )(*args)


def _expert_kernel(be_ref, nu_ref, x_ref, w1_ref, w3_ref, w2_ref, o_ref):
    used = pl.program_id(0) < nu_ref[0]

    @pl.when(used)
    def _():
        d = w1_ref.shape[0]
        x = _load_rows(x_ref, x_ref.shape[0] * LANES // d, d).astype(BF16)
        a = _silu(_dot(x, w1_ref[...])) * _dot(x, w3_ref[...])
        _store_rows(o_ref, _dot(a.astype(BF16), w2_ref[...]))

    @pl.when(jnp.logical_not(used))
    def _():
        o_ref[...] = jnp.zeros_like(o_ref)


def _experts(xs, block_e, n_used, w1, w3, w2):
    d, de = w1.shape[1:]
    blk = EXPERT_BLOCK
    rows = blk * ROW_PARTS
    nblk = xs.shape[0] // rows

    def row_map(i, be, nu):
        return (jnp.minimum(i, nu[0] - 1), 0)

    return pl.pallas_call(
        _expert_kernel,
        grid_spec=pltpu.PrefetchScalarGridSpec(
            num_scalar_prefetch=2,
            grid=(nblk,),
            in_specs=[pl.BlockSpec((rows, LANES), row_map),
                      pl.BlockSpec((None, d, de), lambda i, be, nu: (be[i], 0, 0)),
                      pl.BlockSpec((None, d, de), lambda i, be, nu: (be[i], 0, 0)),
                      pl.BlockSpec((None, de, d), lambda i, be, nu: (be[i], 0, 0))],
            out_specs=pl.BlockSpec((rows, LANES), lambda i, be, nu: (i, 0))),
        out_shape=jax.ShapeDtypeStruct(xs.shape, F32),
        compiler_params=_params("arbitrary"),
        name="experts",
    )(block_e, n_used, xs, w1, w3, w2)


def _combine_kernel(loc_ref, tab_ref, gates_ref, ys_ref, h_ref, x_ref, gate_ref, s1_ref, s3_ref, s2_ref,
                    o_ref, loc_smem, tab_smem, gate_smem, stage_ref, acc_ref, sem, psem, *, tile_base):
    tm = MOE_TM
    tile = tile_base + pl.program_id(0) * pl.num_programs(1) + pl.program_id(1)
    small = [pltpu.make_async_copy(loc_ref.at[pl.ds(tile * tm * 8, tm * 8)], loc_smem, psem),
             pltpu.make_async_copy(gates_ref.at[pl.ds(tile * tm * 8, tm * 8)], gate_smem, psem),
             pltpu.make_async_copy(tab_ref.at[pl.ds(tile * TAB_WORDS, TAB_WORDS)], tab_smem, psem)]
    for cp in small:
        cp.start()
    for cp in small:
        cp.wait()
    _run_copies(tab_smem, stage_ref, ys_ref, sem, to_far=False)
    d = x_ref.shape[1]
    hb = _load_rows(h_ref, tm, d).astype(BF16)
    shared = _dot((_silu(_dot(hb, s1_ref[...])) * _dot(hb, s3_ref[...])).astype(BF16), s2_ref[...])
    _wait_tile(stage_ref, ys_ref, sem, to_far=False)

    def mix(t, c):
        acc = None
        for k in range(TOP_K):
            at = loc_smem[t * 8 + k]
            term = gate_smem[t * 8 + k] * stage_ref[pl.ds(pl.multiple_of(at * ROW_PARTS, ROW_PARTS), ROW_PARTS), :]
            acc = term if acc is None else acc + term
        acc_ref[pl.ds(pl.multiple_of(t * ROW_PARTS, ROW_PARTS), ROW_PARTS), :] = acc
        return c

    lax.fori_loop(0, tm, mix, 0, unroll=4)
    o_ref[...] = x_ref[...] + gate_ref[...] * (_load_rows(acc_ref, tm, d) + shared)


def _combine(ys, loc_flat, tab_flat, gates_flat, h2, x, gate, s1, s3, s2, tile_base):
    b, s, d = x.shape
    tm = MOE_TM
    kern = functools.partial(_combine_kernel, tile_base=tile_base)
    return pl.pallas_call(
        kern,
        grid=(b, s // tm),
        in_specs=[pl.BlockSpec(memory_space=pl.ANY),
                  pl.BlockSpec(memory_space=pl.ANY),
                  pl.BlockSpec(memory_space=pl.ANY),
                  pl.BlockSpec(memory_space=pl.ANY),
                  pl.BlockSpec((None, tm * ROW_PARTS, LANES), lambda bi, i: (bi, i, 0)),
                  pl.BlockSpec((None, tm, d), lambda bi, i: (bi, i, 0)),
                  pl.BlockSpec((None, 1, d), lambda bi, i: (bi, 0, 0)),
                  pl.BlockSpec(s1.shape, lambda bi, i: (0, 0)),
                  pl.BlockSpec(s3.shape, lambda bi, i: (0, 0)),
                  pl.BlockSpec(s2.shape, lambda bi, i: (0, 0))],
        out_specs=pl.BlockSpec((None, tm, d), lambda bi, i: (bi, i, 0)),
        out_shape=jax.ShapeDtypeStruct((b, s, d), F32),
        scratch_shapes=[pltpu.SMEM((tm * 8,), I32),
                        pltpu.SMEM((TAB_WORDS,), I32),
                        pltpu.SMEM((tm * 8,), F32),
                        pltpu.VMEM((TOP_K * tm * ROW_PARTS, LANES), F32),
                        pltpu.VMEM((tm * ROW_PARTS, LANES), F32),
                        pltpu.SemaphoreType.DMA,
                        pltpu.SemaphoreType.DMA],
        compiler_params=_params("arbitrary", "arbitrary"),
        name="combine",
    )(loc_flat, tab_flat, gates_flat, ys, h2, x, gate, s1, s3, s2)


def _dft_tables(length, n_chan):
    scale = 1.0 / math.sqrt(length * n_chan)
    side = 1
    while side * side < length:
        side *= 2
    outer = length // side
    k = jnp.arange(length, dtype=I32)[:, None]
    a_idx = (k * jnp.arange(outer, dtype=I32)[None, :]) % outer
    b_idx = (k * jnp.arange(side, dtype=I32)[None, :]) % length
    ang_a = a_idx.astype(F32) * (2.0 * math.pi / outer)
    ang_b = b_idx.astype(F32) * (2.0 * math.pi / length)
    ca, sa = jnp.cos(ang_a)[:, :, None], jnp.sin(ang_a)[:, :, None]
    cb, sb = jnp.cos(ang_b)[:, None, :], jnp.sin(ang_b)[:, None, :]
    cos_t = (ca * cb - sa * sb).reshape(length, length)
    sin_t = (sa * cb + ca * sb).reshape(length, length)
    table = (jnp.concatenate([cos_t, -sin_t], axis=1) * scale).astype(BF16)
    return table


def _channel_table(n_chan, n_groups):
    m = jnp.arange(n_chan, dtype=I32)
    ang = ((m[:, None] * m[None, :]) % n_chan).astype(F32) * (2.0 * math.pi / n_chan)
    eye = jnp.eye(n_groups, dtype=F32)
    return jnp.concatenate([jnp.kron(eye, jnp.cos(ang)), jnp.kron(eye, jnp.sin(ang))], axis=1).astype(BF16)


def _rope_tables(n_tok):
    rows = n_tok // GRID_W
    axis_dim = HEAD_DIM // 2
    r = jnp.repeat(jnp.arange(rows, dtype=F32), GRID_W)
    col = jnp.tile(jnp.arange(GRID_W, dtype=F32), rows)
    inv = ROPE_BASE ** (-jnp.arange(0, axis_dim, 2, dtype=F32) / axis_dim)
    ar, ac = r[:, None] * inv, col[:, None] * inv
    cos = jnp.concatenate([jnp.cos(ar), jnp.cos(ar), jnp.cos(ac), jnp.cos(ac)], axis=1)
    sin = jnp.concatenate([-jnp.sin(ar), jnp.sin(ar), -jnp.sin(ac), jnp.sin(ac)], axis=1)
    return jnp.tile(cos, (1, N_HEADS)), jnp.tile(sin, (1, N_HEADS))


def _head_mean_matrix(width):
    h = jnp.arange(width) // HEAD_DIM
    return ((h[:, None] == h[None, :]).astype(F32) / HEAD_DIM).astype(BF16)


def _tile_matrices():
    src = jnp.arange(N_KV * HEAD_DIM)
    dst = jnp.arange(Q_PER_KV * HEAD_DIM)
    mats = [((src[:, None] // HEAD_DIM == h) & (src[:, None] % HEAD_DIM == dst[None, :] % HEAD_DIM))
            for h in range(N_KV)]
    return jnp.stack(mats).astype(BF16)


def _even_layer(x, xc, mod, modc, norm_g, w_in, ws, bs, w_out, tables):
    b, s, d = x.shape
    w = w_in.shape[1] // 3
    outs = []
    for stream, m, tm in ((x, mod, 512), (xc, modc, 256)):
        if stream is None:
            outs.append(None)
            continue
        length = stream.shape[1]
        tm = min(tm, length)
        uvf = _norm_mod_matmul(stream, norm_g, m[0], m[1], w_in, tm)
        gm, z = _even_mix(uvf, ws, bs, tables["chan"], tm)
        table = tables["pos"][length]
        y = _matmul(table, z.reshape(2 * length, b * w),
                    min(1024, length), min(1024, b * w), min(1024, 2 * length), BF16)
        outs.append(_outproj(
            gm, pl.BlockSpec((None, tm, w), lambda bi, i: (bi, i, 0)),
            y, pl.BlockSpec((tm, w), lambda bi, i: (i, bi)),
            w_out, stream, m[2], tm))
    return outs


def _odd_layer(x, xc, mod, modc, norm_g, w_in, qg, kg, sink, conv_w, w_out, tables, ctx_out):
    b, s, d = x.shape
    lc = xc.shape[1]
    half = N_HEADS * HEAD_DIM
    tm, tmc = 512, min(256, lc)
    proj = _norm_mod_matmul(x, norm_g, mod[0], mod[1], w_in, tm)
    projc = _norm_mod_matmul(xc, norm_g, modc[0], modc[1], w_in, tmc)
    prep = functools.partial(_qkv_prep, qg=qg, kg=kg, bdq=tables["bdq"], bdk=tables["bdk"], tile=tables["tile"])
    q, k4, v4 = prep(proj, tables["cos"], tables["sin"], tm=tm, with_q=True)
    qc, kc4, vc4 = prep(projc, tables["cos_c"], tables["sin_c"], tm=tmc, with_q=ctx_out)
    att = _attention(q, k4, v4, kc4, vc4, sink, band=True)
    conv = _short_conv(proj, conv_w)
    spec = lambda t: pl.BlockSpec((None, t, half), lambda bi, i: (bi, i, 0))
    y = _outproj(att, spec(tm), conv, spec(tm), w_out, x, mod[2], tm)
    yc = None
    if ctx_out:
        attc = _attention(qc, None, None, kc4, vc4, sink, band=False)
        convc = _short_conv(projc, conv_w)
        yc = _outproj(attc, spec(tmc), convc, spec(tmc), w_out, xc, modc[2], tmc)
    return y, yc


def _moe(x, xc, mod, modc, norm_g, rw_hi, rw_lo, rb, w1, w3, w2, s1, s3, s2, tri, upper):
    b, s, d = x.shape
    n_lat = b * s
    counts0 = jnp.zeros((1, LANES), F32)
    h2, slab, tab, counts = _route(x, norm_g, mod[3], mod[4], rw_hi, rw_lo, rb, counts0, tri, upper)
    slabs, tabs = [slab.reshape(n_lat, LANES)], [tab]
    n_tok = n_lat
    if xc is not None:
        h2c, slabc, tabc, counts = _route(xc, norm_g, modc[3], modc[4], rw_hi, rw_lo, rb, counts, tri, upper)
        slabs.append(slabc.reshape(-1, LANES))
        tabs.append(tabc)
        n_tok += slabs[1].shape[0]
    cnt = counts[0, :N_EXPERTS].astype(I32)
    blk = EXPERT_BLOCK
    padded = (cnt + blk - 1) // blk * blk
    ends = jnp.cumsum(padded).astype(I32)
    starts = ends - padded
    n_rows = (n_tok * TOP_K + N_EXPERTS * (blk - 1) + blk - 1) // blk * blk
    n_blocks = n_rows // blk
    n_used = (ends[-1] // blk).reshape(1).astype(I32)
    blk_start = jnp.minimum(jnp.arange(n_blocks, dtype=I32), n_used[0] - 1) * blk
    block_e = jnp.minimum(jnp.sum(blk_start[:, None] >= ends[None, :], axis=1), N_EXPERTS - 1).astype(I32)
    slab_all = jnp.concatenate(slabs, axis=0)
    tab_all = jnp.concatenate(tabs, axis=0)[:, :, :N_EXPERTS].astype(I32)
    runs = jnp.concatenate([tab_all[:, 0], tab_all[:, 1], tab_all[:, 2] + starts[None, :]], axis=1)
    tab_flat = jnp.pad(runs, ((0, 0), (0, TAB_WORDS - runs.shape[1]))).reshape(-1)
    loc_flat = slab_all[:, SLAB_LOC:SLAB_LOC + 8].astype(I32).reshape(-1)
    gates_flat = slab_all[:, SLAB_GATE:SLAB_GATE + 8].reshape(-1)
    xs = _dispatch(h2.reshape(-1, LANES), loc_flat, tab_flat, ends, n_used, None, n_rows, 0)
    if xc is not None:
        xs = _dispatch(h2c.reshape(-1, LANES), loc_flat, tab_flat, ends, n_used, xs, n_rows, n_lat // MOE_TM)
    ys = _experts(xs, block_e, n_used, w1, w3, w2)
    x_new = _combine(ys, loc_flat, tab_flat, gates_flat, h2, x, mod[5], s1, s3, s2, 0)
    xc_new = None
    if xc is not None:
        xc_new = _combine(ys, loc_flat, tab_flat, gates_flat, h2c, xc, modc[5], s1, s3, s2, n_lat // MOE_TM)
    return x_new, xc_new


def kernel(x, c, ctx, c_ctx, ada_w, ada_b, norm1_g, norm2_g, ev_w_in, ev_w_s, ev_b_s, ev_w_out, od_w_in, od_q_norm_g, od_k_norm_g, od_sink, od_conv_w, od_w_out, router_w, router_b, exp_w_gate, exp_w_up, exp_w_down, sh_w_gate, sh_w_up, sh_w_down):
    b, s, d = x.shape
    lc = ctx.shape[1]
    depth = ada_w.shape[0]
    n_groups = ev_w_s.shape[1]
    half = d // 2

    rows = -(-(b + 1) // 8) * 8
    cond = jnp.zeros((rows, d), F32).at[:b].set(c).at[b].set(c_ctx)
    mod_all = _adaln(cond, ada_w, ada_b)

    tables = {
        "chan": _channel_table(LANES, n_groups),
        "pos": {s: _dft_tables(s, LANES), lc: _dft_tables(lc, LANES)},
        "bdq": _head_mean_matrix(N_HEADS * HEAD_DIM),
        "bdk": _head_mean_matrix(N_KV * HEAD_DIM),
        "tile": _tile_matrices(),
    }
    tables["cos"], tables["sin"] = _rope_tables(s)
    tables["cos_c"] = jnp.ones((lc, N_HEADS * HEAD_DIM), F32)
    tables["sin_c"] = jnp.zeros((lc, N_HEADS * HEAD_DIM), F32)
    tri = (jnp.arange(MOE_TM)[:, None] > jnp.arange(MOE_TM)[None, :]).astype(BF16)
    upper = (jnp.arange(LANES)[:, None] < jnp.arange(LANES)[None, :]).astype(BF16)

    qw, kw = N_HEADS * HEAD_DIM, N_KV * HEAD_DIM
    perm = jnp.concatenate([jnp.arange(0, qw), jnp.arange(qw + 2 * kw, qw + 2 * kw + 3 * half),
                            jnp.arange(qw, qw + 2 * kw)])

    xc = ctx
    for l in range(depth):
        last = l == depth - 1
        even = l % 2 == 0
        need_ctx = not (last and even)
        pieces = [mod_all[l, :, j * d:(j + 1) * d] for j in range(6)]
        mod = [p[:b].reshape(b, 1, d) for p in pieces]
        modc = [jnp.broadcast_to(p[b].reshape(1, 1, d), (b, 1, d)) for p in pieces]
        g1 = norm1_g[l].reshape(1, d)
        g2 = norm2_g[l].reshape(1, d)
        if even:
            e = l // 2
            bs = jnp.broadcast_to(ev_b_s[e][:, :, None], (n_groups, CHUNK, LANES))
            y, yc = _even_layer(x, xc if (need_ctx and not last) else None, mod, modc, g1,
                                ev_w_in[e].astype(BF16), ev_w_s[e].astype(BF16), bs,
                                ev_w_out[e].astype(BF16), tables)
        else:
            o = l // 2
            qg = (jnp.tile(od_q_norm_g[o], N_HEADS) * (HEAD_DIM ** -0.5)).reshape(1, qw)
            kg = jnp.tile(od_k_norm_g[o], N_KV).reshape(1, kw)
            y, yc = _odd_layer(x, xc, mod, modc, g1, od_w_in[o][:, perm].astype(BF16), qg, kg,
                               od_sink[o], od_conv_w[o], od_w_out[o].astype(BF16), tables, not last)
        x = y
        if not last:
            xc = yc
        rw = jnp.zeros((d, LANES), F32).at[:, :N_EXPERTS].set(router_w[l])
        rw_hi = rw.astype(BF16)
        rw_lo = (rw - rw_hi.astype(F32)).astype(BF16)
        rb = jnp.zeros((1, LANES), F32).at[0, :N_EXPERTS].set(router_b[l])
        x, xc_new = _moe(x, None if last else xc, mod, modc, g2, rw_hi, rw_lo, rb,
                         exp_w_gate[l].astype(BF16), exp_w_up[l].astype(BF16), exp_w_down[l].astype(BF16),
                         sh_w_gate[l].astype(BF16), sh_w_up[l].astype(BF16), sh_w_down[l].astype(BF16),
                         tri, upper)
        if not last:
            xc = xc_new
    return x
```

```python
import functools
import math

import jax
import jax.numpy as jnp
from jax import lax
from jax.experimental import pallas as pl
from jax.experimental.pallas import tpu as pltpu

F32 = jnp.float32
BF16 = jnp.bfloat16
I32 = jnp.int32
U32 = jnp.uint32

LANES = 128
VMEM_LIMIT = 48 * 2**20

EPS = 1e-6
GRID_W = 64
CHUNK = 128
HEAD_DIM = 64
N_HEADS = 8
N_KV = 2
Q_PER_KV = N_HEADS // N_KV
ATT_BLOCK = 128
ROPE_BASE = 10000.0
N_EXPERTS = 64
TOP_K = 6
ROUTED_SCALE = 2.5
EXPERT_BLOCK = 512
MOE_TM = 256
ROW_PARTS = 4
SLAB_IDX, SLAB_LOC, SLAB_GATE = 0, 8, 16
TAB_WORDS = 1024


def _params(*sem):
    return pltpu.CompilerParams(dimension_semantics=sem, vmem_limit_bytes=VMEM_LIMIT)


def _sigmoid(x):
    return 1.0 / (1.0 + jnp.exp(-x))


def _silu(x):
    return x * _sigmoid(x)


def _gelu_tanh(x):
    c = math.sqrt(2.0 / math.pi)
    return x * (0.5 * (1.0 + jnp.tanh(c * (x + 0.044715 * (x * x * x)))))


def _dot(a, b):
    return jnp.dot(a, b, preferred_element_type=F32)


def _unpack_words(w):
    return pltpu.bitcast(w << 16, F32), pltpu.bitcast(w & jnp.uint32(0xFFFF0000), F32)


def _unpack_rows(ref):
    halves = [_unpack_words(ref[:, c, :]) for c in range(ref.shape[1])]
    return jnp.concatenate([h[0] for h in halves] + [h[1] for h in halves], axis=-1)


def _pack_rows(ref, val):
    half = val.shape[1] // 2
    bits = pltpu.bitcast(val.astype(BF16).astype(F32), U32)
    words = (bits[:, :half] >> 16) | (bits[:, half:] & jnp.uint32(0xFFFF0000))
    for c in range(half // LANES):
        ref[:, c, :] = words[:, c * LANES:(c + 1) * LANES]


def _adaln_kernel(c_ref, w_ref, b_ref, o_ref):
    o_ref[...] = _dot(_silu(c_ref[...]), w_ref[...]) + b_ref[...]


def _adaln(cond, ada_w, ada_b):
    n_layers, d, n6 = ada_w.shape
    rows = cond.shape[0]
    tn = 768
    return pl.pallas_call(
        _adaln_kernel,
        grid=(n_layers, n6 // tn),
        in_specs=[pl.BlockSpec((rows, d), lambda l, j: (0, 0)),
                  pl.BlockSpec((None, d, tn), lambda l, j: (l, 0, j)),
                  pl.BlockSpec((None, 1, tn), lambda l, j: (l, 0, j))],
        out_specs=pl.BlockSpec((None, rows, tn), lambda l, j: (l, 0, j)),
        out_shape=jax.ShapeDtypeStruct((n_layers, rows, n6), F32),
        compiler_params=_params("parallel", "parallel"),
        name="adaln",
    )(cond, ada_w, ada_b.reshape(n_layers, 1, n6))


def _norm_mod(x, g, shift, scale):
    ms = jnp.mean(x * x, axis=-1, keepdims=True)
    h = (x * lax.rsqrt(ms + EPS)) * g
    return h * (1.0 + scale) + shift


def _nmm_kernel(x_ref, g_ref, sh_ref, sc_ref, w_ref, o_ref):
    h = _norm_mod(x_ref[...], g_ref[...], sh_ref[...], sc_ref[...])
    o_ref[...] = _dot(h.astype(BF16), w_ref[...]).astype(o_ref.dtype)


def _norm_mod_matmul(x, g, shift, scale, w, tm):
    b, s, d = x.shape
    n = w.shape[1]
    return pl.pallas_call(
        _nmm_kernel,
        grid=(b, s // tm),
        in_specs=[pl.BlockSpec((None, tm, d), lambda bi, i: (bi, i, 0)),
                  pl.BlockSpec((1, d), lambda bi, i: (0, 0)),
                  pl.BlockSpec((None, 1, d), lambda bi, i: (bi, 0, 0)),
                  pl.BlockSpec((None, 1, d), lambda bi, i: (bi, 0, 0)),
                  pl.BlockSpec((d, n), lambda bi, i: (0, 0))],
        out_specs=pl.BlockSpec((None, tm, n), lambda bi, i: (bi, i, 0)),
        out_shape=jax.ShapeDtypeStruct((b, s, n), F32),
        compiler_params=_params("parallel", "parallel"),
        name="norm_mod_matmul",
    )(x, g, shift, scale, w)


def _even_mix_kernel(u_ref, v_ref, f_ref, ws_ref, bs_ref, cs_ref, gm_ref, z_ref, *, n_chunks, n_groups):
    for c in range(n_chunks):
        rows = slice(c * CHUNK, (c + 1) * CHUNK)
        for g in range(n_groups):
            cols = slice(g * LANES, (g + 1) * LANES)
            ug = _gelu_tanh(u_ref[rows, cols])
            vg = _gelu_tanh(v_ref[rows, cols])
            mu = jnp.mean(vg, axis=-1, keepdims=True)
            dv = vg - mu
            var = jnp.mean(dv * dv, axis=-1, keepdims=True)
            vn = dv * lax.rsqrt(var + 1e-5)
            fg = _dot(ws_ref[g], vn.astype(BF16)) + bs_ref[g]
            gm_ref[rows, cols] = (ug * fg).astype(gm_ref.dtype)
    fz = _dot(f_ref[...].astype(BF16), cs_ref[...])
    half = fz.shape[1] // 2
    z_ref[0] = fz[:, :half].astype(z_ref.dtype)
    z_ref[1] = fz[:, half:].astype(z_ref.dtype)


def _even_mix(uvf, ws, bs, cs, tm):
    b, s, n3 = uvf.shape
    w = n3 // 3
    n_groups = w // LANES
    kern = functools.partial(_even_mix_kernel, n_chunks=tm // CHUNK, n_groups=n_groups)
    return pl.pallas_call(
        kern,
        grid=(b, s // tm),
        in_specs=[pl.BlockSpec((None, tm, w), lambda bi, i: (bi, i, 0)),
                  pl.BlockSpec((None, tm, w), lambda bi, i: (bi, i, 1)),
                  pl.BlockSpec((None, tm, w), lambda bi, i: (bi, i, 2)),
                  pl.BlockSpec(ws.shape, lambda bi, i: (0, 0, 0)),
                  pl.BlockSpec(bs.shape, lambda bi, i: (0, 0, 0)),
                  pl.BlockSpec(cs.shape, lambda bi, i: (0, 0))],
        out_specs=[pl.BlockSpec((None, tm, w), lambda bi, i: (bi, i, 0)),
                   pl.BlockSpec((2, tm, w), lambda bi, i: (0, i, bi))],
        out_shape=[jax.ShapeDtypeStruct((b, s, w), BF16),
                   jax.ShapeDtypeStruct((2, s, b * w), BF16)],
        compiler_params=_params("parallel", "parallel"),
        name="even_mix",
    )(uvf, uvf, uvf, ws, bs, cs)


def _mm_kernel(a_ref, b_ref, o_ref, acc_ref):
    k = pl.program_id(2)

    @pl.when(k == 0)
    def _():
        acc_ref[...] = jnp.zeros_like(acc_ref)

    acc_ref[...] += _dot(a_ref[...], b_ref[...])

    @pl.when(k == pl.num_programs(2) - 1)
    def _():
        o_ref[...] = acc_ref[...].astype(o_ref.dtype)


def _matmul(a, b, tm, tn, tk, out_dtype):
    m, kd = a.shape
    n = b.shape[1]
    return pl.pallas_call(
        _mm_kernel,
        grid=(m // tm, n // tn, kd // tk),
        in_specs=[pl.BlockSpec((tm, tk), lambda i, j, k: (i, k)),
                  pl.BlockSpec((tk, tn), lambda i, j, k: (k, j))],
        out_specs=pl.BlockSpec((tm, tn), lambda i, j, k: (i, j)),
        out_shape=jax.ShapeDtypeStruct((m, n), out_dtype),
        scratch_shapes=[pltpu.VMEM((tm, tn), F32)],
        compiler_params=_params("parallel", "parallel", "arbitrary"),
        name="dft_matmul",
    )(a, b)


def _outproj_kernel(a_ref, b_ref, w_ref, x_ref, gate_ref, o_ref):
    ab = jnp.concatenate([a_ref[...], b_ref[...]], axis=-1)
    o_ref[...] = x_ref[...] + gate_ref[...] * _dot(ab, w_ref[...])


def _outproj(a, a_spec, bsrc, b_spec, w, x, gate, tm):
    b, s, d = x.shape
    return pl.pallas_call(
        _outproj_kernel,
        grid=(b, s // tm),
        in_specs=[a_spec, b_spec,
                  pl.BlockSpec(w.shape, lambda bi, i: (0, 0)),
                  pl.BlockSpec((None, tm, d), lambda bi, i: (bi, i, 0)),
                  pl.BlockSpec((None, 1, d), lambda bi, i: (bi, 0, 0))],
        out_specs=pl.BlockSpec((None, tm, d), lambda bi, i: (bi, i, 0)),
        out_shape=jax.ShapeDtypeStruct((b, s, d), F32),
        compiler_params=_params("parallel", "parallel"),
        name="outproj",
    )(a, bsrc, w, x, gate)


def _head_rms(x, bd_ref):
    xx = x * x
    hi = xx.astype(BF16)
    lo = (xx - hi.astype(F32)).astype(BF16)
    ms = _dot(hi, bd_ref[...]) + _dot(lo, bd_ref[...])
    return x * lax.rsqrt(ms + EPS)


def _rope(x, cos, sins):
    width = x.shape[1]
    lane = lax.broadcasted_iota(I32, x.shape, 1)
    first = (lane & 31) < 16
    swapped = jnp.where(first, pltpu.roll(x, width - 16, 1), pltpu.roll(x, 16, 1))
    return x * cos + swapped * sins


def _qkv_kernel(q_ref, k_ref, v_ref, cos_ref, sin_ref, qg_ref, kg_ref, bdq_ref, bdk_ref, tile_ref,
                qo_ref, k4_ref, v4_ref, *, with_q):
    kw = k_ref.shape[1]
    if with_q:
        qn = _head_rms(q_ref[...], bdq_ref) * qg_ref[...]
        qo_ref[...] = _rope(qn, cos_ref[...], sin_ref[...]).astype(qo_ref.dtype)
    else:
        qo_ref[...] = jnp.zeros_like(qo_ref)
    kn = _head_rms(k_ref[...], bdk_ref) * kg_ref[...]
    kr = _rope(kn, cos_ref[:, :kw], sin_ref[:, :kw]).astype(BF16)
    vb = v_ref[...].astype(BF16)
    for h in range(N_KV):
        k4_ref[h] = _dot(kr, tile_ref[h]).astype(k4_ref.dtype)
        v4_ref[h] = _dot(vb, tile_ref[h]).astype(v4_ref.dtype)


def _qkv_prep(proj, cos, sin, qg, kg, bdq, bdk, tile, tm, with_q):
    b, s, _ = proj.shape
    qw = N_HEADS * HEAD_DIM
    kw = N_KV * HEAD_DIM
    rep = Q_PER_KV * HEAD_DIM
    k_blk = (4 * qw) // kw
    kern = functools.partial(_qkv_kernel, with_q=with_q)
    return pl.pallas_call(
        kern,
        grid=(b, s // tm),
        in_specs=[pl.BlockSpec((None, tm, qw), lambda bi, i: (bi, i, 0)),
                  pl.BlockSpec((None, tm, kw), lambda bi, i: (bi, i, k_blk)),
                  pl.BlockSpec((None, tm, kw), lambda bi, i: (bi, i, k_blk + 1)),
                  pl.BlockSpec((tm, qw), lambda bi, i: (i, 0)),
                  pl.BlockSpec((tm, qw), lambda bi, i: (i, 0)),
                  pl.BlockSpec((1, qw), lambda bi, i: (0, 0)),
                  pl.BlockSpec((1, kw), lambda bi, i: (0, 0)),
                  pl.BlockSpec(bdq.shape, lambda bi, i: (0, 0)),
                  pl.BlockSpec(bdk.shape, lambda bi, i: (0, 0)),
                  pl.BlockSpec(tile.shape, lambda bi, i: (0, 0, 0))],
        out_specs=[pl.BlockSpec((None, tm, qw), lambda bi, i: (bi, i, 0)),
                   pl.BlockSpec((None, N_KV, tm, rep), lambda bi, i: (bi, 0, i, 0)),
                   pl.BlockSpec((None, N_KV, tm, rep), lambda bi, i: (bi, 0, i, 0))],
        out_shape=[jax.ShapeDtypeStruct((b, s, qw), BF16),
                   jax.ShapeDtypeStruct((b, N_KV, s, rep), BF16),
                   jax.ShapeDtypeStruct((b, N_KV, s, rep), BF16)],
        compiler_params=_params("parallel", "parallel"),
        name="qkv_prep",
    )(proj, proj, proj, cos, sin, qg, kg, bdq, bdk, tile)


def _conv_kernel(gi_ref, go_ref, z_ref, w_ref, o_ref, *, rows):
    s = z_ref.shape[0]
    n = s // rows
    w0, w1, w2 = w_ref[0:1, :], w_ref[1:2, :], w_ref[2:3, :]
    ridx = lax.broadcasted_iota(I32, (rows, z_ref.shape[1]), 0)
    prev_last = jnp.zeros((1, z_ref.shape[1]), F32)
    for j in range(n):
        r0 = j * rows
        zc = gi_ref[r0:r0 + rows, :] * z_ref[r0:r0 + rows, :]
        if j + 1 < n:
            nxt = gi_ref[r0 + rows:r0 + rows + 1, :] * z_ref[r0 + rows:r0 + rows + 1, :]
        else:
            nxt = jnp.zeros_like(prev_last)
        zp = jnp.where(ridx == 0, prev_last, pltpu.roll(zc, 1, 0))
        zn = jnp.where(ridx == rows - 1, nxt, pltpu.roll(zc, rows - 1, 0))
        y = zp * w0 + zc * w1 + zn * w2
        o_ref[r0:r0 + rows, :] = (go_ref[r0:r0 + rows, :] * y).astype(o_ref.dtype)
        prev_last = zc[rows - 1:rows, :]


def _short_conv(proj, conv_w):
    b, s, _ = proj.shape
    dc = conv_w.shape[1]
    nb = dc // LANES
    base = dc // LANES
    rows = min(s, 512)
    kern = functools.partial(_conv_kernel, rows=rows)
    return pl.pallas_call(
        kern,
        grid=(b, nb),
        in_specs=[pl.BlockSpec((None, s, LANES), lambda bi, c: (bi, 0, base + c)),
                  pl.BlockSpec((None, s, LANES), lambda bi, c: (bi, 0, 2 * base + c)),
                  pl.BlockSpec((None, s, LANES), lambda bi, c: (bi, 0, 3 * base + c)),
                  pl.BlockSpec((conv_w.shape[0], LANES), lambda bi, c: (0, c))],
        out_specs=pl.BlockSpec((None, s, LANES), lambda bi, c: (bi, 0, c)),
        out_shape=jax.ShapeDtypeStruct((b, s, dc), BF16),
        compiler_params=_params("parallel", "parallel"),
        name="short_conv",
    )(proj, proj, proj, conv_w)


def _attn_kernel(sink_ref, q_ref, *refs, band, n_blocks):
    o_ref = refs[-1]
    i = pl.program_id(1)
    t = q_ref.shape[0]
    w = Q_PER_KV * HEAD_DIM
    lane = lax.broadcasted_iota(I32, (t, w), 1)
    masks = [(lane >= g * HEAD_DIM) & (lane < (g + 1) * HEAD_DIM) for g in range(Q_PER_KV)]
    bad = None
    if band:
        n_keys = 3 * t + refs[6].shape[1]
        row = lax.broadcasted_iota(I32, (Q_PER_KV * t, n_keys), 0) & (t - 1)
        col = lax.broadcasted_iota(I32, (Q_PER_KV * t, n_keys), 1)
        off_prev = jnp.where(i > 0, 0, 4 * t)
        off_next = jnp.where(i < n_blocks - 1, 0, 4 * t)
        bad_prev = (col < t) & (col < row + off_prev)
        bad_next = (col >= 2 * t) & (col < 3 * t) & (col - 2 * t > row - off_next)
        bad = bad_prev | bad_next
    for h in range(N_KV):
        if band:
            kp, kc_, kn, vp, vc_, vn, kx, vx = refs[:-1]
            kcat = jnp.concatenate([kp[h], kc_[h], kn[h], kx[h]], axis=0)
            vcat = jnp.concatenate([vp[h], vc_[h], vn[h], vx[h]], axis=0)
        else:
            kx, vx = refs[:-1]
            kcat, vcat = kx[h], vx[h]
        q = q_ref[:, h * w:(h + 1) * w]
        q4 = jnp.concatenate([jnp.where(m, q, jnp.zeros_like(q)) for m in masks], axis=0)
        s = lax.dot_general(q4, kcat, (((1,), (1,)), ((), ())), preferred_element_type=F32)
        if band:
            s = jnp.where(bad, -jnp.inf, s)
        sink = jnp.concatenate(
            [jnp.full((t, 1), sink_ref[h * Q_PER_KV + g], F32) for g in range(Q_PER_KV)], axis=0)
        m = jnp.maximum(jnp.max(s, axis=-1, keepdims=True), sink)
        e = jnp.exp(s - m)
        den = jnp.sum(e, axis=-1, keepdims=True) + jnp.exp(sink - m)
        r = _dot(e.astype(BF16), vcat) * (1.0 / den)
        o = jnp.zeros((t, w), F32)
        for g in range(Q_PER_KV):
            o = o + jnp.where(masks[g], r[g * t:(g + 1) * t, :], 0.0)
        o_ref[:, h * w:(h + 1) * w] = o.astype(o_ref.dtype)


def _attention(q, k4, v4, kx4, vx4, sink, band):
    b, s, qw = q.shape
    rep = k4.shape[-1] if band else kx4.shape[-1]
    lc = kx4.shape[2]
    t = ATT_BLOCK
    nb = s // t
    kern = functools.partial(_attn_kernel, band=band, n_blocks=nb)

    def kv_spec(off):
        return pl.BlockSpec((None, N_KV, t, rep),
                            lambda bi, i: (bi, 0, jnp.clip(i + off, 0, nb - 1), 0))

    ctx_spec = pl.BlockSpec((None, N_KV, lc, rep), lambda bi, i: (bi, 0, 0, 0))
    in_specs = [pl.BlockSpec(memory_space=pltpu.SMEM),
                pl.BlockSpec((None, t, qw), lambda bi, i: (bi, i, 0))]
    args = [sink, q]
    if band:
        in_specs += [kv_spec(-1), kv_spec(0), kv_spec(1), kv_spec(-1), kv_spec(0), kv_spec(1)]
        args += [k4, k4, k4, v4, v4, v4]
    in_specs += [ctx_spec, ctx_spec]
    args += [kx4, vx4]
    return pl.pallas_call(
        kern,
        grid=(b, nb),
        in_specs=in_specs,
        out_specs=pl.BlockSpec((None, t, qw), lambda bi, i: (bi, i, 0)),
        out_shape=jax.ShapeDtypeStruct((b, s, qw), BF16),
        compiler_params=_params("parallel", "parallel"),
        name="attention_band" if band else "attention_ctx",
    )(*args)


def _route_kernel(x_ref, g_ref, sh_ref, sc_ref, whi_ref, wlo_ref, rb_ref, cin_ref, tri_ref, upper_ref,
                  h_ref, slab_ref, tab_ref, cnt_ref, carry_ref):
    first = (pl.program_id(0) == 0) & (pl.program_id(1) == 0)

    @pl.when(first)
    def _():
        carry_ref[...] = cin_ref[...]

    h = _norm_mod(x_ref[...], g_ref[...], sh_ref[...], sc_ref[...])
    _pack_rows(h_ref, h)
    hi = h.astype(BF16)
    lo = (h - hi.astype(F32)).astype(BF16)
    logits = _dot(hi, whi_ref[...]) + _dot(lo, whi_ref[...]) + _dot(hi, wlo_ref[...])
    scores = _sigmoid(logits)
    tm, lanes = scores.shape
    lane = lax.broadcasted_iota(I32, (tm, lanes), 1).astype(F32)
    work = jnp.where(lane < N_EXPERTS, scores + rb_ref[...], -jnp.inf)
    hits, idxs, gates = [], [], []
    for _ in range(TOP_K):
        mx = jnp.max(work, axis=-1, keepdims=True)
        idx = jnp.min(jnp.where(work == mx, lane, float(lanes)), axis=-1, keepdims=True)
        hit = lane == idx
        hits.append(hit)
        idxs.append(idx)
        gates.append(jnp.sum(jnp.where(hit, scores, 0.0), axis=-1, keepdims=True))
        work = jnp.where(hit, -jnp.inf, work)
    gsum = gates[0]
    for gk in gates[1:]:
        gsum = gsum + gk
    gscale = ROUTED_SCALE / (gsum + 1e-20)
    onehot = jnp.zeros((tm, lanes), F32)
    for hit in hits:
        onehot = jnp.where(hit, 1.0, onehot)
    cnt = jnp.sum(onehot, axis=0, keepdims=True)
    seg = _dot(jnp.broadcast_to(cnt, (8, lanes)).astype(BF16), upper_ref[...])[0:1]
    before = _dot(tri_ref[...], onehot.astype(BF16)) + seg
    slab = jnp.zeros((tm, lanes), F32)
    for k in range(TOP_K):
        loc = jnp.sum(jnp.where(hits[k], before, 0.0), axis=-1, keepdims=True)
        slab = jnp.where(lane == SLAB_IDX + k, idxs[k], slab)
        slab = jnp.where(lane == SLAB_LOC + k, loc, slab)
        slab = jnp.where(lane == SLAB_GATE + k, gates[k] * gscale, slab)
    slab_ref[...] = slab
    row = lax.broadcasted_iota(I32, (8, lanes), 0)
    tab_ref[...] = jnp.where(row == 0, cnt, jnp.where(row == 1, seg, jnp.where(row == 2, carry_ref[...], 0.0)))
    carry_ref[...] = carry_ref[...] + cnt
    cnt_ref[...] = carry_ref[...]


def _route(x, g, shift, scale, whi, wlo, rb, counts_in, tri, upper):
    b, s, d = x.shape
    tm = MOE_TM
    nt = s // tm
    return pl.pallas_call(
        _route_kernel,
        grid=(b, nt),
        in_specs=[pl.BlockSpec((None, tm, d), lambda bi, i: (bi, i, 0)),
                  pl.BlockSpec((1, d), lambda bi, i: (0, 0)),
                  pl.BlockSpec((None, 1, d), lambda bi, i: (bi, 0, 0)),
                  pl.BlockSpec((None, 1, d), lambda bi, i: (bi, 0, 0)),
                  pl.BlockSpec(whi.shape, lambda bi, i: (0, 0)),
                  pl.BlockSpec(wlo.shape, lambda bi, i: (0, 0)),
                  pl.BlockSpec((1, LANES), lambda bi, i: (0, 0)),
                  pl.BlockSpec((1, LANES), lambda bi, i: (0, 0)),
                  pl.BlockSpec((tm, tm), lambda bi, i: (0, 0)),
                  pl.BlockSpec((LANES, LANES), lambda bi, i: (0, 0))],
        out_specs=[pl.BlockSpec((None, tm, ROW_PARTS, LANES), lambda bi, i: (bi, i, 0, 0)),
                   pl.BlockSpec((None, tm, LANES), lambda bi, i: (bi, i, 0)),
                   pl.BlockSpec((None, 8, LANES), lambda bi, i: (bi * nt + i, 0, 0)),
                   pl.BlockSpec((1, LANES), lambda bi, i: (0, 0))],
        out_shape=[jax.ShapeDtypeStruct((b, s, ROW_PARTS, LANES), U32),
                   jax.ShapeDtypeStruct((b, s, LANES), F32),
                   jax.ShapeDtypeStruct((b * nt, 8, LANES), F32),
                   jax.ShapeDtypeStruct((1, LANES), F32)],
        scratch_shapes=[pltpu.VMEM((1, LANES), F32)],
        compiler_params=_params("arbitrary", "arbitrary"),
        name="route",
    )(x, g, shift, scale, whi, wlo, rb, counts_in, tri, upper)


def _rows(ref, row, n):
    return ref.at[pl.ds(row, n)]


def _run_copies(tab_smem, stage_ref, far_ref, sem, to_far):
    def per_expert(e, carry):
        n = tab_smem[e]
        near = tab_smem[N_EXPERTS + e]
        far = tab_smem[2 * N_EXPERTS + e]
        for bit in range(MOE_TM.bit_length()):
            size = 1 << bit

            @pl.when((n & size) != 0)
            def _():
                done = n & (size - 1)
                a, b = _rows(stage_ref, near + done, size), _rows(far_ref, far + done, size)
                (pltpu.make_async_copy(a, b, sem) if to_far else pltpu.make_async_copy(b, a, sem)).start()
        return carry

    lax.fori_loop(0, N_EXPERTS, per_expert, 0)


def _wait_tile(stage_ref, far_ref, sem, to_far):
    n = TOP_K * MOE_TM
    a, b = _rows(stage_ref, 0, n), _rows(far_ref, 0, n)
    (pltpu.make_async_copy(a, b, sem) if to_far else pltpu.make_async_copy(b, a, sem)).wait()


def _dispatch_kernel(ends_ref, nu_ref, h_ref, loc_ref, tab_ref, *rest, zero_fill, tile_base, n_blocks):
    if zero_fill:
        xs_ref, loc_smem, tab_smem, stage_ref, zero_ref, sem, psem = rest
    else:
        _, xs_ref, loc_smem, tab_smem, stage_ref, sem, psem = rest
        zero_ref = None
    tm = MOE_TM
    i = pl.program_id(0)
    loc_copy = pltpu.make_async_copy(loc_ref.at[pl.ds((tile_base + i) * tm * 8, tm * 8)], loc_smem, psem)
    tab_copy = pltpu.make_async_copy(tab_ref.at[pl.ds((tile_base + i) * TAB_WORDS, TAB_WORDS)], tab_smem, psem)
    loc_copy.start()
    tab_copy.start()

    if zero_fill:
        @pl.when(i == 0)
        def _():
            zero_ref[...] = jnp.zeros_like(zero_ref)

            def block_copy(blk):
                return pltpu.make_async_copy(zero_ref, _rows(xs_ref, blk * EXPERT_BLOCK, EXPERT_BLOCK), sem)

            def fill(e, carry, *, start):
                end = ends_ref[e]
                prev = jnp.where(e > 0, ends_ref[jnp.maximum(e - 1, 0)], 0)

                @pl.when(end > prev)
                def _():
                    cp = block_copy(end // EXPERT_BLOCK - 1)
                    cp.start() if start else cp.wait()
                return carry

            def tail(j, carry, *, start):
                cp = block_copy(j)
                cp.start() if start else cp.wait()
                return carry

            lax.fori_loop(0, N_EXPERTS, functools.partial(fill, start=True), 0)
            lax.fori_loop(nu_ref[0], n_blocks, functools.partial(tail, start=True), 0)
            lax.fori_loop(0, N_EXPERTS, functools.partial(fill, start=False), 0)
            lax.fori_loop(nu_ref[0], n_blocks, functools.partial(tail, start=False), 0)

    loc_copy.wait()
    tab_copy.wait()

    def place(t, c):
        row = h_ref[t]
        for k in range(TOP_K):
            stage_ref[loc_smem[t * 8 + k]] = row
        return c

    lax.fori_loop(0, tm, place, 0, unroll=4)
    _run_copies(tab_smem, stage_ref, xs_ref, sem, to_far=True)
    _wait_tile(stage_ref, xs_ref, sem, to_far=True)


def _dispatch(h2, loc_flat, tab_flat, ends, n_used, xs_prev, n_rows, tile_base):
    n = h2.shape[0]
    tm = MOE_TM
    zero_fill = xs_prev is None
    kern = functools.partial(_dispatch_kernel, zero_fill=zero_fill, tile_base=tile_base,
                             n_blocks=n_rows // EXPERT_BLOCK)
    in_specs = [pl.BlockSpec((tm, ROW_PARTS, LANES), lambda i, e, nu: (i, 0, 0)),
                pl.BlockSpec(memory_space=pl.ANY),
                pl.BlockSpec(memory_space=pl.ANY)]
    args = [ends, n_used, h2, loc_flat, tab_flat]
    scratch = [pltpu.SMEM((tm * 8,), I32), pltpu.SMEM((TAB_WORDS,), I32),
               pltpu.VMEM((TOP_K * tm, ROW_PARTS, LANES), U32)]
    aliases = {}
    if zero_fill:
        scratch.append(pltpu.VMEM((EXPERT_BLOCK, ROW_PARTS, LANES), U32))
    else:
        in_specs.append(pl.BlockSpec(memory_space=pl.ANY))
        args.append(xs_prev)
        aliases = {5: 0}
    scratch += [pltpu.SemaphoreType.DMA, pltpu.SemaphoreType.DMA]
    return pl.pallas_call(
        kern,
        grid_spec=pltpu.PrefetchScalarGridSpec(
            num_scalar_prefetch=2,
            grid=(n // tm,),
            in_specs=in_specs,
            out_specs=pl.BlockSpec(memory_space=pl.ANY),
            scratch_shapes=scratch),
        out_shape=jax.ShapeDtypeStruct((n_rows, ROW_PARTS, LANES), U32),
        input_output_aliases=aliases,
        compiler_params=_params("arbitrary"),
        name="dispatch",
    )(*args)


def _expert_kernel(be_ref, nu_ref, x_ref, w1_ref, w3_ref, w2_ref, o_ref):
    used = pl.program_id(0) < nu_ref[0]

    @pl.when(used)
    def _():
        x = _unpack_rows(x_ref).astype(BF16)
        a = _silu(_dot(x, w1_ref[...])) * _dot(x, w3_ref[...])
        _pack_rows(o_ref, _dot(a.astype(BF16), w2_ref[...]))

    @pl.when(jnp.logical_not(used))
    def _():
        o_ref[...] = jnp.zeros_like(o_ref)


def _experts(xs, block_e, n_used, w1, w3, w2):
    d, de = w1.shape[1:]
    blk = EXPERT_BLOCK
    nblk = xs.shape[0] // blk

    def row_map(i, be, nu):
        return (jnp.minimum(i, nu[0] - 1), 0, 0)

    return pl.pallas_call(
        _expert_kernel,
        grid_spec=pltpu.PrefetchScalarGridSpec(
            num_scalar_prefetch=2,
            grid=(nblk,),
            in_specs=[pl.BlockSpec((blk, ROW_PARTS, LANES), row_map),
                      pl.BlockSpec((None, d, de), lambda i, be, nu: (be[i], 0, 0)),
                      pl.BlockSpec((None, d, de), lambda i, be, nu: (be[i], 0, 0)),
                      pl.BlockSpec((None, de, d), lambda i, be, nu: (be[i], 0, 0))],
            out_specs=pl.BlockSpec((blk, ROW_PARTS, LANES), lambda i, be, nu: (i, 0, 0))),
        out_shape=jax.ShapeDtypeStruct(xs.shape, U32),
        compiler_params=_params("arbitrary"),
        name="experts",
    )(block_e, n_used, xs, w1, w3, w2)


def _combine_kernel(loc_ref, tab_ref, gates_ref, ys_ref, h_ref, x_ref, gate_ref, s1_ref, s3_ref, s2_ref,
                    o_ref, loc_smem, tab_smem, gate_smem, stage_ref, lo_ref, hi_ref, sem, psem, *, tile_base):
    tm = MOE_TM
    tile = tile_base + pl.program_id(0) * pl.num_programs(1) + pl.program_id(1)
    small = [pltpu.make_async_copy(loc_ref.at[pl.ds(tile * tm * 8, tm * 8)], loc_smem, psem),
             pltpu.make_async_copy(gates_ref.at[pl.ds(tile * tm * 8, tm * 8)], gate_smem, psem),
             pltpu.make_async_copy(tab_ref.at[pl.ds(tile * TAB_WORDS, TAB_WORDS)], tab_smem, psem)]
    for cp in small:
        cp.start()
    for cp in small:
        cp.wait()
    _run_copies(tab_smem, stage_ref, ys_ref, sem, to_far=False)
    hb = _unpack_rows(h_ref).astype(BF16)
    shared = _dot((_silu(_dot(hb, s1_ref[...])) * _dot(hb, s3_ref[...])).astype(BF16), s2_ref[...])
    _wait_tile(stage_ref, ys_ref, sem, to_far=False)

    def mix(t, c):
        lo = hi = None
        for k in range(TOP_K):
            g = gate_smem[t * 8 + k]
            wl, wh = _unpack_words(stage_ref[loc_smem[t * 8 + k]])
            lo = g * wl if lo is None else lo + g * wl
            hi = g * wh if hi is None else hi + g * wh
        lo_ref[t] = lo
        hi_ref[t] = hi
        return c

    lax.fori_loop(0, tm, mix, 0, unroll=4)
    parts = range(lo_ref.shape[1])
    routed = jnp.concatenate([lo_ref[:, c, :] for c in parts] + [hi_ref[:, c, :] for c in parts], axis=-1)
    o_ref[...] = x_ref[...] + gate_ref[...] * (routed + shared)


def _combine(ys, loc_flat, tab_flat, gates_flat, h2, x, gate, s1, s3, s2, tile_base):
    b, s, d = x.shape
    tm = MOE_TM
    kern = functools.partial(_combine_kernel, tile_base=tile_base)
    return pl.pallas_call(
        kern,
        grid=(b, s // tm),
        in_specs=[pl.BlockSpec(memory_space=pl.ANY),
                  pl.BlockSpec(memory_space=pl.ANY),
                  pl.BlockSpec(memory_space=pl.ANY),
                  pl.BlockSpec(memory_space=pl.ANY),
                  pl.BlockSpec((None, tm, ROW_PARTS, LANES), lambda bi, i: (bi, i, 0, 0)),
                  pl.BlockSpec((None, tm, d), lambda bi, i: (bi, i, 0)),
                  pl.BlockSpec((None, 1, d), lambda bi, i: (bi, 0, 0)),
                  pl.BlockSpec(s1.shape, lambda bi, i: (0, 0)),
                  pl.BlockSpec(s3.shape, lambda bi, i: (0, 0)),
                  pl.BlockSpec(s2.shape, lambda bi, i: (0, 0))],
        out_specs=pl.BlockSpec((None, tm, d), lambda bi, i: (bi, i, 0)),
        out_shape=jax.ShapeDtypeStruct((b, s, d), F32),
        scratch_shapes=[pltpu.SMEM((tm * 8,), I32),
                        pltpu.SMEM((TAB_WORDS,), I32),
                        pltpu.SMEM((tm * 8,), F32),
                        pltpu.VMEM((TOP_K * tm, ROW_PARTS, LANES), U32),
                        pltpu.VMEM((tm, ROW_PARTS, LANES), F32),
                        pltpu.VMEM((tm, ROW_PARTS, LANES), F32),
                        pltpu.SemaphoreType.DMA,
                        pltpu.SemaphoreType.DMA],
        compiler_params=_params("arbitrary", "arbitrary"),
        name="combine",
    )(loc_flat, tab_flat, gates_flat, ys, h2, x, gate, s1, s3, s2)


def _dft_tables(length, n_chan):
    scale = 1.0 / math.sqrt(length * n_chan)
    side = 1
    while side * side < length:
        side *= 2
    outer = length // side
    k = jnp.arange(length, dtype=I32)[:, None]
    a_idx = (k * jnp.arange(outer, dtype=I32)[None, :]) % outer
    b_idx = (k * jnp.arange(side, dtype=I32)[None, :]) % length
    ang_a = a_idx.astype(F32) * (2.0 * math.pi / outer)
    ang_b = b_idx.astype(F32) * (2.0 * math.pi / length)
    ca, sa = jnp.cos(ang_a)[:, :, None], jnp.sin(ang_a)[:, :, None]
    cb, sb = jnp.cos(ang_b)[:, None, :], jnp.sin(ang_b)[:, None, :]
    cos_t = (ca * cb - sa * sb).reshape(length, length)
    sin_t = (sa * cb + ca * sb).reshape(length, length)
    table = (jnp.concatenate([cos_t, -sin_t], axis=1) * scale).astype(BF16)
    return table


def _channel_table(n_chan, n_groups):
    m = jnp.arange(n_chan, dtype=I32)
    ang = ((m[:, None] * m[None, :]) % n_chan).astype(F32) * (2.0 * math.pi / n_chan)
    eye = jnp.eye(n_groups, dtype=F32)
    return jnp.concatenate([jnp.kron(eye, jnp.cos(ang)), jnp.kron(eye, jnp.sin(ang))], axis=1).astype(BF16)


def _rope_tables(n_tok):
    rows = n_tok // GRID_W
    axis_dim = HEAD_DIM // 2
    r = jnp.repeat(jnp.arange(rows, dtype=F32), GRID_W)
    col = jnp.tile(jnp.arange(GRID_W, dtype=F32), rows)
    inv = ROPE_BASE ** (-jnp.arange(0, axis_dim, 2, dtype=F32) / axis_dim)
    ar, ac = r[:, None] * inv, col[:, None] * inv
    cos = jnp.concatenate([jnp.cos(ar), jnp.cos(ar), jnp.cos(ac), jnp.cos(ac)], axis=1)
    sin = jnp.concatenate([-jnp.sin(ar), jnp.sin(ar), -jnp.sin(ac), jnp.sin(ac)], axis=1)
    return jnp.tile(cos, (1, N_HEADS)), jnp.tile(sin, (1, N_HEADS))


def _head_mean_matrix(width):
    h = jnp.arange(width) // HEAD_DIM
    return ((h[:, None] == h[None, :]).astype(F32) / HEAD_DIM).astype(BF16)


def _tile_matrices():
    src = jnp.arange(N_KV * HEAD_DIM)
    dst = jnp.arange(Q_PER_KV * HEAD_DIM)
    mats = [((src[:, None] // HEAD_DIM == h) & (src[:, None] % HEAD_DIM == dst[None, :] % HEAD_DIM))
            for h in range(N_KV)]
    return jnp.stack(mats).astype(BF16)


def _even_layer(x, xc, mod, modc, norm_g, w_in, ws, bs, w_out, tables):
    b, s, d = x.shape
    w = w_in.shape[1] // 3
    outs = []
    for stream, m, tm in ((x, mod, 512), (xc, modc, 256)):
        if stream is None:
            outs.append(None)
            continue
        length = stream.shape[1]
        tm = min(tm, length)
        uvf = _norm_mod_matmul(stream, norm_g, m[0], m[1], w_in, tm)
        gm, z = _even_mix(uvf, ws, bs, tables["chan"], tm)
        table = tables["pos"][length]
        y = _matmul(table, z.reshape(2 * length, b * w),
                    min(1024, length), min(1024, b * w), min(1024, 2 * length), BF16)
        outs.append(_outproj(
            gm, pl.BlockSpec((None, tm, w), lambda bi, i: (bi, i, 0)),
            y, pl.BlockSpec((tm, w), lambda bi, i: (i, bi)),
            w_out, stream, m[2], tm))
    return outs


def _odd_layer(x, xc, mod, modc, norm_g, w_in, qg, kg, sink, conv_w, w_out, tables, ctx_out):
    b, s, d = x.shape
    lc = xc.shape[1]
    half = N_HEADS * HEAD_DIM
    tm, tmc = 512, min(256, lc)
    proj = _norm_mod_matmul(x, norm_g, mod[0], mod[1], w_in, tm)
    projc = _norm_mod_matmul(xc, norm_g, modc[0], modc[1], w_in, tmc)
    prep = functools.partial(_qkv_prep, qg=qg, kg=kg, bdq=tables["bdq"], bdk=tables["bdk"], tile=tables["tile"])
    q, k4, v4 = prep(proj, tables["cos"], tables["sin"], tm=tm, with_q=True)
    qc, kc4, vc4 = prep(projc, tables["cos_c"], tables["sin_c"], tm=tmc, with_q=ctx_out)
    att = _attention(q, k4, v4, kc4, vc4, sink, band=True)
    conv = _short_conv(proj, conv_w)
    spec = lambda t: pl.BlockSpec((None, t, half), lambda bi, i: (bi, i, 0))
    y = _outproj(att, spec(tm), conv, spec(tm), w_out, x, mod[2], tm)
    yc = None
    if ctx_out:
        attc = _attention(qc, None, None, kc4, vc4, sink, band=False)
        convc = _short_conv(projc, conv_w)
        yc = _outproj(attc, spec(tmc), convc, spec(tmc), w_out, xc, modc[2], tmc)
    return y, yc


def _moe(x, xc, mod, modc, norm_g, rw_hi, rw_lo, rb, w1, w3, w2, s1, s3, s2, tri, upper):
    b, s, d = x.shape
    n_lat = b * s
    counts0 = jnp.zeros((1, LANES), F32)
    h2, slab, tab, counts = _route(x, norm_g, mod[3], mod[4], rw_hi, rw_lo, rb, counts0, tri, upper)
    slabs, tabs = [slab.reshape(n_lat, LANES)], [tab]
    n_tok = n_lat
    if xc is not None:
        h2c, slabc, tabc, counts = _route(xc, norm_g, modc[3], modc[4], rw_hi, rw_lo, rb, counts, tri, upper)
        slabs.append(slabc.reshape(-1, LANES))
        tabs.append(tabc)
        n_tok += slabs[1].shape[0]
    cnt = counts[0, :N_EXPERTS].astype(I32)
    blk = EXPERT_BLOCK
    padded = (cnt + blk - 1) // blk * blk
    ends = jnp.cumsum(padded).astype(I32)
    starts = ends - padded
    n_rows = (n_tok * TOP_K + N_EXPERTS * (blk - 1) + blk - 1) // blk * blk
    n_blocks = n_rows // blk
    n_used = (ends[-1] // blk).reshape(1).astype(I32)
    blk_start = jnp.minimum(jnp.arange(n_blocks, dtype=I32), n_used[0] - 1) * blk
    block_e = jnp.minimum(jnp.sum(blk_start[:, None] >= ends[None, :], axis=1), N_EXPERTS - 1).astype(I32)
    slab_all = jnp.concatenate(slabs, axis=0)
    tab_all = jnp.concatenate(tabs, axis=0)[:, :, :N_EXPERTS].astype(I32)
    runs = jnp.concatenate([tab_all[:, 0], tab_all[:, 1], tab_all[:, 2] + starts[None, :]], axis=1)
    tab_flat = jnp.pad(runs, ((0, 0), (0, TAB_WORDS - runs.shape[1]))).reshape(-1)
    loc_flat = slab_all[:, SLAB_LOC:SLAB_LOC + 8].astype(I32).reshape(-1)
    gates_flat = slab_all[:, SLAB_GATE:SLAB_GATE + 8].reshape(-1)
    packed = lambda a: a.reshape(-1, ROW_PARTS, LANES)
    xs = _dispatch(packed(h2), loc_flat, tab_flat, ends, n_used, None, n_rows, 0)
    if xc is not None:
        xs = _dispatch(packed(h2c), loc_flat, tab_flat, ends, n_used, xs, n_rows, n_lat // MOE_TM)
    ys = _experts(xs, block_e, n_used, w1, w3, w2)
    x_new = _combine(ys, loc_flat, tab_flat, gates_flat, h2, x, mod[5], s1, s3, s2, 0)
    xc_new = None
    if xc is not None:
        xc_new = _combine(ys, loc_flat, tab_flat, gates_flat, h2c, xc, modc[5], s1, s3, s2, n_lat // MOE_TM)
    return x_new, xc_new


def kernel(x, c, ctx, c_ctx, ada_w, ada_b, norm1_g, norm2_g, ev_w_in, ev_w_s, ev_b_s, ev_w_out, od_w_in, od_q_norm_g, od_k_norm_g, od_sink, od_conv_w, od_w_out, router_w, router_b, exp_w_gate, exp_w_up, exp_w_down, sh_w_gate, sh_w_up, sh_w_down):
    b, s, d = x.shape
    lc = ctx.shape[1]
    depth = ada_w.shape[0]
    n_groups = ev_w_s.shape[1]
    half = d // 2

    rows = -(-(b + 1) // 8) * 8
    cond = jnp.zeros((rows, d), F32).at[:b].set(c).at[b].set(c_ctx)
    mod_all = _adaln(cond, ada_w, ada_b)

    tables = {
        "chan": _channel_table(LANES, n_groups),
        "pos": {s: _dft_tables(s, LANES), lc: _dft_tables(lc, LANES)},
        "bdq": _head_mean_matrix(N_HEADS * HEAD_DIM),
        "bdk": _head_mean_matrix(N_KV * HEAD_DIM),
        "tile": _tile_matrices(),
    }
    tables["cos"], tables["sin"] = _rope_tables(s)
    tables["cos_c"] = jnp.ones((lc, N_HEADS * HEAD_DIM), F32)
    tables["sin_c"] = jnp.zeros((lc, N_HEADS * HEAD_DIM), F32)
    tri = (jnp.arange(MOE_TM)[:, None] > jnp.arange(MOE_TM)[None, :]).astype(BF16)
    upper = (jnp.arange(LANES)[:, None] < jnp.arange(LANES)[None, :]).astype(BF16)

    qw, kw = N_HEADS * HEAD_DIM, N_KV * HEAD_DIM
    perm = jnp.concatenate([jnp.arange(0, qw), jnp.arange(qw + 2 * kw, qw + 2 * kw + 3 * half),
                            jnp.arange(qw, qw + 2 * kw)])

    xc = ctx
    for l in range(depth):
        last = l == depth - 1
        even = l % 2 == 0
        need_ctx = not (last and even)
        pieces = [mod_all[l, :, j * d:(j + 1) * d] for j in range(6)]
        mod = [p[:b].reshape(b, 1, d) for p in pieces]
        modc = [jnp.broadcast_to(p[b].reshape(1, 1, d), (b, 1, d)) for p in pieces]
        g1 = norm1_g[l].reshape(1, d)
        g2 = norm2_g[l].reshape(1, d)
        if even:
            e = l // 2
            bs = jnp.broadcast_to(ev_b_s[e][:, :, None], (n_groups, CHUNK, LANES))
            y, yc = _even_layer(x, xc if (need_ctx and not last) else None, mod, modc, g1,
                                ev_w_in[e].astype(BF16), ev_w_s[e].astype(BF16), bs,
                                ev_w_out[e].astype(BF16), tables)
        else:
            o = l // 2
            qg = (jnp.tile(od_q_norm_g[o], N_HEADS) * (HEAD_DIM ** -0.5)).reshape(1, qw)
            kg = jnp.tile(od_k_norm_g[o], N_KV).reshape(1, kw)
            y, yc = _odd_layer(x, xc, mod, modc, g1, od_w_in[o][:, perm].astype(BF16), qg, kg,
                               od_sink[o], od_conv_w[o], od_w_out[o].astype(BF16), tables, not last)
        x = y
        if not last:
            xc = yc
        rw = jnp.zeros((d, LANES), F32).at[:, :N_EXPERTS].set(router_w[l])
        rw_hi = rw.astype(BF16)
        rw_lo = (rw - rw_hi.astype(F32)).astype(BF16)
        rb = jnp.zeros((1, LANES), F32).at[0, :N_EXPERTS].set(router_b[l])
        x, xc_new = _moe(x, None if last else xc, mod, modc, g2, rw_hi, rw_lo, rb,
                         exp_w_gate[l].astype(BF16), exp_w_up[l].astype(BF16), exp_w_down[l].astype(BF16),
                         sh_w_gate[l].astype(BF16), sh_w_up[l].astype(BF16), sh_w_down[l].astype(BF16),
                         tri, upper)
        if not last:
            xc = xc_new
    return x
```

```python
import functools
import math

import jax
import jax.numpy as jnp
from jax import lax
from jax.experimental import pallas as pl
from jax.experimental.pallas import tpu as pltpu

F32 = jnp.float32
BF16 = jnp.bfloat16
I32 = jnp.int32
U32 = jnp.uint32

LANES = 128
VMEM_LIMIT = 48 * 2**20

EPS = 1e-6
GRID_W = 64
CHUNK = 128
HEAD_DIM = 64
N_HEADS = 8
N_KV = 2
Q_PER_KV = N_HEADS // N_KV
ATT_BLOCK = 128
ROPE_BASE = 10000.0
N_EXPERTS = 64
TOP_K = 6
ROUTED_SCALE = 2.5
EXPERT_BLOCK = 512
MOE_TM = 512
ROW_PARTS = 4
SLAB_IDX, SLAB_LOC, SLAB_GATE = 0, 8, 16
TAB_WORDS = 1024


def _params(*sem):
    return pltpu.CompilerParams(dimension_semantics=sem, vmem_limit_bytes=VMEM_LIMIT)


def _sigmoid(x):
    return 1.0 / (1.0 + jnp.exp(-x))


def _silu(x):
    return x * _sigmoid(x)


def _gelu_tanh(x):
    c = math.sqrt(2.0 / math.pi)
    return x * (0.5 * (1.0 + jnp.tanh(c * (x + 0.044715 * (x * x * x)))))


def _dot(a, b):
    return jnp.dot(a, b, preferred_element_type=F32)


def _unpack_words(w):
    return pltpu.bitcast(w << 16, F32), pltpu.bitcast(w & jnp.uint32(0xFFFF0000), F32)


def _row_chunks(ref):
    n, parts, lanes = ref.shape
    flat = ref.reshape(n * parts, lanes)
    return [flat[pl.ds(c, n, stride=parts), :] for c in range(parts)]


def _unpack_rows(ref):
    halves = [_unpack_words(w) for w in _row_chunks(ref)]
    return jnp.concatenate([h[0] for h in halves] + [h[1] for h in halves], axis=-1)


def _pack_rows(ref, val):
    n, parts, lanes = ref.shape
    half = val.shape[1] // 2
    bits = pltpu.bitcast(val.astype(BF16).astype(F32), U32)
    words = (bits[:, :half] >> 16) | (bits[:, half:] & jnp.uint32(0xFFFF0000))
    flat = ref.reshape(n * parts, lanes)
    for c in range(parts):
        flat[pl.ds(c, n, stride=parts), :] = words[:, c * lanes:(c + 1) * lanes]


def _adaln_kernel(c_ref, w_ref, b_ref, o_ref):
    o_ref[...] = _dot(_silu(c_ref[...]), w_ref[...]) + b_ref[...]


def _adaln(cond, ada_w, ada_b):
    n_layers, d, n6 = ada_w.shape
    rows = cond.shape[0]
    tn = 768
    return pl.pallas_call(
        _adaln_kernel,
        grid=(n_layers, n6 // tn),
        in_specs=[pl.BlockSpec((rows, d), lambda l, j: (0, 0)),
                  pl.BlockSpec((None, d, tn), lambda l, j: (l, 0, j)),
                  pl.BlockSpec((None, 1, tn), lambda l, j: (l, 0, j))],
        out_specs=pl.BlockSpec((None, rows, tn), lambda l, j: (l, 0, j)),
        out_shape=jax.ShapeDtypeStruct((n_layers, rows, n6), F32),
        compiler_params=_params("parallel", "parallel"),
        name="adaln",
    )(cond, ada_w, ada_b.reshape(n_layers, 1, n6))


def _norm_mod(x, g, shift, scale):
    ms = jnp.mean(x * x, axis=-1, keepdims=True)
    h = (x * lax.rsqrt(ms + EPS)) * g
    return h * (1.0 + scale) + shift


def _nmm_kernel(x_ref, g_ref, sh_ref, sc_ref, w_ref, o_ref):
    h = _norm_mod(x_ref[...], g_ref[...], sh_ref[...], sc_ref[...])
    o_ref[...] = _dot(h.astype(BF16), w_ref[...]).astype(o_ref.dtype)


def _norm_mod_matmul(x, g, shift, scale, w, tm):
    b, s, d = x.shape
    n = w.shape[1]
    return pl.pallas_call(
        _nmm_kernel,
        grid=(b, s // tm),
        in_specs=[pl.BlockSpec((None, tm, d), lambda bi, i: (bi, i, 0)),
                  pl.BlockSpec((1, d), lambda bi, i: (0, 0)),
                  pl.BlockSpec((None, 1, d), lambda bi, i: (bi, 0, 0)),
                  pl.BlockSpec((None, 1, d), lambda bi, i: (bi, 0, 0)),
                  pl.BlockSpec((d, n), lambda bi, i: (0, 0))],
        out_specs=pl.BlockSpec((None, tm, n), lambda bi, i: (bi, i, 0)),
        out_shape=jax.ShapeDtypeStruct((b, s, n), F32),
        compiler_params=_params("parallel", "parallel"),
        name="norm_mod_matmul",
    )(x, g, shift, scale, w)


def _even_mix_kernel(u_ref, v_ref, f_ref, ws_ref, bs_ref, cs_ref, gm_ref, z_ref, *, n_chunks, n_groups):
    for c in range(n_chunks):
        rows = slice(c * CHUNK, (c + 1) * CHUNK)
        for g in range(n_groups):
            cols = slice(g * LANES, (g + 1) * LANES)
            ug = _gelu_tanh(u_ref[rows, cols])
            vg = _gelu_tanh(v_ref[rows, cols])
            mu = jnp.mean(vg, axis=-1, keepdims=True)
            dv = vg - mu
            var = jnp.mean(dv * dv, axis=-1, keepdims=True)
            vn = dv * lax.rsqrt(var + 1e-5)
            fg = _dot(ws_ref[g], vn.astype(BF16)) + bs_ref[g]
            gm_ref[rows, cols] = (ug * fg).astype(gm_ref.dtype)
    fz = _dot(f_ref[...].astype(BF16), cs_ref[...])
    half = fz.shape[1] // 2
    z_ref[0] = fz[:, :half].astype(z_ref.dtype)
    z_ref[1] = fz[:, half:].astype(z_ref.dtype)


def _even_mix(uvf, ws, bs, cs, tm):
    b, s, n3 = uvf.shape
    w = n3 // 3
    n_groups = w // LANES
    kern = functools.partial(_even_mix_kernel, n_chunks=tm // CHUNK, n_groups=n_groups)
    return pl.pallas_call(
        kern,
        grid=(b, s // tm),
        in_specs=[pl.BlockSpec((None, tm, w), lambda bi, i: (bi, i, 0)),
                  pl.BlockSpec((None, tm, w), lambda bi, i: (bi, i, 1)),
                  pl.BlockSpec((None, tm, w), lambda bi, i: (bi, i, 2)),
                  pl.BlockSpec(ws.shape, lambda bi, i: (0, 0, 0)),
                  pl.BlockSpec(bs.shape, lambda bi, i: (0, 0, 0)),
                  pl.BlockSpec(cs.shape, lambda bi, i: (0, 0))],
        out_specs=[pl.BlockSpec((None, tm, w), lambda bi, i: (bi, i, 0)),
                   pl.BlockSpec((2, tm, w), lambda bi, i: (0, i, bi))],
        out_shape=[jax.ShapeDtypeStruct((b, s, w), BF16),
                   jax.ShapeDtypeStruct((2, s, b * w), BF16)],
        compiler_params=_params("parallel", "parallel"),
        name="even_mix",
    )(uvf, uvf, uvf, ws, bs, cs)


def _mm_kernel(a_ref, b_ref, o_ref, acc_ref):
    k = pl.program_id(2)

    @pl.when(k == 0)
    def _():
        acc_ref[...] = jnp.zeros_like(acc_ref)

    acc_ref[...] += _dot(a_ref[...], b_ref[...])

    @pl.when(k == pl.num_programs(2) - 1)
    def _():
        o_ref[...] = acc_ref[...].astype(o_ref.dtype)


def _matmul(a, b, tm, tn, tk, out_dtype):
    m, kd = a.shape
    n = b.shape[1]
    return pl.pallas_call(
        _mm_kernel,
        grid=(m // tm, n // tn, kd // tk),
        in_specs=[pl.BlockSpec((tm, tk), lambda i, j, k: (i, k)),
                  pl.BlockSpec((tk, tn), lambda i, j, k: (k, j))],
        out_specs=pl.BlockSpec((tm, tn), lambda i, j, k: (i, j)),
        out_shape=jax.ShapeDtypeStruct((m, n), out_dtype),
        scratch_shapes=[pltpu.VMEM((tm, tn), F32)],
        compiler_params=_params("parallel", "parallel", "arbitrary"),
        name="dft_matmul",
    )(a, b)


def _outproj_kernel(a_ref, b_ref, w_ref, x_ref, gate_ref, o_ref):
    ab = jnp.concatenate([a_ref[...], b_ref[...]], axis=-1)
    o_ref[...] = x_ref[...] + gate_ref[...] * _dot(ab, w_ref[...])


def _outproj(a, a_spec, bsrc, b_spec, w, x, gate, tm):
    b, s, d = x.shape
    return pl.pallas_call(
        _outproj_kernel,
        grid=(b, s // tm),
        in_specs=[a_spec, b_spec,
                  pl.BlockSpec(w.shape, lambda bi, i: (0, 0)),
                  pl.BlockSpec((None, tm, d), lambda bi, i: (bi, i, 0)),
                  pl.BlockSpec((None, 1, d), lambda bi, i: (bi, 0, 0))],
        out_specs=pl.BlockSpec((None, tm, d), lambda bi, i: (bi, i, 0)),
        out_shape=jax.ShapeDtypeStruct((b, s, d), F32),
        compiler_params=_params("parallel", "parallel"),
        name="outproj",
    )(a, bsrc, w, x, gate)


def _head_rms(x, bd_ref):
    xx = x * x
    hi = xx.astype(BF16)
    lo = (xx - hi.astype(F32)).astype(BF16)
    ms = _dot(hi, bd_ref[...]) + _dot(lo, bd_ref[...])
    return x * lax.rsqrt(ms + EPS)


def _rope(x, cos, sins):
    width = x.shape[1]
    lane = lax.broadcasted_iota(I32, x.shape, 1)
    first = (lane & 31) < 16
    swapped = jnp.where(first, pltpu.roll(x, width - 16, 1), pltpu.roll(x, 16, 1))
    return x * cos + swapped * sins


def _qkv_kernel(q_ref, k_ref, v_ref, cos_ref, sin_ref, qg_ref, kg_ref, bdq_ref, bdk_ref, tile_ref,
                qo_ref, k4_ref, v4_ref, *, with_q):
    kw = k_ref.shape[1]
    if with_q:
        qn = _head_rms(q_ref[...], bdq_ref) * qg_ref[...]
        qo_ref[...] = _rope(qn, cos_ref[...], sin_ref[...]).astype(qo_ref.dtype)
    else:
        qo_ref[...] = jnp.zeros_like(qo_ref)
    kn = _head_rms(k_ref[...], bdk_ref) * kg_ref[...]
    kr = _rope(kn, cos_ref[:, :kw], sin_ref[:, :kw]).astype(BF16)
    vb = v_ref[...].astype(BF16)
    for h in range(N_KV):
        k4_ref[h] = _dot(kr, tile_ref[h]).astype(k4_ref.dtype)
        v4_ref[h] = _dot(vb, tile_ref[h]).astype(v4_ref.dtype)


def _qkv_prep(proj, cos, sin, qg, kg, bdq, bdk, tile, tm, with_q):
    b, s, _ = proj.shape
    qw = N_HEADS * HEAD_DIM
    kw = N_KV * HEAD_DIM
    rep = Q_PER_KV * HEAD_DIM
    k_blk = (4 * qw) // kw
    kern = functools.partial(_qkv_kernel, with_q=with_q)
    return pl.pallas_call(
        kern,
        grid=(b, s // tm),
        in_specs=[pl.BlockSpec((None, tm, qw), lambda bi, i: (bi, i, 0)),
                  pl.BlockSpec((None, tm, kw), lambda bi, i: (bi, i, k_blk)),
                  pl.BlockSpec((None, tm, kw), lambda bi, i: (bi, i, k_blk + 1)),
                  pl.BlockSpec((tm, qw), lambda bi, i: (i, 0)),
                  pl.BlockSpec((tm, qw), lambda bi, i: (i, 0)),
                  pl.BlockSpec((1, qw), lambda bi, i: (0, 0)),
                  pl.BlockSpec((1, kw), lambda bi, i: (0, 0)),
                  pl.BlockSpec(bdq.shape, lambda bi, i: (0, 0)),
                  pl.BlockSpec(bdk.shape, lambda bi, i: (0, 0)),
                  pl.BlockSpec(tile.shape, lambda bi, i: (0, 0, 0))],
        out_specs=[pl.BlockSpec((None, tm, qw), lambda bi, i: (bi, i, 0)),
                   pl.BlockSpec((None, N_KV, tm, rep), lambda bi, i: (bi, 0, i, 0)),
                   pl.BlockSpec((None, N_KV, tm, rep), lambda bi, i: (bi, 0, i, 0))],
        out_shape=[jax.ShapeDtypeStruct((b, s, qw), BF16),
                   jax.ShapeDtypeStruct((b, N_KV, s, rep), BF16),
                   jax.ShapeDtypeStruct((b, N_KV, s, rep), BF16)],
        compiler_params=_params("parallel", "parallel"),
        name="qkv_prep",
    )(proj, proj, proj, cos, sin, qg, kg, bdq, bdk, tile)


def _conv_kernel(gi_ref, go_ref, z_ref, w_ref, o_ref, *, rows):
    s = z_ref.shape[0]
    n = s // rows
    w0, w1, w2 = w_ref[0:1, :], w_ref[1:2, :], w_ref[2:3, :]
    ridx = lax.broadcasted_iota(I32, (rows, z_ref.shape[1]), 0)
    prev_last = jnp.zeros((1, z_ref.shape[1]), F32)
    for j in range(n):
        r0 = j * rows
        zc = gi_ref[r0:r0 + rows, :] * z_ref[r0:r0 + rows, :]
        if j + 1 < n:
            nxt = gi_ref[r0 + rows:r0 + rows + 1, :] * z_ref[r0 + rows:r0 + rows + 1, :]
        else:
            nxt = jnp.zeros_like(prev_last)
        zp = jnp.where(ridx == 0, prev_last, pltpu.roll(zc, 1, 0))
        zn = jnp.where(ridx == rows - 1, nxt, pltpu.roll(zc, rows - 1, 0))
        y = zp * w0 + zc * w1 + zn * w2
        o_ref[r0:r0 + rows, :] = (go_ref[r0:r0 + rows, :] * y).astype(o_ref.dtype)
        prev_last = zc[rows - 1:rows, :]


def _short_conv(proj, conv_w):
    b, s, _ = proj.shape
    dc = conv_w.shape[1]
    nb = dc // LANES
    base = dc // LANES
    rows = min(s, 512)
    kern = functools.partial(_conv_kernel, rows=rows)
    return pl.pallas_call(
        kern,
        grid=(b, nb),
        in_specs=[pl.BlockSpec((None, s, LANES), lambda bi, c: (bi, 0, base + c)),
                  pl.BlockSpec((None, s, LANES), lambda bi, c: (bi, 0, 2 * base + c)),
                  pl.BlockSpec((None, s, LANES), lambda bi, c: (bi, 0, 3 * base + c)),
                  pl.BlockSpec((conv_w.shape[0], LANES), lambda bi, c: (0, c))],
        out_specs=pl.BlockSpec((None, s, LANES), lambda bi, c: (bi, 0, c)),
        out_shape=jax.ShapeDtypeStruct((b, s, dc), BF16),
        compiler_params=_params("parallel", "parallel"),
        name="short_conv",
    )(proj, proj, proj, conv_w)


def _attn_kernel(sink_ref, q_ref, *refs, band, n_blocks):
    o_ref = refs[-1]
    i = pl.program_id(1)
    t = q_ref.shape[0]
    w = Q_PER_KV * HEAD_DIM
    lane = lax.broadcasted_iota(I32, (t, w), 1)
    masks = [(lane >= g * HEAD_DIM) & (lane < (g + 1) * HEAD_DIM) for g in range(Q_PER_KV)]
    bad = None
    if band:
        n_keys = 3 * t + refs[6].shape[1]
        row = lax.broadcasted_iota(I32, (Q_PER_KV * t, n_keys), 0) & (t - 1)
        col = lax.broadcasted_iota(I32, (Q_PER_KV * t, n_keys), 1)
        off_prev = jnp.where(i > 0, 0, 4 * t)
        off_next = jnp.where(i < n_blocks - 1, 0, 4 * t)
        bad_prev = (col < t) & (col < row + off_prev)
        bad_next = (col >= 2 * t) & (col < 3 * t) & (col - 2 * t > row - off_next)
        bad = bad_prev | bad_next
    for h in range(N_KV):
        if band:
            kp, kc_, kn, vp, vc_, vn, kx, vx = refs[:-1]
            kcat = jnp.concatenate([kp[h], kc_[h], kn[h], kx[h]], axis=0)
            vcat = jnp.concatenate([vp[h], vc_[h], vn[h], vx[h]], axis=0)
        else:
            kx, vx = refs[:-1]
            kcat, vcat = kx[h], vx[h]
        q = q_ref[:, h * w:(h + 1) * w]
        q4 = jnp.concatenate([jnp.where(m, q, jnp.zeros_like(q)) for m in masks], axis=0)
        s = lax.dot_general(q4, kcat, (((1,), (1,)), ((), ())), preferred_element_type=F32)
        if band:
            s = jnp.where(bad, -jnp.inf, s)
        sink = jnp.concatenate(
            [jnp.full((t, 1), sink_ref[h * Q_PER_KV + g], F32) for g in range(Q_PER_KV)], axis=0)
        m = jnp.maximum(jnp.max(s, axis=-1, keepdims=True), sink)
        e = jnp.exp(s - m)
        den = jnp.sum(e, axis=-1, keepdims=True) + jnp.exp(sink - m)
        r = _dot(e.astype(BF16), vcat) * (1.0 / den)
        o = jnp.zeros((t, w), F32)
        for g in range(Q_PER_KV):
            o = o + jnp.where(masks[g], r[g * t:(g + 1) * t, :], 0.0)
        o_ref[:, h * w:(h + 1) * w] = o.astype(o_ref.dtype)


def _attention(q, k4, v4, kx4, vx4, sink, band):
    b, s, qw = q.shape
    rep = k4.shape[-1] if band else kx4.shape[-1]
    lc = kx4.shape[2]
    t = ATT_BLOCK
    nb = s // t
    kern = functools.partial(_attn_kernel, band=band, n_blocks=nb)

    def kv_spec(off):
        return pl.BlockSpec((None, N_KV, t, rep),
                            lambda bi, i: (bi, 0, jnp.clip(i + off, 0, nb - 1), 0))

    ctx_spec = pl.BlockSpec((None, N_KV, lc, rep), lambda bi, i: (bi, 0, 0, 0))
    in_specs = [pl.BlockSpec(memory_space=pltpu.SMEM),
                pl.BlockSpec((None, t, qw), lambda bi, i: (bi, i, 0))]
    args = [sink, q]
    if band:
        in_specs += [kv_spec(-1), kv_spec(0), kv_spec(1), kv_spec(-1), kv_spec(0), kv_spec(1)]
        args += [k4, k4, k4, v4, v4, v4]
    in_specs += [ctx_spec, ctx_spec]
    args += [kx4, vx4]
    return pl.pallas_call(
        kern,
        grid=(b, nb),
        in_specs=in_specs,
        out_specs=pl.BlockSpec((None, t, qw), lambda bi, i: (bi, i, 0)),
        out_shape=jax.ShapeDtypeStruct((b, s, qw), BF16),
        compiler_params=_params("parallel", "parallel"),
        name="attention_band" if band else "attention_ctx",
    )(*args)


def _route_kernel(x_ref, g_ref, sh_ref, sc_ref, whi_ref, wlo_ref, rb_ref, cin_ref, tri_ref, upper_ref,
                  h_ref, slab_ref, tab_ref, cnt_ref, carry_ref):
    first = (pl.program_id(0) == 0) & (pl.program_id(1) == 0)

    @pl.when(first)
    def _():
        carry_ref[...] = cin_ref[...]

    h = _norm_mod(x_ref[...], g_ref[...], sh_ref[...], sc_ref[...])
    _pack_rows(h_ref, h)
    hi = h.astype(BF16)
    lo = (h - hi.astype(F32)).astype(BF16)
    logits = _dot(hi, whi_ref[...]) + _dot(lo, whi_ref[...]) + _dot(hi, wlo_ref[...])
    scores = _sigmoid(logits)
    tm, lanes = scores.shape
    lane = lax.broadcasted_iota(I32, (tm, lanes), 1).astype(F32)
    work = jnp.where(lane < N_EXPERTS, scores + rb_ref[...], -jnp.inf)
    hits, idxs, gates = [], [], []
    for _ in range(TOP_K):
        mx = jnp.max(work, axis=-1, keepdims=True)
        idx = jnp.min(jnp.where(work == mx, lane, float(lanes)), axis=-1, keepdims=True)
        hit = lane == idx
        hits.append(hit)
        idxs.append(idx)
        gates.append(jnp.sum(jnp.where(hit, scores, 0.0), axis=-1, keepdims=True))
        work = jnp.where(hit, -jnp.inf, work)
    gsum = gates[0]
    for gk in gates[1:]:
        gsum = gsum + gk
    gscale = ROUTED_SCALE / (gsum + 1e-20)
    onehot = jnp.zeros((tm, lanes), F32)
    for hit in hits:
        onehot = jnp.where(hit, 1.0, onehot)
    cnt = jnp.sum(onehot, axis=0, keepdims=True)
    seg = _dot(jnp.broadcast_to(cnt, (8, lanes)).astype(BF16), upper_ref[...])[0:1]
    before = _dot(tri_ref[...], onehot.astype(BF16)) + seg
    slab = jnp.zeros((tm, lanes), F32)
    for k in range(TOP_K):
        loc = jnp.sum(jnp.where(hits[k], before, 0.0), axis=-1, keepdims=True)
        slab = jnp.where(lane == SLAB_IDX + k, idxs[k], slab)
        slab = jnp.where(lane == SLAB_LOC + k, loc, slab)
        slab = jnp.where(lane == SLAB_GATE + k, gates[k] * gscale, slab)
    slab_ref[...] = slab
    row = lax.broadcasted_iota(I32, (8, lanes), 0)
    tab_ref[...] = jnp.where(row == 0, cnt, jnp.where(row == 1, seg, jnp.where(row == 2, carry_ref[...], 0.0)))
    carry_ref[...] = carry_ref[...] + cnt
    cnt_ref[...] = carry_ref[...]


def _route(x, g, shift, scale, whi, wlo, rb, counts_in, tri, upper):
    b, s, d = x.shape
    tm = MOE_TM
    nt = s // tm
    return pl.pallas_call(
        _route_kernel,
        grid=(b, nt),
        in_specs=[pl.BlockSpec((None, tm, d), lambda bi, i: (bi, i, 0)),
                  pl.BlockSpec((1, d), lambda bi, i: (0, 0)),
                  pl.BlockSpec((None, 1, d), lambda bi, i: (bi, 0, 0)),
                  pl.BlockSpec((None, 1, d), lambda bi, i: (bi, 0, 0)),
                  pl.BlockSpec(whi.shape, lambda bi, i: (0, 0)),
                  pl.BlockSpec(wlo.shape, lambda bi, i: (0, 0)),
                  pl.BlockSpec((1, LANES), lambda bi, i: (0, 0)),
                  pl.BlockSpec((1, LANES), lambda bi, i: (0, 0)),
                  pl.BlockSpec((tm, tm), lambda bi, i: (0, 0)),
                  pl.BlockSpec((LANES, LANES), lambda bi, i: (0, 0))],
        out_specs=[pl.BlockSpec((None, tm, ROW_PARTS, LANES), lambda bi, i: (bi, i, 0, 0)),
                   pl.BlockSpec((None, tm, LANES), lambda bi, i: (bi, i, 0)),
                   pl.BlockSpec((None, 8, LANES), lambda bi, i: (bi * nt + i, 0, 0)),
                   pl.BlockSpec((1, LANES), lambda bi, i: (0, 0))],
        out_shape=[jax.ShapeDtypeStruct((b, s, ROW_PARTS, LANES), U32),
                   jax.ShapeDtypeStruct((b, s, LANES), F32),
                   jax.ShapeDtypeStruct((b * nt, 8, LANES), F32),
                   jax.ShapeDtypeStruct((1, LANES), F32)],
        scratch_shapes=[pltpu.VMEM((1, LANES), F32)],
        compiler_params=_params("arbitrary", "arbitrary"),
        name="route",
    )(x, g, shift, scale, whi, wlo, rb, counts_in, tri, upper)


def _rows(ref, row, n):
    return ref.at[pl.ds(row, n)]


def _run_copies(tab_smem, stage_ref, far_ref, sem, to_far):
    def per_expert(e, carry):
        n = tab_smem[e]
        near = tab_smem[N_EXPERTS + e]
        far = tab_smem[2 * N_EXPERTS + e]
        for bit in range(MOE_TM.bit_length()):
            size = 1 << bit

            @pl.when((n & size) != 0)
            def _():
                done = n & (size - 1)
                a, b = _rows(stage_ref, near + done, size), _rows(far_ref, far + done, size)
                (pltpu.make_async_copy(a, b, sem) if to_far else pltpu.make_async_copy(b, a, sem)).start()
        return carry

    lax.fori_loop(0, N_EXPERTS, per_expert, 0)


def _wait_tile(stage_ref, far_ref, sem, to_far):
    n = TOP_K * MOE_TM
    a, b = _rows(stage_ref, 0, n), _rows(far_ref, 0, n)
    (pltpu.make_async_copy(a, b, sem) if to_far else pltpu.make_async_copy(b, a, sem)).wait()


def _dispatch_kernel(ends_ref, nu_ref, h_ref, loc_ref, tab_ref, *rest, zero_fill, tile_base, n_blocks):
    if zero_fill:
        xs_ref, loc_smem, tab_smem, stages_ref, zero_ref, sems, psem = rest
    else:
        _, xs_ref, loc_smem, tab_smem, stages_ref, sems, psem = rest
        zero_ref = None
    tm = MOE_TM
    i = pl.program_id(0)
    n_steps = pl.num_programs(0)
    slot = i % 2
    stage_ref, sem = stages_ref.at[slot], sems.at[slot]
    loc_copy = pltpu.make_async_copy(loc_ref.at[pl.ds((tile_base + i) * tm * 8, tm * 8)], loc_smem, psem)
    tab_copy = pltpu.make_async_copy(tab_ref.at[pl.ds((tile_base + i) * TAB_WORDS, TAB_WORDS)], tab_smem, psem)
    loc_copy.start()
    tab_copy.start()

    if zero_fill:
        @pl.when(i == 0)
        def _():
            zero_ref[...] = jnp.zeros_like(zero_ref)

            def block_copy(blk):
                return pltpu.make_async_copy(zero_ref, _rows(xs_ref, blk * EXPERT_BLOCK, EXPERT_BLOCK), sem)

            def fill(e, carry, *, start):
                end = ends_ref[e]
                prev = jnp.where(e > 0, ends_ref[jnp.maximum(e - 1, 0)], 0)

                @pl.when(end > prev)
                def _():
                    cp = block_copy(end // EXPERT_BLOCK - 1)
                    cp.start() if start else cp.wait()
                return carry

            def tail(j, carry, *, start):
                cp = block_copy(j)
                cp.start() if start else cp.wait()
                return carry

            lax.fori_loop(0, N_EXPERTS, functools.partial(fill, start=True), 0)
            lax.fori_loop(nu_ref[0], n_blocks, functools.partial(tail, start=True), 0)
            lax.fori_loop(0, N_EXPERTS, functools.partial(fill, start=False), 0)
            lax.fori_loop(nu_ref[0], n_blocks, functools.partial(tail, start=False), 0)

    loc_copy.wait()
    tab_copy.wait()

    @pl.when(i >= 2)
    def _():
        _wait_tile(stage_ref, xs_ref, sem, to_far=True)

    def place(t, c):
        row = h_ref[t]
        for k in range(TOP_K):
            stage_ref[loc_smem[t * 8 + k]] = row
        return c

    lax.fori_loop(0, tm, place, 0, unroll=4)
    _run_copies(tab_smem, stage_ref, xs_ref, sem, to_far=True)

    @pl.when(i == n_steps - 1)
    def _():
        _wait_tile(stage_ref, xs_ref, sem, to_far=True)

        @pl.when(i >= 1)
        def _():
            _wait_tile(stages_ref.at[1 - slot], xs_ref, sems.at[1 - slot], to_far=True)


def _dispatch(h2, loc_flat, tab_flat, ends, n_used, xs_prev, n_rows, tile_base):
    n = h2.shape[0]
    tm = MOE_TM
    zero_fill = xs_prev is None
    kern = functools.partial(_dispatch_kernel, zero_fill=zero_fill, tile_base=tile_base,
                             n_blocks=n_rows // EXPERT_BLOCK)
    in_specs = [pl.BlockSpec((tm, ROW_PARTS, LANES), lambda i, e, nu: (i, 0, 0)),
                pl.BlockSpec(memory_space=pl.ANY),
                pl.BlockSpec(memory_space=pl.ANY)]
    args = [ends, n_used, h2, loc_flat, tab_flat]
    scratch = [pltpu.SMEM((tm * 8,), I32), pltpu.SMEM((TAB_WORDS,), I32),
               pltpu.VMEM((2, TOP_K * tm, ROW_PARTS, LANES), U32)]
    aliases = {}
    if zero_fill:
        scratch.append(pltpu.VMEM((EXPERT_BLOCK, ROW_PARTS, LANES), U32))
    else:
        in_specs.append(pl.BlockSpec(memory_space=pl.ANY))
        args.append(xs_prev)
        aliases = {5: 0}
    scratch += [pltpu.SemaphoreType.DMA((2,)), pltpu.SemaphoreType.DMA]
    return pl.pallas_call(
        kern,
        grid_spec=pltpu.PrefetchScalarGridSpec(
            num_scalar_prefetch=2,
            grid=(n // tm,),
            in_specs=in_specs,
            out_specs=pl.BlockSpec(memory_space=pl.ANY),
            scratch_shapes=scratch),
        out_shape=jax.ShapeDtypeStruct((n_rows, ROW_PARTS, LANES), U32),
        input_output_aliases=aliases,
        compiler_params=_params("arbitrary"),
        name="dispatch",
    )(*args)


def _expert_kernel(be_ref, nu_ref, x_ref, w1_ref, w3_ref, w2_ref, o_ref):
    used = pl.program_id(0) < nu_ref[0]

    @pl.when(used)
    def _():
        x = _unpack_rows(x_ref).astype(BF16)
        a = _silu(_dot(x, w1_ref[...])) * _dot(x, w3_ref[...])
        _pack_rows(o_ref, _dot(a.astype(BF16), w2_ref[...]))

    @pl.when(jnp.logical_not(used))
    def _():
        o_ref[...] = jnp.zeros_like(o_ref)


def _experts(xs, block_e, n_used, w1, w3, w2, layer):
    d, de = w1.shape[2:]
    blk = EXPERT_BLOCK
    nblk = xs.shape[0] // blk

    def row_map(i, be, nu):
        return (jnp.minimum(i, nu[0] - 1), 0, 0)

    def w_map(i, be, nu):
        return (layer, be[i], 0, 0)

    return pl.pallas_call(
        _expert_kernel,
        grid_spec=pltpu.PrefetchScalarGridSpec(
            num_scalar_prefetch=2,
            grid=(nblk,),
            in_specs=[pl.BlockSpec((blk, ROW_PARTS, LANES), row_map),
                      pl.BlockSpec((None, None, d, de), w_map),
                      pl.BlockSpec((None, None, d, de), w_map),
                      pl.BlockSpec((None, None, de, d), w_map)],
            out_specs=pl.BlockSpec((blk, ROW_PARTS, LANES), lambda i, be, nu: (i, 0, 0))),
        out_shape=jax.ShapeDtypeStruct(xs.shape, U32),
        compiler_params=_params("arbitrary"),
        name="experts",
    )(block_e, n_used, xs, w1, w3, w2)


def _combine_kernel(loc_ref, tab_ref, gates_ref, ys_ref, h_ref, x_ref, gate_ref, s1_ref, s3_ref, s2_ref,
                    o_ref, loc_smem, tab_smem, gate_smem, stage_ref, lo_ref, hi_ref, sem, psem, *, tile_base):
    tm = MOE_TM
    tile = tile_base + pl.program_id(0) * pl.num_programs(1) + pl.program_id(1)
    small = [pltpu.make_async_copy(loc_ref.at[pl.ds(tile * tm * 8, tm * 8)], loc_smem, psem),
             pltpu.make_async_copy(gates_ref.at[pl.ds(tile * tm * 8, tm * 8)], gate_smem, psem),
             pltpu.make_async_copy(tab_ref.at[pl.ds(tile * TAB_WORDS, TAB_WORDS)], tab_smem, psem)]
    for cp in small:
        cp.start()
    for cp in small:
        cp.wait()
    _run_copies(tab_smem, stage_ref, ys_ref, sem, to_far=False)
    hb = _unpack_rows(h_ref).astype(BF16)
    shared = _dot((_silu(_dot(hb, s1_ref[...])) * _dot(hb, s3_ref[...])).astype(BF16), s2_ref[...])
    _wait_tile(stage_ref, ys_ref, sem, to_far=False)

    def mix(t, c):
        lo = hi = None
        for k in range(TOP_K):
            g = gate_smem[t * 8 + k]
            wl, wh = _unpack_words(stage_ref[loc_smem[t * 8 + k]])
            lo = g * wl if lo is None else lo + g * wl
            hi = g * wh if hi is None else hi + g * wh
        lo_ref[t] = lo
        hi_ref[t] = hi
        return c

    lax.fori_loop(0, tm, mix, 0, unroll=4)
    routed = jnp.concatenate(_row_chunks(lo_ref) + _row_chunks(hi_ref), axis=-1)
    o_ref[...] = x_ref[...] + gate_ref[...] * (routed + shared)


def _combine(ys, loc_flat, tab_flat, gates_flat, h2, x, gate, s1, s3, s2, tile_base):
    b, s, d = x.shape
    tm = MOE_TM
    kern = functools.partial(_combine_kernel, tile_base=tile_base)
    return pl.pallas_call(
        kern,
        grid=(b, s // tm),
        in_specs=[pl.BlockSpec(memory_space=pl.ANY),
                  pl.BlockSpec(memory_space=pl.ANY),
                  pl.BlockSpec(memory_space=pl.ANY),
                  pl.BlockSpec(memory_space=pl.ANY),
                  pl.BlockSpec((None, tm, ROW_PARTS, LANES), lambda bi, i: (bi, i, 0, 0)),
                  pl.BlockSpec((None, tm, d), lambda bi, i: (bi, i, 0)),
                  pl.BlockSpec((None, 1, d), lambda bi, i: (bi, 0, 0)),
                  pl.BlockSpec(s1.shape, lambda bi, i: (0, 0)),
                  pl.BlockSpec(s3.shape, lambda bi, i: (0, 0)),
                  pl.BlockSpec(s2.shape, lambda bi, i: (0, 0))],
        out_specs=pl.BlockSpec((None, tm, d), lambda bi, i: (bi, i, 0)),
        out_shape=jax.ShapeDtypeStruct((b, s, d), F32),
        scratch_shapes=[pltpu.SMEM((tm * 8,), I32),
                        pltpu.SMEM((TAB_WORDS,), I32),
                        pltpu.SMEM((tm * 8,), F32),
                        pltpu.VMEM((TOP_K * tm, ROW_PARTS, LANES), U32),
                        pltpu.VMEM((tm, ROW_PARTS, LANES), F32),
                        pltpu.VMEM((tm, ROW_PARTS, LANES), F32),
                        pltpu.SemaphoreType.DMA,
                        pltpu.SemaphoreType.DMA],
        compiler_params=_params("arbitrary", "arbitrary"),
        name="combine",
    )(loc_flat, tab_flat, gates_flat, ys, h2, x, gate, s1, s3, s2)


def _dft_tables(length, n_chan):
    scale = 1.0 / math.sqrt(length * n_chan)
    side = 1
    while side * side < length:
        side *= 2
    outer = length // side
    k = jnp.arange(length, dtype=I32)[:, None]
    a_idx = (k * jnp.arange(outer, dtype=I32)[None, :]) % outer
    b_idx = (k * jnp.arange(side, dtype=I32)[None, :]) % length
    ang_a = a_idx.astype(F32) * (2.0 * math.pi / outer)
    ang_b = b_idx.astype(F32) * (2.0 * math.pi / length)
    ca, sa = jnp.cos(ang_a)[:, :, None], jnp.sin(ang_a)[:, :, None]
    cb, sb = jnp.cos(ang_b)[:, None, :], jnp.sin(ang_b)[:, None, :]
    cos_t = (ca * cb - sa * sb).reshape(length, length)
    sin_t = (sa * cb + ca * sb).reshape(length, length)
    table = (jnp.concatenate([cos_t, -sin_t], axis=1) * scale).astype(BF16)
    return table


def _channel_table(n_chan, n_groups):
    m = jnp.arange(n_chan, dtype=I32)
    ang = ((m[:, None] * m[None, :]) % n_chan).astype(F32) * (2.0 * math.pi / n_chan)
    eye = jnp.eye(n_groups, dtype=F32)
    return jnp.concatenate([jnp.kron(eye, jnp.cos(ang)), jnp.kron(eye, jnp.sin(ang))], axis=1).astype(BF16)


def _rope_tables(n_tok):
    rows = n_tok // GRID_W
    axis_dim = HEAD_DIM // 2
    r = jnp.repeat(jnp.arange(rows, dtype=F32), GRID_W)
    col = jnp.tile(jnp.arange(GRID_W, dtype=F32), rows)
    inv = ROPE_BASE ** (-jnp.arange(0, axis_dim, 2, dtype=F32) / axis_dim)
    ar, ac = r[:, None] * inv, col[:, None] * inv
    cos = jnp.concatenate([jnp.cos(ar), jnp.cos(ar), jnp.cos(ac), jnp.cos(ac)], axis=1)
    sin = jnp.concatenate([-jnp.sin(ar), jnp.sin(ar), -jnp.sin(ac), jnp.sin(ac)], axis=1)
    return jnp.tile(cos, (1, N_HEADS)), jnp.tile(sin, (1, N_HEADS))


def _head_mean_matrix(width):
    h = jnp.arange(width) // HEAD_DIM
    return ((h[:, None] == h[None, :]).astype(F32) / HEAD_DIM).astype(BF16)


def _tile_matrices():
    src = jnp.arange(N_KV * HEAD_DIM)
    dst = jnp.arange(Q_PER_KV * HEAD_DIM)
    mats = [((src[:, None] // HEAD_DIM == h) & (src[:, None] % HEAD_DIM == dst[None, :] % HEAD_DIM))
            for h in range(N_KV)]
    return jnp.stack(mats).astype(BF16)


def _even_layer(x, xc, mod, modc, norm_g, w_in, ws, bs, w_out, tables):
    b, s, d = x.shape
    w = w_in.shape[1] // 3
    outs = []
    for stream, m, tm in ((x, mod, 512), (xc, modc, 256)):
        if stream is None:
            outs.append(None)
            continue
        length = stream.shape[1]
        tm = min(tm, length)
        uvf = _norm_mod_matmul(stream, norm_g, m[0], m[1], w_in, tm)
        gm, z = _even_mix(uvf, ws, bs, tables["chan"], tm)
        table = tables["pos"][length]
        y = _matmul(table, z.reshape(2 * length, b * w),
                    min(1024, length), min(1024, b * w), min(1024, 2 * length), BF16)
        outs.append(_outproj(
            gm, pl.BlockSpec((None, tm, w), lambda bi, i: (bi, i, 0)),
            y, pl.BlockSpec((tm, w), lambda bi, i: (i, bi)),
            w_out, stream, m[2], tm))
    return outs


def _odd_layer(x, xc, mod, modc, norm_g, w_in, qg, kg, sink, conv_w, w_out, tables, ctx_out):
    b, s, d = x.shape
    lc = xc.shape[1]
    half = N_HEADS * HEAD_DIM
    tm, tmc = 512, min(256, lc)
    proj = _norm_mod_matmul(x, norm_g, mod[0], mod[1], w_in, tm)
    projc = _norm_mod_matmul(xc, norm_g, modc[0], modc[1], w_in, tmc)
    prep = functools.partial(_qkv_prep, qg=qg, kg=kg, bdq=tables["bdq"], bdk=tables["bdk"], tile=tables["tile"])
    q, k4, v4 = prep(proj, tables["cos"], tables["sin"], tm=tm, with_q=True)
    qc, kc4, vc4 = prep(projc, tables["cos_c"], tables["sin_c"], tm=tmc, with_q=ctx_out)
    att = _attention(q, k4, v4, kc4, vc4, sink, band=True)
    conv = _short_conv(proj, conv_w)
    spec = lambda t: pl.BlockSpec((None, t, half), lambda bi, i: (bi, i, 0))
    y = _outproj(att, spec(tm), conv, spec(tm), w_out, x, mod[2], tm)
    yc = None
    if ctx_out:
        attc = _attention(qc, None, None, kc4, vc4, sink, band=False)
        convc = _short_conv(projc, conv_w)
        yc = _outproj(attc, spec(tmc), convc, spec(tmc), w_out, xc, modc[2], tmc)
    return y, yc


def _moe(x, xc, mod, modc, norm_g, rw_hi, rw_lo, rb, w1, w3, w2, layer, s1, s3, s2, tri, upper):
    b, s, d = x.shape
    n_lat = b * s
    xc_shape = None
    if xc is not None and xc.shape[1] % MOE_TM:
        xc_shape = xc.shape
        xc = xc.reshape(-1, MOE_TM, d)
        modc = [m[:xc.shape[0]] for m in modc]
    counts0 = jnp.zeros((1, LANES), F32)
    h2, slab, tab, counts = _route(x, norm_g, mod[3], mod[4], rw_hi, rw_lo, rb, counts0, tri, upper)
    slabs, tabs = [slab.reshape(n_lat, LANES)], [tab]
    n_tok = n_lat
    if xc is not None:
        h2c, slabc, tabc, counts = _route(xc, norm_g, modc[3], modc[4], rw_hi, rw_lo, rb, counts, tri, upper)
        slabs.append(slabc.reshape(-1, LANES))
        tabs.append(tabc)
        n_tok += slabs[1].shape[0]
    cnt = counts[0, :N_EXPERTS].astype(I32)
    blk = EXPERT_BLOCK
    padded = (cnt + blk - 1) // blk * blk
    ends = jnp.cumsum(padded).astype(I32)
    starts = ends - padded
    n_rows = (n_tok * TOP_K + N_EXPERTS * (blk - 1) + blk - 1) // blk * blk
    n_blocks = n_rows // blk
    n_used = (ends[-1] // blk).reshape(1).astype(I32)
    blk_start = jnp.minimum(jnp.arange(n_blocks, dtype=I32), n_used[0] - 1) * blk
    block_e = jnp.minimum(jnp.sum(blk_start[:, None] >= ends[None, :], axis=1), N_EXPERTS - 1).astype(I32)
    slab_all = jnp.concatenate(slabs, axis=0)
    tab_all = jnp.concatenate(tabs, axis=0)[:, :, :N_EXPERTS].astype(I32)
    runs = jnp.concatenate([tab_all[:, 0], tab_all[:, 1], tab_all[:, 2] + starts[None, :]], axis=1)
    tab_flat = jnp.pad(runs, ((0, 0), (0, TAB_WORDS - runs.shape[1]))).reshape(-1)
    loc_flat = slab_all[:, SLAB_LOC:SLAB_LOC + 8].astype(I32).reshape(-1)
    gates_flat = slab_all[:, SLAB_GATE:SLAB_GATE + 8].reshape(-1)
    packed = lambda a: a.reshape(-1, ROW_PARTS, LANES)
    xs = _dispatch(packed(h2), loc_flat, tab_flat, ends, n_used, None, n_rows, 0)
    if xc is not None:
        xs = _dispatch(packed(h2c), loc_flat, tab_flat, ends, n_used, xs, n_rows, n_lat // MOE_TM)
    ys = _experts(xs, block_e, n_used, w1, w3, w2, layer)
    x_new = _combine(ys, loc_flat, tab_flat, gates_flat, h2, x, mod[5], s1, s3, s2, 0)
    xc_new = None
    if xc is not None:
        xc_new = _combine(ys, loc_flat, tab_flat, gates_flat, h2c, xc, modc[5], s1, s3, s2, n_lat // MOE_TM)
        if xc_shape is not None:
            xc_new = xc_new.reshape(xc_shape)
    return x_new, xc_new


def kernel(x, c, ctx, c_ctx, ada_w, ada_b, norm1_g, norm2_g, ev_w_in, ev_w_s, ev_b_s, ev_w_out, od_w_in, od_q_norm_g, od_k_norm_g, od_sink, od_conv_w, od_w_out, router_w, router_b, exp_w_gate, exp_w_up, exp_w_down, sh_w_gate, sh_w_up, sh_w_down):
    b, s, d = x.shape
    lc = ctx.shape[1]
    depth = ada_w.shape[0]
    n_groups = ev_w_s.shape[1]
    half = d // 2

    rows = -(-(b + 1) // 8) * 8
    cond = jnp.zeros((rows, d), F32).at[:b].set(c).at[b].set(c_ctx)
    mod_all = _adaln(cond, ada_w, ada_b)

    tables = {
        "chan": _channel_table(LANES, n_groups),
        "pos": {s: _dft_tables(s, LANES), lc: _dft_tables(lc, LANES)},
        "bdq": _head_mean_matrix(N_HEADS * HEAD_DIM),
        "bdk": _head_mean_matrix(N_KV * HEAD_DIM),
        "tile": _tile_matrices(),
    }
    tables["cos"], tables["sin"] = _rope_tables(s)
    tables["cos_c"] = jnp.ones((lc, N_HEADS * HEAD_DIM), F32)
    tables["sin_c"] = jnp.zeros((lc, N_HEADS * HEAD_DIM), F32)
    tri = (jnp.arange(MOE_TM)[:, None] > jnp.arange(MOE_TM)[None, :]).astype(BF16)
    upper = (jnp.arange(LANES)[:, None] < jnp.arange(LANES)[None, :]).astype(BF16)

    qw, kw = N_HEADS * HEAD_DIM, N_KV * HEAD_DIM
    perm = jnp.concatenate([jnp.arange(0, qw), jnp.arange(qw + 2 * kw, qw + 2 * kw + 3 * half),
                            jnp.arange(qw, qw + 2 * kw)])

    w1_all, w3_all, w2_all = exp_w_gate.astype(BF16), exp_w_up.astype(BF16), exp_w_down.astype(BF16)
    xc = ctx
    for l in range(depth):
        last = l == depth - 1
        even = l % 2 == 0
        need_ctx = not (last and even)
        pieces = [mod_all[l, :, j * d:(j + 1) * d] for j in range(6)]
        mod = [p[:b].reshape(b, 1, d) for p in pieces]
        modc = [jnp.broadcast_to(p[b].reshape(1, 1, d), (b, 1, d)) for p in pieces]
        g1 = norm1_g[l].reshape(1, d)
        g2 = norm2_g[l].reshape(1, d)
        if even:
            e = l // 2
            bs = jnp.broadcast_to(ev_b_s[e][:, :, None], (n_groups, CHUNK, LANES))
            y, yc = _even_layer(x, xc if (need_ctx and not last) else None, mod, modc, g1,
                                ev_w_in[e].astype(BF16), ev_w_s[e].astype(BF16), bs,
                                ev_w_out[e].astype(BF16), tables)
        else:
            o = l // 2
            qg = (jnp.tile(od_q_norm_g[o], N_HEADS) * (HEAD_DIM ** -0.5)).reshape(1, qw)
            kg = jnp.tile(od_k_norm_g[o], N_KV).reshape(1, kw)
            y, yc = _odd_layer(x, xc, mod, modc, g1, od_w_in[o][:, perm].astype(BF16), qg, kg,
                               od_sink[o], od_conv_w[o], od_w_out[o].astype(BF16), tables, not last)
        x = y
        if not last:
            xc = yc
        rw = jnp.zeros((d, LANES), F32).at[:, :N_EXPERTS].set(router_w[l])
        rw_hi = rw.astype(BF16)
        rw_lo = (rw - rw_hi.astype(F32)).astype(BF16)
        rb = jnp.zeros((1, LANES), F32).at[0, :N_EXPERTS].set(router_b[l])
        x, xc_new = _moe(x, None if last else xc, mod, modc, g2, rw_hi, rw_lo, rb, w1_all, w3_all, w2_all, l,
                         sh_w_gate[l].astype(BF16), sh_w_up[l].astype(BF16), sh_w_down[l].astype(BF16),
                         tri, upper)
        if not last:
            xc = xc_new
    return x
```

```python
import functools
import math

import jax
import jax.numpy as jnp
from jax import lax
from jax.experimental import pallas as pl
from jax.experimental.pallas import tpu as pltpu

F32 = jnp.float32
BF16 = jnp.bfloat16
I32 = jnp.int32
U32 = jnp.uint32

LANES = 128
VMEM_LIMIT = 48 * 2**20

EPS = 1e-6
GRID_W = 64
CHUNK = 128
HEAD_DIM = 64
N_HEADS = 8
N_KV = 2
Q_PER_KV = N_HEADS // N_KV
ATT_BLOCK = 128
ROPE_BASE = 10000.0
N_EXPERTS = 64
TOP_K = 6
ROUTED_SCALE = 2.5
EXPERT_BLOCK = 512
MOE_TM = 512
ROW_PARTS = 4
SLAB_IDX, SLAB_LOC, SLAB_GATE = 0, 8, 16
TAB_WORDS = 1024


def _params(*sem):
    return pltpu.CompilerParams(dimension_semantics=sem, vmem_limit_bytes=VMEM_LIMIT)


def _sigmoid(x):
    return 1.0 / (1.0 + jnp.exp(-x))


def _silu(x):
    return x * _sigmoid(x)


def _gelu_tanh(x):
    c = math.sqrt(2.0 / math.pi)
    return x * (0.5 * (1.0 + jnp.tanh(c * (x + 0.044715 * (x * x * x)))))


def _dot(a, b):
    return jnp.dot(a, b, preferred_element_type=F32)


def _unpack_words(w):
    return pltpu.bitcast(w << 16, F32), pltpu.bitcast(w & jnp.uint32(0xFFFF0000), F32)


def _row_chunks(ref):
    n, parts, lanes = ref.shape
    flat = ref.reshape(n * parts, lanes)
    return [flat[pl.ds(c, n, stride=parts), :] for c in range(parts)]


def _unpack_rows(ref):
    halves = [_unpack_words(w) for w in _row_chunks(ref)]
    return jnp.concatenate([h[0] for h in halves] + [h[1] for h in halves], axis=-1)


def _pack_rows(ref, val):
    n, parts, lanes = ref.shape
    half = val.shape[1] // 2
    bits = pltpu.bitcast(val.astype(BF16).astype(F32), U32)
    words = (bits[:, :half] >> 16) | (bits[:, half:] & jnp.uint32(0xFFFF0000))
    flat = ref.reshape(n * parts, lanes)
    for c in range(parts):
        flat[pl.ds(c, n, stride=parts), :] = words[:, c * lanes:(c + 1) * lanes]


def _adaln_kernel(c_ref, w_ref, b_ref, o_ref):
    o_ref[...] = _dot(_silu(c_ref[...]), w_ref[...]) + b_ref[...]


def _adaln(cond, ada_w, ada_b):
    n_layers, d, n6 = ada_w.shape
    rows = cond.shape[0]
    tn = 768
    return pl.pallas_call(
        _adaln_kernel,
        grid=(n_layers, n6 // tn),
        in_specs=[pl.BlockSpec((rows, d), lambda l, j: (0, 0)),
                  pl.BlockSpec((None, d, tn), lambda l, j: (l, 0, j)),
                  pl.BlockSpec((None, 1, tn), lambda l, j: (l, 0, j))],
        out_specs=pl.BlockSpec((None, rows, tn), lambda l, j: (l, 0, j)),
        out_shape=jax.ShapeDtypeStruct((n_layers, rows, n6), F32),
        compiler_params=_params("parallel", "parallel"),
        name="adaln",
    )(cond, ada_w, ada_b.reshape(n_layers, 1, n6))


def _norm_mod(x, g, shift, scale):
    ms = jnp.mean(x * x, axis=-1, keepdims=True)
    h = (x * lax.rsqrt(ms + EPS)) * g
    return h * (1.0 + scale) + shift


def _nmm_kernel(x_ref, g_ref, sh_ref, sc_ref, w_ref, o_ref):
    h = _norm_mod(x_ref[...], g_ref[...], sh_ref[...], sc_ref[...])
    o_ref[...] = _dot(h.astype(BF16), w_ref[...]).astype(o_ref.dtype)


def _norm_mod_matmul(x, g, shift, scale, w, tm):
    b, s, d = x.shape
    n = w.shape[1]
    return pl.pallas_call(
        _nmm_kernel,
        grid=(b, s // tm),
        in_specs=[pl.BlockSpec((None, tm, d), lambda bi, i: (bi, i, 0)),
                  pl.BlockSpec((1, d), lambda bi, i: (0, 0)),
                  pl.BlockSpec((None, 1, d), lambda bi, i: (bi, 0, 0)),
                  pl.BlockSpec((None, 1, d), lambda bi, i: (bi, 0, 0)),
                  pl.BlockSpec((d, n), lambda bi, i: (0, 0))],
        out_specs=pl.BlockSpec((None, tm, n), lambda bi, i: (bi, i, 0)),
        out_shape=jax.ShapeDtypeStruct((b, s, n), BF16),
        compiler_params=_params("parallel", "parallel"),
        name="norm_mod_matmul",
    )(x, g, shift, scale, w)


def _even_mix_kernel(u_ref, v_ref, f_ref, ws_ref, bs_ref, cs_ref, gm_ref, z_ref, *, n_chunks, n_groups):
    for c in range(n_chunks):
        rows = slice(c * CHUNK, (c + 1) * CHUNK)
        for g in range(n_groups):
            cols = slice(g * LANES, (g + 1) * LANES)
            ug = _gelu_tanh(u_ref[rows, cols].astype(F32))
            vg = _gelu_tanh(v_ref[rows, cols].astype(F32))
            mu = jnp.mean(vg, axis=-1, keepdims=True)
            dv = vg - mu
            var = jnp.mean(dv * dv, axis=-1, keepdims=True)
            vn = dv * lax.rsqrt(var + 1e-5)
            fg = _dot(ws_ref[g], vn.astype(BF16)) + bs_ref[g]
            gm_ref[rows, cols] = (ug * fg).astype(gm_ref.dtype)
    fz = _dot(f_ref[...].astype(BF16), cs_ref[...])
    half = fz.shape[1] // 2
    z_ref[0] = fz[:, :half].astype(z_ref.dtype)
    z_ref[1] = fz[:, half:].astype(z_ref.dtype)


def _even_mix(uvf, ws, bs, cs, tm):
    b, s, n3 = uvf.shape
    w = n3 // 3
    n_groups = w // LANES
    kern = functools.partial(_even_mix_kernel, n_chunks=tm // CHUNK, n_groups=n_groups)
    return pl.pallas_call(
        kern,
        grid=(b, s // tm),
        in_specs=[pl.BlockSpec((None, tm, w), lambda bi, i: (bi, i, 0)),
                  pl.BlockSpec((None, tm, w), lambda bi, i: (bi, i, 1)),
                  pl.BlockSpec((None, tm, w), lambda bi, i: (bi, i, 2)),
                  pl.BlockSpec(ws.shape, lambda bi, i: (0, 0, 0)),
                  pl.BlockSpec(bs.shape, lambda bi, i: (0, 0, 0)),
                  pl.BlockSpec(cs.shape, lambda bi, i: (0, 0))],
        out_specs=[pl.BlockSpec((None, tm, w), lambda bi, i: (bi, i, 0)),
                   pl.BlockSpec((2, tm, w), lambda bi, i: (0, i, bi))],
        out_shape=[jax.ShapeDtypeStruct((b, s, w), BF16),
                   jax.ShapeDtypeStruct((2, s, b * w), BF16)],
        compiler_params=_params("parallel", "parallel"),
        name="even_mix",
    )(uvf, uvf, uvf, ws, bs, cs)


def _mm_kernel(a_ref, b_ref, o_ref, acc_ref):
    k = pl.program_id(2)

    @pl.when(k == 0)
    def _():
        acc_ref[...] = jnp.zeros_like(acc_ref)

    acc_ref[...] += _dot(a_ref[...], b_ref[...])

    @pl.when(k == pl.num_programs(2) - 1)
    def _():
        o_ref[...] = acc_ref[...].astype(o_ref.dtype)


def _matmul(a, b, tm, tn, tk, out_dtype):
    m, kd = a.shape
    n = b.shape[1]
    return pl.pallas_call(
        _mm_kernel,
        grid=(m // tm, n // tn, kd // tk),
        in_specs=[pl.BlockSpec((tm, tk), lambda i, j, k: (i, k)),
                  pl.BlockSpec((tk, tn), lambda i, j, k: (k, j))],
        out_specs=pl.BlockSpec((tm, tn), lambda i, j, k: (i, j)),
        out_shape=jax.ShapeDtypeStruct((m, n), out_dtype),
        scratch_shapes=[pltpu.VMEM((tm, tn), F32)],
        compiler_params=_params("parallel", "parallel", "arbitrary"),
        name="dft_matmul",
    )(a, b)


def _outproj_kernel(a_ref, b_ref, w_ref, x_ref, gate_ref, o_ref):
    ab = jnp.concatenate([a_ref[...], b_ref[...]], axis=-1)
    o_ref[...] = x_ref[...] + gate_ref[...] * _dot(ab, w_ref[...])


def _outproj(a, a_spec, bsrc, b_spec, w, x, gate, tm):
    b, s, d = x.shape
    return pl.pallas_call(
        _outproj_kernel,
        grid=(b, s // tm),
        in_specs=[a_spec, b_spec,
                  pl.BlockSpec(w.shape, lambda bi, i: (0, 0)),
                  pl.BlockSpec((None, tm, d), lambda bi, i: (bi, i, 0)),
                  pl.BlockSpec((None, 1, d), lambda bi, i: (bi, 0, 0))],
        out_specs=pl.BlockSpec((None, tm, d), lambda bi, i: (bi, i, 0)),
        out_shape=jax.ShapeDtypeStruct((b, s, d), F32),
        compiler_params=_params("parallel", "parallel"),
        name="outproj",
    )(a, bsrc, w, x, gate)


def _head_rms(x, bd_ref):
    xx = x * x
    hi = xx.astype(BF16)
    lo = (xx - hi.astype(F32)).astype(BF16)
    ms = _dot(hi, bd_ref[...]) + _dot(lo, bd_ref[...])
    return x * lax.rsqrt(ms + EPS)


def _rope(x, cos, sins):
    width = x.shape[1]
    lane = lax.broadcasted_iota(I32, x.shape, 1)
    first = (lane & 31) < 16
    swapped = jnp.where(first, pltpu.roll(x, width - 16, 1), pltpu.roll(x, 16, 1))
    return x * cos + swapped * sins


def _qkv_kernel(q_ref, k_ref, v_ref, cos_ref, sin_ref, qg_ref, kg_ref, bdq_ref, bdk_ref, tile_ref,
                qo_ref, k4_ref, v4_ref, *, with_q):
    kw = k_ref.shape[1]
    if with_q:
        qn = _head_rms(q_ref[...].astype(F32), bdq_ref) * qg_ref[...]
        qo_ref[...] = _rope(qn, cos_ref[...], sin_ref[...]).astype(qo_ref.dtype)
    else:
        qo_ref[...] = jnp.zeros_like(qo_ref)
    kn = _head_rms(k_ref[...].astype(F32), bdk_ref) * kg_ref[...]
    kr = _rope(kn, cos_ref[:, :kw], sin_ref[:, :kw]).astype(BF16)
    vb = v_ref[...].astype(BF16)
    for h in range(N_KV):
        k4_ref[h] = _dot(kr, tile_ref[h]).astype(k4_ref.dtype)
        v4_ref[h] = _dot(vb, tile_ref[h]).astype(v4_ref.dtype)


def _qkv_prep(proj, cos, sin, qg, kg, bdq, bdk, tile, tm, with_q):
    b, s, _ = proj.shape
    qw = N_HEADS * HEAD_DIM
    kw = N_KV * HEAD_DIM
    rep = Q_PER_KV * HEAD_DIM
    k_blk = (4 * qw) // kw
    kern = functools.partial(_qkv_kernel, with_q=with_q)
    return pl.pallas_call(
        kern,
        grid=(b, s // tm),
        in_specs=[pl.BlockSpec((None, tm, qw), lambda bi, i: (bi, i, 0)),
                  pl.BlockSpec((None, tm, kw), lambda bi, i: (bi, i, k_blk)),
                  pl.BlockSpec((None, tm, kw), lambda bi, i: (bi, i, k_blk + 1)),
                  pl.BlockSpec((tm, qw), lambda bi, i: (i, 0)),
                  pl.BlockSpec((tm, qw), lambda bi, i: (i, 0)),
                  pl.BlockSpec((1, qw), lambda bi, i: (0, 0)),
                  pl.BlockSpec((1, kw), lambda bi, i: (0, 0)),
                  pl.BlockSpec(bdq.shape, lambda bi, i: (0, 0)),
                  pl.BlockSpec(bdk.shape, lambda bi, i: (0, 0)),
                  pl.BlockSpec(tile.shape, lambda bi, i: (0, 0, 0))],
        out_specs=[pl.BlockSpec((None, tm, qw), lambda bi, i: (bi, i, 0)),
                   pl.BlockSpec((None, N_KV, tm, rep), lambda bi, i: (bi, 0, i, 0)),
                   pl.BlockSpec((None, N_KV, tm, rep), lambda bi, i: (bi, 0, i, 0))],
        out_shape=[jax.ShapeDtypeStruct((b, s, qw), BF16),
                   jax.ShapeDtypeStruct((b, N_KV, s, rep), BF16),
                   jax.ShapeDtypeStruct((b, N_KV, s, rep), BF16)],
        compiler_params=_params("parallel", "parallel"),
        name="qkv_prep",
    )(proj, proj, proj, cos, sin, qg, kg, bdq, bdk, tile)


def _conv_kernel(gi_ref, go_ref, z_ref, w_ref, o_ref, *, rows):
    s = z_ref.shape[0]
    n = s // rows
    w0, w1, w2 = w_ref[0:1, :], w_ref[1:2, :], w_ref[2:3, :]
    ridx = lax.broadcasted_iota(I32, (rows, z_ref.shape[1]), 0)
    prev_last = jnp.zeros((1, z_ref.shape[1]), F32)

    def load(ref, start, count):
        return ref[start:start + count, :].astype(F32)

    for j in range(n):
        r0 = j * rows
        zc = load(gi_ref, r0, rows) * load(z_ref, r0, rows)
        if j + 1 < n:
            nxt = (load(gi_ref, r0 + rows, 16) * load(z_ref, r0 + rows, 16))[0:1, :]
        else:
            nxt = jnp.zeros_like(prev_last)
        zp = jnp.where(ridx == 0, prev_last, pltpu.roll(zc, 1, 0))
        zn = jnp.where(ridx == rows - 1, nxt, pltpu.roll(zc, rows - 1, 0))
        y = zp * w0 + zc * w1 + zn * w2
        o_ref[r0:r0 + rows, :] = (load(go_ref, r0, rows) * y).astype(o_ref.dtype)
        prev_last = zc[rows - 1:rows, :]


def _short_conv(proj, conv_w):
    b, s, _ = proj.shape
    dc = conv_w.shape[1]
    nb = dc // LANES
    base = dc // LANES
    rows = min(s, 512)
    kern = functools.partial(_conv_kernel, rows=rows)
    return pl.pallas_call(
        kern,
        grid=(b, nb),
        in_specs=[pl.BlockSpec((None, s, LANES), lambda bi, c: (bi, 0, base + c)),
                  pl.BlockSpec((None, s, LANES), lambda bi, c: (bi, 0, 2 * base + c)),
                  pl.BlockSpec((None, s, LANES), lambda bi, c: (bi, 0, 3 * base + c)),
                  pl.BlockSpec((conv_w.shape[0], LANES), lambda bi, c: (0, c))],
        out_specs=pl.BlockSpec((None, s, LANES), lambda bi, c: (bi, 0, c)),
        out_shape=jax.ShapeDtypeStruct((b, s, dc), BF16),
        compiler_params=_params("parallel", "parallel"),
        name="short_conv",
    )(proj, proj, proj, conv_w)


def _attn_kernel(sink_ref, q_ref, *refs, band, n_blocks):
    o_ref = refs[-1]
    i = pl.program_id(1)
    t = q_ref.shape[0]
    w = Q_PER_KV * HEAD_DIM
    lane = lax.broadcasted_iota(I32, (t, w), 1)
    masks = [(lane >= g * HEAD_DIM) & (lane < (g + 1) * HEAD_DIM) for g in range(Q_PER_KV)]
    bad = None
    if band:
        n_keys = 3 * t + refs[6].shape[1]
        row = lax.broadcasted_iota(I32, (Q_PER_KV * t, n_keys), 0) & (t - 1)
        col = lax.broadcasted_iota(I32, (Q_PER_KV * t, n_keys), 1)
        off_prev = jnp.where(i > 0, 0, 4 * t)
        off_next = jnp.where(i < n_blocks - 1, 0, 4 * t)
        bad_prev = (col < t) & (col < row + off_prev)
        bad_next = (col >= 2 * t) & (col < 3 * t) & (col - 2 * t > row - off_next)
        bad = bad_prev | bad_next
    for h in range(N_KV):
        if band:
            kp, kc_, kn, vp, vc_, vn, kx, vx = refs[:-1]
            kcat = jnp.concatenate([kp[h], kc_[h], kn[h], kx[h]], axis=0)
            vcat = jnp.concatenate([vp[h], vc_[h], vn[h], vx[h]], axis=0)
        else:
            kx, vx = refs[:-1]
            kcat, vcat = kx[h], vx[h]
        q = q_ref[:, h * w:(h + 1) * w]
        q4 = jnp.concatenate([jnp.where(m, q, jnp.zeros_like(q)) for m in masks], axis=0)
        s = lax.dot_general(q4, kcat, (((1,), (1,)), ((), ())), preferred_element_type=F32)
        if band:
            s = jnp.where(bad, -jnp.inf, s)
        sink = jnp.concatenate(
            [jnp.full((t, 1), sink_ref[h * Q_PER_KV + g], F32) for g in range(Q_PER_KV)], axis=0)
        m = jnp.maximum(jnp.max(s, axis=-1, keepdims=True), sink)
        e = jnp.exp(s - m)
        den = jnp.sum(e, axis=-1, keepdims=True) + jnp.exp(sink - m)
        r = _dot(e.astype(BF16), vcat) * (1.0 / den)
        o = jnp.zeros((t, w), F32)
        for g in range(Q_PER_KV):
            o = o + jnp.where(masks[g], r[g * t:(g + 1) * t, :], 0.0)
        o_ref[:, h * w:(h + 1) * w] = o.astype(o_ref.dtype)


def _attention(q, k4, v4, kx4, vx4, sink, band):
    b, s, qw = q.shape
    rep = k4.shape[-1] if band else kx4.shape[-1]
    lc = kx4.shape[2]
    t = ATT_BLOCK
    nb = s // t
    kern = functools.partial(_attn_kernel, band=band, n_blocks=nb)

    def kv_spec(off):
        return pl.BlockSpec((None, N_KV, t, rep),
                            lambda bi, i: (bi, 0, jnp.clip(i + off, 0, nb - 1), 0))

    ctx_spec = pl.BlockSpec((None, N_KV, lc, rep), lambda bi, i: (bi, 0, 0, 0))
    in_specs = [pl.BlockSpec(memory_space=pltpu.SMEM),
                pl.BlockSpec((None, t, qw), lambda bi, i: (bi, i, 0))]
    args = [sink, q]
    if band:
        in_specs += [kv_spec(-1), kv_spec(0), kv_spec(1), kv_spec(-1), kv_spec(0), kv_spec(1)]
        args += [k4, k4, k4, v4, v4, v4]
    in_specs += [ctx_spec, ctx_spec]
    args += [kx4, vx4]
    return pl.pallas_call(
        kern,
        grid=(b, nb),
        in_specs=in_specs,
        out_specs=pl.BlockSpec((None, t, qw), lambda bi, i: (bi, i, 0)),
        out_shape=jax.ShapeDtypeStruct((b, s, qw), BF16),
        compiler_params=_params("parallel", "parallel"),
        name="attention_band" if band else "attention_ctx",
    )(*args)


def _route_kernel(x_ref, g_ref, sh_ref, sc_ref, whi_ref, wlo_ref, rb_ref, cin_ref, tri_ref, upper_ref,
                  h_ref, slab_ref, tab_ref, cnt_ref, carry_ref):
    first = (pl.program_id(0) == 0) & (pl.program_id(1) == 0)

    @pl.when(first)
    def _():
        carry_ref[...] = cin_ref[...]

    h = _norm_mod(x_ref[...], g_ref[...], sh_ref[...], sc_ref[...])
    _pack_rows(h_ref, h)
    hi = h.astype(BF16)
    lo = (h - hi.astype(F32)).astype(BF16)
    logits = _dot(hi, whi_ref[...]) + _dot(lo, whi_ref[...]) + _dot(hi, wlo_ref[...])
    scores = _sigmoid(logits)
    tm, lanes = scores.shape
    lane = lax.broadcasted_iota(I32, (tm, lanes), 1).astype(F32)
    work = jnp.where(lane < N_EXPERTS, scores + rb_ref[...], -jnp.inf)
    hits, idxs, gates = [], [], []
    for _ in range(TOP_K):
        mx = jnp.max(work, axis=-1, keepdims=True)
        idx = jnp.min(jnp.where(work == mx, lane, float(lanes)), axis=-1, keepdims=True)
        hit = lane == idx
        hits.append(hit)
        idxs.append(idx)
        gates.append(jnp.sum(jnp.where(hit, scores, 0.0), axis=-1, keepdims=True))
        work = jnp.where(hit, -jnp.inf, work)
    gsum = gates[0]
    for gk in gates[1:]:
        gsum = gsum + gk
    gscale = ROUTED_SCALE / (gsum + 1e-20)
    onehot = jnp.zeros((tm, lanes), F32)
    for hit in hits:
        onehot = jnp.where(hit, 1.0, onehot)
    cnt = jnp.sum(onehot, axis=0, keepdims=True)
    seg = _dot(jnp.broadcast_to(cnt, (8, lanes)).astype(BF16), upper_ref[...])[0:1]
    before = _dot(tri_ref[...], onehot.astype(BF16)) + seg
    slab = jnp.zeros((tm, lanes), F32)
    for k in range(TOP_K):
        loc = jnp.sum(jnp.where(hits[k], before, 0.0), axis=-1, keepdims=True)
        slab = jnp.where(lane == SLAB_IDX + k, idxs[k], slab)
        slab = jnp.where(lane == SLAB_LOC + k, loc, slab)
        slab = jnp.where(lane == SLAB_GATE + k, gates[k] * gscale, slab)
    slab_ref[...] = slab
    row = lax.broadcasted_iota(I32, (8, lanes), 0)
    tab_ref[...] = jnp.where(row == 0, cnt, jnp.where(row == 1, seg, jnp.where(row == 2, carry_ref[...], 0.0)))
    carry_ref[...] = carry_ref[...] + cnt
    cnt_ref[...] = carry_ref[...]


def _route(x, g, shift, scale, whi, wlo, rb, counts_in, tri, upper):
    b, s, d = x.shape
    tm = MOE_TM
    nt = s // tm
    return pl.pallas_call(
        _route_kernel,
        grid=(b, nt),
        in_specs=[pl.BlockSpec((None, tm, d), lambda bi, i: (bi, i, 0)),
                  pl.BlockSpec((1, d), lambda bi, i: (0, 0)),
                  pl.BlockSpec((None, 1, d), lambda bi, i: (bi, 0, 0)),
                  pl.BlockSpec((None, 1, d), lambda bi, i: (bi, 0, 0)),
                  pl.BlockSpec(whi.shape, lambda bi, i: (0, 0)),
                  pl.BlockSpec(wlo.shape, lambda bi, i: (0, 0)),
                  pl.BlockSpec((1, LANES), lambda bi, i: (0, 0)),
                  pl.BlockSpec((1, LANES), lambda bi, i: (0, 0)),
                  pl.BlockSpec((tm, tm), lambda bi, i: (0, 0)),
                  pl.BlockSpec((LANES, LANES), lambda bi, i: (0, 0))],
        out_specs=[pl.BlockSpec((None, tm, ROW_PARTS, LANES), lambda bi, i: (bi, i, 0, 0)),
                   pl.BlockSpec((None, tm, LANES), lambda bi, i: (bi, i, 0)),
                   pl.BlockSpec((None, 8, LANES), lambda bi, i: (bi * nt + i, 0, 0)),
                   pl.BlockSpec((1, LANES), lambda bi, i: (0, 0))],
        out_shape=[jax.ShapeDtypeStruct((b, s, ROW_PARTS, LANES), U32),
                   jax.ShapeDtypeStruct((b, s, LANES), F32),
                   jax.ShapeDtypeStruct((b * nt, 8, LANES), F32),
                   jax.ShapeDtypeStruct((1, LANES), F32)],
        scratch_shapes=[pltpu.VMEM((1, LANES), F32)],
        compiler_params=_params("arbitrary", "arbitrary"),
        name="route",
    )(x, g, shift, scale, whi, wlo, rb, counts_in, tri, upper)


def _rows(ref, row, n):
    return ref.at[pl.ds(row, n)]


def _run_copies(tab_smem, base, stage_ref, far_ref, sem, to_far):
    n_bits = MOE_TM.bit_length()
    common = n_bits - 3

    def per_expert(e, carry):
        n = tab_smem[base + e]
        near = tab_smem[base + N_EXPERTS + e]
        far = tab_smem[base + 2 * N_EXPERTS + e]

        def piece(bit):
            size = 1 << bit

            @pl.when((n & size) != 0)
            def _():
                done = n & (size - 1)
                a, b = _rows(stage_ref, near + done, size), _rows(far_ref, far + done, size)
                (pltpu.make_async_copy(a, b, sem) if to_far else pltpu.make_async_copy(b, a, sem)).start()

        for bit in range(common):
            piece(bit)

        @pl.when(n >= (1 << common))
        def _():
            for bit in range(common, n_bits):
                piece(bit)
        return carry

    lax.fori_loop(0, N_EXPERTS, per_expert, 0)


def _wait_tile(stage_ref, far_ref, sem, to_far):
    n = TOP_K * MOE_TM
    a, b = _rows(stage_ref, 0, n), _rows(far_ref, 0, n)
    (pltpu.make_async_copy(a, b, sem) if to_far else pltpu.make_async_copy(b, a, sem)).wait()


def _dispatch_kernel(ends_ref, nu_ref, h_ref, loc_ref, tab_ref, *rest, zero_fill, tile_base, n_blocks):
    if zero_fill:
        xs_ref, loc_smem, tab_smem, stages_ref, zero_ref, sems, psem = rest
    else:
        _, xs_ref, loc_smem, tab_smem, stages_ref, sems, psem = rest
        zero_ref = None
    tm = MOE_TM
    i = pl.program_id(0)
    n_steps = pl.num_programs(0)
    slot = i % 2
    stage_ref, sem = stages_ref.at[slot], sems.at[slot]
    loc_copy = pltpu.make_async_copy(loc_ref.at[pl.ds((tile_base + i) * tm * 8, tm * 8)], loc_smem, psem)
    tab_copy = pltpu.make_async_copy(tab_ref.at[pl.ds((tile_base + i) * TAB_WORDS, TAB_WORDS)], tab_smem, psem)
    loc_copy.start()
    tab_copy.start()

    if zero_fill:
        @pl.when(i == 0)
        def _():
            zero_ref[...] = jnp.zeros_like(zero_ref)

            def block_copy(blk):
                return pltpu.make_async_copy(zero_ref, _rows(xs_ref, blk * EXPERT_BLOCK, EXPERT_BLOCK), sem)

            def fill(e, carry, *, start):
                end = ends_ref[e]
                prev = jnp.where(e > 0, ends_ref[jnp.maximum(e - 1, 0)], 0)

                @pl.when(end > prev)
                def _():
                    cp = block_copy(end // EXPERT_BLOCK - 1)
                    cp.start() if start else cp.wait()
                return carry

            def tail(j, carry, *, start):
                cp = block_copy(j)
                cp.start() if start else cp.wait()
                return carry

            lax.fori_loop(0, N_EXPERTS, functools.partial(fill, start=True), 0)
            lax.fori_loop(nu_ref[0], n_blocks, functools.partial(tail, start=True), 0)
            lax.fori_loop(0, N_EXPERTS, functools.partial(fill, start=False), 0)
            lax.fori_loop(nu_ref[0], n_blocks, functools.partial(tail, start=False), 0)

    loc_copy.wait()
    tab_copy.wait()

    @pl.when(i >= 2)
    def _():
        _wait_tile(stage_ref, xs_ref, sem, to_far=True)

    def place(t, c):
        row = h_ref[t]
        for k in range(TOP_K):
            stage_ref[loc_smem[t * 8 + k]] = row
        return c

    lax.fori_loop(0, tm, place, 0, unroll=4)
    _run_copies(tab_smem, 0, stage_ref, xs_ref, sem, to_far=True)

    @pl.when(i == n_steps - 1)
    def _():
        _wait_tile(stage_ref, xs_ref, sem, to_far=True)

        @pl.when(i >= 1)
        def _():
            _wait_tile(stages_ref.at[1 - slot], xs_ref, sems.at[1 - slot], to_far=True)


def _dispatch(h2, loc_flat, tab_flat, ends, n_used, xs_prev, n_rows, tile_base):
    n = h2.shape[0]
    tm = MOE_TM
    zero_fill = xs_prev is None
    kern = functools.partial(_dispatch_kernel, zero_fill=zero_fill, tile_base=tile_base,
                             n_blocks=n_rows // EXPERT_BLOCK)
    in_specs = [pl.BlockSpec((tm, ROW_PARTS, LANES), lambda i, e, nu: (i, 0, 0)),
                pl.BlockSpec(memory_space=pl.ANY),
                pl.BlockSpec(memory_space=pl.ANY)]
    args = [ends, n_used, h2, loc_flat, tab_flat]
    scratch = [pltpu.SMEM((tm * 8,), I32), pltpu.SMEM((TAB_WORDS,), I32),
               pltpu.VMEM((2, TOP_K * tm, ROW_PARTS, LANES), U32)]
    aliases = {}
    if zero_fill:
        scratch.append(pltpu.VMEM((EXPERT_BLOCK, ROW_PARTS, LANES), U32))
    else:
        in_specs.append(pl.BlockSpec(memory_space=pl.ANY))
        args.append(xs_prev)
        aliases = {5: 0}
    scratch += [pltpu.SemaphoreType.DMA((2,)), pltpu.SemaphoreType.DMA]
    return pl.pallas_call(
        kern,
        grid_spec=pltpu.PrefetchScalarGridSpec(
            num_scalar_prefetch=2,
            grid=(n // tm,),
            in_specs=in_specs,
            out_specs=pl.BlockSpec(memory_space=pl.ANY),
            scratch_shapes=scratch),
        out_shape=jax.ShapeDtypeStruct((n_rows, ROW_PARTS, LANES), U32),
        input_output_aliases=aliases,
        compiler_params=_params("arbitrary"),
        name="dispatch",
    )(*args)


def _expert_kernel(be_ref, nu_ref, x_ref, w1_ref, w3_ref, w2_ref, o_ref):
    used = pl.program_id(0) < nu_ref[0]

    @pl.when(used)
    def _():
        x = _unpack_rows(x_ref).astype(BF16)
        a = _silu(_dot(x, w1_ref[...])) * _dot(x, w3_ref[...])
        _pack_rows(o_ref, _dot(a.astype(BF16), w2_ref[...]))

    @pl.when(jnp.logical_not(used))
    def _():
        o_ref[...] = jnp.zeros_like(o_ref)


def _experts(xs, block_e, n_used, w1, w3, w2, layer):
    d, de = w1.shape[2:]
    blk = EXPERT_BLOCK
    nblk = xs.shape[0] // blk

    def row_map(i, be, nu):
        return (jnp.minimum(i, nu[0] - 1), 0, 0)

    def w_map(i, be, nu):
        return (layer, be[i], 0, 0)

    return pl.pallas_call(
        _expert_kernel,
        grid_spec=pltpu.PrefetchScalarGridSpec(
            num_scalar_prefetch=2,
            grid=(nblk,),
            in_specs=[pl.BlockSpec((blk, ROW_PARTS, LANES), row_map),
                      pl.BlockSpec((None, None, d, de), w_map),
                      pl.BlockSpec((None, None, d, de), w_map),
                      pl.BlockSpec((None, None, de, d), w_map)],
            out_specs=pl.BlockSpec((blk, ROW_PARTS, LANES), lambda i, be, nu: (i, 0, 0))),
        out_shape=jax.ShapeDtypeStruct(xs.shape, U32),
        compiler_params=_params("arbitrary"),
        name="experts",
    )(block_e, n_used, xs, w1, w3, w2)


def _combine_kernel(loc_ref, tab_ref, gates_ref, ys_ref, h_ref, x_ref, gate_ref, s1_ref, s3_ref, s2_ref,
                    o_ref, loc_smem, tab_smem, gate_smem, stages_ref, lo_ref, hi_ref, sems, psem, *, tile_base):
    tm = MOE_TM
    words = tm * 8
    step = pl.program_id(0) * pl.num_programs(1) + pl.program_id(1)
    n_steps = pl.num_programs(0) * pl.num_programs(1)
    slot = step % 2

    def fetch(j, sl):
        tile = tile_base + j
        small = [pltpu.make_async_copy(loc_ref.at[pl.ds(tile * words, words)],
                                       loc_smem.at[pl.ds(sl * words, words)], psem),
                 pltpu.make_async_copy(gates_ref.at[pl.ds(tile * words, words)],
                                       gate_smem.at[pl.ds(sl * words, words)], psem),
                 pltpu.make_async_copy(tab_ref.at[pl.ds(tile * TAB_WORDS, TAB_WORDS)],
                                       tab_smem.at[pl.ds(sl * TAB_WORDS, TAB_WORDS)], psem)]
        for cp in small:
            cp.start()
        for cp in small:
            cp.wait()
        _run_copies(tab_smem, sl * TAB_WORDS, stages_ref.at[sl], ys_ref, sems.at[sl], to_far=False)

    @pl.when(step == 0)
    def _():
        fetch(0, 0)

    @pl.when(step + 1 < n_steps)
    def _():
        fetch(step + 1, 1 - slot)

    hb = _unpack_rows(h_ref).astype(BF16)
    shared = _dot((_silu(_dot(hb, s1_ref[...])) * _dot(hb, s3_ref[...])).astype(BF16), s2_ref[...])
    stage_ref = stages_ref.at[slot]
    _wait_tile(stage_ref, ys_ref, sems.at[slot], to_far=False)
    base = slot * words

    def mix(t, c):
        lo = hi = None
        for k in range(TOP_K):
            g = gate_smem[base + t * 8 + k]
            wl, wh = _unpack_words(stage_ref[loc_smem[base + t * 8 + k]])
            lo = g * wl if lo is None else lo + g * wl
            hi = g * wh if hi is None else hi + g * wh
        lo_ref[t] = lo
        hi_ref[t] = hi
        return c

    lax.fori_loop(0, tm, mix, 0, unroll=4)
    routed = jnp.concatenate(_row_chunks(lo_ref) + _row_chunks(hi_ref), axis=-1)
    o_ref[...] = x_ref[...] + gate_ref[...] * (routed + shared)


def _combine(ys, loc_flat, tab_flat, gates_flat, h2, x, gate, s1, s3, s2, tile_base):
    b, s, d = x.shape
    tm = MOE_TM
    kern = functools.partial(_combine_kernel, tile_base=tile_base)
    return pl.pallas_call(
        kern,
        grid=(b, s // tm),
        in_specs=[pl.BlockSpec(memory_space=pl.ANY),
                  pl.BlockSpec(memory_space=pl.ANY),
                  pl.BlockSpec(memory_space=pl.ANY),
                  pl.BlockSpec(memory_space=pl.ANY),
                  pl.BlockSpec((None, tm, ROW_PARTS, LANES), lambda bi, i: (bi, i, 0, 0)),
                  pl.BlockSpec((None, tm, d), lambda bi, i: (bi, i, 0)),
                  pl.BlockSpec((None, 1, d), lambda bi, i: (bi, 0, 0)),
                  pl.BlockSpec(s1.shape, lambda bi, i: (0, 0)),
                  pl.BlockSpec(s3.shape, lambda bi, i: (0, 0)),
                  pl.BlockSpec(s2.shape, lambda bi, i: (0, 0))],
        out_specs=pl.BlockSpec((None, tm, d), lambda bi, i: (bi, i, 0)),
        out_shape=jax.ShapeDtypeStruct((b, s, d), F32),
        scratch_shapes=[pltpu.SMEM((2 * tm * 8,), I32),
                        pltpu.SMEM((2 * TAB_WORDS,), I32),
                        pltpu.SMEM((2 * tm * 8,), F32),
                        pltpu.VMEM((2, TOP_K * tm, ROW_PARTS, LANES), U32),
                        pltpu.VMEM((tm, ROW_PARTS, LANES), F32),
                        pltpu.VMEM((tm, ROW_PARTS, LANES), F32),
                        pltpu.SemaphoreType.DMA((2,)),
                        pltpu.SemaphoreType.DMA],
        compiler_params=_params("arbitrary", "arbitrary"),
        name="combine",
    )(loc_flat, tab_flat, gates_flat, ys, h2, x, gate, s1, s3, s2)


def _dft_tables(length, n_chan):
    scale = 1.0 / math.sqrt(length * n_chan)
    side = 1
    while side * side < length:
        side *= 2
    outer = length // side
    k = jnp.arange(length, dtype=I32)[:, None]
    a_idx = (k * jnp.arange(outer, dtype=I32)[None, :]) % outer
    b_idx = (k * jnp.arange(side, dtype=I32)[None, :]) % length
    ang_a = a_idx.astype(F32) * (2.0 * math.pi / outer)
    ang_b = b_idx.astype(F32) * (2.0 * math.pi / length)
    ca, sa = jnp.cos(ang_a)[:, :, None], jnp.sin(ang_a)[:, :, None]
    cb, sb = jnp.cos(ang_b)[:, None, :], jnp.sin(ang_b)[:, None, :]
    cos_t = (ca * cb - sa * sb).reshape(length, length)
    sin_t = (sa * cb + ca * sb).reshape(length, length)
    table = (jnp.concatenate([cos_t, -sin_t], axis=1) * scale).astype(BF16)
    return table


def _channel_table(n_chan, n_groups):
    m = jnp.arange(n_chan, dtype=I32)
    ang = ((m[:, None] * m[None, :]) % n_chan).astype(F32) * (2.0 * math.pi / n_chan)
    eye = jnp.eye(n_groups, dtype=F32)
    return jnp.concatenate([jnp.kron(eye, jnp.cos(ang)), jnp.kron(eye, jnp.sin(ang))], axis=1).astype(BF16)


def _rope_tables(n_tok):
    rows = n_tok // GRID_W
    axis_dim = HEAD_DIM // 2
    r = jnp.repeat(jnp.arange(rows, dtype=F32), GRID_W)
    col = jnp.tile(jnp.arange(GRID_W, dtype=F32), rows)
    inv = ROPE_BASE ** (-jnp.arange(0, axis_dim, 2, dtype=F32) / axis_dim)
    ar, ac = r[:, None] * inv, col[:, None] * inv
    cos = jnp.concatenate([jnp.cos(ar), jnp.cos(ar), jnp.cos(ac), jnp.cos(ac)], axis=1)
    sin = jnp.concatenate([-jnp.sin(ar), jnp.sin(ar), -jnp.sin(ac), jnp.sin(ac)], axis=1)
    return jnp.tile(cos, (1, N_HEADS)), jnp.tile(sin, (1, N_HEADS))


def _head_mean_matrix(width):
    h = jnp.arange(width) // HEAD_DIM
    return ((h[:, None] == h[None, :]).astype(F32) / HEAD_DIM).astype(BF16)


def _tile_matrices():
    src = jnp.arange(N_KV * HEAD_DIM)
    dst = jnp.arange(Q_PER_KV * HEAD_DIM)
    mats = [((src[:, None] // HEAD_DIM == h) & (src[:, None] % HEAD_DIM == dst[None, :] % HEAD_DIM))
            for h in range(N_KV)]
    return jnp.stack(mats).astype(BF16)


def _even_layer(x, xc, mod, modc, norm_g, w_in, ws, bs, w_out, tables):
    b, s, d = x.shape
    w = w_in.shape[1] // 3
    outs = []
    for stream, m, tm in ((x, mod, 512), (xc, modc, 256)):
        if stream is None:
            outs.append(None)
            continue
        length = stream.shape[1]
        tm = min(tm, length)
        uvf = _norm_mod_matmul(stream, norm_g, m[0], m[1], w_in, tm)
        gm, z = _even_mix(uvf, ws, bs, tables["chan"], tm)
        table = tables["pos"][length]
        y = _matmul(table, z.reshape(2 * length, b * w),
                    min(1024, length), min(1024, b * w), min(1024, 2 * length), BF16)
        outs.append(_outproj(
            gm, pl.BlockSpec((None, tm, w), lambda bi, i: (bi, i, 0)),
            y, pl.BlockSpec((tm, w), lambda bi, i: (i, bi)),
            w_out, stream, m[2], tm))
    return outs


def _odd_layer(x, xc, mod, modc, norm_g, w_in, qg, kg, sink, conv_w, w_out, tables, ctx_out):
    b, s, d = x.shape
    lc = xc.shape[1]
    half = N_HEADS * HEAD_DIM
    tm, tmc = 512, min(256, lc)
    proj = _norm_mod_matmul(x, norm_g, mod[0], mod[1], w_in, tm)
    projc = _norm_mod_matmul(xc, norm_g, modc[0], modc[1], w_in, tmc)
    prep = functools.partial(_qkv_prep, qg=qg, kg=kg, bdq=tables["bdq"], bdk=tables["bdk"], tile=tables["tile"])
    q, k4, v4 = prep(proj, tables["cos"], tables["sin"], tm=tm, with_q=True)
    qc, kc4, vc4 = prep(projc, tables["cos_c"], tables["sin_c"], tm=tmc, with_q=ctx_out)
    att = _attention(q, k4, v4, kc4, vc4, sink, band=True)
    conv = _short_conv(proj, conv_w)
    spec = lambda t: pl.BlockSpec((None, t, half), lambda bi, i: (bi, i, 0))
    y = _outproj(att, spec(tm), conv, spec(tm), w_out, x, mod[2], tm)
    yc = None
    if ctx_out:
        attc = _attention(qc, None, None, kc4, vc4, sink, band=False)
        convc = _short_conv(projc, conv_w)
        yc = _outproj(attc, spec(tmc), convc, spec(tmc), w_out, xc, modc[2], tmc)
    return y, yc


def _moe(x, xc, mod, modc, norm_g, rw_hi, rw_lo, rb, w1, w3, w2, layer, s1, s3, s2, tri, upper):
    b, s, d = x.shape
    n_lat = b * s
    xc_shape = None
    if xc is not None and xc.shape[1] % MOE_TM:
        xc_shape = xc.shape
        xc = xc.reshape(-1, MOE_TM, d)
        modc = [m[:xc.shape[0]] for m in modc]
    counts0 = jnp.zeros((1, LANES), F32)
    h2, slab, tab, counts = _route(x, norm_g, mod[3], mod[4], rw_hi, rw_lo, rb, counts0, tri, upper)
    slabs, tabs = [slab.reshape(n_lat, LANES)], [tab]
    n_tok = n_lat
    if xc is not None:
        h2c, slabc, tabc, counts = _route(xc, norm_g, modc[3], modc[4], rw_hi, rw_lo, rb, counts, tri, upper)
        slabs.append(slabc.reshape(-1, LANES))
        tabs.append(tabc)
        n_tok += slabs[1].shape[0]
    cnt = counts[0, :N_EXPERTS].astype(I32)
    blk = EXPERT_BLOCK
    padded = (cnt + blk - 1) // blk * blk
    ends = jnp.cumsum(padded).astype(I32)
    starts = ends - padded
    n_rows = (n_tok * TOP_K + N_EXPERTS * (blk - 1) + blk - 1) // blk * blk
    n_blocks = n_rows // blk
    n_used = (ends[-1] // blk).reshape(1).astype(I32)
    blk_start = jnp.minimum(jnp.arange(n_blocks, dtype=I32), n_used[0] - 1) * blk
    block_e = jnp.minimum(jnp.sum(blk_start[:, None] >= ends[None, :], axis=1), N_EXPERTS - 1).astype(I32)
    slab_all = jnp.concatenate(slabs, axis=0)
    tab_all = jnp.concatenate(tabs, axis=0)[:, :, :N_EXPERTS].astype(I32)
    runs = jnp.concatenate([tab_all[:, 0], tab_all[:, 1], tab_all[:, 2] + starts[None, :]], axis=1)
    tab_flat = jnp.pad(runs, ((0, 0), (0, TAB_WORDS - runs.shape[1]))).reshape(-1)
    loc_flat = slab_all[:, SLAB_LOC:SLAB_LOC + 8].astype(I32).reshape(-1)
    gates_flat = slab_all[:, SLAB_GATE:SLAB_GATE + 8].reshape(-1)
    packed = lambda a: a.reshape(-1, ROW_PARTS, LANES)
    xs = _dispatch(packed(h2), loc_flat, tab_flat, ends, n_used, None, n_rows, 0)
    if xc is not None:
        xs = _dispatch(packed(h2c), loc_flat, tab_flat, ends, n_used, xs, n_rows, n_lat // MOE_TM)
    ys = _experts(xs, block_e, n_used, w1, w3, w2, layer)
    x_new = _combine(ys, loc_flat, tab_flat, gates_flat, h2, x, mod[5], s1, s3, s2, 0)
    xc_new = None
    if xc is not None:
        xc_new = _combine(ys, loc_flat, tab_flat, gates_flat, h2c, xc, modc[5], s1, s3, s2, n_lat // MOE_TM)
        if xc_shape is not None:
            xc_new = xc_new.reshape(xc_shape)
    return x_new, xc_new


def kernel(x, c, ctx, c_ctx, ada_w, ada_b, norm1_g, norm2_g, ev_w_in, ev_w_s, ev_b_s, ev_w_out, od_w_in, od_q_norm_g, od_k_norm_g, od_sink, od_conv_w, od_w_out, router_w, router_b, exp_w_gate, exp_w_up, exp_w_down, sh_w_gate, sh_w_up, sh_w_down):
    b, s, d = x.shape
    lc = ctx.shape[1]
    depth = ada_w.shape[0]
    n_groups = ev_w_s.shape[1]
    half = d // 2

    rows = -(-(b + 1) // 8) * 8
    cond = jnp.zeros((rows, d), F32).at[:b].set(c).at[b].set(c_ctx)
    mod_all = _adaln(cond, ada_w, ada_b)

    tables = {
        "chan": _channel_table(LANES, n_groups),
        "pos": {s: _dft_tables(s, LANES), lc: _dft_tables(lc, LANES)},
        "bdq": _head_mean_matrix(N_HEADS * HEAD_DIM),
        "bdk": _head_mean_matrix(N_KV * HEAD_DIM),
        "tile": _tile_matrices(),
    }
    tables["cos"], tables["sin"] = _rope_tables(s)
    tables["cos_c"] = jnp.ones((lc, N_HEADS * HEAD_DIM), F32)
    tables["sin_c"] = jnp.zeros((lc, N_HEADS * HEAD_DIM), F32)
    tri = (jnp.arange(MOE_TM)[:, None] > jnp.arange(MOE_TM)[None, :]).astype(BF16)
    upper = (jnp.arange(LANES)[:, None] < jnp.arange(LANES)[None, :]).astype(BF16)

    qw, kw = N_HEADS * HEAD_DIM, N_KV * HEAD_DIM
    perm = jnp.concatenate([jnp.arange(0, qw), jnp.arange(qw + 2 * kw, qw + 2 * kw + 3 * half),
                            jnp.arange(qw, qw + 2 * kw)])

    w1_all, w3_all, w2_all = exp_w_gate.astype(BF16), exp_w_up.astype(BF16), exp_w_down.astype(BF16)
    xc = ctx
    for l in range(depth):
        last = l == depth - 1
        even = l % 2 == 0
        need_ctx = not (last and even)
        pieces = [mod_all[l, :, j * d:(j + 1) * d] for j in range(6)]
        mod = [p[:b].reshape(b, 1, d) for p in pieces]
        modc = [jnp.broadcast_to(p[b].reshape(1, 1, d), (b, 1, d)) for p in pieces]
        g1 = norm1_g[l].reshape(1, d)
        g2 = norm2_g[l].reshape(1, d)
        if even:
            e = l // 2
            bs = jnp.broadcast_to(ev_b_s[e][:, :, None], (n_groups, CHUNK, LANES))
            y, yc = _even_layer(x, xc if (need_ctx and not last) else None, mod, modc, g1,
                                ev_w_in[e].astype(BF16), ev_w_s[e].astype(BF16), bs,
                                ev_w_out[e].astype(BF16), tables)
        else:
            o = l // 2
            qg = (jnp.tile(od_q_norm_g[o], N_HEADS) * (HEAD_DIM ** -0.5)).reshape(1, qw)
            kg = jnp.tile(od_k_norm_g[o], N_KV).reshape(1, kw)
            y, yc = _odd_layer(x, xc, mod, modc, g1, od_w_in[o][:, perm].astype(BF16), qg, kg,
                               od_sink[o], od_conv_w[o], od_w_out[o].astype(BF16), tables, not last)
        x = y
        if not last:
            xc = yc
        rw = jnp.zeros((d, LANES), F32).at[:, :N_EXPERTS].set(router_w[l])
        rw_hi = rw.astype(BF16)
        rw_lo = (rw - rw_hi.astype(F32)).astype(BF16)
        rb = jnp.zeros((1, LANES), F32).at[0, :N_EXPERTS].set(router_b[l])
        x, xc_new = _moe(x, None if last else xc, mod, modc, g2, rw_hi, rw_lo, rb, w1_all, w3_all, w2_all, l,
                         sh_w_gate[l].astype(BF16), sh_w_up[l].astype(BF16), sh_w_down[l].astype(BF16),
                         tri, upper)
        if not last:
            xc = xc_new
    return x
```

```python
import functools
import math

import jax
import jax.numpy as jnp
from jax import lax
from jax.experimental import pallas as pl
from jax.experimental.pallas import tpu as pltpu

F32 = jnp.float32
BF16 = jnp.bfloat16
I32 = jnp.int32
U32 = jnp.uint32

LANES = 128
VMEM_LIMIT = 48 * 2**20

EPS = 1e-6
GRID_W = 64
CHUNK = 128
HEAD_DIM = 64
N_HEADS = 8
N_KV = 2
Q_PER_KV = N_HEADS // N_KV
ATT_BLOCK = 128
ROPE_BASE = 10000.0
N_EXPERTS = 64
TOP_K = 6
ROUTED_SCALE = 2.5
EXPERT_BLOCK = 512
MOE_TM = 1024
ROW_PARTS = 4
SLAB_IDX, SLAB_LOC, SLAB_GATE = 0, 8, 16
TAB_WORDS = 1024


def _params(*sem):
    return pltpu.CompilerParams(dimension_semantics=sem, vmem_limit_bytes=VMEM_LIMIT)


def _sigmoid(x):
    return 1.0 / (1.0 + jnp.exp(-x))


def _silu(x):
    return x * _sigmoid(x)


def _gelu_tanh(x):
    c = math.sqrt(2.0 / math.pi)
    return x * (0.5 * (1.0 + jnp.tanh(c * (x + 0.044715 * (x * x * x)))))


def _dot(a, b):
    return jnp.dot(a, b, preferred_element_type=F32)


def _unpack_words(w):
    return pltpu.bitcast(w << 16, F32), pltpu.bitcast(w & jnp.uint32(0xFFFF0000), F32)


def _row_chunks(ref):
    n, parts, lanes = ref.shape
    flat = ref.reshape(n * parts, lanes)
    return [flat[pl.ds(c, n, stride=parts), :] for c in range(parts)]


def _unpack_rows(ref):
    halves = [_unpack_words(w) for w in _row_chunks(ref)]
    return jnp.concatenate([h[0] for h in halves] + [h[1] for h in halves], axis=-1)


def _pack_rows(ref, val):
    n, parts, lanes = ref.shape
    half = val.shape[1] // 2
    bits = pltpu.bitcast(val.astype(BF16).astype(F32), U32)
    words = (bits[:, :half] >> 16) | (bits[:, half:] & jnp.uint32(0xFFFF0000))
    flat = ref.reshape(n * parts, lanes)
    for c in range(parts):
        flat[pl.ds(c, n, stride=parts), :] = words[:, c * lanes:(c + 1) * lanes]


def _adaln_kernel(c_ref, w_ref, b_ref, o_ref):
    o_ref[...] = _dot(_silu(c_ref[...]), w_ref[...]) + b_ref[...]


def _adaln(cond, ada_w, ada_b):
    n_layers, d, n6 = ada_w.shape
    rows = cond.shape[0]
    tn = 768
    return pl.pallas_call(
        _adaln_kernel,
        grid=(n_layers, n6 // tn),
        in_specs=[pl.BlockSpec((rows, d), lambda l, j: (0, 0)),
                  pl.BlockSpec((None, d, tn), lambda l, j: (l, 0, j)),
                  pl.BlockSpec((None, 1, tn), lambda l, j: (l, 0, j))],
        out_specs=pl.BlockSpec((None, rows, tn), lambda l, j: (l, 0, j)),
        out_shape=jax.ShapeDtypeStruct((n_layers, rows, n6), F32),
        compiler_params=_params("parallel", "parallel"),
        name="adaln",
    )(cond, ada_w, ada_b.reshape(n_layers, 1, n6))


def _norm_mod(x, g, shift, scale):
    ms = jnp.mean(x * x, axis=-1, keepdims=True)
    h = (x * lax.rsqrt(ms + EPS)) * g
    return h * (1.0 + scale) + shift


def _nmm_kernel(x_ref, g_ref, sh_ref, sc_ref, w_ref, o_ref):
    h = _norm_mod(x_ref[...], g_ref[...], sh_ref[...], sc_ref[...])
    o_ref[...] = _dot(h.astype(BF16), w_ref[...]).astype(o_ref.dtype)


def _norm_mod_matmul(x, g, shift, scale, w, tm):
    b, s, d = x.shape
    n = w.shape[1]
    return pl.pallas_call(
        _nmm_kernel,
        grid=(b, s // tm),
        in_specs=[pl.BlockSpec((None, tm, d), lambda bi, i: (bi, i, 0)),
                  pl.BlockSpec((1, d), lambda bi, i: (0, 0)),
                  pl.BlockSpec((None, 1, d), lambda bi, i: (bi, 0, 0)),
                  pl.BlockSpec((None, 1, d), lambda bi, i: (bi, 0, 0)),
                  pl.BlockSpec((d, n), lambda bi, i: (0, 0))],
        out_specs=pl.BlockSpec((None, tm, n), lambda bi, i: (bi, i, 0)),
        out_shape=jax.ShapeDtypeStruct((b, s, n), BF16),
        compiler_params=_params("parallel", "parallel"),
        name="norm_mod_matmul",
    )(x, g, shift, scale, w)


def _even_mix_kernel(u_ref, v_ref, f_ref, ws_ref, bs_ref, cs_ref, gm_ref, z_ref, *, n_chunks, n_groups):
    for c in range(n_chunks):
        rows = slice(c * CHUNK, (c + 1) * CHUNK)
        for g in range(n_groups):
            cols = slice(g * LANES, (g + 1) * LANES)
            ug = _gelu_tanh(u_ref[rows, cols].astype(F32))
            vg = _gelu_tanh(v_ref[rows, cols].astype(F32))
            mu = jnp.mean(vg, axis=-1, keepdims=True)
            dv = vg - mu
            var = jnp.mean(dv * dv, axis=-1, keepdims=True)
            vn = dv * lax.rsqrt(var + 1e-5)
            fg = _dot(ws_ref[g], vn.astype(BF16)) + bs_ref[g]
            gm_ref[rows, cols] = (ug * fg).astype(gm_ref.dtype)
    fz = _dot(f_ref[...].astype(BF16), cs_ref[...])
    half = fz.shape[1] // 2
    z_ref[0] = fz[:, :half].astype(z_ref.dtype)
    z_ref[1] = fz[:, half:].astype(z_ref.dtype)


def _even_mix(uvf, ws, bs, cs, tm):
    b, s, n3 = uvf.shape
    w = n3 // 3
    n_groups = w // LANES
    kern = functools.partial(_even_mix_kernel, n_chunks=tm // CHUNK, n_groups=n_groups)
    return pl.pallas_call(
        kern,
        grid=(b, s // tm),
        in_specs=[pl.BlockSpec((None, tm, w), lambda bi, i: (bi, i, 0)),
                  pl.BlockSpec((None, tm, w), lambda bi, i: (bi, i, 1)),
                  pl.BlockSpec((None, tm, w), lambda bi, i: (bi, i, 2)),
                  pl.BlockSpec(ws.shape, lambda bi, i: (0, 0, 0)),
                  pl.BlockSpec(bs.shape, lambda bi, i: (0, 0, 0)),
                  pl.BlockSpec(cs.shape, lambda bi, i: (0, 0))],
        out_specs=[pl.BlockSpec((None, tm, w), lambda bi, i: (bi, i, 0)),
                   pl.BlockSpec((2, tm, w), lambda bi, i: (0, i, bi))],
        out_shape=[jax.ShapeDtypeStruct((b, s, w), BF16),
                   jax.ShapeDtypeStruct((2, s, b * w), BF16)],
        compiler_params=_params("parallel", "parallel"),
        name="even_mix",
    )(uvf, uvf, uvf, ws, bs, cs)


def _mm_kernel(a_ref, b_ref, o_ref, acc_ref):
    k = pl.program_id(2)

    @pl.when(k == 0)
    def _():
        acc_ref[...] = jnp.zeros_like(acc_ref)

    acc_ref[...] += _dot(a_ref[...], b_ref[...])

    @pl.when(k == pl.num_programs(2) - 1)
    def _():
        o_ref[...] = acc_ref[...].astype(o_ref.dtype)


def _matmul(a, b, tm, tn, tk, out_dtype):
    m, kd = a.shape
    n = b.shape[1]
    return pl.pallas_call(
        _mm_kernel,
        grid=(m // tm, n // tn, kd // tk),
        in_specs=[pl.BlockSpec((tm, tk), lambda i, j, k: (i, k)),
                  pl.BlockSpec((tk, tn), lambda i, j, k: (k, j))],
        out_specs=pl.BlockSpec((tm, tn), lambda i, j, k: (i, j)),
        out_shape=jax.ShapeDtypeStruct((m, n), out_dtype),
        scratch_shapes=[pltpu.VMEM((tm, tn), F32)],
        compiler_params=_params("parallel", "parallel", "arbitrary"),
        name="dft_matmul",
    )(a, b)


def _outproj_kernel(a_ref, b_ref, w_ref, x_ref, gate_ref, o_ref):
    ab = jnp.concatenate([a_ref[...], b_ref[...]], axis=-1)
    o_ref[...] = x_ref[...] + gate_ref[...] * _dot(ab, w_ref[...])


def _outproj(a, a_spec, bsrc, b_spec, w, x, gate, tm):
    b, s, d = x.shape
    return pl.pallas_call(
        _outproj_kernel,
        grid=(b, s // tm),
        in_specs=[a_spec, b_spec,
                  pl.BlockSpec(w.shape, lambda bi, i: (0, 0)),
                  pl.BlockSpec((None, tm, d), lambda bi, i: (bi, i, 0)),
                  pl.BlockSpec((None, 1, d), lambda bi, i: (bi, 0, 0))],
        out_specs=pl.BlockSpec((None, tm, d), lambda bi, i: (bi, i, 0)),
        out_shape=jax.ShapeDtypeStruct((b, s, d), F32),
        compiler_params=_params("parallel", "parallel"),
        name="outproj",
    )(a, bsrc, w, x, gate)


def _head_rms(x, bd_ref):
    xx = x * x
    hi = xx.astype(BF16)
    lo = (xx - hi.astype(F32)).astype(BF16)
    ms = _dot(hi, bd_ref[...]) + _dot(lo, bd_ref[...])
    return x * lax.rsqrt(ms + EPS)


def _rope(x, cos, sins):
    width = x.shape[1]
    lane = lax.broadcasted_iota(I32, x.shape, 1)
    first = (lane & 31) < 16
    swapped = jnp.where(first, pltpu.roll(x, width - 16, 1), pltpu.roll(x, 16, 1))
    return x * cos + swapped * sins


def _qkv_kernel(q_ref, k_ref, v_ref, cos_ref, sin_ref, qg_ref, kg_ref, bdq_ref, bdk_ref, tile_ref,
                qo_ref, k4_ref, v4_ref, *, with_q):
    kw = k_ref.shape[1]
    if with_q:
        qn = _head_rms(q_ref[...].astype(F32), bdq_ref) * qg_ref[...]
        qo_ref[...] = _rope(qn, cos_ref[...], sin_ref[...]).astype(qo_ref.dtype)
    else:
        qo_ref[...] = jnp.zeros_like(qo_ref)
    kn = _head_rms(k_ref[...].astype(F32), bdk_ref) * kg_ref[...]
    kr = _rope(kn, cos_ref[:, :kw], sin_ref[:, :kw]).astype(BF16)
    vb = v_ref[...].astype(BF16)
    for h in range(N_KV):
        k4_ref[h] = _dot(kr, tile_ref[h]).astype(k4_ref.dtype)
        v4_ref[h] = _dot(vb, tile_ref[h]).astype(v4_ref.dtype)


def _qkv_prep(proj, cos, sin, qg, kg, bdq, bdk, tile, tm, with_q):
    b, s, _ = proj.shape
    qw = N_HEADS * HEAD_DIM
    kw = N_KV * HEAD_DIM
    rep = Q_PER_KV * HEAD_DIM
    k_blk = (4 * qw) // kw
    kern = functools.partial(_qkv_kernel, with_q=with_q)
    return pl.pallas_call(
        kern,
        grid=(b, s // tm),
        in_specs=[pl.BlockSpec((None, tm, qw), lambda bi, i: (bi, i, 0)),
                  pl.BlockSpec((None, tm, kw), lambda bi, i: (bi, i, k_blk)),
                  pl.BlockSpec((None, tm, kw), lambda bi, i: (bi, i, k_blk + 1)),
                  pl.BlockSpec((tm, qw), lambda bi, i: (i, 0)),
                  pl.BlockSpec((tm, qw), lambda bi, i: (i, 0)),
                  pl.BlockSpec((1, qw), lambda bi, i: (0, 0)),
                  pl.BlockSpec((1, kw), lambda bi, i: (0, 0)),
                  pl.BlockSpec(bdq.shape, lambda bi, i: (0, 0)),
                  pl.BlockSpec(bdk.shape, lambda bi, i: (0, 0)),
                  pl.BlockSpec(tile.shape, lambda bi, i: (0, 0, 0))],
        out_specs=[pl.BlockSpec((None, tm, qw), lambda bi, i: (bi, i, 0)),
                   pl.BlockSpec((None, N_KV, tm, rep), lambda bi, i: (bi, 0, i, 0)),
                   pl.BlockSpec((None, N_KV, tm, rep), lambda bi, i: (bi, 0, i, 0))],
        out_shape=[jax.ShapeDtypeStruct((b, s, qw), BF16),
                   jax.ShapeDtypeStruct((b, N_KV, s, rep), BF16),
                   jax.ShapeDtypeStruct((b, N_KV, s, rep), BF16)],
        compiler_params=_params("parallel", "parallel"),
        name="qkv_prep",
    )(proj, proj, proj, cos, sin, qg, kg, bdq, bdk, tile)


def _conv_kernel(gi_ref, go_ref, z_ref, w_ref, o_ref, *, rows):
    s = z_ref.shape[0]
    n = s // rows
    w0, w1, w2 = w_ref[0:1, :], w_ref[1:2, :], w_ref[2:3, :]
    ridx = lax.broadcasted_iota(I32, (rows, z_ref.shape[1]), 0)
    prev_last = jnp.zeros((1, z_ref.shape[1]), F32)

    def load(ref, start, count):
        return ref[start:start + count, :].astype(F32)

    for j in range(n):
        r0 = j * rows
        zc = load(gi_ref, r0, rows) * load(z_ref, r0, rows)
        if j + 1 < n:
            nxt = (load(gi_ref, r0 + rows, 16) * load(z_ref, r0 + rows, 16))[0:1, :]
        else:
            nxt = jnp.zeros_like(prev_last)
        zp = jnp.where(ridx == 0, prev_last, pltpu.roll(zc, 1, 0))
        zn = jnp.where(ridx == rows - 1, nxt, pltpu.roll(zc, rows - 1, 0))
        y = zp * w0 + zc * w1 + zn * w2
        o_ref[r0:r0 + rows, :] = (load(go_ref, r0, rows) * y).astype(o_ref.dtype)
        prev_last = zc[rows - 1:rows, :]


def _short_conv(proj, conv_w):
    b, s, _ = proj.shape
    dc = conv_w.shape[1]
    nb = dc // LANES
    base = dc // LANES
    rows = min(s, 512)
    kern = functools.partial(_conv_kernel, rows=rows)
    return pl.pallas_call(
        kern,
        grid=(b, nb),
        in_specs=[pl.BlockSpec((None, s, LANES), lambda bi, c: (bi, 0, base + c)),
                  pl.BlockSpec((None, s, LANES), lambda bi, c: (bi, 0, 2 * base + c)),
                  pl.BlockSpec((None, s, LANES), lambda bi, c: (bi, 0, 3 * base + c)),
                  pl.BlockSpec((conv_w.shape[0], LANES), lambda bi, c: (0, c))],
        out_specs=pl.BlockSpec((None, s, LANES), lambda bi, c: (bi, 0, c)),
        out_shape=jax.ShapeDtypeStruct((b, s, dc), BF16),
        compiler_params=_params("parallel", "parallel"),
        name="short_conv",
    )(proj, proj, proj, conv_w)


def _attn_kernel(sink_ref, q_ref, *refs, band, n_blocks):
    o_ref = refs[-1]
    i = pl.program_id(1)
    t = q_ref.shape[0]
    w = Q_PER_KV * HEAD_DIM
    lane = lax.broadcasted_iota(I32, (t, w), 1)
    masks = [(lane >= g * HEAD_DIM) & (lane < (g + 1) * HEAD_DIM) for g in range(Q_PER_KV)]
    bad = None
    if band:
        n_keys = 3 * t + refs[6].shape[1]
        row = lax.broadcasted_iota(I32, (Q_PER_KV * t, n_keys), 0) & (t - 1)
        col = lax.broadcasted_iota(I32, (Q_PER_KV * t, n_keys), 1)
        off_prev = jnp.where(i > 0, 0, 4 * t)
        off_next = jnp.where(i < n_blocks - 1, 0, 4 * t)
        bad_prev = (col < t) & (col < row + off_prev)
        bad_next = (col >= 2 * t) & (col < 3 * t) & (col - 2 * t > row - off_next)
        bad = bad_prev | bad_next
    for h in range(N_KV):
        if band:
            kp, kc_, kn, vp, vc_, vn, kx, vx = refs[:-1]
            kcat = jnp.concatenate([kp[h], kc_[h], kn[h], kx[h]], axis=0)
            vcat = jnp.concatenate([vp[h], vc_[h], vn[h], vx[h]], axis=0)
        else:
            kx, vx = refs[:-1]
            kcat, vcat = kx[h], vx[h]
        q = q_ref[:, h * w:(h + 1) * w]
        q4 = jnp.concatenate([jnp.where(m, q, jnp.zeros_like(q)) for m in masks], axis=0)
        s = lax.dot_general(q4, kcat, (((1,), (1,)), ((), ())), preferred_element_type=F32)
        if band:
            s = jnp.where(bad, -jnp.inf, s)
        sink = jnp.concatenate(
            [jnp.full((t, 1), sink_ref[h * Q_PER_KV + g], F32) for g in range(Q_PER_KV)], axis=0)
        m = jnp.maximum(jnp.max(s, axis=-1, keepdims=True), sink)
        e = jnp.exp(s - m)
        den = jnp.sum(e, axis=-1, keepdims=True) + jnp.exp(sink - m)
        r = _dot(e.astype(BF16), vcat) * (1.0 / den)
        o = jnp.zeros((t, w), F32)
        for g in range(Q_PER_KV):
            o = o + jnp.where(masks[g], r[g * t:(g + 1) * t, :], 0.0)
        o_ref[:, h * w:(h + 1) * w] = o.astype(o_ref.dtype)


def _attention(q, k4, v4, kx4, vx4, sink, band):
    b, s, qw = q.shape
    rep = k4.shape[-1] if band else kx4.shape[-1]
    lc = kx4.shape[2]
    t = ATT_BLOCK
    nb = s // t
    kern = functools.partial(_attn_kernel, band=band, n_blocks=nb)

    def kv_spec(off):
        return pl.BlockSpec((None, N_KV, t, rep),
                            lambda bi, i: (bi, 0, jnp.clip(i + off, 0, nb - 1), 0))

    ctx_spec = pl.BlockSpec((None, N_KV, lc, rep), lambda bi, i: (bi, 0, 0, 0))
    in_specs = [pl.BlockSpec(memory_space=pltpu.SMEM),
                pl.BlockSpec((None, t, qw), lambda bi, i: (bi, i, 0))]
    args = [sink, q]
    if band:
        in_specs += [kv_spec(-1), kv_spec(0), kv_spec(1), kv_spec(-1), kv_spec(0), kv_spec(1)]
        args += [k4, k4, k4, v4, v4, v4]
    in_specs += [ctx_spec, ctx_spec]
    args += [kx4, vx4]
    return pl.pallas_call(
        kern,
        grid=(b, nb),
        in_specs=in_specs,
        out_specs=pl.BlockSpec((None, t, qw), lambda bi, i: (bi, i, 0)),
        out_shape=jax.ShapeDtypeStruct((b, s, qw), BF16),
        compiler_params=_params("parallel", "parallel"),
        name="attention_band" if band else "attention_ctx",
    )(*args)


def _route_kernel(x_ref, g_ref, sh_ref, sc_ref, whi_ref, wlo_ref, rb_ref, cin_ref, tri_ref, upper_ref,
                  h_ref, slab_ref, tab_ref, cnt_ref, carry_ref):
    first = (pl.program_id(0) == 0) & (pl.program_id(1) == 0)

    @pl.when(first)
    def _():
        carry_ref[...] = cin_ref[...]

    h = _norm_mod(x_ref[...], g_ref[...], sh_ref[...], sc_ref[...])
    _pack_rows(h_ref, h)
    hi = h.astype(BF16)
    lo = (h - hi.astype(F32)).astype(BF16)
    logits = _dot(hi, whi_ref[...]) + _dot(lo, whi_ref[...]) + _dot(hi, wlo_ref[...])
    scores = _sigmoid(logits)
    tm, lanes = scores.shape
    lane = lax.broadcasted_iota(I32, (tm, lanes), 1).astype(F32)
    work = jnp.where(lane < N_EXPERTS, scores + rb_ref[...], -jnp.inf)
    hits, idxs, gates = [], [], []
    for _ in range(TOP_K):
        mx = jnp.max(work, axis=-1, keepdims=True)
        idx = jnp.min(jnp.where(work == mx, lane, float(lanes)), axis=-1, keepdims=True)
        hit = lane == idx
        hits.append(hit)
        idxs.append(idx)
        gates.append(jnp.sum(jnp.where(hit, scores, 0.0), axis=-1, keepdims=True))
        work = jnp.where(hit, -jnp.inf, work)
    gsum = gates[0]
    for gk in gates[1:]:
        gsum = gsum + gk
    gscale = ROUTED_SCALE / (gsum + 1e-20)
    onehot = jnp.zeros((tm, lanes), F32)
    for hit in hits:
        onehot = jnp.where(hit, 1.0, onehot)
    cnt = jnp.sum(onehot, axis=0, keepdims=True)
    cnt_hi = jnp.floor(cnt * (1.0 / 256.0))
    cnt_lo = cnt - 256.0 * cnt_hi
    parts = jnp.concatenate([jnp.broadcast_to(cnt_hi, (8, lanes)), jnp.broadcast_to(cnt_lo, (8, lanes))], axis=0)
    sums = _dot(parts.astype(BF16), upper_ref[...])
    seg = 256.0 * sums[0:1] + sums[8:9]
    before = _dot(tri_ref[...], onehot.astype(BF16)) + seg
    slab = jnp.zeros((tm, lanes), F32)
    for k in range(TOP_K):
        loc = jnp.sum(jnp.where(hits[k], before, 0.0), axis=-1, keepdims=True)
        slab = jnp.where(lane == SLAB_IDX + k, idxs[k], slab)
        slab = jnp.where(lane == SLAB_LOC + k, loc, slab)
        slab = jnp.where(lane == SLAB_GATE + k, gates[k] * gscale, slab)
    slab_ref[...] = slab
    row = lax.broadcasted_iota(I32, (8, lanes), 0)
    tab_ref[...] = jnp.where(row == 0, cnt, jnp.where(row == 1, seg, jnp.where(row == 2, carry_ref[...], 0.0)))
    carry_ref[...] = carry_ref[...] + cnt
    cnt_ref[...] = carry_ref[...]


def _route(x, g, shift, scale, whi, wlo, rb, counts_in, tri, upper):
    b, s, d = x.shape
    tm = MOE_TM
    nt = s // tm
    return pl.pallas_call(
        _route_kernel,
        grid=(b, nt),
        in_specs=[pl.BlockSpec((None, tm, d), lambda bi, i: (bi, i, 0)),
                  pl.BlockSpec((1, d), lambda bi, i: (0, 0)),
                  pl.BlockSpec((None, 1, d), lambda bi, i: (bi, 0, 0)),
                  pl.BlockSpec((None, 1, d), lambda bi, i: (bi, 0, 0)),
                  pl.BlockSpec(whi.shape, lambda bi, i: (0, 0)),
                  pl.BlockSpec(wlo.shape, lambda bi, i: (0, 0)),
                  pl.BlockSpec((1, LANES), lambda bi, i: (0, 0)),
                  pl.BlockSpec((1, LANES), lambda bi, i: (0, 0)),
                  pl.BlockSpec((tm, tm), lambda bi, i: (0, 0)),
                  pl.BlockSpec((LANES, LANES), lambda bi, i: (0, 0))],
        out_specs=[pl.BlockSpec((None, tm, ROW_PARTS, LANES), lambda bi, i: (bi, i, 0, 0)),
                   pl.BlockSpec((None, tm, LANES), lambda bi, i: (bi, i, 0)),
                   pl.BlockSpec((None, 8, LANES), lambda bi, i: (bi * nt + i, 0, 0)),
                   pl.BlockSpec((1, LANES), lambda bi, i: (0, 0))],
        out_shape=[jax.ShapeDtypeStruct((b, s, ROW_PARTS, LANES), U32),
                   jax.ShapeDtypeStruct((b, s, LANES), F32),
                   jax.ShapeDtypeStruct((b * nt, 8, LANES), F32),
                   jax.ShapeDtypeStruct((1, LANES), F32)],
        scratch_shapes=[pltpu.VMEM((1, LANES), F32)],
        compiler_params=_params("arbitrary", "arbitrary"),
        name="route",
    )(x, g, shift, scale, whi, wlo, rb, counts_in, tri, upper)


def _rows(ref, row, n):
    return ref.at[pl.ds(row, n)]


def _run_copies(tab_smem, base, stage_ref, far_ref, sem, to_far):
    n_bits = MOE_TM.bit_length()
    common = n_bits - 3

    def per_expert(e, carry):
        n = tab_smem[base + e]
        near = tab_smem[base + N_EXPERTS + e]
        far = tab_smem[base + 2 * N_EXPERTS + e]

        def piece(bit):
            size = 1 << bit

            @pl.when((n & size) != 0)
            def _():
                done = n & (size - 1)
                a, b = _rows(stage_ref, near + done, size), _rows(far_ref, far + done, size)
                (pltpu.make_async_copy(a, b, sem) if to_far else pltpu.make_async_copy(b, a, sem)).start()

        for bit in range(common):
            piece(bit)

        @pl.when(n >= (1 << common))
        def _():
            for bit in range(common, n_bits):
                piece(bit)
        return carry

    lax.fori_loop(0, N_EXPERTS, per_expert, 0)


def _wait_tile(stage_ref, far_ref, sem, to_far):
    n = TOP_K * MOE_TM
    a, b = _rows(stage_ref, 0, n), _rows(far_ref, 0, n)
    (pltpu.make_async_copy(a, b, sem) if to_far else pltpu.make_async_copy(b, a, sem)).wait()


def _dispatch_kernel(ends_ref, nu_ref, h_ref, loc_ref, tab_ref, *rest, zero_fill, tile_base, n_blocks):
    if zero_fill:
        xs_ref, loc_smem, tab_smem, stages_ref, zero_ref, sems, psem = rest
    else:
        _, xs_ref, loc_smem, tab_smem, stages_ref, sems, psem = rest
        zero_ref = None
    tm = MOE_TM
    i = pl.program_id(0)
    n_steps = pl.num_programs(0)
    slot = i % 2
    stage_ref, sem = stages_ref.at[slot], sems.at[slot]
    loc_copy = pltpu.make_async_copy(loc_ref.at[pl.ds((tile_base + i) * tm * 8, tm * 8)], loc_smem, psem)
    tab_copy = pltpu.make_async_copy(tab_ref.at[pl.ds((tile_base + i) * TAB_WORDS, TAB_WORDS)], tab_smem, psem)
    loc_copy.start()
    tab_copy.start()

    if zero_fill:
        @pl.when(i == 0)
        def _():
            zero_ref[...] = jnp.zeros_like(zero_ref)

            def block_copy(blk):
                return pltpu.make_async_copy(zero_ref, _rows(xs_ref, blk * EXPERT_BLOCK, EXPERT_BLOCK), sem)

            def fill(e, carry, *, start):
                end = ends_ref[e]
                prev = jnp.where(e > 0, ends_ref[jnp.maximum(e - 1, 0)], 0)

                @pl.when(end > prev)
                def _():
                    cp = block_copy(end // EXPERT_BLOCK - 1)
                    cp.start() if start else cp.wait()
                return carry

            def tail(j, carry, *, start):
                cp = block_copy(j)
                cp.start() if start else cp.wait()
                return carry

            lax.fori_loop(0, N_EXPERTS, functools.partial(fill, start=True), 0)
            lax.fori_loop(nu_ref[0], n_blocks, functools.partial(tail, start=True), 0)
            lax.fori_loop(0, N_EXPERTS, functools.partial(fill, start=False), 0)
            lax.fori_loop(nu_ref[0], n_blocks, functools.partial(tail, start=False), 0)

    loc_copy.wait()
    tab_copy.wait()

    @pl.when(i >= 2)
    def _():
        _wait_tile(stage_ref, xs_ref, sem, to_far=True)

    flat = stage_ref.reshape(TOP_K * tm * ROW_PARTS, LANES)
    h_flat = h_ref.reshape(tm * ROW_PARTS, LANES)

    def place(t, c):
        row = h_flat[pl.ds(pl.multiple_of(t * ROW_PARTS, ROW_PARTS), ROW_PARTS), :]
        for k in range(TOP_K):
            flat[pl.ds(pl.multiple_of(loc_smem[t * 8 + k], ROW_PARTS), ROW_PARTS), :] = row
        return c

    lax.fori_loop(0, tm, place, 0, unroll=4)
    _run_copies(tab_smem, 0, stage_ref, xs_ref, sem, to_far=True)

    @pl.when(i == n_steps - 1)
    def _():
        _wait_tile(stage_ref, xs_ref, sem, to_far=True)

        @pl.when(i >= 1)
        def _():
            _wait_tile(stages_ref.at[1 - slot], xs_ref, sems.at[1 - slot], to_far=True)


def _dispatch(h2, loc_flat, tab_flat, ends, n_used, xs_prev, n_rows, tile_base):
    n = h2.shape[0]
    tm = MOE_TM
    zero_fill = xs_prev is None
    kern = functools.partial(_dispatch_kernel, zero_fill=zero_fill, tile_base=tile_base,
                             n_blocks=n_rows // EXPERT_BLOCK)
    in_specs = [pl.BlockSpec((tm, ROW_PARTS, LANES), lambda i, e, nu: (i, 0, 0)),
                pl.BlockSpec(memory_space=pl.ANY),
                pl.BlockSpec(memory_space=pl.ANY)]
    args = [ends, n_used, h2, loc_flat, tab_flat]
    scratch = [pltpu.SMEM((tm * 8,), I32), pltpu.SMEM((TAB_WORDS,), I32),
               pltpu.VMEM((2, TOP_K * tm, ROW_PARTS, LANES), U32)]
    aliases = {}
    if zero_fill:
        scratch.append(pltpu.VMEM((EXPERT_BLOCK, ROW_PARTS, LANES), U32))
    else:
        in_specs.append(pl.BlockSpec(memory_space=pl.ANY))
        args.append(xs_prev)
        aliases = {5: 0}
    scratch += [pltpu.SemaphoreType.DMA((2,)), pltpu.SemaphoreType.DMA]
    return pl.pallas_call(
        kern,
        grid_spec=pltpu.PrefetchScalarGridSpec(
            num_scalar_prefetch=2,
            grid=(n // tm,),
            in_specs=in_specs,
            out_specs=pl.BlockSpec(memory_space=pl.ANY),
            scratch_shapes=scratch),
        out_shape=jax.ShapeDtypeStruct((n_rows, ROW_PARTS, LANES), U32),
        input_output_aliases=aliases,
        compiler_params=_params("arbitrary"),
        name="dispatch",
    )(*args)


def _expert_kernel(be_ref, nu_ref, x_ref, w1_ref, w3_ref, w2_ref, o_ref):
    used = pl.program_id(0) < nu_ref[0]

    @pl.when(used)
    def _():
        x = _unpack_rows(x_ref).astype(BF16)
        a = _silu(_dot(x, w1_ref[...])) * _dot(x, w3_ref[...])
        _pack_rows(o_ref, _dot(a.astype(BF16), w2_ref[...]))

    @pl.when(jnp.logical_not(used))
    def _():
        o_ref[...] = jnp.zeros_like(o_ref)


def _experts(xs, block_e, n_used, w1, w3, w2, layer):
    d, de = w1.shape[2:]
    blk = EXPERT_BLOCK
    nblk = xs.shape[0] // blk

    def row_map(i, be, nu):
        return (jnp.minimum(i, nu[0] - 1), 0, 0)

    def w_map(i, be, nu):
        return (layer, be[i], 0, 0)

    return pl.pallas_call(
        _expert_kernel,
        grid_spec=pltpu.PrefetchScalarGridSpec(
            num_scalar_prefetch=2,
            grid=(nblk,),
            in_specs=[pl.BlockSpec((blk, ROW_PARTS, LANES), row_map),
                      pl.BlockSpec((None, None, d, de), w_map),
                      pl.BlockSpec((None, None, d, de), w_map),
                      pl.BlockSpec((None, None, de, d), w_map)],
            out_specs=pl.BlockSpec((blk, ROW_PARTS, LANES), lambda i, be, nu: (i, 0, 0))),
        out_shape=jax.ShapeDtypeStruct(xs.shape, U32),
        compiler_params=_params("arbitrary"),
        name="experts",
    )(block_e, n_used, xs, w1, w3, w2)


def _combine_kernel(loc_ref, tab_ref, gates_ref, ys_ref, h_ref, x_ref, gate_ref, s1_ref, s3_ref, s2_ref,
                    o_ref, loc_smem, tab_smem, gate_smem, stage_ref, lo_ref, hi_ref, sem, psem, *, tile_base):
    tm = MOE_TM
    words = tm * 8
    tile = tile_base + pl.program_id(0) * pl.num_programs(1) + pl.program_id(1)
    small = [pltpu.make_async_copy(loc_ref.at[pl.ds(tile * words, words)], loc_smem, psem),
             pltpu.make_async_copy(gates_ref.at[pl.ds(tile * words, words)], gate_smem, psem),
             pltpu.make_async_copy(tab_ref.at[pl.ds(tile * TAB_WORDS, TAB_WORDS)], tab_smem, psem)]
    for cp in small:
        cp.start()
    for cp in small:
        cp.wait()
    _run_copies(tab_smem, 0, stage_ref, ys_ref, sem, to_far=False)
    hb = _unpack_rows(h_ref).astype(BF16)
    shared = _dot((_silu(_dot(hb, s1_ref[...])) * _dot(hb, s3_ref[...])).astype(BF16), s2_ref[...])
    _wait_tile(stage_ref, ys_ref, sem, to_far=False)

    flat = stage_ref.reshape(TOP_K * tm * ROW_PARTS, LANES)

    def mix(t, c):
        lo = hi = None
        for k in range(TOP_K):
            g = gate_smem[t * 8 + k]
            at = pl.multiple_of(loc_smem[t * 8 + k], ROW_PARTS)
            wl, wh = _unpack_words(flat[pl.ds(at, ROW_PARTS), :])
            lo = g * wl if lo is None else lo + g * wl
            hi = g * wh if hi is None else hi + g * wh
        lo_ref[t] = lo
        hi_ref[t] = hi
        return c

    lax.fori_loop(0, tm, mix, 0, unroll=4)
    routed = jnp.concatenate(_row_chunks(lo_ref) + _row_chunks(hi_ref), axis=-1)
    o_ref[...] = x_ref[...] + gate_ref[...] * (routed + shared)


def _combine(ys, loc_flat, tab_flat, gates_flat, h2, x, gate, s1, s3, s2, tile_base):
    b, s, d = x.shape
    tm = MOE_TM
    kern = functools.partial(_combine_kernel, tile_base=tile_base)
    return pl.pallas_call(
        kern,
        grid=(b, s // tm),
        in_specs=[pl.BlockSpec(memory_space=pl.ANY),
                  pl.BlockSpec(memory_space=pl.ANY),
                  pl.BlockSpec(memory_space=pl.ANY),
                  pl.BlockSpec(memory_space=pl.ANY),
                  pl.BlockSpec((None, tm, ROW_PARTS, LANES), lambda bi, i: (bi, i, 0, 0)),
                  pl.BlockSpec((None, tm, d), lambda bi, i: (bi, i, 0)),
                  pl.BlockSpec((None, 1, d), lambda bi, i: (bi, 0, 0)),
                  pl.BlockSpec(s1.shape, lambda bi, i: (0, 0)),
                  pl.BlockSpec(s3.shape, lambda bi, i: (0, 0)),
                  pl.BlockSpec(s2.shape, lambda bi, i: (0, 0))],
        out_specs=pl.BlockSpec((None, tm, d), lambda bi, i: (bi, i, 0)),
        out_shape=jax.ShapeDtypeStruct((b, s, d), F32),
        scratch_shapes=[pltpu.SMEM((tm * 8,), I32),
                        pltpu.SMEM((TAB_WORDS,), I32),
                        pltpu.SMEM((tm * 8,), F32),
                        pltpu.VMEM((TOP_K * tm, ROW_PARTS, LANES), U32),
                        pltpu.VMEM((tm, ROW_PARTS, LANES), F32),
                        pltpu.VMEM((tm, ROW_PARTS, LANES), F32),
                        pltpu.SemaphoreType.DMA,
                        pltpu.SemaphoreType.DMA],
        compiler_params=_params("arbitrary", "arbitrary"),
        name="combine",
    )(loc_flat, tab_flat, gates_flat, ys, h2, x, gate, s1, s3, s2)


def _dft_tables(length, n_chan):
    scale = 1.0 / math.sqrt(length * n_chan)
    side = 1
    while side * side < length:
        side *= 2
    outer = length // side
    k = jnp.arange(length, dtype=I32)[:, None]
    a_idx = (k * jnp.arange(outer, dtype=I32)[None, :]) % outer
    b_idx = (k * jnp.arange(side, dtype=I32)[None, :]) % length
    ang_a = a_idx.astype(F32) * (2.0 * math.pi / outer)
    ang_b = b_idx.astype(F32) * (2.0 * math.pi / length)
    ca, sa = jnp.cos(ang_a)[:, :, None], jnp.sin(ang_a)[:, :, None]
    cb, sb = jnp.cos(ang_b)[:, None, :], jnp.sin(ang_b)[:, None, :]
    cos_t = (ca * cb - sa * sb).reshape(length, length)
    sin_t = (sa * cb + ca * sb).reshape(length, length)
    table = (jnp.concatenate([cos_t, -sin_t], axis=1) * scale).astype(BF16)
    return table


def _channel_table(n_chan, n_groups):
    m = jnp.arange(n_chan, dtype=I32)
    ang = ((m[:, None] * m[None, :]) % n_chan).astype(F32) * (2.0 * math.pi / n_chan)
    eye = jnp.eye(n_groups, dtype=F32)
    return jnp.concatenate([jnp.kron(eye, jnp.cos(ang)), jnp.kron(eye, jnp.sin(ang))], axis=1).astype(BF16)


def _rope_tables(n_tok):
    rows = n_tok // GRID_W
    axis_dim = HEAD_DIM // 2
    r = jnp.repeat(jnp.arange(rows, dtype=F32), GRID_W)
    col = jnp.tile(jnp.arange(GRID_W, dtype=F32), rows)
    inv = ROPE_BASE ** (-jnp.arange(0, axis_dim, 2, dtype=F32) / axis_dim)
    ar, ac = r[:, None] * inv, col[:, None] * inv
    cos = jnp.concatenate([jnp.cos(ar), jnp.cos(ar), jnp.cos(ac), jnp.cos(ac)], axis=1)
    sin = jnp.concatenate([-jnp.sin(ar), jnp.sin(ar), -jnp.sin(ac), jnp.sin(ac)], axis=1)
    return jnp.tile(cos, (1, N_HEADS)), jnp.tile(sin, (1, N_HEADS))


def _head_mean_matrix(width):
    h = jnp.arange(width) // HEAD_DIM
    return ((h[:, None] == h[None, :]).astype(F32) / HEAD_DIM).astype(BF16)


def _tile_matrices():
    src = jnp.arange(N_KV * HEAD_DIM)
    dst = jnp.arange(Q_PER_KV * HEAD_DIM)
    mats = [((src[:, None] // HEAD_DIM == h) & (src[:, None] % HEAD_DIM == dst[None, :] % HEAD_DIM))
            for h in range(N_KV)]
    return jnp.stack(mats).astype(BF16)


def _even_layer(x, xc, mod, modc, norm_g, w_in, ws, bs, w_out, tables):
    b, s, d = x.shape
    w = w_in.shape[1] // 3
    outs = []
    for stream, m, tm in ((x, mod, 512), (xc, modc, 256)):
        if stream is None:
            outs.append(None)
            continue
        length = stream.shape[1]
        tm = min(tm, length)
        uvf = _norm_mod_matmul(stream, norm_g, m[0], m[1], w_in, tm)
        gm, z = _even_mix(uvf, ws, bs, tables["chan"], tm)
        table = tables["pos"][length]
        y = _matmul(table, z.reshape(2 * length, b * w),
                    min(1024, length), min(1024, b * w), min(1024, 2 * length), BF16)
        outs.append(_outproj(
            gm, pl.BlockSpec((None, tm, w), lambda bi, i: (bi, i, 0)),
            y, pl.BlockSpec((tm, w), lambda bi, i: (i, bi)),
            w_out, stream, m[2], tm))
    return outs


def _odd_layer(x, xc, mod, modc, norm_g, w_in, qg, kg, sink, conv_w, w_out, tables, ctx_out):
    b, s, d = x.shape
    lc = xc.shape[1]
    half = N_HEADS * HEAD_DIM
    tm, tmc = 512, min(256, lc)
    proj = _norm_mod_matmul(x, norm_g, mod[0], mod[1], w_in, tm)
    projc = _norm_mod_matmul(xc, norm_g, modc[0], modc[1], w_in, tmc)
    prep = functools.partial(_qkv_prep, qg=qg, kg=kg, bdq=tables["bdq"], bdk=tables["bdk"], tile=tables["tile"])
    q, k4, v4 = prep(proj, tables["cos"], tables["sin"], tm=tm, with_q=True)
    qc, kc4, vc4 = prep(projc, tables["cos_c"], tables["sin_c"], tm=tmc, with_q=ctx_out)
    att = _attention(q, k4, v4, kc4, vc4, sink, band=True)
    conv = _short_conv(proj, conv_w)
    spec = lambda t: pl.BlockSpec((None, t, half), lambda bi, i: (bi, i, 0))
    y = _outproj(att, spec(tm), conv, spec(tm), w_out, x, mod[2], tm)
    yc = None
    if ctx_out:
        attc = _attention(qc, None, None, kc4, vc4, sink, band=False)
        convc = _short_conv(projc, conv_w)
        yc = _outproj(attc, spec(tmc), convc, spec(tmc), w_out, xc, modc[2], tmc)
    return y, yc


def _moe(x, xc, mod, modc, norm_g, rw_hi, rw_lo, rb, w1, w3, w2, layer, s1, s3, s2, tri, upper):
    b, s, d = x.shape
    n_lat = b * s
    xc_shape = None
    if xc is not None and xc.shape[1] % MOE_TM:
        xc_shape = xc.shape
        xc = xc.reshape(-1, MOE_TM, d)
        modc = [m[:xc.shape[0]] for m in modc]
    counts0 = jnp.zeros((1, LANES), F32)
    h2, slab, tab, counts = _route(x, norm_g, mod[3], mod[4], rw_hi, rw_lo, rb, counts0, tri, upper)
    slabs, tabs = [slab.reshape(n_lat, LANES)], [tab]
    n_tok = n_lat
    if xc is not None:
        h2c, slabc, tabc, counts = _route(xc, norm_g, modc[3], modc[4], rw_hi, rw_lo, rb, counts, tri, upper)
        slabs.append(slabc.reshape(-1, LANES))
        tabs.append(tabc)
        n_tok += slabs[1].shape[0]
    cnt = counts[0, :N_EXPERTS].astype(I32)
    blk = EXPERT_BLOCK
    padded = (cnt + blk - 1) // blk * blk
    ends = jnp.cumsum(padded).astype(I32)
    starts = ends - padded
    n_rows = (n_tok * TOP_K + N_EXPERTS * (blk - 1) + blk - 1) // blk * blk
    n_blocks = n_rows // blk
    n_used = (ends[-1] // blk).reshape(1).astype(I32)
    blk_start = jnp.minimum(jnp.arange(n_blocks, dtype=I32), n_used[0] - 1) * blk
    block_e = jnp.minimum(jnp.sum(blk_start[:, None] >= ends[None, :], axis=1), N_EXPERTS - 1).astype(I32)
    slab_all = jnp.concatenate(slabs, axis=0)
    tab_all = jnp.concatenate(tabs, axis=0)[:, :, :N_EXPERTS].astype(I32)
    runs = jnp.concatenate([tab_all[:, 0], tab_all[:, 1], tab_all[:, 2] + starts[None, :]], axis=1)
    tab_flat = jnp.pad(runs, ((0, 0), (0, TAB_WORDS - runs.shape[1]))).reshape(-1)
    loc_flat = (slab_all[:, SLAB_LOC:SLAB_LOC + 8].astype(I32) * ROW_PARTS).reshape(-1)
    gates_flat = slab_all[:, SLAB_GATE:SLAB_GATE + 8].reshape(-1)
    packed = lambda a: a.reshape(-1, ROW_PARTS, LANES)
    xs = _dispatch(packed(h2), loc_flat, tab_flat, ends, n_used, None, n_rows, 0)
    if xc is not None:
        xs = _dispatch(packed(h2c), loc_flat, tab_flat, ends, n_used, xs, n_rows, n_lat // MOE_TM)
    ys = _experts(xs, block_e, n_used, w1, w3, w2, layer)
    x_new = _combine(ys, loc_flat, tab_flat, gates_flat, h2, x, mod[5], s1, s3, s2, 0)
    xc_new = None
    if xc is not None:
        xc_new = _combine(ys, loc_flat, tab_flat, gates_flat, h2c, xc, modc[5], s1, s3, s2, n_lat // MOE_TM)
        if xc_shape is not None:
            xc_new = xc_new.reshape(xc_shape)
    return x_new, xc_new


def kernel(x, c, ctx, c_ctx, ada_w, ada_b, norm1_g, norm2_g, ev_w_in, ev_w_s, ev_b_s, ev_w_out, od_w_in, od_q_norm_g, od_k_norm_g, od_sink, od_conv_w, od_w_out, router_w, router_b, exp_w_gate, exp_w_up, exp_w_down, sh_w_gate, sh_w_up, sh_w_down):
    b, s, d = x.shape
    lc = ctx.shape[1]
    depth = ada_w.shape[0]
    n_groups = ev_w_s.shape[1]
    half = d // 2

    rows = -(-(b + 1) // 8) * 8
    cond = jnp.zeros((rows, d), F32).at[:b].set(c).at[b].set(c_ctx)
    mod_all = _adaln(cond, ada_w, ada_b)

    tables = {
        "chan": _channel_table(LANES, n_groups),
        "pos": {s: _dft_tables(s, LANES), lc: _dft_tables(lc, LANES)},
        "bdq": _head_mean_matrix(N_HEADS * HEAD_DIM),
        "bdk": _head_mean_matrix(N_KV * HEAD_DIM),
        "tile": _tile_matrices(),
    }
    tables["cos"], tables["sin"] = _rope_tables(s)
    tables["cos_c"] = jnp.ones((lc, N_HEADS * HEAD_DIM), F32)
    tables["sin_c"] = jnp.zeros((lc, N_HEADS * HEAD_DIM), F32)
    tri = (jnp.arange(MOE_TM)[:, None] > jnp.arange(MOE_TM)[None, :]).astype(BF16)
    upper = (jnp.arange(LANES)[:, None] < jnp.arange(LANES)[None, :]).astype(BF16)

    qw, kw = N_HEADS * HEAD_DIM, N_KV * HEAD_DIM
    perm = jnp.concatenate([jnp.arange(0, qw), jnp.arange(qw + 2 * kw, qw + 2 * kw + 3 * half),
                            jnp.arange(qw, qw + 2 * kw)])

    w1_all, w3_all, w2_all = exp_w_gate.astype(BF16), exp_w_up.astype(BF16), exp_w_down.astype(BF16)
    xc = ctx
    for l in range(depth):
        last = l == depth - 1
        even = l % 2 == 0
        need_ctx = not (last and even)
        pieces = [mod_all[l, :, j * d:(j + 1) * d] for j in range(6)]
        mod = [p[:b].reshape(b, 1, d) for p in pieces]
        modc = [jnp.broadcast_to(p[b].reshape(1, 1, d), (b, 1, d)) for p in pieces]
        g1 = norm1_g[l].reshape(1, d)
        g2 = norm2_g[l].reshape(1, d)
        if even:
            e = l // 2
            bs = jnp.broadcast_to(ev_b_s[e][:, :, None], (n_groups, CHUNK, LANES))
            y, yc = _even_layer(x, xc if (need_ctx and not last) else None, mod, modc, g1,
                                ev_w_in[e].astype(BF16), ev_w_s[e].astype(BF16), bs,
                                ev_w_out[e].astype(BF16), tables)
        else:
            o = l // 2
            qg = (jnp.tile(od_q_norm_g[o], N_HEADS) * (HEAD_DIM ** -0.5)).reshape(1, qw)
            kg = jnp.tile(od_k_norm_g[o], N_KV).reshape(1, kw)
            y, yc = _odd_layer(x, xc, mod, modc, g1, od_w_in[o][:, perm].astype(BF16), qg, kg,
                               od_sink[o], od_conv_w[o], od_w_out[o].astype(BF16), tables, not last)
        x = y
        if not last:
            xc = yc
        rw = jnp.zeros((d, LANES), F32).at[:, :N_EXPERTS].set(router_w[l])
        rw_hi = rw.astype(BF16)
        rw_lo = (rw - rw_hi.astype(F32)).astype(BF16)
        rb = jnp.zeros((1, LANES), F32).at[0, :N_EXPERTS].set(router_b[l])
        x, xc_new = _moe(x, None if last else xc, mod, modc, g2, rw_hi, rw_lo, rb, w1_all, w3_all, w2_all, l,
                         sh_w_gate[l].astype(BF16), sh_w_up[l].astype(BF16), sh_w_down[l].astype(BF16),
                         tri, upper)
        if not last:
            xc = xc_new
    return x
```

```python
import functools
import math

import jax
import jax.numpy as jnp
from jax import lax
from jax.experimental import pallas as pl
from jax.experimental.pallas import tpu as pltpu

F32 = jnp.float32
BF16 = jnp.bfloat16
I32 = jnp.int32
U32 = jnp.uint32

LANES = 128
VMEM_LIMIT = 48 * 2**20

EPS = 1e-6
GRID_W = 64
CHUNK = 128
HEAD_DIM = 64
N_HEADS = 8
N_KV = 2
Q_PER_KV = N_HEADS // N_KV
ATT_BLOCK = 128
ROPE_BASE = 10000.0
N_EXPERTS = 64
TOP_K = 6
ROUTED_SCALE = 2.5
EXPERT_BLOCK = 512
EXPERT_GROUP = 2
MOE_TM = 1024
ROW_PARTS = 4
SLAB_IDX, SLAB_LOC, SLAB_GATE = 0, 8, 16
TAB_WORDS = 1024


def _params(*sem):
    return pltpu.CompilerParams(dimension_semantics=sem, vmem_limit_bytes=VMEM_LIMIT)


def _sigmoid(x):
    return 1.0 / (1.0 + jnp.exp(-x))


def _silu(x):
    return x * _sigmoid(x)


def _gelu_tanh(x):
    c = math.sqrt(2.0 / math.pi)
    return x * (0.5 * (1.0 + jnp.tanh(c * (x + 0.044715 * (x * x * x)))))


def _dot(a, b):
    return jnp.dot(a, b, preferred_element_type=F32)


def _unpack_words(w):
    return pltpu.bitcast(w << 16, F32), pltpu.bitcast(w & jnp.uint32(0xFFFF0000), F32)


def _row_chunks(ref, row0=0, n=None):
    total, parts, lanes = ref.shape
    n = total if n is None else n
    flat = ref.reshape(total * parts, lanes)
    return [flat[pl.ds(row0 * parts + c, n, stride=parts), :] for c in range(parts)]


def _unpack_rows(ref, row0=0, n=None):
    halves = [_unpack_words(w) for w in _row_chunks(ref, row0, n)]
    return jnp.concatenate([h[0] for h in halves] + [h[1] for h in halves], axis=-1)


def _pack_rows(ref, val, row0=0):
    total, parts, lanes = ref.shape
    n, half = val.shape[0], val.shape[1] // 2
    bits = pltpu.bitcast(val.astype(BF16).astype(F32), U32)
    words = (bits[:, :half] >> 16) | (bits[:, half:] & jnp.uint32(0xFFFF0000))
    flat = ref.reshape(total * parts, lanes)
    for c in range(parts):
        flat[pl.ds(row0 * parts + c, n, stride=parts), :] = words[:, c * lanes:(c + 1) * lanes]


def _adaln_kernel(c_ref, w_ref, b_ref, o_ref):
    o_ref[...] = _dot(_silu(c_ref[...]), w_ref[...]) + b_ref[...]


def _adaln(cond, ada_w, ada_b):
    n_layers, d, n6 = ada_w.shape
    rows = cond.shape[0]
    tn = 768
    return pl.pallas_call(
        _adaln_kernel,
        grid=(n_layers, n6 // tn),
        in_specs=[pl.BlockSpec((rows, d), lambda l, j: (0, 0)),
                  pl.BlockSpec((None, d, tn), lambda l, j: (l, 0, j)),
                  pl.BlockSpec((None, 1, tn), lambda l, j: (l, 0, j))],
        out_specs=pl.BlockSpec((None, rows, tn), lambda l, j: (l, 0, j)),
        out_shape=jax.ShapeDtypeStruct((n_layers, rows, n6), F32),
        compiler_params=_params("parallel", "parallel"),
        name="adaln",
    )(cond, ada_w, ada_b.reshape(n_layers, 1, n6))


def _norm_mod(x, g, shift, scale):
    ms = jnp.mean(x * x, axis=-1, keepdims=True)
    h = (x * lax.rsqrt(ms + EPS)) * g
    return h * (1.0 + scale) + shift


def _nmm_kernel(x_ref, g_ref, sh_ref, sc_ref, w_ref, o_ref):
    h = _norm_mod(x_ref[...], g_ref[...], sh_ref[...], sc_ref[...])
    o_ref[...] = _dot(h.astype(BF16), w_ref[...]).astype(o_ref.dtype)


def _norm_mod_matmul(x, g, shift, scale, w, tm):
    b, s, d = x.shape
    n = w.shape[1]
    return pl.pallas_call(
        _nmm_kernel,
        grid=(b, s // tm),
        in_specs=[pl.BlockSpec((None, tm, d), lambda bi, i: (bi, i, 0)),
                  pl.BlockSpec((1, d), lambda bi, i: (0, 0)),
                  pl.BlockSpec((None, 1, d), lambda bi, i: (bi, 0, 0)),
                  pl.BlockSpec((None, 1, d), lambda bi, i: (bi, 0, 0)),
                  pl.BlockSpec((d, n), lambda bi, i: (0, 0))],
        out_specs=pl.BlockSpec((None, tm, n), lambda bi, i: (bi, i, 0)),
        out_shape=jax.ShapeDtypeStruct((b, s, n), BF16),
        compiler_params=_params("parallel", "parallel"),
        name="norm_mod_matmul",
    )(x, g, shift, scale, w)


def _even_mix_kernel(u_ref, v_ref, f_ref, ws_ref, bs_ref, cs_ref, gm_ref, z_ref, *, n_chunks, n_groups):
    for c in range(n_chunks):
        rows = slice(c * CHUNK, (c + 1) * CHUNK)
        for g in range(n_groups):
            cols = slice(g * LANES, (g + 1) * LANES)
            ug = _gelu_tanh(u_ref[rows, cols].astype(F32))
            vg = _gelu_tanh(v_ref[rows, cols].astype(F32))
            mu = jnp.mean(vg, axis=-1, keepdims=True)
            dv = vg - mu
            var = jnp.mean(dv * dv, axis=-1, keepdims=True)
            vn = dv * lax.rsqrt(var + 1e-5)
            fg = _dot(ws_ref[g], vn.astype(BF16)) + bs_ref[g]
            gm_ref[rows, cols] = (ug * fg).astype(gm_ref.dtype)
    fz = _dot(f_ref[...].astype(BF16), cs_ref[...])
    half = fz.shape[1] // 2
    z_ref[0] = fz[:, :half].astype(z_ref.dtype)
    z_ref[1] = fz[:, half:].astype(z_ref.dtype)


def _even_mix(uvf, ws, bs, cs, tm):
    b, s, n3 = uvf.shape
    w = n3 // 3
    n_groups = w // LANES
    kern = functools.partial(_even_mix_kernel, n_chunks=tm // CHUNK, n_groups=n_groups)
    return pl.pallas_call(
        kern,
        grid=(b, s // tm),
        in_specs=[pl.BlockSpec((None, tm, w), lambda bi, i: (bi, i, 0)),
                  pl.BlockSpec((None, tm, w), lambda bi, i: (bi, i, 1)),
                  pl.BlockSpec((None, tm, w), lambda bi, i: (bi, i, 2)),
                  pl.BlockSpec(ws.shape, lambda bi, i: (0, 0, 0)),
                  pl.BlockSpec(bs.shape, lambda bi, i: (0, 0, 0)),
                  pl.BlockSpec(cs.shape, lambda bi, i: (0, 0))],
        out_specs=[pl.BlockSpec((None, tm, w), lambda bi, i: (bi, i, 0)),
                   pl.BlockSpec((2, tm, w), lambda bi, i: (0, i, bi))],
        out_shape=[jax.ShapeDtypeStruct((b, s, w), BF16),
                   jax.ShapeDtypeStruct((2, s, b * w), BF16)],
        compiler_params=_params("parallel", "parallel"),
        name="even_mix",
    )(uvf, uvf, uvf, ws, bs, cs)


def _mm_kernel(a_ref, b_ref, o_ref, acc_ref):
    k = pl.program_id(2)

    @pl.when(k == 0)
    def _():
        acc_ref[...] = jnp.zeros_like(acc_ref)

    acc_ref[...] += _dot(a_ref[...], b_ref[...])

    @pl.when(k == pl.num_programs(2) - 1)
    def _():
        o_ref[...] = acc_ref[...].astype(o_ref.dtype)


def _matmul(a, b, tm, tn, tk, out_dtype):
    m, kd = a.shape
    n = b.shape[1]
    return pl.pallas_call(
        _mm_kernel,
        grid=(m // tm, n // tn, kd // tk),
        in_specs=[pl.BlockSpec((tm, tk), lambda i, j, k: (i, k)),
                  pl.BlockSpec((tk, tn), lambda i, j, k: (k, j))],
        out_specs=pl.BlockSpec((tm, tn), lambda i, j, k: (i, j)),
        out_shape=jax.ShapeDtypeStruct((m, n), out_dtype),
        scratch_shapes=[pltpu.VMEM((tm, tn), F32)],
        compiler_params=_params("parallel", "parallel", "arbitrary"),
        name="dft_matmul",
    )(a, b)


def _outproj_kernel(a_ref, b_ref, w_ref, x_ref, gate_ref, o_ref):
    ab = jnp.concatenate([a_ref[...], b_ref[...]], axis=-1)
    o_ref[...] = x_ref[...] + gate_ref[...] * _dot(ab, w_ref[...])


def _outproj(a, a_spec, bsrc, b_spec, w, x, gate, tm):
    b, s, d = x.shape
    return pl.pallas_call(
        _outproj_kernel,
        grid=(b, s // tm),
        in_specs=[a_spec, b_spec,
                  pl.BlockSpec(w.shape, lambda bi, i: (0, 0)),
                  pl.BlockSpec((None, tm, d), lambda bi, i: (bi, i, 0)),
                  pl.BlockSpec((None, 1, d), lambda bi, i: (bi, 0, 0))],
        out_specs=pl.BlockSpec((None, tm, d), lambda bi, i: (bi, i, 0)),
        out_shape=jax.ShapeDtypeStruct((b, s, d), F32),
        compiler_params=_params("parallel", "parallel"),
        name="outproj",
    )(a, bsrc, w, x, gate)


def _head_rms(x, bd_ref):
    xx = x * x
    hi = xx.astype(BF16)
    lo = (xx - hi.astype(F32)).astype(BF16)
    ms = _dot(hi, bd_ref[...]) + _dot(lo, bd_ref[...])
    return x * lax.rsqrt(ms + EPS)


def _rope(x, cos, sins):
    width = x.shape[1]
    lane = lax.broadcasted_iota(I32, x.shape, 1)
    first = (lane & 31) < 16
    swapped = jnp.where(first, pltpu.roll(x, width - 16, 1), pltpu.roll(x, 16, 1))
    return x * cos + swapped * sins


def _qkv_kernel(q_ref, k_ref, v_ref, cos_ref, sin_ref, qg_ref, kg_ref, bdq_ref, bdk_ref, tile_ref,
                qo_ref, k4_ref, v4_ref, *, with_q):
    kw = k_ref.shape[1]
    if with_q:
        qn = _head_rms(q_ref[...].astype(F32), bdq_ref) * qg_ref[...]
        qo_ref[...] = _rope(qn, cos_ref[...], sin_ref[...]).astype(qo_ref.dtype)
    else:
        qo_ref[...] = jnp.zeros_like(qo_ref)
    kn = _head_rms(k_ref[...].astype(F32), bdk_ref) * kg_ref[...]
    kr = _rope(kn, cos_ref[:, :kw], sin_ref[:, :kw]).astype(BF16)
    vb = v_ref[...].astype(BF16)
    for h in range(N_KV):
        k4_ref[h] = _dot(kr, tile_ref[h]).astype(k4_ref.dtype)
        v4_ref[h] = _dot(vb, tile_ref[h]).astype(v4_ref.dtype)


def _qkv_prep(proj, cos, sin, qg, kg, bdq, bdk, tile, tm, with_q):
    b, s, _ = proj.shape
    qw = N_HEADS * HEAD_DIM
    kw = N_KV * HEAD_DIM
    rep = Q_PER_KV * HEAD_DIM
    k_blk = (4 * qw) // kw
    kern = functools.partial(_qkv_kernel, with_q=with_q)
    return pl.pallas_call(
        kern,
        grid=(b, s // tm),
        in_specs=[pl.BlockSpec((None, tm, qw), lambda bi, i: (bi, i, 0)),
                  pl.BlockSpec((None, tm, kw), lambda bi, i: (bi, i, k_blk)),
                  pl.BlockSpec((None, tm, kw), lambda bi, i: (bi, i, k_blk + 1)),
                  pl.BlockSpec((tm, qw), lambda bi, i: (i, 0)),
                  pl.BlockSpec((tm, qw), lambda bi, i: (i, 0)),
                  pl.BlockSpec((1, qw), lambda bi, i: (0, 0)),
                  pl.BlockSpec((1, kw), lambda bi, i: (0, 0)),
                  pl.BlockSpec(bdq.shape, lambda bi, i: (0, 0)),
                  pl.BlockSpec(bdk.shape, lambda bi, i: (0, 0)),
                  pl.BlockSpec(tile.shape, lambda bi, i: (0, 0, 0))],
        out_specs=[pl.BlockSpec((None, tm, qw), lambda bi, i: (bi, i, 0)),
                   pl.BlockSpec((None, N_KV, tm, rep), lambda bi, i: (bi, 0, i, 0)),
                   pl.BlockSpec((None, N_KV, tm, rep), lambda bi, i: (bi, 0, i, 0))],
        out_shape=[jax.ShapeDtypeStruct((b, s, qw), BF16),
                   jax.ShapeDtypeStruct((b, N_KV, s, rep), BF16),
                   jax.ShapeDtypeStruct((b, N_KV, s, rep), BF16)],
        compiler_params=_params("parallel", "parallel"),
        name="qkv_prep",
    )(proj, proj, proj, cos, sin, qg, kg, bdq, bdk, tile)


def _conv_kernel(gi_ref, go_ref, z_ref, w_ref, o_ref, *, rows):
    s = z_ref.shape[0]
    n = s // rows
    w0, w1, w2 = w_ref[0:1, :], w_ref[1:2, :], w_ref[2:3, :]
    ridx = lax.broadcasted_iota(I32, (rows, z_ref.shape[1]), 0)
    prev_last = jnp.zeros((1, z_ref.shape[1]), F32)

    def load(ref, start, count):
        return ref[start:start + count, :].astype(F32)

    for j in range(n):
        r0 = j * rows
        zc = load(gi_ref, r0, rows) * load(z_ref, r0, rows)
        if j + 1 < n:
            nxt = (load(gi_ref, r0 + rows, 16) * load(z_ref, r0 + rows, 16))[0:1, :]
        else:
            nxt = jnp.zeros_like(prev_last)
        zp = jnp.where(ridx == 0, prev_last, pltpu.roll(zc, 1, 0))
        zn = jnp.where(ridx == rows - 1, nxt, pltpu.roll(zc, rows - 1, 0))
        y = zp * w0 + zc * w1 + zn * w2
        o_ref[r0:r0 + rows, :] = (load(go_ref, r0, rows) * y).astype(o_ref.dtype)
        prev_last = zc[rows - 1:rows, :]


def _short_conv(proj, conv_w):
    b, s, _ = proj.shape
    dc = conv_w.shape[1]
    nb = dc // LANES
    base = dc // LANES
    rows = min(s, 512)
    kern = functools.partial(_conv_kernel, rows=rows)
    return pl.pallas_call(
        kern,
        grid=(b, nb),
        in_specs=[pl.BlockSpec((None, s, LANES), lambda bi, c: (bi, 0, base + c)),
                  pl.BlockSpec((None, s, LANES), lambda bi, c: (bi, 0, 2 * base + c)),
                  pl.BlockSpec((None, s, LANES), lambda bi, c: (bi, 0, 3 * base + c)),
                  pl.BlockSpec((conv_w.shape[0], LANES), lambda bi, c: (0, c))],
        out_specs=pl.BlockSpec((None, s, LANES), lambda bi, c: (bi, 0, c)),
        out_shape=jax.ShapeDtypeStruct((b, s, dc), BF16),
        compiler_params=_params("parallel", "parallel"),
        name="short_conv",
    )(proj, proj, proj, conv_w)


def _attn_kernel(sink_ref, q_ref, *refs, band, n_blocks):
    o_ref = refs[-1]
    i = pl.program_id(1)
    t = q_ref.shape[0]
    w = Q_PER_KV * HEAD_DIM
    lane = lax.broadcasted_iota(I32, (t, w), 1)
    masks = [(lane >= g * HEAD_DIM) & (lane < (g + 1) * HEAD_DIM) for g in range(Q_PER_KV)]
    bad = None
    if band:
        n_keys = 3 * t + refs[6].shape[1]
        row = lax.broadcasted_iota(I32, (Q_PER_KV * t, n_keys), 0) & (t - 1)
        col = lax.broadcasted_iota(I32, (Q_PER_KV * t, n_keys), 1)
        off_prev = jnp.where(i > 0, 0, 4 * t)
        off_next = jnp.where(i < n_blocks - 1, 0, 4 * t)
        bad_prev = (col < t) & (col < row + off_prev)
        bad_next = (col >= 2 * t) & (col < 3 * t) & (col - 2 * t > row - off_next)
        bad = bad_prev | bad_next
    for h in range(N_KV):
        if band:
            kp, kc_, kn, vp, vc_, vn, kx, vx = refs[:-1]
            kcat = jnp.concatenate([kp[h], kc_[h], kn[h], kx[h]], axis=0)
            vcat = jnp.concatenate([vp[h], vc_[h], vn[h], vx[h]], axis=0)
        else:
            kx, vx = refs[:-1]
            kcat, vcat = kx[h], vx[h]
        q = q_ref[:, h * w:(h + 1) * w]
        q4 = jnp.concatenate([jnp.where(m, q, jnp.zeros_like(q)) for m in masks], axis=0)
        s = lax.dot_general(q4, kcat, (((1,), (1,)), ((), ())), preferred_element_type=F32)
        if band:
            s = jnp.where(bad, -jnp.inf, s)
        sink = jnp.concatenate(
            [jnp.full((t, 1), sink_ref[h * Q_PER_KV + g], F32) for g in range(Q_PER_KV)], axis=0)
        m = jnp.maximum(jnp.max(s, axis=-1, keepdims=True), sink)
        e = jnp.exp(s - m)
        den = jnp.sum(e, axis=-1, keepdims=True) + jnp.exp(sink - m)
        r = _dot(e.astype(BF16), vcat) * (1.0 / den)
        o = jnp.zeros((t, w), F32)
        for g in range(Q_PER_KV):
            o = o + jnp.where(masks[g], r[g * t:(g + 1) * t, :], 0.0)
        o_ref[:, h * w:(h + 1) * w] = o.astype(o_ref.dtype)


def _attention(q, k4, v4, kx4, vx4, sink, band):
    b, s, qw = q.shape
    rep = k4.shape[-1] if band else kx4.shape[-1]
    lc = kx4.shape[2]
    t = ATT_BLOCK
    nb = s // t
    kern = functools.partial(_attn_kernel, band=band, n_blocks=nb)

    def kv_spec(off):
        return pl.BlockSpec((None, N_KV, t, rep),
                            lambda bi, i: (bi, 0, jnp.clip(i + off, 0, nb - 1), 0))

    ctx_spec = pl.BlockSpec((None, N_KV, lc, rep), lambda bi, i: (bi, 0, 0, 0))
    in_specs = [pl.BlockSpec(memory_space=pltpu.SMEM),
                pl.BlockSpec((None, t, qw), lambda bi, i: (bi, i, 0))]
    args = [sink, q]
    if band:
        in_specs += [kv_spec(-1), kv_spec(0), kv_spec(1), kv_spec(-1), kv_spec(0), kv_spec(1)]
        args += [k4, k4, k4, v4, v4, v4]
    in_specs += [ctx_spec, ctx_spec]
    args += [kx4, vx4]
    return pl.pallas_call(
        kern,
        grid=(b, nb),
        in_specs=in_specs,
        out_specs=pl.BlockSpec((None, t, qw), lambda bi, i: (bi, i, 0)),
        out_shape=jax.ShapeDtypeStruct((b, s, qw), BF16),
        compiler_params=_params("parallel", "parallel"),
        name="attention_band" if band else "attention_ctx",
    )(*args)


def _route_kernel(x_ref, g_ref, sh_ref, sc_ref, whi_ref, wlo_ref, rb_ref, cin_ref, tri_ref, upper_ref,
                  h_ref, slab_ref, tab_ref, cnt_ref, carry_ref):
    first = (pl.program_id(0) == 0) & (pl.program_id(1) == 0)

    @pl.when(first)
    def _():
        carry_ref[...] = cin_ref[...]

    h = _norm_mod(x_ref[...], g_ref[...], sh_ref[...], sc_ref[...])
    _pack_rows(h_ref, h)
    hi = h.astype(BF16)
    lo = (h - hi.astype(F32)).astype(BF16)
    logits = _dot(hi, whi_ref[...]) + _dot(lo, whi_ref[...]) + _dot(hi, wlo_ref[...])
    scores = _sigmoid(logits)
    tm, lanes = scores.shape
    lane = lax.broadcasted_iota(I32, (tm, lanes), 1).astype(F32)
    work = jnp.where(lane < N_EXPERTS, scores + rb_ref[...], -jnp.inf)
    hits, idxs, gates = [], [], []
    for _ in range(TOP_K):
        mx = jnp.max(work, axis=-1, keepdims=True)
        idx = jnp.min(jnp.where(work == mx, lane, float(lanes)), axis=-1, keepdims=True)
        hit = lane == idx
        hits.append(hit)
        idxs.append(idx)
        gates.append(jnp.sum(jnp.where(hit, scores, 0.0), axis=-1, keepdims=True))
        work = jnp.where(hit, -jnp.inf, work)
    gsum = gates[0]
    for gk in gates[1:]:
        gsum = gsum + gk
    gscale = ROUTED_SCALE / (gsum + 1e-20)
    onehot = jnp.zeros((tm, lanes), F32)
    for hit in hits:
        onehot = jnp.where(hit, 1.0, onehot)
    cnt = jnp.sum(onehot, axis=0, keepdims=True)
    cnt_hi = jnp.floor(cnt * (1.0 / 256.0))
    cnt_lo = cnt - 256.0 * cnt_hi
    parts = jnp.concatenate([jnp.broadcast_to(cnt_hi, (8, lanes)), jnp.broadcast_to(cnt_lo, (8, lanes))], axis=0)
    sums = _dot(parts.astype(BF16), upper_ref[...])
    seg = 256.0 * sums[0:1] + sums[8:9]
    before = _dot(tri_ref[...], onehot.astype(BF16)) + seg
    slab = jnp.zeros((tm, lanes), F32)
    for k in range(TOP_K):
        loc = jnp.sum(jnp.where(hits[k], before, 0.0), axis=-1, keepdims=True)
        slab = jnp.where(lane == SLAB_IDX + k, idxs[k], slab)
        slab = jnp.where(lane == SLAB_LOC + k, loc, slab)
        slab = jnp.where(lane == SLAB_GATE + k, gates[k] * gscale, slab)
    slab_ref[...] = slab
    row = lax.broadcasted_iota(I32, (8, lanes), 0)
    tab_ref[...] = jnp.where(row == 0, cnt, jnp.where(row == 1, seg, jnp.where(row == 2, carry_ref[...], 0.0)))
    carry_ref[...] = carry_ref[...] + cnt
    cnt_ref[...] = carry_ref[...]


def _route(x, g, shift, scale, whi, wlo, rb, counts_in, tri, upper):
    b, s, d = x.shape
    tm = MOE_TM
    nt = s // tm
    return pl.pallas_call(
        _route_kernel,
        grid=(b, nt),
        in_specs=[pl.BlockSpec((None, tm, d), lambda bi, i: (bi, i, 0)),
                  pl.BlockSpec((1, d), lambda bi, i: (0, 0)),
                  pl.BlockSpec((None, 1, d), lambda bi, i: (bi, 0, 0)),
                  pl.BlockSpec((None, 1, d), lambda bi, i: (bi, 0, 0)),
                  pl.BlockSpec(whi.shape, lambda bi, i: (0, 0)),
                  pl.BlockSpec(wlo.shape, lambda bi, i: (0, 0)),
                  pl.BlockSpec((1, LANES), lambda bi, i: (0, 0)),
                  pl.BlockSpec((1, LANES), lambda bi, i: (0, 0)),
                  pl.BlockSpec((tm, tm), lambda bi, i: (0, 0)),
                  pl.BlockSpec((LANES, LANES), lambda bi, i: (0, 0))],
        out_specs=[pl.BlockSpec((None, tm, ROW_PARTS, LANES), lambda bi, i: (bi, i, 0, 0)),
                   pl.BlockSpec((None, tm, LANES), lambda bi, i: (bi, i, 0)),
                   pl.BlockSpec((None, 8, LANES), lambda bi, i: (bi * nt + i, 0, 0)),
                   pl.BlockSpec((1, LANES), lambda bi, i: (0, 0))],
        out_shape=[jax.ShapeDtypeStruct((b, s, ROW_PARTS, LANES), U32),
                   jax.ShapeDtypeStruct((b, s, LANES), F32),
                   jax.ShapeDtypeStruct((b * nt, 8, LANES), F32),
                   jax.ShapeDtypeStruct((1, LANES), F32)],
        scratch_shapes=[pltpu.VMEM((1, LANES), F32)],
        compiler_params=_params("arbitrary", "arbitrary"),
        name="route",
    )(x, g, shift, scale, whi, wlo, rb, counts_in, tri, upper)


def _rows(ref, row, n):
    return ref.at[pl.ds(row, n)]


def _run_copies(tab_smem, base, stage_ref, far_ref, sem, to_far):
    n_bits = MOE_TM.bit_length()
    common = n_bits - 3

    def per_expert(e, carry):
        n = tab_smem[base + e]
        near = tab_smem[base + N_EXPERTS + e]
        far = tab_smem[base + 2 * N_EXPERTS + e]

        def piece(bit):
            size = 1 << bit

            @pl.when((n & size) != 0)
            def _():
                done = n & (size - 1)
                a, b = _rows(stage_ref, near + done, size), _rows(far_ref, far + done, size)
                (pltpu.make_async_copy(a, b, sem) if to_far else pltpu.make_async_copy(b, a, sem)).start()

        for bit in range(common):
            piece(bit)

        @pl.when(n >= (1 << common))
        def _():
            for bit in range(common, n_bits):
                piece(bit)
        return carry

    lax.fori_loop(0, N_EXPERTS, per_expert, 0)


def _wait_tile(stage_ref, far_ref, sem, to_far):
    n = TOP_K * MOE_TM
    a, b = _rows(stage_ref, 0, n), _rows(far_ref, 0, n)
    (pltpu.make_async_copy(a, b, sem) if to_far else pltpu.make_async_copy(b, a, sem)).wait()


def _dispatch_kernel(ends_ref, nu_ref, h_ref, loc_ref, tab_ref, *rest, zero_fill, tile_base, n_blocks):
    if zero_fill:
        xs_ref, loc_smem, tab_smem, stages_ref, zero_ref, sems, psem = rest
    else:
        _, xs_ref, loc_smem, tab_smem, stages_ref, sems, psem = rest
        zero_ref = None
    tm = MOE_TM
    i = pl.program_id(0)
    n_steps = pl.num_programs(0)
    slot = i % 2
    stage_ref, sem = stages_ref.at[slot], sems.at[slot]
    loc_copy = pltpu.make_async_copy(loc_ref.at[pl.ds((tile_base + i) * tm * 8, tm * 8)], loc_smem, psem)
    tab_copy = pltpu.make_async_copy(tab_ref.at[pl.ds((tile_base + i) * TAB_WORDS, TAB_WORDS)], tab_smem, psem)
    loc_copy.start()
    tab_copy.start()

    if zero_fill:
        @pl.when(i == 0)
        def _():
            zero_ref[...] = jnp.zeros_like(zero_ref)

            def block_copy(blk):
                return pltpu.make_async_copy(zero_ref, _rows(xs_ref, blk * EXPERT_BLOCK, EXPERT_BLOCK), sem)

            def fill(e, carry, *, start):
                end = ends_ref[e]
                prev = jnp.where(e > 0, ends_ref[jnp.maximum(e - 1, 0)], 0)

                @pl.when(end > prev)
                def _():
                    cp = block_copy(end // EXPERT_BLOCK - 1)
                    cp.start() if start else cp.wait()
                return carry

            def tail(j, carry, *, start):
                cp = block_copy(j)
                cp.start() if start else cp.wait()
                return carry

            lax.fori_loop(0, N_EXPERTS, functools.partial(fill, start=True), 0)
            lax.fori_loop(nu_ref[0], n_blocks, functools.partial(tail, start=True), 0)
            lax.fori_loop(0, N_EXPERTS, functools.partial(fill, start=False), 0)
            lax.fori_loop(nu_ref[0], n_blocks, functools.partial(tail, start=False), 0)

    loc_copy.wait()
    tab_copy.wait()

    @pl.when(i >= 2)
    def _():
        _wait_tile(stage_ref, xs_ref, sem, to_far=True)

    flat = stage_ref.reshape(TOP_K * tm * ROW_PARTS, LANES)
    h_flat = h_ref.reshape(tm * ROW_PARTS, LANES)

    def place(t, c):
        row = h_flat[pl.ds(pl.multiple_of(t * ROW_PARTS, ROW_PARTS), ROW_PARTS), :]
        for k in range(TOP_K):
            flat[pl.ds(pl.multiple_of(loc_smem[t * 8 + k], ROW_PARTS), ROW_PARTS), :] = row
        return c

    lax.fori_loop(0, tm, place, 0, unroll=4)
    _run_copies(tab_smem, 0, stage_ref, xs_ref, sem, to_far=True)

    @pl.when(i == n_steps - 1)
    def _():
        _wait_tile(stage_ref, xs_ref, sem, to_far=True)

        @pl.when(i >= 1)
        def _():
            _wait_tile(stages_ref.at[1 - slot], xs_ref, sems.at[1 - slot], to_far=True)


def _dispatch(h2, loc_flat, tab_flat, ends, n_used, xs_prev, n_rows, tile_base):
    n = h2.shape[0]
    tm = MOE_TM
    zero_fill = xs_prev is None
    kern = functools.partial(_dispatch_kernel, zero_fill=zero_fill, tile_base=tile_base,
                             n_blocks=n_rows // EXPERT_BLOCK)
    in_specs = [pl.BlockSpec((tm, ROW_PARTS, LANES), lambda i, e, nu: (i, 0, 0)),
                pl.BlockSpec(memory_space=pl.ANY),
                pl.BlockSpec(memory_space=pl.ANY)]
    args = [ends, n_used, h2, loc_flat, tab_flat]
    scratch = [pltpu.SMEM((tm * 8,), I32), pltpu.SMEM((TAB_WORDS,), I32),
               pltpu.VMEM((2, TOP_K * tm, ROW_PARTS, LANES), U32)]
    aliases = {}
    if zero_fill:
        scratch.append(pltpu.VMEM((EXPERT_BLOCK, ROW_PARTS, LANES), U32))
    else:
        in_specs.append(pl.BlockSpec(memory_space=pl.ANY))
        args.append(xs_prev)
        aliases = {5: 0}
    scratch += [pltpu.SemaphoreType.DMA((2,)), pltpu.SemaphoreType.DMA]
    return pl.pallas_call(
        kern,
        grid_spec=pltpu.PrefetchScalarGridSpec(
            num_scalar_prefetch=2,
            grid=(n // tm,),
            in_specs=in_specs,
            out_specs=pl.BlockSpec(memory_space=pl.ANY),
            scratch_shapes=scratch),
        out_shape=jax.ShapeDtypeStruct((n_rows, ROW_PARTS, LANES), U32),
        input_output_aliases=aliases,
        compiler_params=_params("arbitrary"),
        name="dispatch",
    )(*args)


def _expert_kernel(be_ref, nu_ref, x_ref, *refs):
    o_ref = refs[-1]
    blk = EXPERT_BLOCK
    used = pl.program_id(0) * EXPERT_GROUP < nu_ref[0]

    @pl.when(used)
    def _():
        for j in range(EXPERT_GROUP):
            w1_ref, w3_ref, w2_ref = refs[3 * j:3 * j + 3]
            x = _unpack_rows(x_ref, j * blk, blk).astype(BF16)
            a = _silu(_dot(x, w1_ref[...])) * _dot(x, w3_ref[...])
            _pack_rows(o_ref, _dot(a.astype(BF16), w2_ref[...]), j * blk)

    @pl.when(jnp.logical_not(used))
    def _():
        o_ref[...] = jnp.zeros_like(o_ref)


def _experts(xs, block_e, n_used, w1, w3, w2, layer):
    d, de = w1.shape[2:]
    rows = EXPERT_BLOCK * EXPERT_GROUP
    n_steps = xs.shape[0] // rows

    def row_map(i, be, nu):
        return (jnp.minimum(i, (nu[0] + EXPERT_GROUP - 1) // EXPERT_GROUP - 1), 0, 0)

    w_specs = []
    for j in range(EXPERT_GROUP):
        w_map = lambda i, be, nu, j=j: (layer, be[i * EXPERT_GROUP + j], 0, 0)
        w_specs += [pl.BlockSpec((None, None, d, de), w_map),
                    pl.BlockSpec((None, None, d, de), w_map),
                    pl.BlockSpec((None, None, de, d), w_map)]

    return pl.pallas_call(
        _expert_kernel,
        grid_spec=pltpu.PrefetchScalarGridSpec(
            num_scalar_prefetch=2,
            grid=(n_steps,),
            in_specs=[pl.BlockSpec((rows, ROW_PARTS, LANES), row_map)] + w_specs,
            out_specs=pl.BlockSpec((rows, ROW_PARTS, LANES), lambda i, be, nu: (i, 0, 0))),
        out_shape=jax.ShapeDtypeStruct(xs.shape, U32),
        compiler_params=_params("arbitrary"),
        name="experts",
    )(block_e, n_used, xs, *([w1, w3, w2] * EXPERT_GROUP))


def _combine_kernel(loc_ref, tab_ref, ys_ref, slab_ref, h_ref, x_ref, gate_ref, s1_ref, s3_ref, s2_ref,
                    o_ref, loc_smem, tab_smem, gk_ref, stage_ref, lo_ref, hi_ref, sem, psem, *, tile_base):
    tm = MOE_TM
    words = tm * 8
    tile = tile_base + pl.program_id(0) * pl.num_programs(1) + pl.program_id(1)
    small = [pltpu.make_async_copy(loc_ref.at[pl.ds(tile * words, words)], loc_smem, psem),
             pltpu.make_async_copy(tab_ref.at[pl.ds(tile * TAB_WORDS, TAB_WORDS)], tab_smem, psem)]
    for cp in small:
        cp.start()
    for cp in small:
        cp.wait()
    _run_copies(tab_smem, 0, stage_ref, ys_ref, sem, to_far=False)
    hb = _unpack_rows(h_ref).astype(BF16)
    shared = _dot((_silu(_dot(hb, s1_ref[...])) * _dot(hb, s3_ref[...])).astype(BF16), s2_ref[...])
    slab = slab_ref[...]
    for k in range(TOP_K):
        gk_ref[k] = jnp.broadcast_to(slab[:, SLAB_GATE + k:SLAB_GATE + k + 1], (tm, LANES))
    _wait_tile(stage_ref, ys_ref, sem, to_far=False)

    flat = stage_ref.reshape(TOP_K * tm * ROW_PARTS, LANES)

    def mix(t, c):
        lo = hi = None
        for k in range(TOP_K):
            g = gk_ref[k, pl.ds(t, ROW_PARTS, stride=0), :]
            at = pl.multiple_of(loc_smem[t * 8 + k], ROW_PARTS)
            wl, wh = _unpack_words(flat[pl.ds(at, ROW_PARTS), :])
            lo = g * wl if lo is None else lo + g * wl
            hi = g * wh if hi is None else hi + g * wh
        lo_ref[t] = lo
        hi_ref[t] = hi
        return c

    lax.fori_loop(0, tm, mix, 0, unroll=4)
    routed = jnp.concatenate(_row_chunks(lo_ref) + _row_chunks(hi_ref), axis=-1)
    o_ref[...] = x_ref[...] + gate_ref[...] * (routed + shared)


def _combine(ys, loc_flat, tab_flat, slab, h2, x, gate, s1, s3, s2, tile_base):
    b, s, d = x.shape
    tm = MOE_TM
    kern = functools.partial(_combine_kernel, tile_base=tile_base)
    return pl.pallas_call(
        kern,
        grid=(b, s // tm),
        in_specs=[pl.BlockSpec(memory_space=pl.ANY),
                  pl.BlockSpec(memory_space=pl.ANY),
                  pl.BlockSpec(memory_space=pl.ANY),
                  pl.BlockSpec((None, tm, LANES), lambda bi, i: (bi, i, 0)),
                  pl.BlockSpec((None, tm, ROW_PARTS, LANES), lambda bi, i: (bi, i, 0, 0)),
                  pl.BlockSpec((None, tm, d), lambda bi, i: (bi, i, 0)),
                  pl.BlockSpec((None, 1, d), lambda bi, i: (bi, 0, 0)),
                  pl.BlockSpec(s1.shape, lambda bi, i: (0, 0)),
                  pl.BlockSpec(s3.shape, lambda bi, i: (0, 0)),
                  pl.BlockSpec(s2.shape, lambda bi, i: (0, 0))],
        out_specs=pl.BlockSpec((None, tm, d), lambda bi, i: (bi, i, 0)),
        out_shape=jax.ShapeDtypeStruct((b, s, d), F32),
        scratch_shapes=[pltpu.SMEM((tm * 8,), I32),
                        pltpu.SMEM((TAB_WORDS,), I32),
                        pltpu.VMEM((TOP_K, tm, LANES), F32),
                        pltpu.VMEM((TOP_K * tm, ROW_PARTS, LANES), U32),
                        pltpu.VMEM((tm, ROW_PARTS, LANES), F32),
                        pltpu.VMEM((tm, ROW_PARTS, LANES), F32),
                        pltpu.SemaphoreType.DMA,
                        pltpu.SemaphoreType.DMA],
        compiler_params=_params("arbitrary", "arbitrary"),
        name="combine",
    )(loc_flat, tab_flat, ys, slab, h2, x, gate, s1, s3, s2)


def _dft_tables(length, n_chan):
    scale = 1.0 / math.sqrt(length * n_chan)
    side = 1
    while side * side < length:
        side *= 2
    outer = length // side
    k = jnp.arange(length, dtype=I32)[:, None]
    a_idx = (k * jnp.arange(outer, dtype=I32)[None, :]) % outer
    b_idx = (k * jnp.arange(side, dtype=I32)[None, :]) % length
    ang_a = a_idx.astype(F32) * (2.0 * math.pi / outer)
    ang_b = b_idx.astype(F32) * (2.0 * math.pi / length)
    ca, sa = jnp.cos(ang_a)[:, :, None], jnp.sin(ang_a)[:, :, None]
    cb, sb = jnp.cos(ang_b)[:, None, :], jnp.sin(ang_b)[:, None, :]
    cos_t = (ca * cb - sa * sb).reshape(length, length)
    sin_t = (sa * cb + ca * sb).reshape(length, length)
    table = (jnp.concatenate([cos_t, -sin_t], axis=1) * scale).astype(BF16)
    return table


def _channel_table(n_chan, n_groups):
    m = jnp.arange(n_chan, dtype=I32)
    ang = ((m[:, None] * m[None, :]) % n_chan).astype(F32) * (2.0 * math.pi / n_chan)
    eye = jnp.eye(n_groups, dtype=F32)
    return jnp.concatenate([jnp.kron(eye, jnp.cos(ang)), jnp.kron(eye, jnp.sin(ang))], axis=1).astype(BF16)


def _rope_tables(n_tok):
    rows = n_tok // GRID_W
    axis_dim = HEAD_DIM // 2
    r = jnp.repeat(jnp.arange(rows, dtype=F32), GRID_W)
    col = jnp.tile(jnp.arange(GRID_W, dtype=F32), rows)
    inv = ROPE_BASE ** (-jnp.arange(0, axis_dim, 2, dtype=F32) / axis_dim)
    ar, ac = r[:, None] * inv, col[:, None] * inv
    cos = jnp.concatenate([jnp.cos(ar), jnp.cos(ar), jnp.cos(ac), jnp.cos(ac)], axis=1)
    sin = jnp.concatenate([-jnp.sin(ar), jnp.sin(ar), -jnp.sin(ac), jnp.sin(ac)], axis=1)
    return jnp.tile(cos, (1, N_HEADS)), jnp.tile(sin, (1, N_HEADS))


def _head_mean_matrix(width):
    h = jnp.arange(width) // HEAD_DIM
    return ((h[:, None] == h[None, :]).astype(F32) / HEAD_DIM).astype(BF16)


def _tile_matrices():
    src = jnp.arange(N_KV * HEAD_DIM)
    dst = jnp.arange(Q_PER_KV * HEAD_DIM)
    mats = [((src[:, None] // HEAD_DIM == h) & (src[:, None] % HEAD_DIM == dst[None, :] % HEAD_DIM))
            for h in range(N_KV)]
    return jnp.stack(mats).astype(BF16)


def _even_layer(x, xc, mod, modc, norm_g, w_in, ws, bs, w_out, tables):
    b, s, d = x.shape
    w = w_in.shape[1] // 3
    outs = []
    for stream, m, tm in ((x, mod, 512), (xc, modc, 256)):
        if stream is None:
            outs.append(None)
            continue
        length = stream.shape[1]
        tm = min(tm, length)
        uvf = _norm_mod_matmul(stream, norm_g, m[0], m[1], w_in, tm)
        gm, z = _even_mix(uvf, ws, bs, tables["chan"], tm)
        table = tables["pos"][length]
        y = _matmul(table, z.reshape(2 * length, b * w),
                    min(1024, length), min(1024, b * w), min(1024, 2 * length), BF16)
        outs.append(_outproj(
            gm, pl.BlockSpec((None, tm, w), lambda bi, i: (bi, i, 0)),
            y, pl.BlockSpec((tm, w), lambda bi, i: (i, bi)),
            w_out, stream, m[2], tm))
    return outs


def _odd_layer(x, xc, mod, modc, norm_g, w_in, qg, kg, sink, conv_w, w_out, tables, ctx_out):
    b, s, d = x.shape
    lc = xc.shape[1]
    half = N_HEADS * HEAD_DIM
    tm, tmc = 512, min(256, lc)
    proj = _norm_mod_matmul(x, norm_g, mod[0], mod[1], w_in, tm)
    projc = _norm_mod_matmul(xc, norm_g, modc[0], modc[1], w_in, tmc)
    prep = functools.partial(_qkv_prep, qg=qg, kg=kg, bdq=tables["bdq"], bdk=tables["bdk"], tile=tables["tile"])
    q, k4, v4 = prep(proj, tables["cos"], tables["sin"], tm=tm, with_q=True)
    qc, kc4, vc4 = prep(projc, tables["cos_c"], tables["sin_c"], tm=tmc, with_q=ctx_out)
    att = _attention(q, k4, v4, kc4, vc4, sink, band=True)
    conv = _short_conv(proj, conv_w)
    spec = lambda t: pl.BlockSpec((None, t, half), lambda bi, i: (bi, i, 0))
    y = _outproj(att, spec(tm), conv, spec(tm), w_out, x, mod[2], tm)
    yc = None
    if ctx_out:
        attc = _attention(qc, None, None, kc4, vc4, sink, band=False)
        convc = _short_conv(projc, conv_w)
        yc = _outproj(attc, spec(tmc), convc, spec(tmc), w_out, xc, modc[2], tmc)
    return y, yc


def _moe(x, xc, mod, modc, norm_g, rw_hi, rw_lo, rb, w1, w3, w2, layer, s1, s3, s2, tri, upper):
    b, s, d = x.shape
    n_lat = b * s
    xc_shape = None
    if xc is not None and xc.shape[1] % MOE_TM:
        xc_shape = xc.shape
        xc = xc.reshape(-1, MOE_TM, d)
        modc = [m[:xc.shape[0]] for m in modc]
    counts0 = jnp.zeros((1, LANES), F32)
    h2, slab, tab, counts = _route(x, norm_g, mod[3], mod[4], rw_hi, rw_lo, rb, counts0, tri, upper)
    slabs, tabs = [slab.reshape(n_lat, LANES)], [tab]
    n_tok = n_lat
    if xc is not None:
        h2c, slabc, tabc, counts = _route(xc, norm_g, modc[3], modc[4], rw_hi, rw_lo, rb, counts, tri, upper)
        slabs.append(slabc.reshape(-1, LANES))
        tabs.append(tabc)
        n_tok += slabs[1].shape[0]
    cnt = counts[0, :N_EXPERTS].astype(I32)
    blk = EXPERT_BLOCK
    padded = (cnt + blk - 1) // blk * blk
    ends = jnp.cumsum(padded).astype(I32)
    starts = ends - padded
    step_rows = blk * EXPERT_GROUP
    n_rows = (n_tok * TOP_K + N_EXPERTS * (blk - 1) + step_rows - 1) // step_rows * step_rows
    n_blocks = n_rows // blk
    n_used = (ends[-1] // blk).reshape(1).astype(I32)
    blk_start = jnp.minimum(jnp.arange(n_blocks, dtype=I32), n_used[0] - 1) * blk
    block_e = jnp.minimum(jnp.sum(blk_start[:, None] >= ends[None, :], axis=1), N_EXPERTS - 1).astype(I32)
    slab_all = jnp.concatenate(slabs, axis=0)
    tab_all = jnp.concatenate(tabs, axis=0)[:, :, :N_EXPERTS].astype(I32)
    runs = jnp.concatenate([tab_all[:, 0], tab_all[:, 1], tab_all[:, 2] + starts[None, :]], axis=1)
    tab_flat = jnp.pad(runs, ((0, 0), (0, TAB_WORDS - runs.shape[1]))).reshape(-1)
    loc_flat = (slab_all[:, SLAB_LOC:SLAB_LOC + 8].astype(I32) * ROW_PARTS).reshape(-1)
    packed = lambda a: a.reshape(-1, ROW_PARTS, LANES)
    xs = _dispatch(packed(h2), loc_flat, tab_flat, ends, n_used, None, n_rows, 0)
    if xc is not None:
        xs = _dispatch(packed(h2c), loc_flat, tab_flat, ends, n_used, xs, n_rows, n_lat // MOE_TM)
    ys = _experts(xs, block_e, n_used, w1, w3, w2, layer)
    x_new = _combine(ys, loc_flat, tab_flat, slab, h2, x, mod[5], s1, s3, s2, 0)
    xc_new = None
    if xc is not None:
        xc_new = _combine(ys, loc_flat, tab_flat, slabc, h2c, xc, modc[5], s1, s3, s2, n_lat // MOE_TM)
        if xc_shape is not None:
            xc_new = xc_new.reshape(xc_shape)
    return x_new, xc_new


def kernel(x, c, ctx, c_ctx, ada_w, ada_b, norm1_g, norm2_g, ev_w_in, ev_w_s, ev_b_s, ev_w_out, od_w_in, od_q_norm_g, od_k_norm_g, od_sink, od_conv_w, od_w_out, router_w, router_b, exp_w_gate, exp_w_up, exp_w_down, sh_w_gate, sh_w_up, sh_w_down):
    b, s, d = x.shape
    lc = ctx.shape[1]
    depth = ada_w.shape[0]
    n_groups = ev_w_s.shape[1]
    half = d // 2

    rows = -(-(b + 1) // 8) * 8
    cond = jnp.zeros((rows, d), F32).at[:b].set(c).at[b].set(c_ctx)
    mod_all = _adaln(cond, ada_w, ada_b)

    tables = {
        "chan": _channel_table(LANES, n_groups),
        "pos": {s: _dft_tables(s, LANES), lc: _dft_tables(lc, LANES)},
        "bdq": _head_mean_matrix(N_HEADS * HEAD_DIM),
        "bdk": _head_mean_matrix(N_KV * HEAD_DIM),
        "tile": _tile_matrices(),
    }
    tables["cos"], tables["sin"] = _rope_tables(s)
    tables["cos_c"] = jnp.ones((lc, N_HEADS * HEAD_DIM), F32)
    tables["sin_c"] = jnp.zeros((lc, N_HEADS * HEAD_DIM), F32)
    tri = (jnp.arange(MOE_TM)[:, None] > jnp.arange(MOE_TM)[None, :]).astype(BF16)
    upper = (jnp.arange(LANES)[:, None] < jnp.arange(LANES)[None, :]).astype(BF16)

    qw, kw = N_HEADS * HEAD_DIM, N_KV * HEAD_DIM
    perm = jnp.concatenate([jnp.arange(0, qw), jnp.arange(qw + 2 * kw, qw + 2 * kw + 3 * half),
                            jnp.arange(qw, qw + 2 * kw)])

    w1_all, w3_all, w2_all = exp_w_gate.astype(BF16), exp_w_up.astype(BF16), exp_w_down.astype(BF16)
    xc = ctx
    for l in range(depth):
        last = l == depth - 1
        even = l % 2 == 0
        need_ctx = not (last and even)
        pieces = [mod_all[l, :, j * d:(j + 1) * d] for j in range(6)]
        mod = [p[:b].reshape(b, 1, d) for p in pieces]
        modc = [jnp.broadcast_to(p[b].reshape(1, 1, d), (b, 1, d)) for p in pieces]
        g1 = norm1_g[l].reshape(1, d)
        g2 = norm2_g[l].reshape(1, d)
        if even:
            e = l // 2
            bs = jnp.broadcast_to(ev_b_s[e][:, :, None], (n_groups, CHUNK, LANES))
            y, yc = _even_layer(x, xc if (need_ctx and not last) else None, mod, modc, g1,
                                ev_w_in[e].astype(BF16), ev_w_s[e].astype(BF16), bs,
                                ev_w_out[e].astype(BF16), tables)
        else:
            o = l // 2
            qg = (jnp.tile(od_q_norm_g[o], N_HEADS) * (HEAD_DIM ** -0.5)).reshape(1, qw)
            kg = jnp.tile(od_k_norm_g[o], N_KV).reshape(1, kw)
            y, yc = _odd_layer(x, xc, mod, modc, g1, od_w_in[o][:, perm].astype(BF16), qg, kg,
                               od_sink[o], od_conv_w[o], od_w_out[o].astype(BF16), tables, not last)
        x = y
        if not last:
            xc = yc
        rw = jnp.zeros((d, LANES), F32).at[:, :N_EXPERTS].set(router_w[l])
        rw_hi = rw.astype(BF16)
        rw_lo = (rw - rw_hi.astype(F32)).astype(BF16)
        rb = jnp.zeros((1, LANES), F32).at[0, :N_EXPERTS].set(router_b[l])
        x, xc_new = _moe(x, None if last else xc, mod, modc, g2, rw_hi, rw_lo, rb, w1_all, w3_all, w2_all, l,
                         sh_w_gate[l].astype(BF16), sh_w_up[l].astype(BF16), sh_w_down[l].astype(BF16),
                         tri, upper)
        if not last:
            xc = xc_new
    return x
```

```python
import functools
import math

import jax
import jax.numpy as jnp
from jax import lax
from jax.experimental import pallas as pl
from jax.experimental.pallas import tpu as pltpu

F32 = jnp.float32
BF16 = jnp.bfloat16
I32 = jnp.int32
U32 = jnp.uint32

LANES = 128
VMEM_LIMIT = 48 * 2**20

EPS = 1e-6
GRID_W = 64
CHUNK = 128
HEAD_DIM = 64
N_HEADS = 8
N_KV = 2
Q_PER_KV = N_HEADS // N_KV
MIX_TM = 1024
ATT_BLOCK = 128
ATT_ROWS = 64
ROPE_BASE = 10000.0
N_EXPERTS = 64
TOP_K = 6
ROUTED_SCALE = 2.5
EXPERT_BLOCK = 512
EXPERT_GROUP = 2
MOE_TM = 1024
ROW_PARTS = 4
SLAB_IDX, SLAB_LOC, SLAB_GATE = 0, 8, 16
TAB_WORDS = 1024


def _params(*sem):
    return pltpu.CompilerParams(dimension_semantics=sem, vmem_limit_bytes=VMEM_LIMIT)


def _sigmoid(x):
    return 1.0 / (1.0 + jnp.exp(-x))


def _silu(x):
    return x * _sigmoid(x)


def _gelu_tanh(x):
    c = math.sqrt(2.0 / math.pi)
    return x * (0.5 * (1.0 + jnp.tanh(c * (x + 0.044715 * (x * x * x)))))


def _dot(a, b):
    return jnp.dot(a, b, preferred_element_type=F32)


def _unpack_words(w):
    return pltpu.bitcast(w << 16, F32), pltpu.bitcast(w & jnp.uint32(0xFFFF0000), F32)


def _row_chunks(ref, row0=0, n=None):
    total, parts, lanes = ref.shape
    n = total if n is None else n
    flat = ref.reshape(total * parts, lanes)
    return [flat[pl.ds(row0 * parts + c, n, stride=parts), :] for c in range(parts)]


def _unpack_rows(ref, row0=0, n=None):
    halves = [_unpack_words(w) for w in _row_chunks(ref, row0, n)]
    return jnp.concatenate([h[0] for h in halves] + [h[1] for h in halves], axis=-1)


def _pack_rows(ref, val, row0=0):
    total, parts, lanes = ref.shape
    n, half = val.shape[0], val.shape[1] // 2
    bits = pltpu.bitcast(val.astype(BF16).astype(F32), U32)
    words = (bits[:, :half] >> 16) | (bits[:, half:] & jnp.uint32(0xFFFF0000))
    flat = ref.reshape(total * parts, lanes)
    for c in range(parts):
        flat[pl.ds(row0 * parts + c, n, stride=parts), :] = words[:, c * lanes:(c + 1) * lanes]


def _adaln_kernel(c_ref, w_ref, b_ref, o_ref):
    o_ref[...] = _dot(_silu(c_ref[...]), w_ref[...]) + b_ref[...]


def _adaln(cond, ada_w, ada_b):
    n_layers, d, n6 = ada_w.shape
    rows = cond.shape[0]
    tn = 768
    return pl.pallas_call(
        _adaln_kernel,
        grid=(n_layers, n6 // tn),
        in_specs=[pl.BlockSpec((rows, d), lambda l, j: (0, 0)),
                  pl.BlockSpec((None, d, tn), lambda l, j: (l, 0, j)),
                  pl.BlockSpec((None, 1, tn), lambda l, j: (l, 0, j))],
        out_specs=pl.BlockSpec((None, rows, tn), lambda l, j: (l, 0, j)),
        out_shape=jax.ShapeDtypeStruct((n_layers, rows, n6), F32),
        compiler_params=_params("parallel", "parallel"),
        name="adaln",
    )(cond, ada_w, ada_b.reshape(n_layers, 1, n6))


def _norm_mod(x, g, shift, scale):
    ms = jnp.mean(x * x, axis=-1, keepdims=True)
    h = (x * lax.rsqrt(ms + EPS)) * g
    return h * (1.0 + scale) + shift


def _nmm_kernel(x_ref, g_ref, sh_ref, sc_ref, w_ref, o_ref):
    h = _norm_mod(x_ref[...], g_ref[...], sh_ref[...], sc_ref[...])
    o_ref[...] = _dot(h.astype(BF16), w_ref[...]).astype(o_ref.dtype)


def _norm_mod_matmul(x, g, shift, scale, w, tm):
    b, s, d = x.shape
    n = w.shape[1]
    return pl.pallas_call(
        _nmm_kernel,
        grid=(b, s // tm),
        in_specs=[pl.BlockSpec((None, tm, d), lambda bi, i: (bi, i, 0)),
                  pl.BlockSpec((1, d), lambda bi, i: (0, 0)),
                  pl.BlockSpec((None, 1, d), lambda bi, i: (bi, 0, 0)),
                  pl.BlockSpec((None, 1, d), lambda bi, i: (bi, 0, 0)),
                  pl.BlockSpec((d, n), lambda bi, i: (0, 0))],
        out_specs=pl.BlockSpec((None, tm, n), lambda bi, i: (bi, i, 0)),
        out_shape=jax.ShapeDtypeStruct((b, s, n), BF16),
        compiler_params=_params("parallel", "parallel"),
        name="norm_mod_matmul",
    )(x, g, shift, scale, w)


def _even_mix_kernel(u_ref, v_ref, f_ref, ws_ref, bs_ref, cs_ref, gm_ref, z_ref, *, n_chunks, n_groups):
    for c in range(n_chunks):
        rows = slice(c * CHUNK, (c + 1) * CHUNK)
        for g in range(n_groups):
            cols = slice(g * LANES, (g + 1) * LANES)
            ug = _gelu_tanh(u_ref[rows, cols].astype(F32))
            vg = _gelu_tanh(v_ref[rows, cols].astype(F32))
            mu = jnp.mean(vg, axis=-1, keepdims=True)
            dv = vg - mu
            var = jnp.mean(dv * dv, axis=-1, keepdims=True)
            vn = dv * lax.rsqrt(var + 1e-5)
            fg = _dot(ws_ref[g], vn.astype(BF16)) + bs_ref[g]
            gm_ref[rows, cols] = (ug * fg).astype(gm_ref.dtype)
    fz = _dot(f_ref[...].astype(BF16), cs_ref[...])
    half = fz.shape[1] // 2
    z_ref[0] = fz[:, :half].astype(z_ref.dtype)
    z_ref[1] = fz[:, half:].astype(z_ref.dtype)


def _even_mix(uvf, ws, bs, cs, tm):
    b, s, n3 = uvf.shape
    w = n3 // 3
    n_groups = w // LANES
    kern = functools.partial(_even_mix_kernel, n_chunks=tm // CHUNK, n_groups=n_groups)
    return pl.pallas_call(
        kern,
        grid=(b, s // tm),
        in_specs=[pl.BlockSpec((None, tm, w), lambda bi, i: (bi, i, 0)),
                  pl.BlockSpec((None, tm, w), lambda bi, i: (bi, i, 1)),
                  pl.BlockSpec((None, tm, w), lambda bi, i: (bi, i, 2)),
                  pl.BlockSpec(ws.shape, lambda bi, i: (0, 0, 0)),
                  pl.BlockSpec(bs.shape, lambda bi, i: (0, 0, 0)),
                  pl.BlockSpec(cs.shape, lambda bi, i: (0, 0))],
        out_specs=[pl.BlockSpec((None, tm, w), lambda bi, i: (bi, i, 0)),
                   pl.BlockSpec((2, tm, w), lambda bi, i: (0, i, bi))],
        out_shape=[jax.ShapeDtypeStruct((b, s, w), BF16),
                   jax.ShapeDtypeStruct((2, s, b * w), BF16)],
        compiler_params=_params("parallel", "parallel"),
        name="even_mix",
    )(uvf, uvf, uvf, ws, bs, cs)


def _mm_kernel(a_ref, b_ref, o_ref, acc_ref):
    k = pl.program_id(2)

    @pl.when(k == 0)
    def _():
        acc_ref[...] = jnp.zeros_like(acc_ref)

    acc_ref[...] += _dot(a_ref[...], b_ref[...])

    @pl.when(k == pl.num_programs(2) - 1)
    def _():
        o_ref[...] = acc_ref[...].astype(o_ref.dtype)


def _matmul(a, b, tm, tn, tk, out_dtype):
    m, kd = a.shape
    n = b.shape[1]
    return pl.pallas_call(
        _mm_kernel,
        grid=(m // tm, n // tn, kd // tk),
        in_specs=[pl.BlockSpec((tm, tk), lambda i, j, k: (i, k)),
                  pl.BlockSpec((tk, tn), lambda i, j, k: (k, j))],
        out_specs=pl.BlockSpec((tm, tn), lambda i, j, k: (i, j)),
        out_shape=jax.ShapeDtypeStruct((m, n), out_dtype),
        scratch_shapes=[pltpu.VMEM((tm, tn), F32)],
        compiler_params=_params("parallel", "parallel", "arbitrary"),
        name="dft_matmul",
    )(a, b)


def _outproj_kernel(a_ref, b_ref, w_ref, x_ref, gate_ref, o_ref):
    ab = jnp.concatenate([a_ref[...], b_ref[...]], axis=-1)
    o_ref[...] = x_ref[...] + gate_ref[...] * _dot(ab, w_ref[...])


def _outproj(a, a_spec, bsrc, b_spec, w, x, gate, tm):
    b, s, d = x.shape
    return pl.pallas_call(
        _outproj_kernel,
        grid=(b, s // tm),
        in_specs=[a_spec, b_spec,
                  pl.BlockSpec(w.shape, lambda bi, i: (0, 0)),
                  pl.BlockSpec((None, tm, d), lambda bi, i: (bi, i, 0)),
                  pl.BlockSpec((None, 1, d), lambda bi, i: (bi, 0, 0))],
        out_specs=pl.BlockSpec((None, tm, d), lambda bi, i: (bi, i, 0)),
        out_shape=jax.ShapeDtypeStruct((b, s, d), F32),
        compiler_params=_params("parallel", "parallel"),
        name="outproj",
    )(a, bsrc, w, x, gate)


def _head_rms(x, bd_ref):
    xx = x * x
    hi = xx.astype(BF16)
    lo = (xx - hi.astype(F32)).astype(BF16)
    ms = _dot(hi, bd_ref[...]) + _dot(lo, bd_ref[...])
    return x * lax.rsqrt(ms + EPS)


def _rope(x, cos, sins):
    width = x.shape[1]
    lane = lax.broadcasted_iota(I32, x.shape, 1)
    first = (lane & 31) < 16
    swapped = jnp.where(first, pltpu.roll(x, width - 16, 1), pltpu.roll(x, 16, 1))
    return x * cos + swapped * sins


def _qkv_kernel(q_ref, k_ref, v_ref, cos_ref, sin_ref, qg_ref, kg_ref, bdq_ref, bdk_ref, tile_ref, tile_t_ref,
                qo_ref, k4_ref, v4_ref, *, with_q):
    kw = k_ref.shape[1]
    if with_q:
        qn = _head_rms(q_ref[...].astype(F32), bdq_ref) * qg_ref[...]
        qo_ref[...] = _rope(qn, cos_ref[...], sin_ref[...]).astype(qo_ref.dtype)
    else:
        qo_ref[...] = jnp.zeros_like(qo_ref)
    kn = _head_rms(k_ref[...].astype(F32), bdk_ref) * kg_ref[...]
    kr = _rope(kn, cos_ref[:, :kw], sin_ref[:, :kw]).astype(BF16)
    vb = v_ref[...].astype(BF16)
    for h in range(N_KV):
        k4_ref[h] = lax.dot_general(tile_t_ref[h], kr, (((1,), (1,)), ((), ())),
                                    preferred_element_type=F32).astype(k4_ref.dtype)
        v4_ref[h] = _dot(vb, tile_ref[h]).astype(v4_ref.dtype)


def _qkv_prep(proj, cos, sin, qg, kg, bdq, bdk, tile, tm, with_q):
    b, s, _ = proj.shape
    tile_t = jnp.swapaxes(tile, 1, 2)
    qw = N_HEADS * HEAD_DIM
    kw = N_KV * HEAD_DIM
    rep = Q_PER_KV * HEAD_DIM
    k_blk = (4 * qw) // kw
    kern = functools.partial(_qkv_kernel, with_q=with_q)
    return pl.pallas_call(
        kern,
        grid=(b, s // tm),
        in_specs=[pl.BlockSpec((None, tm, qw), lambda bi, i: (bi, i, 0)),
                  pl.BlockSpec((None, tm, kw), lambda bi, i: (bi, i, k_blk)),
                  pl.BlockSpec((None, tm, kw), lambda bi, i: (bi, i, k_blk + 1)),
                  pl.BlockSpec((tm, qw), lambda bi, i: (i, 0)),
                  pl.BlockSpec((tm, qw), lambda bi, i: (i, 0)),
                  pl.BlockSpec((1, qw), lambda bi, i: (0, 0)),
                  pl.BlockSpec((1, kw), lambda bi, i: (0, 0)),
                  pl.BlockSpec(bdq.shape, lambda bi, i: (0, 0)),
                  pl.BlockSpec(bdk.shape, lambda bi, i: (0, 0)),
                  pl.BlockSpec(tile.shape, lambda bi, i: (0, 0, 0)),
                  pl.BlockSpec(tile_t.shape, lambda bi, i: (0, 0, 0))],
        out_specs=[pl.BlockSpec((None, tm, qw), lambda bi, i: (bi, i, 0)),
                   pl.BlockSpec((None, N_KV, rep, tm), lambda bi, i: (bi, 0, 0, i)),
                   pl.BlockSpec((None, N_KV, tm, rep), lambda bi, i: (bi, 0, i, 0))],
        out_shape=[jax.ShapeDtypeStruct((b, s, qw), BF16),
                   jax.ShapeDtypeStruct((b, N_KV, rep, s), BF16),
                   jax.ShapeDtypeStruct((b, N_KV, s, rep), BF16)],
        compiler_params=_params("parallel", "parallel"),
        name="qkv_prep",
    )(proj, proj, proj, cos, sin, qg, kg, bdq, bdk, tile, tile_t)


def _conv_kernel(gi_ref, go_ref, z_ref, w_ref, o_ref, *, rows):
    s = z_ref.shape[0]
    n = s // rows
    w0, w1, w2 = w_ref[0:1, :], w_ref[1:2, :], w_ref[2:3, :]
    ridx = lax.broadcasted_iota(I32, (rows, z_ref.shape[1]), 0)
    prev_last = jnp.zeros((1, z_ref.shape[1]), F32)

    def load(ref, start, count):
        return ref[start:start + count, :].astype(F32)

    for j in range(n):
        r0 = j * rows
        zc = load(gi_ref, r0, rows) * load(z_ref, r0, rows)
        if j + 1 < n:
            nxt = (load(gi_ref, r0 + rows, 16) * load(z_ref, r0 + rows, 16))[0:1, :]
        else:
            nxt = jnp.zeros_like(prev_last)
        zp = jnp.where(ridx == 0, prev_last, pltpu.roll(zc, 1, 0))
        zn = jnp.where(ridx == rows - 1, nxt, pltpu.roll(zc, rows - 1, 0))
        y = zp * w0 + zc * w1 + zn * w2
        o_ref[r0:r0 + rows, :] = (load(go_ref, r0, rows) * y).astype(o_ref.dtype)
        prev_last = zc[rows - 1:rows, :]


def _short_conv(proj, conv_w):
    b, s, _ = proj.shape
    dc = conv_w.shape[1]
    nb = dc // LANES
    base = dc // LANES
    rows = min(s, 512)
    kern = functools.partial(_conv_kernel, rows=rows)
    return pl.pallas_call(
        kern,
        grid=(b, nb),
        in_specs=[pl.BlockSpec((None, s, LANES), lambda bi, c: (bi, 0, base + c)),
                  pl.BlockSpec((None, s, LANES), lambda bi, c: (bi, 0, 2 * base + c)),
                  pl.BlockSpec((None, s, LANES), lambda bi, c: (bi, 0, 3 * base + c)),
                  pl.BlockSpec((conv_w.shape[0], LANES), lambda bi, c: (0, c))],
        out_specs=pl.BlockSpec((None, s, LANES), lambda bi, c: (bi, 0, c)),
        out_shape=jax.ShapeDtypeStruct((b, s, dc), BF16),
        compiler_params=_params("parallel", "parallel"),
        name="short_conv",
    )(proj, proj, proj, conv_w)


def _attn_kernel(sink_ref, q_ref, *refs, band, n_blocks):
    o_ref = refs[-1]
    i = pl.program_id(1)
    t = q_ref.shape[0]
    w = Q_PER_KV * HEAD_DIM
    lane = lax.broadcasted_iota(I32, (t, w), 1)
    masks = [(lane >= g * HEAD_DIM) & (lane < (g + 1) * HEAD_DIM) for g in range(Q_PER_KV)]
    bad = None
    if band:
        n_keys = 3 * t + refs[7].shape[1]
        row = lax.broadcasted_iota(I32, (Q_PER_KV * t, n_keys), 0) & (t - 1)
        col = lax.broadcasted_iota(I32, (Q_PER_KV * t, n_keys), 1)
        off_prev = jnp.where(i > 0, 0, 4 * t)
        off_next = jnp.where(i < n_blocks - 1, 0, 4 * t)
        bad_prev = (col < t) & (col < row + off_prev)
        bad_next = (col >= 2 * t) & (col < 3 * t) & (col - 2 * t > row - off_next)
        bad = bad_prev | bad_next
    for h in range(N_KV):
        if band:
            kp, kc_, kn, vp, vc_, vn, kx, vx = refs[:-1]
            kcat = jnp.concatenate([kp[h], kc_[h], kn[h], kx[h]], axis=1)
            vcat = jnp.concatenate([vp[h], vc_[h], vn[h], vx[h]], axis=0)
        else:
            kx, vx = refs[:-1]
            kcat, vcat = kx[h], vx[h]
        q = q_ref[:, h * w:(h + 1) * w]
        q4 = jnp.concatenate([jnp.where(m, q, jnp.zeros_like(q)) for m in masks], axis=0)
        s = _dot(q4, kcat)
        es, invs = [], []
        for c in range(Q_PER_KV * t // ATT_ROWS):
            rows = slice(c * ATT_ROWS, (c + 1) * ATT_ROWS)
            sc = s[rows]
            if band:
                sc = jnp.where(bad[rows], -jnp.inf, sc)
            sink = jnp.full((ATT_ROWS, 1), sink_ref[h * Q_PER_KV + (c * ATT_ROWS) // t], F32)
            m = jnp.maximum(jnp.max(sc, axis=-1, keepdims=True), sink)
            ec = jnp.exp(sc - m)
            invs.append(1.0 / (jnp.sum(ec, axis=-1, keepdims=True) + jnp.exp(sink - m)))
            es.append(ec.astype(BF16))
        e = jnp.concatenate(es, axis=0)
        r = _dot(e, vcat) * jnp.concatenate(invs, axis=0)
        o = jnp.zeros((t, w), F32)
        for g in range(Q_PER_KV):
            o = o + jnp.where(masks[g], r[g * t:(g + 1) * t, :], 0.0)
        o_ref[:, h * w:(h + 1) * w] = o.astype(o_ref.dtype)


def _attention(q, k4, v4, kx4, vx4, sink, band):
    b, s, qw = q.shape
    rep = vx4.shape[-1]
    lc = vx4.shape[2]
    t = ATT_BLOCK
    nb = s // t
    kern = functools.partial(_attn_kernel, band=band, n_blocks=nb)

    def k_spec(off):
        return pl.BlockSpec((None, N_KV, rep, t),
                            lambda bi, i: (bi, 0, 0, jnp.clip(i + off, 0, nb - 1)))

    def v_spec(off):
        return pl.BlockSpec((None, N_KV, t, rep),
                            lambda bi, i: (bi, 0, jnp.clip(i + off, 0, nb - 1), 0))

    in_specs = [pl.BlockSpec(memory_space=pltpu.SMEM),
                pl.BlockSpec((None, t, qw), lambda bi, i: (bi, i, 0))]
    args = [sink, q]
    if band:
        in_specs += [k_spec(-1), k_spec(0), k_spec(1), v_spec(-1), v_spec(0), v_spec(1)]
        args += [k4, k4, k4, v4, v4, v4]
    in_specs += [pl.BlockSpec((None, N_KV, rep, lc), lambda bi, i: (bi, 0, 0, 0)),
                 pl.BlockSpec((None, N_KV, lc, rep), lambda bi, i: (bi, 0, 0, 0))]
    args += [kx4, vx4]
    return pl.pallas_call(
        kern,
        grid=(b, nb),
        in_specs=in_specs,
        out_specs=pl.BlockSpec((None, t, qw), lambda bi, i: (bi, i, 0)),
        out_shape=jax.ShapeDtypeStruct((b, s, qw), BF16),
        compiler_params=_params("parallel", "parallel"),
        name="attention_band" if band else "attention_ctx",
    )(*args)


def _route_kernel(x_ref, g_ref, sh_ref, sc_ref, whi_ref, wlo_ref, rb_ref, cin_ref, tri_ref, upper_ref,
                  h_ref, slab_ref, tab_ref, cnt_ref, carry_ref):
    first = (pl.program_id(0) == 0) & (pl.program_id(1) == 0)

    @pl.when(first)
    def _():
        carry_ref[...] = cin_ref[...]

    h = _norm_mod(x_ref[...], g_ref[...], sh_ref[...], sc_ref[...])
    _pack_rows(h_ref, h)
    hi = h.astype(BF16)
    lo = (h - hi.astype(F32)).astype(BF16)
    logits = _dot(hi, whi_ref[...]) + _dot(lo, whi_ref[...]) + _dot(hi, wlo_ref[...])
    scores = _sigmoid(logits)
    tm, lanes = scores.shape
    lane = lax.broadcasted_iota(I32, (tm, lanes), 1).astype(F32)
    work = jnp.where(lane < N_EXPERTS, scores + rb_ref[...], -jnp.inf)
    hits, idxs, gates = [], [], []
    for _ in range(TOP_K):
        mx = jnp.max(work, axis=-1, keepdims=True)
        idx = jnp.min(jnp.where(work == mx, lane, float(lanes)), axis=-1, keepdims=True)
        hit = lane == idx
        hits.append(hit)
        idxs.append(idx)
        gates.append(jnp.sum(jnp.where(hit, scores, 0.0), axis=-1, keepdims=True))
        work = jnp.where(hit, -jnp.inf, work)
    gsum = gates[0]
    for gk in gates[1:]:
        gsum = gsum + gk
    gscale = ROUTED_SCALE / (gsum + 1e-20)
    onehot = jnp.zeros((tm, lanes), F32)
    for hit in hits:
        onehot = jnp.where(hit, 1.0, onehot)
    cnt = jnp.sum(onehot, axis=0, keepdims=True)
    cnt_hi = jnp.floor(cnt * (1.0 / 256.0))
    cnt_lo = cnt - 256.0 * cnt_hi
    parts = jnp.concatenate([jnp.broadcast_to(cnt_hi, (8, lanes)), jnp.broadcast_to(cnt_lo, (8, lanes))], axis=0)
    sums = _dot(parts.astype(BF16), upper_ref[...])
    seg = 256.0 * sums[0:1] + sums[8:9]
    before = _dot(tri_ref[...], onehot.astype(BF16)) + seg
    slab = jnp.zeros((tm, lanes), F32)
    for k in range(TOP_K):
        loc = jnp.sum(jnp.where(hits[k], before, 0.0), axis=-1, keepdims=True)
        slab = jnp.where(lane == SLAB_IDX + k, idxs[k], slab)
        slab = jnp.where(lane == SLAB_LOC + k, loc, slab)
        slab = jnp.where(lane == SLAB_GATE + k, gates[k] * gscale, slab)
    slab_ref[...] = slab
    row = lax.broadcasted_iota(I32, (8, lanes), 0)
    tab_ref[...] = jnp.where(row == 0, cnt, jnp.where(row == 1, seg, jnp.where(row == 2, carry_ref[...], 0.0)))
    carry_ref[...] = carry_ref[...] + cnt
    cnt_ref[...] = carry_ref[...]


def _route(x, g, shift, scale, whi, wlo, rb, counts_in, tri, upper):
    b, s, d = x.shape
    tm = MOE_TM
    nt = s // tm
    return pl.pallas_call(
        _route_kernel,
        grid=(b, nt),
        in_specs=[pl.BlockSpec((None, tm, d), lambda bi, i: (bi, i, 0)),
                  pl.BlockSpec((1, d), lambda bi, i: (0, 0)),
                  pl.BlockSpec((None, 1, d), lambda bi, i: (bi, 0, 0)),
                  pl.BlockSpec((None, 1, d), lambda bi, i: (bi, 0, 0)),
                  pl.BlockSpec(whi.shape, lambda bi, i: (0, 0)),
                  pl.BlockSpec(wlo.shape, lambda bi, i: (0, 0)),
                  pl.BlockSpec((1, LANES), lambda bi, i: (0, 0)),
                  pl.BlockSpec((1, LANES), lambda bi, i: (0, 0)),
                  pl.BlockSpec((tm, tm), lambda bi, i: (0, 0)),
                  pl.BlockSpec((LANES, LANES), lambda bi, i: (0, 0))],
        out_specs=[pl.BlockSpec((None, tm, ROW_PARTS, LANES), lambda bi, i: (bi, i, 0, 0)),
                   pl.BlockSpec((None, tm, LANES), lambda bi, i: (bi, i, 0)),
                   pl.BlockSpec((None, 8, LANES), lambda bi, i: (bi * nt + i, 0, 0)),
                   pl.BlockSpec((1, LANES), lambda bi, i: (0, 0))],
        out_shape=[jax.ShapeDtypeStruct((b, s, ROW_PARTS, LANES), U32),
                   jax.ShapeDtypeStruct((b, s, LANES), F32),
                   jax.ShapeDtypeStruct((b * nt, 8, LANES), F32),
                   jax.ShapeDtypeStruct((1, LANES), F32)],
        scratch_shapes=[pltpu.VMEM((1, LANES), F32)],
        compiler_params=_params("arbitrary", "arbitrary"),
        name="route",
    )(x, g, shift, scale, whi, wlo, rb, counts_in, tri, upper)


def _rows(ref, row, n):
    return ref.at[pl.ds(row, n)]


def _run_copies(tab_smem, base, stage_ref, far_ref, sem, to_far):
    n_bits = MOE_TM.bit_length()
    common = n_bits - 3

    def per_expert(e, carry):
        n = tab_smem[base + e]
        near = tab_smem[base + N_EXPERTS + e]
        far = tab_smem[base + 2 * N_EXPERTS + e]

        def piece(bit):
            size = 1 << bit

            @pl.when((n & size) != 0)
            def _():
                done = n & (size - 1)
                a, b = _rows(stage_ref, near + done, size), _rows(far_ref, far + done, size)
                (pltpu.make_async_copy(a, b, sem) if to_far else pltpu.make_async_copy(b, a, sem)).start()

        for bit in range(common):
            piece(bit)

        @pl.when(n >= (1 << common))
        def _():
            for bit in range(common, n_bits):
                piece(bit)
        return carry

    lax.fori_loop(0, N_EXPERTS, per_expert, 0)


def _wait_tile(stage_ref, far_ref, sem, to_far):
    n = TOP_K * MOE_TM
    a, b = _rows(stage_ref, 0, n), _rows(far_ref, 0, n)
    (pltpu.make_async_copy(a, b, sem) if to_far else pltpu.make_async_copy(b, a, sem)).wait()


def _dispatch_kernel(ends_ref, nu_ref, h_ref, loc_ref, tab_ref, *rest, zero_fill, tile_base, n_blocks):
    if zero_fill:
        xs_ref, loc_smem, tab_smem, stages_ref, zero_ref, sems, psem = rest
    else:
        _, xs_ref, loc_smem, tab_smem, stages_ref, sems, psem = rest
        zero_ref = None
    tm = MOE_TM
    i = pl.program_id(0)
    n_steps = pl.num_programs(0)
    slot = i % 2
    stage_ref, sem = stages_ref.at[slot], sems.at[slot]
    loc_copy = pltpu.make_async_copy(loc_ref.at[pl.ds((tile_base + i) * tm * 8, tm * 8)], loc_smem, psem)
    tab_copy = pltpu.make_async_copy(tab_ref.at[pl.ds((tile_base + i) * TAB_WORDS, TAB_WORDS)], tab_smem, psem)
    loc_copy.start()
    tab_copy.start()

    if zero_fill:
        @pl.when(i == 0)
        def _():
            zero_ref[...] = jnp.zeros_like(zero_ref)

            def block_copy(blk):
                return pltpu.make_async_copy(zero_ref, _rows(xs_ref, blk * EXPERT_BLOCK, EXPERT_BLOCK), sem)

            def fill(e, carry, *, start):
                end = ends_ref[e]
                prev = jnp.where(e > 0, ends_ref[jnp.maximum(e - 1, 0)], 0)

                @pl.when(end > prev)
                def _():
                    cp = block_copy(end // EXPERT_BLOCK - 1)
                    cp.start() if start else cp.wait()
                return carry

            def tail(j, carry, *, start):
                cp = block_copy(j)
                cp.start() if start else cp.wait()
                return carry

            lax.fori_loop(0, N_EXPERTS, functools.partial(fill, start=True), 0)
            lax.fori_loop(nu_ref[0], n_blocks, functools.partial(tail, start=True), 0)
            lax.fori_loop(0, N_EXPERTS, functools.partial(fill, start=False), 0)
            lax.fori_loop(nu_ref[0], n_blocks, functools.partial(tail, start=False), 0)

    loc_copy.wait()
    tab_copy.wait()

    @pl.when(i >= 2)
    def _():
        _wait_tile(stage_ref, xs_ref, sem, to_far=True)

    flat = stage_ref.reshape(TOP_K * tm * ROW_PARTS, LANES)
    h_flat = h_ref.reshape(tm * ROW_PARTS, LANES)

    def place(t, c):
        row = h_flat[pl.ds(pl.multiple_of(t * ROW_PARTS, ROW_PARTS), ROW_PARTS), :]
        for k in range(TOP_K):
            flat[pl.ds(pl.multiple_of(loc_smem[t * 8 + k], ROW_PARTS), ROW_PARTS), :] = row
        return c

    lax.fori_loop(0, tm, place, 0, unroll=4)
    _run_copies(tab_smem, 0, stage_ref, xs_ref, sem, to_far=True)

    @pl.when(i == n_steps - 1)
    def _():
        _wait_tile(stage_ref, xs_ref, sem, to_far=True)

        @pl.when(i >= 1)
        def _():
            _wait_tile(stages_ref.at[1 - slot], xs_ref, sems.at[1 - slot], to_far=True)


def _dispatch(h2, loc_flat, tab_flat, ends, n_used, xs_prev, n_rows, tile_base):
    n = h2.shape[0]
    tm = MOE_TM
    zero_fill = xs_prev is None
    kern = functools.partial(_dispatch_kernel, zero_fill=zero_fill, tile_base=tile_base,
                             n_blocks=n_rows // EXPERT_BLOCK)
    in_specs = [pl.BlockSpec((tm, ROW_PARTS, LANES), lambda i, e, nu: (i, 0, 0)),
                pl.BlockSpec(memory_space=pl.ANY),
                pl.BlockSpec(memory_space=pl.ANY)]
    args = [ends, n_used, h2, loc_flat, tab_flat]
    scratch = [pltpu.SMEM((tm * 8,), I32), pltpu.SMEM((TAB_WORDS,), I32),
               pltpu.VMEM((2, TOP_K * tm, ROW_PARTS, LANES), U32)]
    aliases = {}
    if zero_fill:
        scratch.append(pltpu.VMEM((EXPERT_BLOCK, ROW_PARTS, LANES), U32))
    else:
        in_specs.append(pl.BlockSpec(memory_space=pl.ANY))
        args.append(xs_prev)
        aliases = {5: 0}
    scratch += [pltpu.SemaphoreType.DMA((2,)), pltpu.SemaphoreType.DMA]
    return pl.pallas_call(
        kern,
        grid_spec=pltpu.PrefetchScalarGridSpec(
            num_scalar_prefetch=2,
            grid=(n // tm,),
            in_specs=in_specs,
            out_specs=pl.BlockSpec(memory_space=pl.ANY),
            scratch_shapes=scratch),
        out_shape=jax.ShapeDtypeStruct((n_rows, ROW_PARTS, LANES), U32),
        input_output_aliases=aliases,
        compiler_params=_params("arbitrary"),
        name="dispatch",
    )(*args)


def _expert_kernel(be_ref, nu_ref, x_ref, *refs):
    o_ref = refs[-1]
    blk = EXPERT_BLOCK
    used = pl.program_id(0) * EXPERT_GROUP < nu_ref[0]

    @pl.when(used)
    def _():
        for j in range(EXPERT_GROUP):
            w1_ref, w3_ref, w2_ref = refs[3 * j:3 * j + 3]
            x = _unpack_rows(x_ref, j * blk, blk).astype(BF16)
            a = _silu(_dot(x, w1_ref[...])) * _dot(x, w3_ref[...])
            _pack_rows(o_ref, _dot(a.astype(BF16), w2_ref[...]), j * blk)

    @pl.when(jnp.logical_not(used))
    def _():
        o_ref[...] = jnp.zeros_like(o_ref)


def _experts(xs, block_e, n_used, w1, w3, w2, layer):
    d, de = w1.shape[2:]
    rows = EXPERT_BLOCK * EXPERT_GROUP
    n_steps = xs.shape[0] // rows

    def row_map(i, be, nu):
        return (jnp.minimum(i, (nu[0] + EXPERT_GROUP - 1) // EXPERT_GROUP - 1), 0, 0)

    w_specs = []
    for j in range(EXPERT_GROUP):
        w_map = lambda i, be, nu, j=j: (layer, be[i * EXPERT_GROUP + j], 0, 0)
        w_specs += [pl.BlockSpec((None, None, d, de), w_map),
                    pl.BlockSpec((None, None, d, de), w_map),
                    pl.BlockSpec((None, None, de, d), w_map)]

    return pl.pallas_call(
        _expert_kernel,
        grid_spec=pltpu.PrefetchScalarGridSpec(
            num_scalar_prefetch=2,
            grid=(n_steps,),
            in_specs=[pl.BlockSpec((rows, ROW_PARTS, LANES), row_map)] + w_specs,
            out_specs=pl.BlockSpec((rows, ROW_PARTS, LANES), lambda i, be, nu: (i, 0, 0))),
        out_shape=jax.ShapeDtypeStruct(xs.shape, U32),
        compiler_params=_params("arbitrary"),
        name="experts",
    )(block_e, n_used, xs, *([w1, w3, w2] * EXPERT_GROUP))


def _combine_kernel(loc_ref, tab_ref, ys_ref, slab_ref, h_ref, x_ref, gate_ref, s1_ref, s3_ref, s2_ref,
                    o_ref, loc_smem, tab_smem, gk_ref, stage_ref, lo_ref, hi_ref, sem, psem, *, tile_base):
    tm = MOE_TM
    words = tm * 8
    tile = tile_base + pl.program_id(0) * pl.num_programs(1) + pl.program_id(1)
    small = [pltpu.make_async_copy(loc_ref.at[pl.ds(tile * words, words)], loc_smem, psem),
             pltpu.make_async_copy(tab_ref.at[pl.ds(tile * TAB_WORDS, TAB_WORDS)], tab_smem, psem)]
    for cp in small:
        cp.start()
    for cp in small:
        cp.wait()
    _run_copies(tab_smem, 0, stage_ref, ys_ref, sem, to_far=False)
    hb = _unpack_rows(h_ref).astype(BF16)
    shared = _dot((_silu(_dot(hb, s1_ref[...])) * _dot(hb, s3_ref[...])).astype(BF16), s2_ref[...])
    slab = slab_ref[...]
    for k in range(TOP_K):
        gk_ref[k] = jnp.broadcast_to(slab[:, SLAB_GATE + k:SLAB_GATE + k + 1], (tm, LANES))
    _wait_tile(stage_ref, ys_ref, sem, to_far=False)

    flat = stage_ref.reshape(TOP_K * tm * ROW_PARTS, LANES)

    def mix(t, c):
        lo = hi = None
        for k in range(TOP_K):
            g = gk_ref[k, pl.ds(t, ROW_PARTS, stride=0), :]
            at = pl.multiple_of(loc_smem[t * 8 + k], ROW_PARTS)
            wl, wh = _unpack_words(flat[pl.ds(at, ROW_PARTS), :])
            lo = g * wl if lo is None else lo + g * wl
            hi = g * wh if hi is None else hi + g * wh
        lo_ref[t] = lo
        hi_ref[t] = hi
        return c

    lax.fori_loop(0, tm, mix, 0, unroll=4)
    routed = jnp.concatenate(_row_chunks(lo_ref) + _row_chunks(hi_ref), axis=-1)
    o_ref[...] = x_ref[...] + gate_ref[...] * (routed + shared)


def _combine(ys, loc_flat, tab_flat, slab, h2, x, gate, s1, s3, s2, tile_base):
    b, s, d = x.shape
    tm = MOE_TM
    kern = functools.partial(_combine_kernel, tile_base=tile_base)
    return pl.pallas_call(
        kern,
        grid=(b, s // tm),
        in_specs=[pl.BlockSpec(memory_space=pl.ANY),
                  pl.BlockSpec(memory_space=pl.ANY),
                  pl.BlockSpec(memory_space=pl.ANY),
                  pl.BlockSpec((None, tm, LANES), lambda bi, i: (bi, i, 0)),
                  pl.BlockSpec((None, tm, ROW_PARTS, LANES), lambda bi, i: (bi, i, 0, 0)),
                  pl.BlockSpec((None, tm, d), lambda bi, i: (bi, i, 0)),
                  pl.BlockSpec((None, 1, d), lambda bi, i: (bi, 0, 0)),
                  pl.BlockSpec(s1.shape, lambda bi, i: (0, 0)),
                  pl.BlockSpec(s3.shape, lambda bi, i: (0, 0)),
                  pl.BlockSpec(s2.shape, lambda bi, i: (0, 0))],
        out_specs=pl.BlockSpec((None, tm, d), lambda bi, i: (bi, i, 0)),
        out_shape=jax.ShapeDtypeStruct((b, s, d), F32),
        scratch_shapes=[pltpu.SMEM((tm * 8,), I32),
                        pltpu.SMEM((TAB_WORDS,), I32),
                        pltpu.VMEM((TOP_K, tm, LANES), F32),
                        pltpu.VMEM((TOP_K * tm, ROW_PARTS, LANES), U32),
                        pltpu.VMEM((tm, ROW_PARTS, LANES), F32),
                        pltpu.VMEM((tm, ROW_PARTS, LANES), F32),
                        pltpu.SemaphoreType.DMA,
                        pltpu.SemaphoreType.DMA],
        compiler_params=_params("arbitrary", "arbitrary"),
        name="combine",
    )(loc_flat, tab_flat, ys, slab, h2, x, gate, s1, s3, s2)


def _dft_tables(length, n_chan):
    scale = 1.0 / math.sqrt(length * n_chan)
    side = 1
    while side * side < length:
        side *= 2
    outer = length // side
    k = jnp.arange(length, dtype=I32)[:, None]
    a_idx = (k * jnp.arange(outer, dtype=I32)[None, :]) % outer
    b_idx = (k * jnp.arange(side, dtype=I32)[None, :]) % length
    ang_a = a_idx.astype(F32) * (2.0 * math.pi / outer)
    ang_b = b_idx.astype(F32) * (2.0 * math.pi / length)
    ca, sa = jnp.cos(ang_a)[:, :, None], jnp.sin(ang_a)[:, :, None]
    cb, sb = jnp.cos(ang_b)[:, None, :], jnp.sin(ang_b)[:, None, :]
    cos_t = (ca * cb - sa * sb).reshape(length, length)
    sin_t = (sa * cb + ca * sb).reshape(length, length)
    table = (jnp.concatenate([cos_t, -sin_t], axis=1) * scale).astype(BF16)
    return table


def _channel_table(n_chan, n_groups):
    m = jnp.arange(n_chan, dtype=I32)
    ang = ((m[:, None] * m[None, :]) % n_chan).astype(F32) * (2.0 * math.pi / n_chan)
    eye = jnp.eye(n_groups, dtype=F32)
    return jnp.concatenate([jnp.kron(eye, jnp.cos(ang)), jnp.kron(eye, jnp.sin(ang))], axis=1).astype(BF16)


def _rope_tables(n_tok):
    rows = n_tok // GRID_W
    axis_dim = HEAD_DIM // 2
    r = jnp.repeat(jnp.arange(rows, dtype=F32), GRID_W)
    col = jnp.tile(jnp.arange(GRID_W, dtype=F32), rows)
    inv = ROPE_BASE ** (-jnp.arange(0, axis_dim, 2, dtype=F32) / axis_dim)
    ar, ac = r[:, None] * inv, col[:, None] * inv
    cos = jnp.concatenate([jnp.cos(ar), jnp.cos(ar), jnp.cos(ac), jnp.cos(ac)], axis=1)
    sin = jnp.concatenate([-jnp.sin(ar), jnp.sin(ar), -jnp.sin(ac), jnp.sin(ac)], axis=1)
    return jnp.tile(cos, (1, N_HEADS)), jnp.tile(sin, (1, N_HEADS))


def _head_mean_matrix(width):
    h = jnp.arange(width) // HEAD_DIM
    return ((h[:, None] == h[None, :]).astype(F32) / HEAD_DIM).astype(BF16)


def _tile_matrices():
    src = jnp.arange(N_KV * HEAD_DIM)
    dst = jnp.arange(Q_PER_KV * HEAD_DIM)
    mats = [((src[:, None] // HEAD_DIM == h) & (src[:, None] % HEAD_DIM == dst[None, :] % HEAD_DIM))
            for h in range(N_KV)]
    return jnp.stack(mats).astype(BF16)


def _even_layer(x, xc, mod, modc, norm_g, w_in, ws, bs, w_out, tables):
    b, s, d = x.shape
    w = w_in.shape[1] // 3
    outs = []
    for stream, m, tm in ((x, mod, MIX_TM), (xc, modc, 256)):
        if stream is None:
            outs.append(None)
            continue
        length = stream.shape[1]
        tm = min(tm, length)
        uvf = _norm_mod_matmul(stream, norm_g, m[0], m[1], w_in, tm)
        gm, z = _even_mix(uvf, ws, bs, tables["chan"], tm)
        table = tables["pos"][length]
        y = _matmul(table, z.reshape(2 * length, b * w),
                    min(1024, length), min(1024, b * w), min(2048, 2 * length), BF16)
        outs.append(_outproj(
            gm, pl.BlockSpec((None, tm, w), lambda bi, i: (bi, i, 0)),
            y, pl.BlockSpec((tm, w), lambda bi, i: (i, bi)),
            w_out, stream, m[2], tm))
    return outs


def _odd_layer(x, xc, mod, modc, norm_g, w_in, qg, kg, sink, conv_w, w_out, tables, ctx_out):
    b, s, d = x.shape
    lc = xc.shape[1]
    half = N_HEADS * HEAD_DIM
    tm, tmc = MIX_TM, min(256, lc)
    proj = _norm_mod_matmul(x, norm_g, mod[0], mod[1], w_in, tm)
    projc = _norm_mod_matmul(xc, norm_g, modc[0], modc[1], w_in, tmc)
    prep = functools.partial(_qkv_prep, qg=qg, kg=kg, bdq=tables["bdq"], bdk=tables["bdk"], tile=tables["tile"])
    q, k4, v4 = prep(proj, tables["cos"], tables["sin"], tm=tm, with_q=True)
    qc, kc4, vc4 = prep(projc, tables["cos_c"], tables["sin_c"], tm=tmc, with_q=ctx_out)
    att = _attention(q, k4, v4, kc4, vc4, sink, band=True)
    conv = _short_conv(proj, conv_w)
    spec = lambda t: pl.BlockSpec((None, t, half), lambda bi, i: (bi, i, 0))
    y = _outproj(att, spec(tm), conv, spec(tm), w_out, x, mod[2], tm)
    yc = None
    if ctx_out:
        attc = _attention(qc, None, None, kc4, vc4, sink, band=False)
        convc = _short_conv(projc, conv_w)
        yc = _outproj(attc, spec(tmc), convc, spec(tmc), w_out, xc, modc[2], tmc)
    return y, yc


def _moe(x, xc, mod, modc, norm_g, rw_hi, rw_lo, rb, w1, w3, w2, layer, s1, s3, s2, tri, upper):
    b, s, d = x.shape
    n_lat = b * s
    xc_shape = None
    if xc is not None and xc.shape[1] % MOE_TM:
        xc_shape = xc.shape
        xc = xc.reshape(-1, MOE_TM, d)
        modc = [m[:xc.shape[0]] for m in modc]
    counts0 = jnp.zeros((1, LANES), F32)
    h2, slab, tab, counts = _route(x, norm_g, mod[3], mod[4], rw_hi, rw_lo, rb, counts0, tri, upper)
    slabs, tabs = [slab.reshape(n_lat, LANES)], [tab]
    n_tok = n_lat
    if xc is not None:
        h2c, slabc, tabc, counts = _route(xc, norm_g, modc[3], modc[4], rw_hi, rw_lo, rb, counts, tri, upper)
        slabs.append(slabc.reshape(-1, LANES))
        tabs.append(tabc)
        n_tok += slabs[1].shape[0]
    cnt = counts[0, :N_EXPERTS].astype(I32)
    blk = EXPERT_BLOCK
    padded = (cnt + blk - 1) // blk * blk
    ends = jnp.cumsum(padded).astype(I32)
    starts = ends - padded
    step_rows = blk * EXPERT_GROUP
    n_rows = (n_tok * TOP_K + N_EXPERTS * (blk - 1) + step_rows - 1) // step_rows * step_rows
    n_blocks = n_rows // blk
    n_used = (ends[-1] // blk).reshape(1).astype(I32)
    blk_start = jnp.minimum(jnp.arange(n_blocks, dtype=I32), n_used[0] - 1) * blk
    block_e = jnp.minimum(jnp.sum(blk_start[:, None] >= ends[None, :], axis=1), N_EXPERTS - 1).astype(I32)
    slab_all = jnp.concatenate(slabs, axis=0)
    tab_all = jnp.concatenate(tabs, axis=0)[:, :, :N_EXPERTS].astype(I32)
    runs = jnp.concatenate([tab_all[:, 0], tab_all[:, 1], tab_all[:, 2] + starts[None, :]], axis=1)
    tab_flat = jnp.pad(runs, ((0, 0), (0, TAB_WORDS - runs.shape[1]))).reshape(-1)
    loc_flat = (slab_all[:, SLAB_LOC:SLAB_LOC + 8].astype(I32) * ROW_PARTS).reshape(-1)
    packed = lambda a: a.reshape(-1, ROW_PARTS, LANES)
    xs = _dispatch(packed(h2), loc_flat, tab_flat, ends, n_used, None, n_rows, 0)
    if xc is not None:
        xs = _dispatch(packed(h2c), loc_flat, tab_flat, ends, n_used, xs, n_rows, n_lat // MOE_TM)
    ys = _experts(xs, block_e, n_used, w1, w3, w2, layer)
    x_new = _combine(ys, loc_flat, tab_flat, slab, h2, x, mod[5], s1, s3, s2, 0)
    xc_new = None
    if xc is not None:
        xc_new = _combine(ys, loc_flat, tab_flat, slabc, h2c, xc, modc[5], s1, s3, s2, n_lat // MOE_TM)
        if xc_shape is not None:
            xc_new = xc_new.reshape(xc_shape)
    return x_new, xc_new


def kernel(x, c, ctx, c_ctx, ada_w, ada_b, norm1_g, norm2_g, ev_w_in, ev_w_s, ev_b_s, ev_w_out, od_w_in, od_q_norm_g, od_k_norm_g, od_sink, od_conv_w, od_w_out, router_w, router_b, exp_w_gate, exp_w_up, exp_w_down, sh_w_gate, sh_w_up, sh_w_down):
    b, s, d = x.shape
    lc = ctx.shape[1]
    depth = ada_w.shape[0]
    n_groups = ev_w_s.shape[1]
    half = d // 2

    rows = -(-(b + 1) // 8) * 8
    cond = jnp.zeros((rows, d), F32).at[:b].set(c).at[b].set(c_ctx)
    mod_all = _adaln(cond, ada_w, ada_b)

    tables = {
        "chan": _channel_table(LANES, n_groups),
        "pos": {s: _dft_tables(s, LANES), lc: _dft_tables(lc, LANES)},
        "bdq": _head_mean_matrix(N_HEADS * HEAD_DIM),
        "bdk": _head_mean_matrix(N_KV * HEAD_DIM),
        "tile": _tile_matrices(),
    }
    tables["cos"], tables["sin"] = _rope_tables(s)
    tables["cos_c"] = jnp.ones((lc, N_HEADS * HEAD_DIM), F32)
    tables["sin_c"] = jnp.zeros((lc, N_HEADS * HEAD_DIM), F32)
    tri = (jnp.arange(MOE_TM)[:, None] > jnp.arange(MOE_TM)[None, :]).astype(BF16)
    upper = (jnp.arange(LANES)[:, None] < jnp.arange(LANES)[None, :]).astype(BF16)

    qw, kw = N_HEADS * HEAD_DIM, N_KV * HEAD_DIM
    perm = jnp.concatenate([jnp.arange(0, qw), jnp.arange(qw + 2 * kw, qw + 2 * kw + 3 * half),
                            jnp.arange(qw, qw + 2 * kw)])

    w1_all, w3_all, w2_all = exp_w_gate.astype(BF16), exp_w_up.astype(BF16), exp_w_down.astype(BF16)
    xc = ctx
    for l in range(depth):
        last = l == depth - 1
        even = l % 2 == 0
        need_ctx = not (last and even)
        pieces = [mod_all[l, :, j * d:(j + 1) * d] for j in range(6)]
        mod = [p[:b].reshape(b, 1, d) for p in pieces]
        modc = [jnp.broadcast_to(p[b].reshape(1, 1, d), (b, 1, d)) for p in pieces]
        g1 = norm1_g[l].reshape(1, d)
        g2 = norm2_g[l].reshape(1, d)
        if even:
            e = l // 2
            bs = jnp.broadcast_to(ev_b_s[e][:, :, None], (n_groups, CHUNK, LANES))
            y, yc = _even_layer(x, xc if (need_ctx and not last) else None, mod, modc, g1,
                                ev_w_in[e].astype(BF16), ev_w_s[e].astype(BF16), bs,
                                ev_w_out[e].astype(BF16), tables)
        else:
            o = l // 2
            qg = (jnp.tile(od_q_norm_g[o], N_HEADS) * (HEAD_DIM ** -0.5)).reshape(1, qw)
            kg = jnp.tile(od_k_norm_g[o], N_KV).reshape(1, kw)
            y, yc = _odd_layer(x, xc, mod, modc, g1, od_w_in[o][:, perm].astype(BF16), qg, kg,
                               od_sink[o], od_conv_w[o], od_w_out[o].astype(BF16), tables, not last)
        x = y
        if not last:
            xc = yc
        rw = jnp.zeros((d, LANES), F32).at[:, :N_EXPERTS].set(router_w[l])
        rw_hi = rw.astype(BF16)
        rw_lo = (rw - rw_hi.astype(F32)).astype(BF16)
        rb = jnp.zeros((1, LANES), F32).at[0, :N_EXPERTS].set(router_b[l])
        x, xc_new = _moe(x, None if last else xc, mod, modc, g2, rw_hi, rw_lo, rb, w1_all, w3_all, w2_all, l,
                         sh_w_gate[l].astype(BF16), sh_w_up[l].astype(BF16), sh_w_down[l].astype(BF16),
                         tri, upper)
        if not last:
            xc = xc_new
    return x
```

```python
import functools
import math

import jax
import jax.numpy as jnp
from jax import lax
from jax.experimental import pallas as pl
from jax.experimental.pallas import tpu as pltpu

F32 = jnp.float32
BF16 = jnp.bfloat16
I32 = jnp.int32
U32 = jnp.uint32

LANES = 128
VMEM_LIMIT = 48 * 2**20

EPS = 1e-6
GRID_W = 64
CHUNK = 128
HEAD_DIM = 64
N_HEADS = 8
N_KV = 2
Q_PER_KV = N_HEADS // N_KV
MIX_TM = 1024
ATT_BLOCK = 128
ATT_ROWS = 64
ROPE_BASE = 10000.0
N_EXPERTS = 64
TOP_K = 6
ROUTED_SCALE = 2.5
EXPERT_BLOCK = 512
EXPERT_GROUP = 2
MOE_TM = 1024
ROW_PARTS = 4
SLAB_IDX, SLAB_LOC, SLAB_GATE = 0, 8, 16
TAB_WORDS = 1024


def _params(*sem):
    return pltpu.CompilerParams(dimension_semantics=sem, vmem_limit_bytes=VMEM_LIMIT)


def _sigmoid(x):
    return 1.0 / (1.0 + jnp.exp(-x))


def _silu(x):
    return x * _sigmoid(x)


def _gelu_tanh(x):
    c = math.sqrt(2.0 / math.pi)
    return x * (0.5 * (1.0 + jnp.tanh(c * (x + 0.044715 * (x * x * x)))))


def _dot(a, b):
    return jnp.dot(a, b, preferred_element_type=F32)


def _unpack_words(w):
    return pltpu.bitcast(w << 16, F32), pltpu.bitcast(w & jnp.uint32(0xFFFF0000), F32)


def _row_chunks(ref, row0=0, n=None):
    total, parts, lanes = ref.shape
    n = total if n is None else n
    flat = ref.reshape(total * parts, lanes)
    return [flat[pl.ds(row0 * parts + c, n, stride=parts), :] for c in range(parts)]


def _unpack_rows(ref, row0=0, n=None):
    halves = [_unpack_words(w) for w in _row_chunks(ref, row0, n)]
    return jnp.concatenate([h[0] for h in halves] + [h[1] for h in halves], axis=-1)


def _pack_rows(ref, val, row0=0):
    total, parts, lanes = ref.shape
    n, half = val.shape[0], val.shape[1] // 2
    bits = pltpu.bitcast(val.astype(BF16).astype(F32), U32)
    words = (bits[:, :half] >> 16) | (bits[:, half:] & jnp.uint32(0xFFFF0000))
    flat = ref.reshape(total * parts, lanes)
    for c in range(parts):
        flat[pl.ds(row0 * parts + c, n, stride=parts), :] = words[:, c * lanes:(c + 1) * lanes]


def _adaln_kernel(c_ref, w_ref, b_ref, o_ref):
    o_ref[...] = _dot(_silu(c_ref[...]), w_ref[...]) + b_ref[...]


def _adaln(cond, ada_w, ada_b):
    n_layers, d, n6 = ada_w.shape
    rows = cond.shape[0]
    tn = 768
    return pl.pallas_call(
        _adaln_kernel,
        grid=(n_layers, n6 // tn),
        in_specs=[pl.BlockSpec((rows, d), lambda l, j: (0, 0)),
                  pl.BlockSpec((None, d, tn), lambda l, j: (l, 0, j)),
                  pl.BlockSpec((None, 1, tn), lambda l, j: (l, 0, j))],
        out_specs=pl.BlockSpec((None, rows, tn), lambda l, j: (l, 0, j)),
        out_shape=jax.ShapeDtypeStruct((n_layers, rows, n6), F32),
        compiler_params=_params("parallel", "parallel"),
        name="adaln",
    )(cond, ada_w, ada_b.reshape(n_layers, 1, n6))


def _norm_mod(x, g, shift, scale):
    ms = jnp.mean(x * x, axis=-1, keepdims=True)
    h = (x * lax.rsqrt(ms + EPS)) * g
    return h * (1.0 + scale) + shift


def _nmm_kernel(x_ref, g_ref, sh_ref, sc_ref, w_ref, o_ref):
    h = _norm_mod(x_ref[...], g_ref[...], sh_ref[...], sc_ref[...])
    o_ref[...] = _dot(h.astype(BF16), w_ref[...]).astype(o_ref.dtype)


def _norm_mod_matmul(x, g, shift, scale, w, tm):
    b, s, d = x.shape
    n = w.shape[1]
    return pl.pallas_call(
        _nmm_kernel,
        grid=(b, s // tm),
        in_specs=[pl.BlockSpec((None, tm, d), lambda bi, i: (bi, i, 0)),
                  pl.BlockSpec((1, d), lambda bi, i: (0, 0)),
                  pl.BlockSpec((None, 1, d), lambda bi, i: (bi, 0, 0)),
                  pl.BlockSpec((None, 1, d), lambda bi, i: (bi, 0, 0)),
                  pl.BlockSpec((d, n), lambda bi, i: (0, 0))],
        out_specs=pl.BlockSpec((None, tm, n), lambda bi, i: (bi, i, 0)),
        out_shape=jax.ShapeDtypeStruct((b, s, n), BF16),
        compiler_params=_params("parallel", "parallel"),
        name="norm_mod_matmul",
    )(x, g, shift, scale, w)


def _even_mix_kernel(x_ref, g_ref, sh_ref, sc_ref, w_ref, ws_ref, bs_ref, cs_ref, gm_ref, z_ref, p_ref,
                     *, n_chunks, n_groups):
    h = _norm_mod(x_ref[...], g_ref[...], sh_ref[...], sc_ref[...])
    p_ref[...] = _dot(h.astype(BF16), w_ref[...])
    half = gm_ref.shape[1]
    for c in range(n_chunks):
        rows = slice(c * CHUNK, (c + 1) * CHUNK)
        for g in range(n_groups):
            cols = slice(g * LANES, (g + 1) * LANES)
            ug = _gelu_tanh(p_ref[rows, cols])
            vg = _gelu_tanh(p_ref[rows, half + g * LANES:half + (g + 1) * LANES])
            mu = jnp.mean(vg, axis=-1, keepdims=True)
            dv = vg - mu
            var = jnp.mean(dv * dv, axis=-1, keepdims=True)
            vn = dv * lax.rsqrt(var + 1e-5)
            fg = _dot(ws_ref[g], vn.astype(BF16)) + bs_ref[g]
            gm_ref[rows, cols] = (ug * fg).astype(gm_ref.dtype)
    fz = _dot(p_ref[:, 2 * half:].astype(BF16), cs_ref[...])
    z_ref[0] = fz[:, :half].astype(z_ref.dtype)
    z_ref[1] = fz[:, half:].astype(z_ref.dtype)


def _even_mix(x, g, shift, scale, w_in, ws, bs, cs, tm):
    b, s, d = x.shape
    n3 = w_in.shape[1]
    w = n3 // 3
    n_groups = w // LANES
    kern = functools.partial(_even_mix_kernel, n_chunks=tm // CHUNK, n_groups=n_groups)
    return pl.pallas_call(
        kern,
        grid=(b, s // tm),
        in_specs=[pl.BlockSpec((None, tm, d), lambda bi, i: (bi, i, 0)),
                  pl.BlockSpec((1, d), lambda bi, i: (0, 0)),
                  pl.BlockSpec((None, 1, d), lambda bi, i: (bi, 0, 0)),
                  pl.BlockSpec((None, 1, d), lambda bi, i: (bi, 0, 0)),
                  pl.BlockSpec((d, n3), lambda bi, i: (0, 0)),
                  pl.BlockSpec(ws.shape, lambda bi, i: (0, 0, 0)),
                  pl.BlockSpec(bs.shape, lambda bi, i: (0, 0, 0)),
                  pl.BlockSpec(cs.shape, lambda bi, i: (0, 0))],
        out_specs=[pl.BlockSpec((None, tm, w), lambda bi, i: (bi, i, 0)),
                   pl.BlockSpec((2, tm, w), lambda bi, i: (0, i, bi))],
        out_shape=[jax.ShapeDtypeStruct((b, s, w), BF16),
                   jax.ShapeDtypeStruct((2, s, b * w), BF16)],
        scratch_shapes=[pltpu.VMEM((tm, n3), F32)],
        compiler_params=_params("parallel", "parallel"),
        name="even_mix",
    )(x, g, shift, scale, w_in, ws, bs, cs)


def _mm_kernel(a_ref, b_ref, o_ref, acc_ref):
    k = pl.program_id(2)

    @pl.when(k == 0)
    def _():
        acc_ref[...] = jnp.zeros_like(acc_ref)

    acc_ref[...] += _dot(a_ref[...], b_ref[...])

    @pl.when(k == pl.num_programs(2) - 1)
    def _():
        o_ref[...] = acc_ref[...].astype(o_ref.dtype)


def _matmul(a, b, tm, tn, tk, out_dtype):
    m, kd = a.shape
    n = b.shape[1]
    return pl.pallas_call(
        _mm_kernel,
        grid=(m // tm, n // tn, kd // tk),
        in_specs=[pl.BlockSpec((tm, tk), lambda i, j, k: (i, k)),
                  pl.BlockSpec((tk, tn), lambda i, j, k: (k, j))],
        out_specs=pl.BlockSpec((tm, tn), lambda i, j, k: (i, j)),
        out_shape=jax.ShapeDtypeStruct((m, n), out_dtype),
        scratch_shapes=[pltpu.VMEM((tm, tn), F32)],
        compiler_params=_params("parallel", "parallel", "arbitrary"),
        name="dft_matmul",
    )(a, b)


def _outproj_kernel(a_ref, b_ref, w_ref, x_ref, gate_ref, o_ref):
    ab = jnp.concatenate([a_ref[...], b_ref[...]], axis=-1)
    o_ref[...] = x_ref[...] + gate_ref[...] * _dot(ab, w_ref[...])


def _outproj(a, a_spec, bsrc, b_spec, w, x, gate, tm):
    b, s, d = x.shape
    return pl.pallas_call(
        _outproj_kernel,
        grid=(b, s // tm),
        in_specs=[a_spec, b_spec,
                  pl.BlockSpec(w.shape, lambda bi, i: (0, 0)),
                  pl.BlockSpec((None, tm, d), lambda bi, i: (bi, i, 0)),
                  pl.BlockSpec((None, 1, d), lambda bi, i: (bi, 0, 0))],
        out_specs=pl.BlockSpec((None, tm, d), lambda bi, i: (bi, i, 0)),
        out_shape=jax.ShapeDtypeStruct((b, s, d), F32),
        compiler_params=_params("parallel", "parallel"),
        name="outproj",
    )(a, bsrc, w, x, gate)


def _head_rms(x, bd_ref):
    xx = x * x
    hi = xx.astype(BF16)
    lo = (xx - hi.astype(F32)).astype(BF16)
    ms = _dot(hi, bd_ref[...]) + _dot(lo, bd_ref[...])
    return x * lax.rsqrt(ms + EPS)


def _rope(x, cos, sins):
    width = x.shape[1]
    lane = lax.broadcasted_iota(I32, x.shape, 1)
    first = (lane & 31) < 16
    swapped = jnp.where(first, pltpu.roll(x, width - 16, 1), pltpu.roll(x, 16, 1))
    return x * cos + swapped * sins


def _qkv_kernel(q_ref, k_ref, v_ref, cos_ref, sin_ref, qg_ref, kg_ref, bdq_ref, bdk_ref, tile_ref, tile_t_ref,
                qo_ref, k4_ref, v4_ref, *, with_q):
    kw = k_ref.shape[1]
    if with_q:
        qn = _head_rms(q_ref[...].astype(F32), bdq_ref) * qg_ref[...]
        qo_ref[...] = _rope(qn, cos_ref[...], sin_ref[...]).astype(qo_ref.dtype)
    else:
        qo_ref[...] = jnp.zeros_like(qo_ref)
    kn = _head_rms(k_ref[...].astype(F32), bdk_ref) * kg_ref[...]
    kr = _rope(kn, cos_ref[:, :kw], sin_ref[:, :kw]).astype(BF16)
    vb = v_ref[...].astype(BF16)
    for h in range(N_KV):
        k4_ref[h] = lax.dot_general(tile_t_ref[h], kr, (((1,), (1,)), ((), ())),
                                    preferred_element_type=F32).astype(k4_ref.dtype)
        v4_ref[h] = _dot(vb, tile_ref[h]).astype(v4_ref.dtype)


def _qkv_prep(proj, cos, sin, qg, kg, bdq, bdk, tile, tm, with_q):
    b, s, _ = proj.shape
    tile_t = jnp.swapaxes(tile, 1, 2)
    qw = N_HEADS * HEAD_DIM
    kw = N_KV * HEAD_DIM
    rep = Q_PER_KV * HEAD_DIM
    k_blk = (4 * qw) // kw
    kern = functools.partial(_qkv_kernel, with_q=with_q)
    return pl.pallas_call(
        kern,
        grid=(b, s // tm),
        in_specs=[pl.BlockSpec((None, tm, qw), lambda bi, i: (bi, i, 0)),
                  pl.BlockSpec((None, tm, kw), lambda bi, i: (bi, i, k_blk)),
                  pl.BlockSpec((None, tm, kw), lambda bi, i: (bi, i, k_blk + 1)),
                  pl.BlockSpec((tm, qw), lambda bi, i: (i, 0)),
                  pl.BlockSpec((tm, qw), lambda bi, i: (i, 0)),
                  pl.BlockSpec((1, qw), lambda bi, i: (0, 0)),
                  pl.BlockSpec((1, kw), lambda bi, i: (0, 0)),
                  pl.BlockSpec(bdq.shape, lambda bi, i: (0, 0)),
                  pl.BlockSpec(bdk.shape, lambda bi, i: (0, 0)),
                  pl.BlockSpec(tile.shape, lambda bi, i: (0, 0, 0)),
                  pl.BlockSpec(tile_t.shape, lambda bi, i: (0, 0, 0))],
        out_specs=[pl.BlockSpec((None, tm, qw), lambda bi, i: (bi, i, 0)),
                   pl.BlockSpec((None, N_KV, rep, tm), lambda bi, i: (bi, 0, 0, i)),
                   pl.BlockSpec((None, N_KV, tm, rep), lambda bi, i: (bi, 0, i, 0))],
        out_shape=[jax.ShapeDtypeStruct((b, s, qw), BF16),
                   jax.ShapeDtypeStruct((b, N_KV, rep, s), BF16),
                   jax.ShapeDtypeStruct((b, N_KV, s, rep), BF16)],
        compiler_params=_params("parallel", "parallel"),
        name="qkv_prep",
    )(proj, proj, proj, cos, sin, qg, kg, bdq, bdk, tile, tile_t)


def _conv_kernel(gi_ref, go_ref, z_ref, w_ref, o_ref, *, rows):
    s = z_ref.shape[0]
    n = s // rows
    w0, w1, w2 = w_ref[0:1, :], w_ref[1:2, :], w_ref[2:3, :]
    ridx = lax.broadcasted_iota(I32, (rows, z_ref.shape[1]), 0)
    prev_last = jnp.zeros((1, z_ref.shape[1]), F32)

    def load(ref, start, count):
        return ref[start:start + count, :].astype(F32)

    for j in range(n):
        r0 = j * rows
        zc = load(gi_ref, r0, rows) * load(z_ref, r0, rows)
        if j + 1 < n:
            nxt = (load(gi_ref, r0 + rows, 16) * load(z_ref, r0 + rows, 16))[0:1, :]
        else:
            nxt = jnp.zeros_like(prev_last)
        zp = jnp.where(ridx == 0, prev_last, pltpu.roll(zc, 1, 0))
        zn = jnp.where(ridx == rows - 1, nxt, pltpu.roll(zc, rows - 1, 0))
        y = zp * w0 + zc * w1 + zn * w2
        o_ref[r0:r0 + rows, :] = (load(go_ref, r0, rows) * y).astype(o_ref.dtype)
        prev_last = zc[rows - 1:rows, :]


def _short_conv(proj, conv_w):
    b, s, _ = proj.shape
    dc = conv_w.shape[1]
    nb = dc // LANES
    base = dc // LANES
    rows = min(s, 512)
    kern = functools.partial(_conv_kernel, rows=rows)
    return pl.pallas_call(
        kern,
        grid=(b, nb),
        in_specs=[pl.BlockSpec((None, s, LANES), lambda bi, c: (bi, 0, base + c)),
                  pl.BlockSpec((None, s, LANES), lambda bi, c: (bi, 0, 2 * base + c)),
                  pl.BlockSpec((None, s, LANES), lambda bi, c: (bi, 0, 3 * base + c)),
                  pl.BlockSpec((conv_w.shape[0], LANES), lambda bi, c: (0, c))],
        out_specs=pl.BlockSpec((None, s, LANES), lambda bi, c: (bi, 0, c)),
        out_shape=jax.ShapeDtypeStruct((b, s, dc), BF16),
        compiler_params=_params("parallel", "parallel"),
        name="short_conv",
    )(proj, proj, proj, conv_w)


def _attn_kernel(sink_ref, q_ref, *refs, band, n_blocks):
    o_ref = refs[-1]
    i = pl.program_id(1)
    t = q_ref.shape[0]
    w = Q_PER_KV * HEAD_DIM
    lane = lax.broadcasted_iota(I32, (t, w), 1)
    masks = [(lane >= g * HEAD_DIM) & (lane < (g + 1) * HEAD_DIM) for g in range(Q_PER_KV)]
    bad = None
    if band:
        n_keys = 3 * t + refs[7].shape[1]
        row = lax.broadcasted_iota(I32, (Q_PER_KV * t, n_keys), 0) & (t - 1)
        col = lax.broadcasted_iota(I32, (Q_PER_KV * t, n_keys), 1)
        off_prev = jnp.where(i > 0, 0, 4 * t)
        off_next = jnp.where(i < n_blocks - 1, 0, 4 * t)
        bad_prev = (col < t) & (col < row + off_prev)
        bad_next = (col >= 2 * t) & (col < 3 * t) & (col - 2 * t > row - off_next)
        bad = bad_prev | bad_next
    for h in range(N_KV):
        if band:
            kp, kc_, kn, vp, vc_, vn, kx, vx = refs[:-1]
            kcat = jnp.concatenate([kp[h], kc_[h], kn[h], kx[h]], axis=1)
            vcat = jnp.concatenate([vp[h], vc_[h], vn[h], vx[h]], axis=0)
        else:
            kx, vx = refs[:-1]
            kcat, vcat = kx[h], vx[h]
        q = q_ref[:, h * w:(h + 1) * w]
        q4 = jnp.concatenate([jnp.where(m, q, jnp.zeros_like(q)) for m in masks], axis=0)
        s = _dot(q4, kcat)
        es, invs = [], []
        for c in range(Q_PER_KV * t // ATT_ROWS):
            rows = slice(c * ATT_ROWS, (c + 1) * ATT_ROWS)
            sc = s[rows]
            if band:
                sc = jnp.where(bad[rows], -jnp.inf, sc)
            sink = jnp.full((ATT_ROWS, 1), sink_ref[h * Q_PER_KV + (c * ATT_ROWS) // t], F32)
            m = jnp.maximum(jnp.max(sc, axis=-1, keepdims=True), sink)
            ec = jnp.exp(sc - m)
            invs.append(1.0 / (jnp.sum(ec, axis=-1, keepdims=True) + jnp.exp(sink - m)))
            es.append(ec.astype(BF16))
        e = jnp.concatenate(es, axis=0)
        r = _dot(e, vcat) * jnp.concatenate(invs, axis=0)
        o = jnp.zeros((t, w), F32)
        for g in range(Q_PER_KV):
            o = o + jnp.where(masks[g], r[g * t:(g + 1) * t, :], 0.0)
        o_ref[:, h * w:(h + 1) * w] = o.astype(o_ref.dtype)


def _attention(q, k4, v4, kx4, vx4, sink, band):
    b, s, qw = q.shape
    rep = vx4.shape[-1]
    lc = vx4.shape[2]
    t = ATT_BLOCK
    nb = s // t
    kern = functools.partial(_attn_kernel, band=band, n_blocks=nb)

    def k_spec(off):
        return pl.BlockSpec((None, N_KV, rep, t),
                            lambda bi, i: (bi, 0, 0, jnp.clip(i + off, 0, nb - 1)))

    def v_spec(off):
        return pl.BlockSpec((None, N_KV, t, rep),
                            lambda bi, i: (bi, 0, jnp.clip(i + off, 0, nb - 1), 0))

    in_specs = [pl.BlockSpec(memory_space=pltpu.SMEM),
                pl.BlockSpec((None, t, qw), lambda bi, i: (bi, i, 0))]
    args = [sink, q]
    if band:
        in_specs += [k_spec(-1), k_spec(0), k_spec(1), v_spec(-1), v_spec(0), v_spec(1)]
        args += [k4, k4, k4, v4, v4, v4]
    in_specs += [pl.BlockSpec((None, N_KV, rep, lc), lambda bi, i: (bi, 0, 0, 0)),
                 pl.BlockSpec((None, N_KV, lc, rep), lambda bi, i: (bi, 0, 0, 0))]
    args += [kx4, vx4]
    return pl.pallas_call(
        kern,
        grid=(b, nb),
        in_specs=in_specs,
        out_specs=pl.BlockSpec((None, t, qw), lambda bi, i: (bi, i, 0)),
        out_shape=jax.ShapeDtypeStruct((b, s, qw), BF16),
        compiler_params=_params("parallel", "parallel"),
        name="attention_band" if band else "attention_ctx",
    )(*args)


def _route_kernel(*refs, with_outproj):
    if with_outproj:
        a_ref, b_ref, wout_ref, xin_ref, gate1_ref = refs[:5]
        xnew_ref = refs[14]
        refs = refs[5:14] + refs[15:]
        x = xin_ref[...] + gate1_ref[...] * _dot(jnp.concatenate([a_ref[...], b_ref[...]], axis=-1), wout_ref[...])
        xnew_ref[...] = x
    else:
        x = refs[0][...]
        refs = refs[1:]
    (g_ref, sh_ref, sc_ref, whi_ref, wlo_ref, rb_ref, cin_ref, tri_ref, upper_ref,
     h_ref, slab_ref, tab_ref, cnt_ref, carry_ref) = refs
    first = (pl.program_id(0) == 0) & (pl.program_id(1) == 0)

    @pl.when(first)
    def _():
        carry_ref[...] = cin_ref[...]

    h = _norm_mod(x, g_ref[...], sh_ref[...], sc_ref[...])
    _pack_rows(h_ref, h)
    hi = h.astype(BF16)
    lo = (h - hi.astype(F32)).astype(BF16)
    logits = _dot(hi, whi_ref[...]) + _dot(lo, whi_ref[...]) + _dot(hi, wlo_ref[...])
    scores = _sigmoid(logits)
    tm, lanes = scores.shape
    lane = lax.broadcasted_iota(I32, (tm, lanes), 1).astype(F32)
    work = jnp.where(lane < N_EXPERTS, scores + rb_ref[...], -jnp.inf)
    hits, idxs, gates = [], [], []
    for _ in range(TOP_K):
        mx = jnp.max(work, axis=-1, keepdims=True)
        idx = jnp.min(jnp.where(work == mx, lane, float(lanes)), axis=-1, keepdims=True)
        hit = lane == idx
        hits.append(hit)
        idxs.append(idx)
        gates.append(jnp.sum(jnp.where(hit, scores, 0.0), axis=-1, keepdims=True))
        work = jnp.where(hit, -jnp.inf, work)
    gsum = gates[0]
    for gk in gates[1:]:
        gsum = gsum + gk
    gscale = ROUTED_SCALE / (gsum + 1e-20)
    onehot = jnp.zeros((tm, lanes), F32)
    for hit in hits:
        onehot = jnp.where(hit, 1.0, onehot)
    cnt = jnp.sum(onehot, axis=0, keepdims=True)
    cnt_hi = jnp.floor(cnt * (1.0 / 256.0))
    cnt_lo = cnt - 256.0 * cnt_hi
    parts = jnp.concatenate([jnp.broadcast_to(cnt_hi, (8, lanes)), jnp.broadcast_to(cnt_lo, (8, lanes))], axis=0)
    sums = _dot(parts.astype(BF16), upper_ref[...])
    seg = 256.0 * sums[0:1] + sums[8:9]
    before = _dot(tri_ref[...], onehot.astype(BF16)) + seg
    slab = jnp.zeros((tm, lanes), F32)
    for k in range(TOP_K):
        loc = jnp.sum(jnp.where(hits[k], before, 0.0), axis=-1, keepdims=True)
        slab = jnp.where(lane == SLAB_IDX + k, idxs[k], slab)
        slab = jnp.where(lane == SLAB_LOC + k, loc, slab)
        slab = jnp.where(lane == SLAB_GATE + k, gates[k] * gscale, slab)
    slab_ref[...] = slab
    row = lax.broadcasted_iota(I32, (8, lanes), 0)
    tab_ref[...] = jnp.where(row == 0, cnt, jnp.where(row == 1, seg, jnp.where(row == 2, carry_ref[...], 0.0)))
    carry_ref[...] = carry_ref[...] + cnt
    cnt_ref[...] = carry_ref[...]


def _route(x, g, shift, scale, whi, wlo, rb, counts_in, tri, upper, pending=None):
    b, s, d = x.shape
    tm = MOE_TM
    nt = s // tm
    x_spec = pl.BlockSpec((None, tm, d), lambda bi, i: (bi, i, 0))
    in_specs = [x_spec,
                pl.BlockSpec((1, d), lambda bi, i: (0, 0)),
                pl.BlockSpec((None, 1, d), lambda bi, i: (bi, 0, 0)),
                pl.BlockSpec((None, 1, d), lambda bi, i: (bi, 0, 0)),
                pl.BlockSpec(whi.shape, lambda bi, i: (0, 0)),
                pl.BlockSpec(wlo.shape, lambda bi, i: (0, 0)),
                pl.BlockSpec((1, LANES), lambda bi, i: (0, 0)),
                pl.BlockSpec((1, LANES), lambda bi, i: (0, 0)),
                pl.BlockSpec((tm, tm), lambda bi, i: (0, 0)),
                pl.BlockSpec((LANES, LANES), lambda bi, i: (0, 0))]
    args = [x, g, shift, scale, whi, wlo, rb, counts_in, tri, upper]
    out_specs = [pl.BlockSpec((None, tm, ROW_PARTS, LANES), lambda bi, i: (bi, i, 0, 0)),
                 pl.BlockSpec((None, tm, LANES), lambda bi, i: (bi, i, 0)),
                 pl.BlockSpec((None, 8, LANES), lambda bi, i: (bi * nt + i, 0, 0)),
                 pl.BlockSpec((1, LANES), lambda bi, i: (0, 0))]
    out_shape = [jax.ShapeDtypeStruct((b, s, ROW_PARTS, LANES), U32),
                 jax.ShapeDtypeStruct((b, s, LANES), F32),
                 jax.ShapeDtypeStruct((b * nt, 8, LANES), F32),
                 jax.ShapeDtypeStruct((1, LANES), F32)]
    if pending is not None:
        a, a_spec, bsrc, b_spec, w_out, gate = pending
        in_specs = [a_spec, b_spec, pl.BlockSpec(w_out.shape, lambda bi, i: (0, 0)), x_spec,
                    pl.BlockSpec((None, 1, d), lambda bi, i: (bi, 0, 0))] + in_specs[1:]
        args = [a, bsrc, w_out, x, gate] + args[1:]
        out_specs = [x_spec] + out_specs
        out_shape = [jax.ShapeDtypeStruct((b, s, d), F32)] + out_shape
    return pl.pallas_call(
        functools.partial(_route_kernel, with_outproj=pending is not None),
        grid=(b, nt),
        in_specs=in_specs,
        out_specs=out_specs,
        out_shape=out_shape,
        scratch_shapes=[pltpu.VMEM((1, LANES), F32)],
        compiler_params=_params("arbitrary", "arbitrary"),
        name="route",
    )(*args)


def _rows(ref, row, n):
    return ref.at[pl.ds(row, n)]


def _run_copies(tab_smem, base, stage_ref, far_ref, sem, to_far):
    n_bits = MOE_TM.bit_length()
    common = n_bits - 3

    def per_expert(e, carry):
        n = tab_smem[base + e]
        near = tab_smem[base + N_EXPERTS + e]
        far = tab_smem[base + 2 * N_EXPERTS + e]

        def piece(bit):
            size = 1 << bit

            @pl.when((n & size) != 0)
            def _():
                done = n & (size - 1)
                a, b = _rows(stage_ref, near + done, size), _rows(far_ref, far + done, size)
                (pltpu.make_async_copy(a, b, sem) if to_far else pltpu.make_async_copy(b, a, sem)).start()

        for bit in range(common):
            piece(bit)

        @pl.when(n >= (1 << common))
        def _():
            for bit in range(common, n_bits):
                piece(bit)
        return carry

    lax.fori_loop(0, N_EXPERTS, per_expert, 0)


def _wait_tile(stage_ref, far_ref, sem, to_far):
    n = TOP_K * MOE_TM
    a, b = _rows(stage_ref, 0, n), _rows(far_ref, 0, n)
    (pltpu.make_async_copy(a, b, sem) if to_far else pltpu.make_async_copy(b, a, sem)).wait()


def _dispatch_kernel(ends_ref, nu_ref, h_ref, loc_ref, tab_ref, *rest, zero_fill, tile_base, n_blocks):
    if zero_fill:
        xs_ref, loc_smem, tab_smem, stages_ref, zero_ref, sems, psem = rest
    else:
        _, xs_ref, loc_smem, tab_smem, stages_ref, sems, psem = rest
        zero_ref = None
    tm = MOE_TM
    i = pl.program_id(0)
    n_steps = pl.num_programs(0)
    slot = i % 2
    stage_ref, sem = stages_ref.at[slot], sems.at[slot]
    loc_copy = pltpu.make_async_copy(loc_ref.at[pl.ds((tile_base + i) * tm * 8, tm * 8)], loc_smem, psem)
    tab_copy = pltpu.make_async_copy(tab_ref.at[pl.ds((tile_base + i) * TAB_WORDS, TAB_WORDS)], tab_smem, psem)
    loc_copy.start()
    tab_copy.start()

    if zero_fill:
        @pl.when(i == 0)
        def _():
            zero_ref[...] = jnp.zeros_like(zero_ref)

            def block_copy(blk):
                return pltpu.make_async_copy(zero_ref, _rows(xs_ref, blk * EXPERT_BLOCK, EXPERT_BLOCK), sem)

            def fill(e, carry, *, start):
                end = ends_ref[e]
                prev = jnp.where(e > 0, ends_ref[jnp.maximum(e - 1, 0)], 0)

                @pl.when(end > prev)
                def _():
                    cp = block_copy(end // EXPERT_BLOCK - 1)
                    cp.start() if start else cp.wait()
                return carry

            def tail(j, carry, *, start):
                cp = block_copy(j)
                cp.start() if start else cp.wait()
                return carry

            lax.fori_loop(0, N_EXPERTS, functools.partial(fill, start=True), 0)
            lax.fori_loop(nu_ref[0], n_blocks, functools.partial(tail, start=True), 0)
            lax.fori_loop(0, N_EXPERTS, functools.partial(fill, start=False), 0)
            lax.fori_loop(nu_ref[0], n_blocks, functools.partial(tail, start=False), 0)

    loc_copy.wait()
    tab_copy.wait()

    @pl.when(i >= 2)
    def _():
        _wait_tile(stage_ref, xs_ref, sem, to_far=True)

    flat = stage_ref.reshape(TOP_K * tm * ROW_PARTS, LANES)
    h_flat = h_ref.reshape(tm * ROW_PARTS, LANES)

    def place(t, c):
        row = h_flat[pl.ds(pl.multiple_of(t * ROW_PARTS, ROW_PARTS), ROW_PARTS), :]
        for k in range(TOP_K):
            flat[pl.ds(pl.multiple_of(loc_smem[t * 8 + k], ROW_PARTS), ROW_PARTS), :] = row
        return c

    lax.fori_loop(0, tm, place, 0, unroll=4)
    _run_copies(tab_smem, 0, stage_ref, xs_ref, sem, to_far=True)

    @pl.when(i == n_steps - 1)
    def _():
        _wait_tile(stage_ref, xs_ref, sem, to_far=True)

        @pl.when(i >= 1)
        def _():
            _wait_tile(stages_ref.at[1 - slot], xs_ref, sems.at[1 - slot], to_far=True)


def _dispatch(h2, loc_flat, tab_flat, ends, n_used, xs_prev, n_rows, tile_base):
    n = h2.shape[0]
    tm = MOE_TM
    zero_fill = xs_prev is None
    kern = functools.partial(_dispatch_kernel, zero_fill=zero_fill, tile_base=tile_base,
                             n_blocks=n_rows // EXPERT_BLOCK)
    in_specs = [pl.BlockSpec((tm, ROW_PARTS, LANES), lambda i, e, nu: (i, 0, 0)),
                pl.BlockSpec(memory_space=pl.ANY),
                pl.BlockSpec(memory_space=pl.ANY)]
    args = [ends, n_used, h2, loc_flat, tab_flat]
    scratch = [pltpu.SMEM((tm * 8,), I32), pltpu.SMEM((TAB_WORDS,), I32),
               pltpu.VMEM((2, TOP_K * tm, ROW_PARTS, LANES), U32)]
    aliases = {}
    if zero_fill:
        scratch.append(pltpu.VMEM((EXPERT_BLOCK, ROW_PARTS, LANES), U32))
    else:
        in_specs.append(pl.BlockSpec(memory_space=pl.ANY))
        args.append(xs_prev)
        aliases = {5: 0}
    scratch += [pltpu.SemaphoreType.DMA((2,)), pltpu.SemaphoreType.DMA]
    return pl.pallas_call(
        kern,
        grid_spec=pltpu.PrefetchScalarGridSpec(
            num_scalar_prefetch=2,
            grid=(n // tm,),
            in_specs=in_specs,
            out_specs=pl.BlockSpec(memory_space=pl.ANY),
            scratch_shapes=scratch),
        out_shape=jax.ShapeDtypeStruct((n_rows, ROW_PARTS, LANES), U32),
        input_output_aliases=aliases,
        compiler_params=_params("arbitrary"),
        name="dispatch",
    )(*args)


def _expert_kernel(be_ref, nu_ref, x_ref, *refs):
    o_ref = refs[-1]
    blk = EXPERT_BLOCK
    used = pl.program_id(0) * EXPERT_GROUP < nu_ref[0]

    @pl.when(used)
    def _():
        for j in range(EXPERT_GROUP):
            w1_ref, w3_ref, w2_ref = refs[3 * j:3 * j + 3]
            x = _unpack_rows(x_ref, j * blk, blk).astype(BF16)
            a = _silu(_dot(x, w1_ref[...])) * _dot(x, w3_ref[...])
            _pack_rows(o_ref, _dot(a.astype(BF16), w2_ref[...]), j * blk)

    @pl.when(jnp.logical_not(used))
    def _():
        o_ref[...] = jnp.zeros_like(o_ref)


def _experts(xs, block_e, n_used, w1, w3, w2, layer):
    d, de = w1.shape[2:]
    rows = EXPERT_BLOCK * EXPERT_GROUP
    n_steps = xs.shape[0] // rows

    def row_map(i, be, nu):
        return (jnp.minimum(i, (nu[0] + EXPERT_GROUP - 1) // EXPERT_GROUP - 1), 0, 0)

    w_specs = []
    for j in range(EXPERT_GROUP):
        w_map = lambda i, be, nu, j=j: (layer, be[i * EXPERT_GROUP + j], 0, 0)
        w_specs += [pl.BlockSpec((None, None, d, de), w_map),
                    pl.BlockSpec((None, None, d, de), w_map),
                    pl.BlockSpec((None, None, de, d), w_map)]

    return pl.pallas_call(
        _expert_kernel,
        grid_spec=pltpu.PrefetchScalarGridSpec(
            num_scalar_prefetch=2,
            grid=(n_steps,),
            in_specs=[pl.BlockSpec((rows, ROW_PARTS, LANES), row_map)] + w_specs,
            out_specs=pl.BlockSpec((rows, ROW_PARTS, LANES), lambda i, be, nu: (i, 0, 0))),
        out_shape=jax.ShapeDtypeStruct(xs.shape, U32),
        compiler_params=_params("arbitrary"),
        name="experts",
    )(block_e, n_used, xs, *([w1, w3, w2] * EXPERT_GROUP))


def _combine_kernel(loc_ref, tab_ref, ys_ref, slab_ref, h_ref, x_ref, gate_ref, s1_ref, s3_ref, s2_ref,
                    o_ref, loc_smem, tab_smem, gk_ref, stage_ref, lo_ref, hi_ref, sem, psem, *, tile_base):
    tm = MOE_TM
    words = tm * 8
    tile = tile_base + pl.program_id(0) * pl.num_programs(1) + pl.program_id(1)
    small = [pltpu.make_async_copy(loc_ref.at[pl.ds(tile * words, words)], loc_smem, psem),
             pltpu.make_async_copy(tab_ref.at[pl.ds(tile * TAB_WORDS, TAB_WORDS)], tab_smem, psem)]
    for cp in small:
        cp.start()
    for cp in small:
        cp.wait()
    _run_copies(tab_smem, 0, stage_ref, ys_ref, sem, to_far=False)
    hb = _unpack_rows(h_ref).astype(BF16)
    shared = _dot((_silu(_dot(hb, s1_ref[...])) * _dot(hb, s3_ref[...])).astype(BF16), s2_ref[...])
    slab = slab_ref[...]
    for k in range(TOP_K):
        gk_ref[k] = jnp.broadcast_to(slab[:, SLAB_GATE + k:SLAB_GATE + k + 1], (tm, LANES))
    _wait_tile(stage_ref, ys_ref, sem, to_far=False)

    flat = stage_ref.reshape(TOP_K * tm * ROW_PARTS, LANES)

    def mix(t, c):
        lo = hi = None
        for k in range(TOP_K):
            g = gk_ref[k, pl.ds(t, ROW_PARTS, stride=0), :]
            at = pl.multiple_of(loc_smem[t * 8 + k], ROW_PARTS)
            wl, wh = _unpack_words(flat[pl.ds(at, ROW_PARTS), :])
            lo = g * wl if lo is None else lo + g * wl
            hi = g * wh if hi is None else hi + g * wh
        lo_ref[t] = lo
        hi_ref[t] = hi
        return c

    lax.fori_loop(0, tm, mix, 0, unroll=4)
    routed = jnp.concatenate(_row_chunks(lo_ref) + _row_chunks(hi_ref), axis=-1)
    o_ref[...] = x_ref[...] + gate_ref[...] * (routed + shared)


def _combine(ys, loc_flat, tab_flat, slab, h2, x, gate, s1, s3, s2, tile_base):
    b, s, d = x.shape
    tm = MOE_TM
    kern = functools.partial(_combine_kernel, tile_base=tile_base)
    return pl.pallas_call(
        kern,
        grid=(b, s // tm),
        in_specs=[pl.BlockSpec(memory_space=pl.ANY),
                  pl.BlockSpec(memory_space=pl.ANY),
                  pl.BlockSpec(memory_space=pl.ANY),
                  pl.BlockSpec((None, tm, LANES), lambda bi, i: (bi, i, 0)),
                  pl.BlockSpec((None, tm, ROW_PARTS, LANES), lambda bi, i: (bi, i, 0, 0)),
                  pl.BlockSpec((None, tm, d), lambda bi, i: (bi, i, 0)),
                  pl.BlockSpec((None, 1, d), lambda bi, i: (bi, 0, 0)),
                  pl.BlockSpec(s1.shape, lambda bi, i: (0, 0)),
                  pl.BlockSpec(s3.shape, lambda bi, i: (0, 0)),
                  pl.BlockSpec(s2.shape, lambda bi, i: (0, 0))],
        out_specs=pl.BlockSpec((None, tm, d), lambda bi, i: (bi, i, 0)),
        out_shape=jax.ShapeDtypeStruct((b, s, d), F32),
        scratch_shapes=[pltpu.SMEM((tm * 8,), I32),
                        pltpu.SMEM((TAB_WORDS,), I32),
                        pltpu.VMEM((TOP_K, tm, LANES), F32),
                        pltpu.VMEM((TOP_K * tm, ROW_PARTS, LANES), U32),
                        pltpu.VMEM((tm, ROW_PARTS, LANES), F32),
                        pltpu.VMEM((tm, ROW_PARTS, LANES), F32),
                        pltpu.SemaphoreType.DMA,
                        pltpu.SemaphoreType.DMA],
        compiler_params=_params("arbitrary", "arbitrary"),
        name="combine",
    )(loc_flat, tab_flat, ys, slab, h2, x, gate, s1, s3, s2)


def _dft_tables(length, n_chan):
    scale = 1.0 / math.sqrt(length * n_chan)
    side = 1
    while side * side < length:
        side *= 2
    outer = length // side
    k = jnp.arange(length, dtype=I32)[:, None]
    a_idx = (k * jnp.arange(outer, dtype=I32)[None, :]) % outer
    b_idx = (k * jnp.arange(side, dtype=I32)[None, :]) % length
    ang_a = a_idx.astype(F32) * (2.0 * math.pi / outer)
    ang_b = b_idx.astype(F32) * (2.0 * math.pi / length)
    ca, sa = jnp.cos(ang_a)[:, :, None], jnp.sin(ang_a)[:, :, None]
    cb, sb = jnp.cos(ang_b)[:, None, :], jnp.sin(ang_b)[:, None, :]
    cos_t = (ca * cb - sa * sb).reshape(length, length)
    sin_t = (sa * cb + ca * sb).reshape(length, length)
    table = (jnp.concatenate([cos_t, -sin_t], axis=1) * scale).astype(BF16)
    return table


def _channel_table(n_chan, n_groups):
    m = jnp.arange(n_chan, dtype=I32)
    ang = ((m[:, None] * m[None, :]) % n_chan).astype(F32) * (2.0 * math.pi / n_chan)
    eye = jnp.eye(n_groups, dtype=F32)
    return jnp.concatenate([jnp.kron(eye, jnp.cos(ang)), jnp.kron(eye, jnp.sin(ang))], axis=1).astype(BF16)


def _rope_tables(n_tok):
    rows = n_tok // GRID_W
    axis_dim = HEAD_DIM // 2
    r = jnp.repeat(jnp.arange(rows, dtype=F32), GRID_W)
    col = jnp.tile(jnp.arange(GRID_W, dtype=F32), rows)
    inv = ROPE_BASE ** (-jnp.arange(0, axis_dim, 2, dtype=F32) / axis_dim)
    ar, ac = r[:, None] * inv, col[:, None] * inv
    cos = jnp.concatenate([jnp.cos(ar), jnp.cos(ar), jnp.cos(ac), jnp.cos(ac)], axis=1)
    sin = jnp.concatenate([-jnp.sin(ar), jnp.sin(ar), -jnp.sin(ac), jnp.sin(ac)], axis=1)
    return jnp.tile(cos, (1, N_HEADS)), jnp.tile(sin, (1, N_HEADS))


def _head_mean_matrix(width):
    h = jnp.arange(width) // HEAD_DIM
    return ((h[:, None] == h[None, :]).astype(F32) / HEAD_DIM).astype(BF16)


def _tile_matrices():
    src = jnp.arange(N_KV * HEAD_DIM)
    dst = jnp.arange(Q_PER_KV * HEAD_DIM)
    mats = [((src[:, None] // HEAD_DIM == h) & (src[:, None] % HEAD_DIM == dst[None, :] % HEAD_DIM))
            for h in range(N_KV)]
    return jnp.stack(mats).astype(BF16)


def _even_layer(x, xc, mod, modc, norm_g, w_in, ws, bs, w_out, tables):
    b, s, d = x.shape
    w = w_in.shape[1] // 3
    outs = []
    for stream, m, tm in ((x, mod, MIX_TM), (xc, modc, 256)):
        if stream is None:
            outs.append(None)
            continue
        length = stream.shape[1]
        tm = min(tm, length)
        gm, z = _even_mix(stream, norm_g, m[0], m[1], w_in, ws, bs, tables["chan"], tm)
        table = tables["pos"][length]
        y = _matmul(table, z.reshape(2 * length, b * w),
                    min(1024, length), min(1024, b * w), min(2048, 2 * length), BF16)
        if stream is x:
            t = MOE_TM
            outs.append((gm, pl.BlockSpec((None, t, w), lambda bi, i: (bi, i, 0)),
                         y, pl.BlockSpec((t, w), lambda bi, i: (i, bi)), w_out, m[2]))
        else:
            outs.append(_outproj(
                gm, pl.BlockSpec((None, tm, w), lambda bi, i: (bi, i, 0)),
                y, pl.BlockSpec((tm, w), lambda bi, i: (i, bi)),
                w_out, stream, m[2], tm))
    return outs


def _odd_layer(x, xc, mod, modc, norm_g, w_in, qg, kg, sink, conv_w, w_out, tables, ctx_out):
    b, s, d = x.shape
    lc = xc.shape[1]
    half = N_HEADS * HEAD_DIM
    tm, tmc = MIX_TM, min(256, lc)
    proj = _norm_mod_matmul(x, norm_g, mod[0], mod[1], w_in, tm)
    projc = _norm_mod_matmul(xc, norm_g, modc[0], modc[1], w_in, tmc)
    prep = functools.partial(_qkv_prep, qg=qg, kg=kg, bdq=tables["bdq"], bdk=tables["bdk"], tile=tables["tile"])
    q, k4, v4 = prep(proj, tables["cos"], tables["sin"], tm=tm, with_q=True)
    qc, kc4, vc4 = prep(projc, tables["cos_c"], tables["sin_c"], tm=tmc, with_q=ctx_out)
    att = _attention(q, k4, v4, kc4, vc4, sink, band=True)
    conv = _short_conv(proj, conv_w)
    spec = lambda t: pl.BlockSpec((None, t, half), lambda bi, i: (bi, i, 0))
    y = (att, spec(MOE_TM), conv, spec(MOE_TM), w_out, mod[2])
    yc = None
    if ctx_out:
        attc = _attention(qc, None, None, kc4, vc4, sink, band=False)
        convc = _short_conv(projc, conv_w)
        yc = _outproj(attc, spec(tmc), convc, spec(tmc), w_out, xc, modc[2], tmc)
    return y, yc


def _moe(x, pending, xc, mod, modc, norm_g, rw_hi, rw_lo, rb, w1, w3, w2, layer, s1, s3, s2, tri, upper):
    b, s, d = x.shape
    n_lat = b * s
    xc_shape = None
    if xc is not None and xc.shape[1] % MOE_TM:
        xc_shape = xc.shape
        xc = xc.reshape(-1, MOE_TM, d)
        modc = [m[:xc.shape[0]] for m in modc]
    counts0 = jnp.zeros((1, LANES), F32)
    x, h2, slab, tab, counts = _route(x, norm_g, mod[3], mod[4], rw_hi, rw_lo, rb, counts0, tri, upper, pending)
    slabs, tabs = [slab.reshape(n_lat, LANES)], [tab]
    n_tok = n_lat
    if xc is not None:
        h2c, slabc, tabc, counts = _route(xc, norm_g, modc[3], modc[4], rw_hi, rw_lo, rb, counts, tri, upper)
        slabs.append(slabc.reshape(-1, LANES))
        tabs.append(tabc)
        n_tok += slabs[1].shape[0]
    cnt = counts[0, :N_EXPERTS].astype(I32)
    blk = EXPERT_BLOCK
    padded = (cnt + blk - 1) // blk * blk
    ends = jnp.cumsum(padded).astype(I32)
    starts = ends - padded
    step_rows = blk * EXPERT_GROUP
    n_rows = (n_tok * TOP_K + N_EXPERTS * (blk - 1) + step_rows - 1) // step_rows * step_rows
    n_blocks = n_rows // blk
    n_used = (ends[-1] // blk).reshape(1).astype(I32)
    blk_start = jnp.minimum(jnp.arange(n_blocks, dtype=I32), n_used[0] - 1) * blk
    block_e = jnp.minimum(jnp.sum(blk_start[:, None] >= ends[None, :], axis=1), N_EXPERTS - 1).astype(I32)
    slab_all = jnp.concatenate(slabs, axis=0)
    tab_all = jnp.concatenate(tabs, axis=0)[:, :, :N_EXPERTS].astype(I32)
    runs = jnp.concatenate([tab_all[:, 0], tab_all[:, 1], tab_all[:, 2] + starts[None, :]], axis=1)
    tab_flat = jnp.pad(runs, ((0, 0), (0, TAB_WORDS - runs.shape[1]))).reshape(-1)
    loc_flat = (slab_all[:, SLAB_LOC:SLAB_LOC + 8].astype(I32) * ROW_PARTS).reshape(-1)
    packed = lambda a: a.reshape(-1, ROW_PARTS, LANES)
    xs = _dispatch(packed(h2), loc_flat, tab_flat, ends, n_used, None, n_rows, 0)
    if xc is not None:
        xs = _dispatch(packed(h2c), loc_flat, tab_flat, ends, n_used, xs, n_rows, n_lat // MOE_TM)
    ys = _experts(xs, block_e, n_used, w1, w3, w2, layer)
    x_new = _combine(ys, loc_flat, tab_flat, slab, h2, x, mod[5], s1, s3, s2, 0)
    xc_new = None
    if xc is not None:
        xc_new = _combine(ys, loc_flat, tab_flat, slabc, h2c, xc, modc[5], s1, s3, s2, n_lat // MOE_TM)
        if xc_shape is not None:
            xc_new = xc_new.reshape(xc_shape)
    return x_new, xc_new


def kernel(x, c, ctx, c_ctx, ada_w, ada_b, norm1_g, norm2_g, ev_w_in, ev_w_s, ev_b_s, ev_w_out, od_w_in, od_q_norm_g, od_k_norm_g, od_sink, od_conv_w, od_w_out, router_w, router_b, exp_w_gate, exp_w_up, exp_w_down, sh_w_gate, sh_w_up, sh_w_down):
    b, s, d = x.shape
    lc = ctx.shape[1]
    depth = ada_w.shape[0]
    n_groups = ev_w_s.shape[1]
    half = d // 2

    rows = -(-(b + 1) // 8) * 8
    cond = jnp.zeros((rows, d), F32).at[:b].set(c).at[b].set(c_ctx)
    mod_all = _adaln(cond, ada_w, ada_b)

    tables = {
        "chan": _channel_table(LANES, n_groups),
        "pos": {s: _dft_tables(s, LANES), lc: _dft_tables(lc, LANES)},
        "bdq": _head_mean_matrix(N_HEADS * HEAD_DIM),
        "bdk": _head_mean_matrix(N_KV * HEAD_DIM),
        "tile": _tile_matrices(),
    }
    tables["cos"], tables["sin"] = _rope_tables(s)
    tables["cos_c"] = jnp.ones((lc, N_HEADS * HEAD_DIM), F32)
    tables["sin_c"] = jnp.zeros((lc, N_HEADS * HEAD_DIM), F32)
    tri = (jnp.arange(MOE_TM)[:, None] > jnp.arange(MOE_TM)[None, :]).astype(BF16)
    upper = (jnp.arange(LANES)[:, None] < jnp.arange(LANES)[None, :]).astype(BF16)

    qw, kw = N_HEADS * HEAD_DIM, N_KV * HEAD_DIM
    perm = jnp.concatenate([jnp.arange(0, qw), jnp.arange(qw + 2 * kw, qw + 2 * kw + 3 * half),
                            jnp.arange(qw, qw + 2 * kw)])

    w1_all, w3_all, w2_all = exp_w_gate.astype(BF16), exp_w_up.astype(BF16), exp_w_down.astype(BF16)
    xc = ctx
    for l in range(depth):
        last = l == depth - 1
        even = l % 2 == 0
        need_ctx = not (last and even)
        pieces = [mod_all[l, :, j * d:(j + 1) * d] for j in range(6)]
        mod = [p[:b].reshape(b, 1, d) for p in pieces]
        modc = [jnp.broadcast_to(p[b].reshape(1, 1, d), (b, 1, d)) for p in pieces]
        g1 = norm1_g[l].reshape(1, d)
        g2 = norm2_g[l].reshape(1, d)
        if even:
            e = l // 2
            bs = jnp.broadcast_to(ev_b_s[e][:, :, None], (n_groups, CHUNK, LANES))
            y, yc = _even_layer(x, xc if (need_ctx and not last) else None, mod, modc, g1,
                                ev_w_in[e].astype(BF16), ev_w_s[e].astype(BF16), bs,
                                ev_w_out[e].astype(BF16), tables)
        else:
            o = l // 2
            qg = (jnp.tile(od_q_norm_g[o], N_HEADS) * (HEAD_DIM ** -0.5)).reshape(1, qw)
            kg = jnp.tile(od_k_norm_g[o], N_KV).reshape(1, kw)
            y, yc = _odd_layer(x, xc, mod, modc, g1, od_w_in[o][:, perm].astype(BF16), qg, kg,
                               od_sink[o], od_conv_w[o], od_w_out[o].astype(BF16), tables, not last)
        pending = y
        if not last:
            xc = yc
        rw = jnp.zeros((d, LANES), F32).at[:, :N_EXPERTS].set(router_w[l])
        rw_hi = rw.astype(BF16)
        rw_lo = (rw - rw_hi.astype(F32)).astype(BF16)
        rb = jnp.zeros((1, LANES), F32).at[0, :N_EXPERTS].set(router_b[l])
        x, xc_new = _moe(x, pending, None if last else xc, mod, modc, g2, rw_hi, rw_lo, rb, w1_all, w3_all, w2_all, l,
                         sh_w_gate[l].astype(BF16), sh_w_up[l].astype(BF16), sh_w_down[l].astype(BF16),
                         tri, upper)
        if not last:
            xc = xc_new
    return x
```

```python
import functools
import math

import jax
import jax.numpy as jnp
from jax import lax
from jax.experimental import pallas as pl
from jax.experimental.pallas import tpu as pltpu

F32 = jnp.float32
BF16 = jnp.bfloat16
I32 = jnp.int32
U32 = jnp.uint32

LANES = 128
VMEM_LIMIT = 48 * 2**20

EPS = 1e-6
GRID_W = 64
CHUNK = 128
HEAD_DIM = 64
N_HEADS = 8
N_KV = 2
Q_PER_KV = N_HEADS // N_KV
MIX_TM = 1024
ATT_BLOCK = 128
ATT_ROWS = 64
ROPE_BASE = 10000.0
N_EXPERTS = 64
TOP_K = 6
ROUTED_SCALE = 2.5
EXPERT_BLOCK = 512
EXPERT_GROUP = 2
MOE_TM = 1024
ROW_PARTS = 4
SLAB_IDX, SLAB_LOC, SLAB_GATE = 0, 8, 16
TAB_WORDS = 1024


def _params(*sem):
    return pltpu.CompilerParams(dimension_semantics=sem, vmem_limit_bytes=VMEM_LIMIT)


def _sigmoid(x):
    return 1.0 / (1.0 + jnp.exp(-x))


def _silu(x):
    return x * _sigmoid(x)


def _gelu_tanh(x):
    c = math.sqrt(2.0 / math.pi)
    return x * (0.5 * (1.0 + jnp.tanh(c * (x + 0.044715 * (x * x * x)))))


def _dot(a, b):
    return jnp.dot(a, b, preferred_element_type=F32)


def _unpack_words(w):
    return pltpu.bitcast(w << 16, F32), pltpu.bitcast(w & jnp.uint32(0xFFFF0000), F32)


def _row_chunks(ref, row0=0, n=None):
    total, parts, lanes = ref.shape
    n = total if n is None else n
    flat = ref.reshape(total * parts, lanes)
    return [flat[pl.ds(row0 * parts + c, n, stride=parts), :] for c in range(parts)]


def _unpack_rows(ref, row0=0, n=None):
    halves = [_unpack_words(w) for w in _row_chunks(ref, row0, n)]
    return jnp.concatenate([h[0] for h in halves] + [h[1] for h in halves], axis=-1)


def _pack_rows(ref, val, row0=0):
    total, parts, lanes = ref.shape
    n, half = val.shape[0], val.shape[1] // 2
    bits = pltpu.bitcast(val.astype(BF16).astype(F32), U32)
    words = (bits[:, :half] >> 16) | (bits[:, half:] & jnp.uint32(0xFFFF0000))
    flat = ref.reshape(total * parts, lanes)
    for c in range(parts):
        flat[pl.ds(row0 * parts + c, n, stride=parts), :] = words[:, c * lanes:(c + 1) * lanes]


def _adaln_kernel(c_ref, w_ref, b_ref, o_ref):
    o_ref[...] = _dot(_silu(c_ref[...]), w_ref[...]) + b_ref[...]


def _adaln(cond, ada_w, ada_b):
    n_layers, d, n6 = ada_w.shape
    rows = cond.shape[0]
    tn = 1536
    return pl.pallas_call(
        _adaln_kernel,
        grid=(n_layers, n6 // tn),
        in_specs=[pl.BlockSpec((rows, d), lambda l, j: (0, 0)),
                  pl.BlockSpec((None, d, tn), lambda l, j: (l, 0, j)),
                  pl.BlockSpec((None, 1, tn), lambda l, j: (l, 0, j))],
        out_specs=pl.BlockSpec((None, rows, tn), lambda l, j: (l, 0, j)),
        out_shape=jax.ShapeDtypeStruct((n_layers, rows, n6), F32),
        compiler_params=_params("parallel", "parallel"),
        name="adaln",
    )(cond, ada_w, ada_b.reshape(n_layers, 1, n6))


def _norm_mod(x, g, shift, scale):
    ms = jnp.mean(x * x, axis=-1, keepdims=True)
    h = (x * lax.rsqrt(ms + EPS)) * g
    return h * (1.0 + scale) + shift


def _even_mix_kernel(x_ref, g_ref, sh_ref, sc_ref, w_ref, ws_ref, bs_ref, cs_ref, gm_ref, z_ref, p_ref,
                     *, n_chunks, n_groups):
    h = _norm_mod(x_ref[...], g_ref[...], sh_ref[...], sc_ref[...])
    p_ref[...] = _dot(h.astype(BF16), w_ref[...])
    half = gm_ref.shape[1]
    for c in range(n_chunks):
        rows = slice(c * CHUNK, (c + 1) * CHUNK)
        for g in range(n_groups):
            cols = slice(g * LANES, (g + 1) * LANES)
            ug = _gelu_tanh(p_ref[rows, cols])
            vg = _gelu_tanh(p_ref[rows, half + g * LANES:half + (g + 1) * LANES])
            mu = jnp.mean(vg, axis=-1, keepdims=True)
            dv = vg - mu
            var = jnp.mean(dv * dv, axis=-1, keepdims=True)
            vn = dv * lax.rsqrt(var + 1e-5)
            fg = _dot(ws_ref[g], vn.astype(BF16)) + bs_ref[g]
            gm_ref[rows, cols] = (ug * fg).astype(gm_ref.dtype)
    fz = _dot(p_ref[:, 2 * half:].astype(BF16), cs_ref[...])
    z_ref[0] = fz[:, :half].astype(z_ref.dtype)
    z_ref[1] = fz[:, half:].astype(z_ref.dtype)


def _even_mix(x, g, shift, scale, w_in, ws, bs, cs, tm):
    b, s, d = x.shape
    n3 = w_in.shape[1]
    w = n3 // 3
    n_groups = w // LANES
    kern = functools.partial(_even_mix_kernel, n_chunks=tm // CHUNK, n_groups=n_groups)
    return pl.pallas_call(
        kern,
        grid=(b, s // tm),
        in_specs=[pl.BlockSpec((None, tm, d), lambda bi, i: (bi, i, 0)),
                  pl.BlockSpec((1, d), lambda bi, i: (0, 0)),
                  pl.BlockSpec((None, 1, d), lambda bi, i: (bi, 0, 0)),
                  pl.BlockSpec((None, 1, d), lambda bi, i: (bi, 0, 0)),
                  pl.BlockSpec((d, n3), lambda bi, i: (0, 0)),
                  pl.BlockSpec(ws.shape, lambda bi, i: (0, 0, 0)),
                  pl.BlockSpec(bs.shape, lambda bi, i: (0, 0, 0)),
                  pl.BlockSpec(cs.shape, lambda bi, i: (0, 0))],
        out_specs=[pl.BlockSpec((None, tm, w), lambda bi, i: (bi, i, 0)),
                   pl.BlockSpec((2, tm, w), lambda bi, i: (0, i, bi))],
        out_shape=[jax.ShapeDtypeStruct((b, s, w), BF16),
                   jax.ShapeDtypeStruct((2, s, b * w), BF16)],
        scratch_shapes=[pltpu.VMEM((tm, n3), F32)],
        compiler_params=_params("parallel", "parallel"),
        name="even_mix",
    )(x, g, shift, scale, w_in, ws, bs, cs)


def _mm_kernel(a_ref, b_ref, o_ref, acc_ref):
    k = pl.program_id(2)

    @pl.when(k == 0)
    def _():
        acc_ref[...] = jnp.zeros_like(acc_ref)

    acc_ref[...] += _dot(a_ref[...], b_ref[...])

    @pl.when(k == pl.num_programs(2) - 1)
    def _():
        o_ref[...] = acc_ref[...].astype(o_ref.dtype)


def _matmul(a, b, tm, tn, tk, out_dtype):
    m, kd = a.shape
    n = b.shape[1]
    return pl.pallas_call(
        _mm_kernel,
        grid=(m // tm, n // tn, kd // tk),
        in_specs=[pl.BlockSpec((tm, tk), lambda i, j, k: (i, k)),
                  pl.BlockSpec((tk, tn), lambda i, j, k: (k, j))],
        out_specs=pl.BlockSpec((tm, tn), lambda i, j, k: (i, j)),
        out_shape=jax.ShapeDtypeStruct((m, n), out_dtype),
        scratch_shapes=[pltpu.VMEM((tm, tn), F32)],
        compiler_params=_params("parallel", "parallel", "arbitrary"),
        name="dft_matmul",
    )(a, b)


def _outproj_kernel(a_ref, b_ref, w_ref, x_ref, gate_ref, o_ref):
    ab = jnp.concatenate([a_ref[...], b_ref[...]], axis=-1)
    o_ref[...] = x_ref[...] + gate_ref[...] * _dot(ab, w_ref[...])


def _outproj(a, a_spec, bsrc, b_spec, w, x, gate, tm):
    b, s, d = x.shape
    return pl.pallas_call(
        _outproj_kernel,
        grid=(b, s // tm),
        in_specs=[a_spec, b_spec,
                  pl.BlockSpec(w.shape, lambda bi, i: (0, 0)),
                  pl.BlockSpec((None, tm, d), lambda bi, i: (bi, i, 0)),
                  pl.BlockSpec((None, 1, d), lambda bi, i: (bi, 0, 0))],
        out_specs=pl.BlockSpec((None, tm, d), lambda bi, i: (bi, i, 0)),
        out_shape=jax.ShapeDtypeStruct((b, s, d), F32),
        compiler_params=_params("parallel", "parallel"),
        name="outproj",
    )(a, bsrc, w, x, gate)


def _head_rms(x, bd_ref):
    xx = x * x
    hi = xx.astype(BF16)
    lo = (xx - hi.astype(F32)).astype(BF16)
    ms = _dot(hi, bd_ref[...]) + _dot(lo, bd_ref[...])
    return x * lax.rsqrt(ms + EPS)


def _rope(x, cos, sins):
    width = x.shape[1]
    lane = lax.broadcasted_iota(I32, x.shape, 1)
    first = (lane & 31) < 16
    swapped = jnp.where(first, pltpu.roll(x, width - 16, 1), pltpu.roll(x, 16, 1))
    return x * cos + swapped * sins


HALO = 8


def _odd_in_kernel(x_ref, xp_ref, xn_ref, g_ref, sh_ref, sc_ref, w_ref, cos_ref, sin_ref, qg_ref, kg_ref,
                   bdq_ref, bdk_ref, tile_ref, tile_t_ref, cw_ref, qo_ref, k4_ref, v4_ref, co_ref, p_ref, *, with_q):
    i = pl.program_id(1)
    n_tiles = pl.num_programs(1)
    tm = x_ref.shape[0]
    qw, dc, kw = qo_ref.shape[1], co_ref.shape[1], kg_ref.shape[1]
    g, sh, sc = g_ref[...], sh_ref[...], sc_ref[...]
    p_ref[...] = _dot(_norm_mod(x_ref[...], g, sh, sc).astype(BF16), w_ref[...])
    cos, sin = cos_ref[...], sin_ref[...]
    if with_q:
        reps = qw // kw
        qn = _head_rms(p_ref[:, :qw], bdq_ref) * qg_ref[...]
        qo_ref[...] = _rope(qn, jnp.concatenate([cos] * reps, axis=1),
                            jnp.concatenate([sin] * reps, axis=1)).astype(qo_ref.dtype)
    else:
        qo_ref[...] = jnp.zeros_like(qo_ref)
    k0 = qw + 3 * dc
    kn = _head_rms(p_ref[:, k0:k0 + kw], bdk_ref) * kg_ref[...]
    kr = _rope(kn, cos, sin).astype(BF16)
    vb = p_ref[:, k0 + kw:k0 + 2 * kw].astype(BF16)
    for h in range(N_KV):
        k4_ref[h] = lax.dot_general(tile_t_ref[h], kr, (((1,), (1,)), ((), ())),
                                    preferred_element_type=F32).astype(k4_ref.dtype)
        v4_ref[h] = _dot(vb, tile_ref[h]).astype(v4_ref.dtype)
    halo = _norm_mod(jnp.concatenate([xp_ref[...], xn_ref[...]], axis=0), g, sh, sc).astype(BF16)
    gz = _dot(halo, w_ref[:, qw:qw + dc]) * _dot(halo, w_ref[:, qw + 2 * dc:qw + 3 * dc])
    before = jnp.where(i > 0, gz[HALO - 1:HALO, :], 0.0)
    after = jnp.where(i < n_tiles - 1, gz[HALO:HALO + 1, :], 0.0)
    ridx = lax.broadcasted_iota(I32, (tm, LANES), 0)
    for c in range(dc // LANES):
        cols = slice(c * LANES, (c + 1) * LANES)
        zc = p_ref[:, qw + c * LANES:qw + (c + 1) * LANES] * p_ref[:, qw + 2 * dc + c * LANES:qw + 2 * dc + (c + 1) * LANES]
        zp = jnp.where(ridx == 0, before[:, cols], pltpu.roll(zc, 1, 0))
        zn = jnp.where(ridx == tm - 1, after[:, cols], pltpu.roll(zc, tm - 1, 0))
        y = zp * cw_ref[0:1, cols] + zc * cw_ref[1:2, cols] + zn * cw_ref[2:3, cols]
        co_ref[:, cols] = (p_ref[:, qw + dc + c * LANES:qw + dc + (c + 1) * LANES] * y).astype(co_ref.dtype)


def _odd_in(x, g, shift, scale, w, cos, sin, qg, kg, bdq, bdk, tile, conv_w, tm, with_q):
    b, s, d = x.shape
    n = w.shape[1]
    tile_t = jnp.swapaxes(tile, 1, 2)
    qw = N_HEADS * HEAD_DIM
    kw = N_KV * HEAD_DIM
    rep = Q_PER_KV * HEAD_DIM
    dc = conv_w.shape[1]
    per_tile = tm // HALO
    kern = functools.partial(_odd_in_kernel, with_q=with_q)
    return pl.pallas_call(
        kern,
        grid=(b, s // tm),
        in_specs=[pl.BlockSpec((None, tm, d), lambda bi, i: (bi, i, 0)),
                  pl.BlockSpec((None, HALO, d), lambda bi, i: (bi, jnp.maximum(i * per_tile - 1, 0), 0)),
                  pl.BlockSpec((None, HALO, d), lambda bi, i: (bi, jnp.minimum((i + 1) * per_tile, s // HALO - 1), 0)),
                  pl.BlockSpec((1, d), lambda bi, i: (0, 0)),
                  pl.BlockSpec((None, 1, d), lambda bi, i: (bi, 0, 0)),
                  pl.BlockSpec((None, 1, d), lambda bi, i: (bi, 0, 0)),
                  pl.BlockSpec((d, n), lambda bi, i: (0, 0)),
                  pl.BlockSpec((tm, kw), lambda bi, i: (i, 0)),
                  pl.BlockSpec((tm, kw), lambda bi, i: (i, 0)),
                  pl.BlockSpec((1, qw), lambda bi, i: (0, 0)),
                  pl.BlockSpec((1, kw), lambda bi, i: (0, 0)),
                  pl.BlockSpec(bdq.shape, lambda bi, i: (0, 0)),
                  pl.BlockSpec(bdk.shape, lambda bi, i: (0, 0)),
                  pl.BlockSpec(tile.shape, lambda bi, i: (0, 0, 0)),
                  pl.BlockSpec(tile_t.shape, lambda bi, i: (0, 0, 0)),
                  pl.BlockSpec(conv_w.shape, lambda bi, i: (0, 0))],
        out_specs=[pl.BlockSpec((None, tm, qw), lambda bi, i: (bi, i, 0)),
                   pl.BlockSpec((None, N_KV, rep, tm), lambda bi, i: (bi, 0, 0, i)),
                   pl.BlockSpec((None, N_KV, tm, rep), lambda bi, i: (bi, 0, i, 0)),
                   pl.BlockSpec((None, tm, dc), lambda bi, i: (bi, i, 0))],
        out_shape=[jax.ShapeDtypeStruct((b, s, qw), BF16),
                   jax.ShapeDtypeStruct((b, N_KV, rep, s), BF16),
                   jax.ShapeDtypeStruct((b, N_KV, s, rep), BF16),
                   jax.ShapeDtypeStruct((b, s, dc), BF16)],
        scratch_shapes=[pltpu.VMEM((tm, n), F32)],
        compiler_params=_params("parallel", "parallel"),
        name="odd_in",
    )(x, x, x, g, shift, scale, w, cos, sin, qg, kg, bdq, bdk, tile, tile_t, conv_w)


def _attn_kernel(sink_ref, q_ref, *refs, band, n_blocks):
    o_ref = refs[-1]
    i = pl.program_id(1)
    t = q_ref.shape[0]
    w = Q_PER_KV * HEAD_DIM
    lane = lax.broadcasted_iota(I32, (t, w), 1)
    masks = [(lane >= g * HEAD_DIM) & (lane < (g + 1) * HEAD_DIM) for g in range(Q_PER_KV)]
    bad = None
    if band:
        n_keys = 3 * t + refs[7].shape[1]
        row = lax.broadcasted_iota(I32, (Q_PER_KV * t, n_keys), 0) & (t - 1)
        col = lax.broadcasted_iota(I32, (Q_PER_KV * t, n_keys), 1)
        off_prev = jnp.where(i > 0, 0, 4 * t)
        off_next = jnp.where(i < n_blocks - 1, 0, 4 * t)
        bad_prev = (col < t) & (col < row + off_prev)
        bad_next = (col >= 2 * t) & (col < 3 * t) & (col - 2 * t > row - off_next)
        bad = bad_prev | bad_next
    for h in range(N_KV):
        if band:
            kp, kc_, kn, vp, vc_, vn, kx, vx = refs[:-1]
            kcat = jnp.concatenate([kp[h], kc_[h], kn[h], kx[h]], axis=1)
            vcat = jnp.concatenate([vp[h], vc_[h], vn[h], vx[h]], axis=0)
        else:
            kx, vx = refs[:-1]
            kcat, vcat = kx[h], vx[h]
        q = q_ref[:, h * w:(h + 1) * w]
        q4 = jnp.concatenate([jnp.where(m, q, jnp.zeros_like(q)) for m in masks], axis=0)
        s = _dot(q4, kcat)
        es, invs = [], []
        for c in range(Q_PER_KV * t // ATT_ROWS):
            rows = slice(c * ATT_ROWS, (c + 1) * ATT_ROWS)
            sc = s[rows]
            if band:
                sc = jnp.where(bad[rows], -jnp.inf, sc)
            sink = jnp.full((ATT_ROWS, 1), sink_ref[h * Q_PER_KV + (c * ATT_ROWS) // t], F32)
            m = jnp.maximum(jnp.max(sc, axis=-1, keepdims=True), sink)
            ec = jnp.exp(sc - m)
            invs.append(1.0 / (jnp.sum(ec, axis=-1, keepdims=True) + jnp.exp(sink - m)))
            es.append(ec.astype(BF16))
        e = jnp.concatenate(es, axis=0)
        r = _dot(e, vcat) * jnp.concatenate(invs, axis=0)
        o = jnp.zeros((t, w), F32)
        for g in range(Q_PER_KV):
            o = o + jnp.where(masks[g], r[g * t:(g + 1) * t, :], 0.0)
        o_ref[:, h * w:(h + 1) * w] = o.astype(o_ref.dtype)


def _attention(q, k4, v4, kx4, vx4, sink, band):
    b, s, qw = q.shape
    rep = vx4.shape[-1]
    lc = vx4.shape[2]
    t = ATT_BLOCK
    nb = s // t
    kern = functools.partial(_attn_kernel, band=band, n_blocks=nb)

    def k_spec(off):
        return pl.BlockSpec((None, N_KV, rep, t),
                            lambda bi, i: (bi, 0, 0, jnp.clip(i + off, 0, nb - 1)))

    def v_spec(off):
        return pl.BlockSpec((None, N_KV, t, rep),
                            lambda bi, i: (bi, 0, jnp.clip(i + off, 0, nb - 1), 0))

    in_specs = [pl.BlockSpec(memory_space=pltpu.SMEM),
                pl.BlockSpec((None, t, qw), lambda bi, i: (bi, i, 0))]
    args = [sink, q]
    if band:
        in_specs += [k_spec(-1), k_spec(0), k_spec(1), v_spec(-1), v_spec(0), v_spec(1)]
        args += [k4, k4, k4, v4, v4, v4]
    in_specs += [pl.BlockSpec((None, N_KV, rep, lc), lambda bi, i: (bi, 0, 0, 0)),
                 pl.BlockSpec((None, N_KV, lc, rep), lambda bi, i: (bi, 0, 0, 0))]
    args += [kx4, vx4]
    return pl.pallas_call(
        kern,
        grid=(b, nb),
        in_specs=in_specs,
        out_specs=pl.BlockSpec((None, t, qw), lambda bi, i: (bi, i, 0)),
        out_shape=jax.ShapeDtypeStruct((b, s, qw), BF16),
        compiler_params=_params("parallel", "parallel"),
        name="attention_band" if band else "attention_ctx",
    )(*args)


def _route_kernel(x_ref, g_ref, sh_ref, sc_ref, whi_ref, wlo_ref, rb_ref, cin_ref, tri_ref, upper_ref,
                  h_ref, slab_ref, tab_ref, cnt_ref, carry_ref):
    first = (pl.program_id(0) == 0) & (pl.program_id(1) == 0)

    @pl.when(first)
    def _():
        carry_ref[...] = cin_ref[...]

    h = _norm_mod(x_ref[...], g_ref[...], sh_ref[...], sc_ref[...])
    _pack_rows(h_ref, h)
    hi = h.astype(BF16)
    lo = (h - hi.astype(F32)).astype(BF16)
    logits = _dot(hi, whi_ref[...]) + _dot(lo, whi_ref[...]) + _dot(hi, wlo_ref[...])
    scores = _sigmoid(logits)
    tm, lanes = scores.shape
    lane = lax.broadcasted_iota(I32, (tm, lanes), 1).astype(F32)
    work = jnp.where(lane < N_EXPERTS, scores + rb_ref[...], -jnp.inf)
    hits, idxs, gates = [], [], []
    for _ in range(TOP_K):
        mx = jnp.max(work, axis=-1, keepdims=True)
        idx = jnp.min(jnp.where(work == mx, lane, float(lanes)), axis=-1, keepdims=True)
        hit = lane == idx
        hits.append(hit)
        idxs.append(idx)
        gates.append(jnp.sum(jnp.where(hit, scores, 0.0), axis=-1, keepdims=True))
        work = jnp.where(hit, -jnp.inf, work)
    gsum = gates[0]
    for gk in gates[1:]:
        gsum = gsum + gk
    gscale = ROUTED_SCALE / (gsum + 1e-20)
    onehot = jnp.zeros((tm, lanes), F32)
    for hit in hits:
        onehot = jnp.where(hit, 1.0, onehot)
    cnt = jnp.sum(onehot, axis=0, keepdims=True)
    cnt_hi = jnp.floor(cnt * (1.0 / 256.0))
    cnt_lo = cnt - 256.0 * cnt_hi
    parts = jnp.concatenate([jnp.broadcast_to(cnt_hi, (8, lanes)), jnp.broadcast_to(cnt_lo, (8, lanes))], axis=0)
    sums = _dot(parts.astype(BF16), upper_ref[...])
    seg = 256.0 * sums[0:1] + sums[8:9]
    before = _dot(tri_ref[...], onehot.astype(BF16)) + seg
    slab = jnp.zeros((tm, lanes), F32)
    for k in range(TOP_K):
        loc = jnp.sum(jnp.where(hits[k], before, 0.0), axis=-1, keepdims=True)
        slab = jnp.where(lane == SLAB_IDX + k, idxs[k], slab)
        slab = jnp.where(lane == SLAB_LOC + k, loc, slab)
        slab = jnp.where(lane == SLAB_GATE + k, gates[k] * gscale, slab)
    slab_ref[...] = slab
    row = lax.broadcasted_iota(I32, (8, lanes), 0)
    tab_ref[...] = jnp.where(row == 0, cnt, jnp.where(row == 1, seg, jnp.where(row == 2, carry_ref[...], 0.0)))
    carry_ref[...] = carry_ref[...] + cnt
    cnt_ref[...] = carry_ref[...]


def _route(x, g, shift, scale, whi, wlo, rb, counts_in, tri, upper):
    b, s, d = x.shape
    tm = MOE_TM
    nt = s // tm
    return pl.pallas_call(
        _route_kernel,
        grid=(b, nt),
        in_specs=[pl.BlockSpec((None, tm, d), lambda bi, i: (bi, i, 0)),
                  pl.BlockSpec((1, d), lambda bi, i: (0, 0)),
                  pl.BlockSpec((None, 1, d), lambda bi, i: (bi, 0, 0)),
                  pl.BlockSpec((None, 1, d), lambda bi, i: (bi, 0, 0)),
                  pl.BlockSpec(whi.shape, lambda bi, i: (0, 0)),
                  pl.BlockSpec(wlo.shape, lambda bi, i: (0, 0)),
                  pl.BlockSpec((1, LANES), lambda bi, i: (0, 0)),
                  pl.BlockSpec((1, LANES), lambda bi, i: (0, 0)),
                  pl.BlockSpec((tm, tm), lambda bi, i: (0, 0)),
                  pl.BlockSpec((LANES, LANES), lambda bi, i: (0, 0))],
        out_specs=[pl.BlockSpec((None, tm, ROW_PARTS, LANES), lambda bi, i: (bi, i, 0, 0)),
                   pl.BlockSpec((None, tm, LANES), lambda bi, i: (bi, i, 0)),
                   pl.BlockSpec((None, 8, LANES), lambda bi, i: (bi * nt + i, 0, 0)),
                   pl.BlockSpec((1, LANES), lambda bi, i: (0, 0))],
        out_shape=[jax.ShapeDtypeStruct((b, s, ROW_PARTS, LANES), U32),
                   jax.ShapeDtypeStruct((b, s, LANES), F32),
                   jax.ShapeDtypeStruct((b * nt, 8, LANES), F32),
                   jax.ShapeDtypeStruct((1, LANES), F32)],
        scratch_shapes=[pltpu.VMEM((1, LANES), F32)],
        compiler_params=_params("arbitrary", "arbitrary"),
        name="route",
    )(x, g, shift, scale, whi, wlo, rb, counts_in, tri, upper)


def _rows(ref, row, n):
    return ref.at[pl.ds(row, n)]


def _run_copies(tab_smem, base, stage_ref, far_ref, sem, to_far):
    n_bits = MOE_TM.bit_length()
    common = n_bits - 3

    def per_expert(e, carry):
        n = tab_smem[base + e]
        near = tab_smem[base + N_EXPERTS + e]
        far = tab_smem[base + 2 * N_EXPERTS + e]

        def piece(bit):
            size = 1 << bit

            @pl.when((n & size) != 0)
            def _():
                done = n & (size - 1)
                a, b = _rows(stage_ref, near + done, size), _rows(far_ref, far + done, size)
                (pltpu.make_async_copy(a, b, sem) if to_far else pltpu.make_async_copy(b, a, sem)).start()

        for bit in range(common):
            piece(bit)

        @pl.when(n >= (1 << common))
        def _():
            for bit in range(common, n_bits):
                piece(bit)
        return carry

    lax.fori_loop(0, N_EXPERTS, per_expert, 0)


def _wait_tile(stage_ref, far_ref, sem, to_far):
    n = TOP_K * MOE_TM
    a, b = _rows(stage_ref, 0, n), _rows(far_ref, 0, n)
    (pltpu.make_async_copy(a, b, sem) if to_far else pltpu.make_async_copy(b, a, sem)).wait()


def _dispatch_kernel(ends_ref, nu_ref, h_ref, loc_ref, tab_ref, *rest, zero_fill, tile_base, n_blocks):
    if zero_fill:
        xs_ref, loc_smem, tab_smem, stages_ref, zero_ref, sems, psem = rest
    else:
        _, xs_ref, loc_smem, tab_smem, stages_ref, sems, psem = rest
        zero_ref = None
    tm = MOE_TM
    i = pl.program_id(0)
    n_steps = pl.num_programs(0)
    slot = i % 2
    stage_ref, sem = stages_ref.at[slot], sems.at[slot]
    loc_copy = pltpu.make_async_copy(loc_ref.at[pl.ds((tile_base + i) * tm * 8, tm * 8)], loc_smem, psem)
    tab_copy = pltpu.make_async_copy(tab_ref.at[pl.ds((tile_base + i) * TAB_WORDS, TAB_WORDS)], tab_smem, psem)
    loc_copy.start()
    tab_copy.start()

    if zero_fill:
        @pl.when(i == 0)
        def _():
            zero_ref[...] = jnp.zeros_like(zero_ref)

            def block_copy(blk):
                return pltpu.make_async_copy(zero_ref, _rows(xs_ref, blk * EXPERT_BLOCK, EXPERT_BLOCK), sem)

            def fill(e, carry, *, start):
                end = ends_ref[e]
                prev = jnp.where(e > 0, ends_ref[jnp.maximum(e - 1, 0)], 0)

                @pl.when(end > prev)
                def _():
                    cp = block_copy(end // EXPERT_BLOCK - 1)
                    cp.start() if start else cp.wait()
                return carry

            def tail(j, carry, *, start):
                cp = block_copy(j)
                cp.start() if start else cp.wait()
                return carry

            lax.fori_loop(0, N_EXPERTS, functools.partial(fill, start=True), 0)
            lax.fori_loop(nu_ref[0], n_blocks, functools.partial(tail, start=True), 0)
            lax.fori_loop(0, N_EXPERTS, functools.partial(fill, start=False), 0)
            lax.fori_loop(nu_ref[0], n_blocks, functools.partial(tail, start=False), 0)

    loc_copy.wait()
    tab_copy.wait()

    @pl.when(i >= 2)
    def _():
        _wait_tile(stage_ref, xs_ref, sem, to_far=True)

    flat = stage_ref.reshape(TOP_K * tm * ROW_PARTS, LANES)
    h_flat = h_ref.reshape(tm * ROW_PARTS, LANES)

    def place(t, c):
        row = h_flat[pl.ds(pl.multiple_of(t * ROW_PARTS, ROW_PARTS), ROW_PARTS), :]
        for k in range(TOP_K):
            flat[pl.ds(pl.multiple_of(loc_smem[t * 8 + k], ROW_PARTS), ROW_PARTS), :] = row
        return c

    lax.fori_loop(0, tm, place, 0, unroll=4)
    _run_copies(tab_smem, 0, stage_ref, xs_ref, sem, to_far=True)

    @pl.when(i == n_steps - 1)
    def _():
        _wait_tile(stage_ref, xs_ref, sem, to_far=True)

        @pl.when(i >= 1)
        def _():
            _wait_tile(stages_ref.at[1 - slot], xs_ref, sems.at[1 - slot], to_far=True)


def _dispatch(h2, loc_flat, tab_flat, ends, n_used, xs_prev, n_rows, tile_base):
    n = h2.shape[0]
    tm = MOE_TM
    zero_fill = xs_prev is None
    kern = functools.partial(_dispatch_kernel, zero_fill=zero_fill, tile_base=tile_base,
                             n_blocks=n_rows // EXPERT_BLOCK)
    in_specs = [pl.BlockSpec((tm, ROW_PARTS, LANES), lambda i, e, nu: (i, 0, 0)),
                pl.BlockSpec(memory_space=pl.ANY),
                pl.BlockSpec(memory_space=pl.ANY)]
    args = [ends, n_used, h2, loc_flat, tab_flat]
    scratch = [pltpu.SMEM((tm * 8,), I32), pltpu.SMEM((TAB_WORDS,), I32),
               pltpu.VMEM((2, TOP_K * tm, ROW_PARTS, LANES), U32)]
    aliases = {}
    if zero_fill:
        scratch.append(pltpu.VMEM((EXPERT_BLOCK, ROW_PARTS, LANES), U32))
    else:
        in_specs.append(pl.BlockSpec(memory_space=pl.ANY))
        args.append(xs_prev)
        aliases = {5: 0}
    scratch += [pltpu.SemaphoreType.DMA((2,)), pltpu.SemaphoreType.DMA]
    return pl.pallas_call(
        kern,
        grid_spec=pltpu.PrefetchScalarGridSpec(
            num_scalar_prefetch=2,
            grid=(n // tm,),
            in_specs=in_specs,
            out_specs=pl.BlockSpec(memory_space=pl.ANY),
            scratch_shapes=scratch),
        out_shape=jax.ShapeDtypeStruct((n_rows, ROW_PARTS, LANES), U32),
        input_output_aliases=aliases,
        compiler_params=_params("arbitrary"),
        name="dispatch",
    )(*args)


def _expert_kernel(be_ref, nu_ref, x_ref, *refs):
    o_ref = refs[-1]
    blk = EXPERT_BLOCK
    used = pl.program_id(0) * EXPERT_GROUP < nu_ref[0]

    @pl.when(used)
    def _():
        for j in range(EXPERT_GROUP):
            w1_ref, w3_ref, w2_ref = refs[3 * j:3 * j + 3]
            x = _unpack_rows(x_ref, j * blk, blk).astype(BF16)
            a = _silu(_dot(x, w1_ref[...])) * _dot(x, w3_ref[...])
            _pack_rows(o_ref, _dot(a.astype(BF16), w2_ref[...]), j * blk)

    @pl.when(jnp.logical_not(used))
    def _():
        o_ref[...] = jnp.zeros_like(o_ref)


def _experts(xs, block_e, n_used, w1, w3, w2, layer):
    d, de = w1.shape[2:]
    rows = EXPERT_BLOCK * EXPERT_GROUP
    n_steps = xs.shape[0] // rows

    def row_map(i, be, nu):
        return (jnp.minimum(i, (nu[0] + EXPERT_GROUP - 1) // EXPERT_GROUP - 1), 0, 0)

    w_specs = []
    for j in range(EXPERT_GROUP):
        w_map = lambda i, be, nu, j=j: (layer, be[i * EXPERT_GROUP + j], 0, 0)
        w_specs += [pl.BlockSpec((None, None, d, de), w_map),
                    pl.BlockSpec((None, None, d, de), w_map),
                    pl.BlockSpec((None, None, de, d), w_map)]

    return pl.pallas_call(
        _expert_kernel,
        grid_spec=pltpu.PrefetchScalarGridSpec(
            num_scalar_prefetch=2,
            grid=(n_steps,),
            in_specs=[pl.BlockSpec((rows, ROW_PARTS, LANES), row_map)] + w_specs,
            out_specs=pl.BlockSpec((rows, ROW_PARTS, LANES), lambda i, be, nu: (i, 0, 0))),
        out_shape=jax.ShapeDtypeStruct(xs.shape, U32),
        compiler_params=_params("arbitrary"),
        name="experts",
    )(block_e, n_used, xs, *([w1, w3, w2] * EXPERT_GROUP))


def _combine_kernel(loc_ref, tab_ref, ys_ref, slab_ref, h_ref, x_ref, gate_ref, s1_ref, s3_ref, s2_ref,
                    o_ref, loc_smem, tab_smem, gk_ref, stage_ref, lo_ref, hi_ref, sem, psem, *, tile_base):
    tm = MOE_TM
    words = tm * 8
    step = pl.program_id(0) * pl.num_programs(1) + pl.program_id(1)
    n_steps = pl.num_programs(0) * pl.num_programs(1)

    def fetch(j):
        tile = tile_base + j
        small = [pltpu.make_async_copy(loc_ref.at[pl.ds(tile * words, words)], loc_smem, psem),
                 pltpu.make_async_copy(tab_ref.at[pl.ds(tile * TAB_WORDS, TAB_WORDS)], tab_smem, psem)]
        for cp in small:
            cp.start()
        for cp in small:
            cp.wait()
        _run_copies(tab_smem, 0, stage_ref, ys_ref, sem, to_far=False)

    @pl.when(step == 0)
    def _():
        fetch(0)

    hb = _unpack_rows(h_ref).astype(BF16)
    shared = _dot((_silu(_dot(hb, s1_ref[...])) * _dot(hb, s3_ref[...])).astype(BF16), s2_ref[...])
    slab = slab_ref[...]
    for k in range(TOP_K):
        gk_ref[k] = jnp.broadcast_to(slab[:, SLAB_GATE + k:SLAB_GATE + k + 1], (tm, LANES))
    _wait_tile(stage_ref, ys_ref, sem, to_far=False)

    flat = stage_ref.reshape(TOP_K * tm * ROW_PARTS, LANES)

    def mix(t, c):
        lo = hi = None
        for k in range(TOP_K):
            g = gk_ref[k, pl.ds(t, ROW_PARTS, stride=0), :]
            at = pl.multiple_of(loc_smem[t * 8 + k], ROW_PARTS)
            wl, wh = _unpack_words(flat[pl.ds(at, ROW_PARTS), :])
            lo = g * wl if lo is None else lo + g * wl
            hi = g * wh if hi is None else hi + g * wh
        lo_ref[t] = lo
        hi_ref[t] = hi
        return c

    lax.fori_loop(0, tm, mix, 0, unroll=4)

    @pl.when(step + 1 < n_steps)
    def _():
        fetch(step + 1)

    routed = jnp.concatenate(_row_chunks(lo_ref) + _row_chunks(hi_ref), axis=-1)
    o_ref[...] = x_ref[...] + gate_ref[...] * (routed + shared)


def _combine(ys, loc_flat, tab_flat, slab, h2, x, gate, s1, s3, s2, tile_base):
    b, s, d = x.shape
    tm = MOE_TM
    kern = functools.partial(_combine_kernel, tile_base=tile_base)
    return pl.pallas_call(
        kern,
        grid=(b, s // tm),
        in_specs=[pl.BlockSpec(memory_space=pl.ANY),
                  pl.BlockSpec(memory_space=pl.ANY),
                  pl.BlockSpec(memory_space=pl.ANY),
                  pl.BlockSpec((None, tm, LANES), lambda bi, i: (bi, i, 0)),
                  pl.BlockSpec((None, tm, ROW_PARTS, LANES), lambda bi, i: (bi, i, 0, 0)),
                  pl.BlockSpec((None, tm, d), lambda bi, i: (bi, i, 0)),
                  pl.BlockSpec((None, 1, d), lambda bi, i: (bi, 0, 0)),
                  pl.BlockSpec(s1.shape, lambda bi, i: (0, 0)),
                  pl.BlockSpec(s3.shape, lambda bi, i: (0, 0)),
                  pl.BlockSpec(s2.shape, lambda bi, i: (0, 0))],
        out_specs=pl.BlockSpec((None, tm, d), lambda bi, i: (bi, i, 0)),
        out_shape=jax.ShapeDtypeStruct((b, s, d), F32),
        scratch_shapes=[pltpu.SMEM((tm * 8,), I32),
                        pltpu.SMEM((TAB_WORDS,), I32),
                        pltpu.VMEM((TOP_K, tm, LANES), F32),
                        pltpu.VMEM((TOP_K * tm, ROW_PARTS, LANES), U32),
                        pltpu.VMEM((tm, ROW_PARTS, LANES), F32),
                        pltpu.VMEM((tm, ROW_PARTS, LANES), F32),
                        pltpu.SemaphoreType.DMA,
                        pltpu.SemaphoreType.DMA],
        compiler_params=_params("arbitrary", "arbitrary"),
        name="combine",
    )(loc_flat, tab_flat, ys, slab, h2, x, gate, s1, s3, s2)


def _dft_tables(length, n_chan):
    scale = 1.0 / math.sqrt(length * n_chan)
    side = 1
    while side * side < length:
        side *= 2
    outer = length // side
    k = jnp.arange(length, dtype=I32)[:, None]
    a_idx = (k * jnp.arange(outer, dtype=I32)[None, :]) % outer
    b_idx = (k * jnp.arange(side, dtype=I32)[None, :]) % length
    ang_a = a_idx.astype(F32) * (2.0 * math.pi / outer)
    ang_b = b_idx.astype(F32) * (2.0 * math.pi / length)
    ca, sa = jnp.cos(ang_a)[:, :, None], jnp.sin(ang_a)[:, :, None]
    cb, sb = jnp.cos(ang_b)[:, None, :], jnp.sin(ang_b)[:, None, :]
    cos_t = (ca * cb - sa * sb).reshape(length, length)
    sin_t = (sa * cb + ca * sb).reshape(length, length)
    table = (jnp.concatenate([cos_t, -sin_t], axis=1) * scale).astype(BF16)
    return table


def _channel_table(n_chan, n_groups):
    m = jnp.arange(n_chan, dtype=I32)
    ang = ((m[:, None] * m[None, :]) % n_chan).astype(F32) * (2.0 * math.pi / n_chan)
    eye = jnp.eye(n_groups, dtype=F32)
    return jnp.concatenate([jnp.kron(eye, jnp.cos(ang)), jnp.kron(eye, jnp.sin(ang))], axis=1).astype(BF16)


def _rope_tables(n_tok):
    rows = n_tok // GRID_W
    axis_dim = HEAD_DIM // 2
    r = jnp.repeat(jnp.arange(rows, dtype=F32), GRID_W)
    col = jnp.tile(jnp.arange(GRID_W, dtype=F32), rows)
    inv = ROPE_BASE ** (-jnp.arange(0, axis_dim, 2, dtype=F32) / axis_dim)
    ar, ac = r[:, None] * inv, col[:, None] * inv
    cos = jnp.concatenate([jnp.cos(ar), jnp.cos(ar), jnp.cos(ac), jnp.cos(ac)], axis=1)
    sin = jnp.concatenate([-jnp.sin(ar), jnp.sin(ar), -jnp.sin(ac), jnp.sin(ac)], axis=1)
    return jnp.tile(cos, (1, N_KV)), jnp.tile(sin, (1, N_KV))


def _head_mean_matrix(width):
    h = jnp.arange(width) // HEAD_DIM
    return ((h[:, None] == h[None, :]).astype(F32) / HEAD_DIM).astype(BF16)


def _tile_matrices():
    src = jnp.arange(N_KV * HEAD_DIM)
    dst = jnp.arange(Q_PER_KV * HEAD_DIM)
    mats = [((src[:, None] // HEAD_DIM == h) & (src[:, None] % HEAD_DIM == dst[None, :] % HEAD_DIM))
            for h in range(N_KV)]
    return jnp.stack(mats).astype(BF16)


def _even_layer(x, xc, mod, modc, norm_g, w_in, ws, bs, w_out, tables):
    b, s, d = x.shape
    w = w_in.shape[1] // 3
    outs = []
    for stream, m, tm in ((x, mod, MIX_TM), (xc, modc, 256)):
        if stream is None:
            outs.append(None)
            continue
        length = stream.shape[1]
        tm = min(tm, length)
        gm, z = _even_mix(stream, norm_g, m[0], m[1], w_in, ws, bs, tables["chan"], tm)
        table = tables["pos"][length]
        y = _matmul(table, z.reshape(2 * length, b * w),
                    min(1024, length), min(1024, b * w), min(2048, 2 * length), BF16)
        outs.append(_outproj(
            gm, pl.BlockSpec((None, tm, w), lambda bi, i: (bi, i, 0)),
            y, pl.BlockSpec((tm, w), lambda bi, i: (i, bi)),
            w_out, stream, m[2], tm))
    return outs


def _odd_layer(x, xc, mod, modc, norm_g, w_in, qg, kg, sink, conv_w, w_out, tables, ctx_out):
    b, s, d = x.shape
    lc = xc.shape[1]
    half = N_HEADS * HEAD_DIM
    tm, tmc = MIX_TM, min(256, lc)
    prep = functools.partial(_odd_in, g=norm_g, w=w_in, qg=qg, kg=kg, bdq=tables["bdq"], bdk=tables["bdk"],
                             tile=tables["tile"], conv_w=conv_w)
    q, k4, v4, conv = prep(x, shift=mod[0], scale=mod[1], cos=tables["cos"], sin=tables["sin"], tm=tm, with_q=True)
    qc, kc4, vc4, convc = prep(xc, shift=modc[0], scale=modc[1], cos=tables["cos_c"], sin=tables["sin_c"],
                               tm=tmc, with_q=ctx_out)
    att = _attention(q, k4, v4, kc4, vc4, sink, band=True)
    spec = lambda t: pl.BlockSpec((None, t, half), lambda bi, i: (bi, i, 0))
    y = _outproj(att, spec(tm), conv, spec(tm), w_out, x, mod[2], tm)
    yc = None
    if ctx_out:
        attc = _attention(qc, None, None, kc4, vc4, sink, band=False)
        yc = _outproj(attc, spec(tmc), convc, spec(tmc), w_out, xc, modc[2], tmc)
    return y, yc


def _moe(x, xc, mod, modc, norm_g, rw_hi, rw_lo, rb, w1, w3, w2, layer, s1, s3, s2, tri, upper):
    b, s, d = x.shape
    n_lat = b * s
    xc_shape = None
    if xc is not None and xc.shape[1] % MOE_TM:
        xc_shape = xc.shape
        xc = xc.reshape(-1, MOE_TM, d)
        modc = [m[:xc.shape[0]] for m in modc]
    counts0 = jnp.zeros((1, LANES), F32)
    h2, slab, tab, counts = _route(x, norm_g, mod[3], mod[4], rw_hi, rw_lo, rb, counts0, tri, upper)
    slabs, tabs = [slab.reshape(n_lat, LANES)], [tab]
    n_tok = n_lat
    if xc is not None:
        h2c, slabc, tabc, counts = _route(xc, norm_g, modc[3], modc[4], rw_hi, rw_lo, rb, counts, tri, upper)
        slabs.append(slabc.reshape(-1, LANES))
        tabs.append(tabc)
        n_tok += slabs[1].shape[0]
    cnt = counts[0, :N_EXPERTS].astype(I32)
    blk = EXPERT_BLOCK
    padded = (cnt + blk - 1) // blk * blk
    ends = jnp.cumsum(padded).astype(I32)
    starts = ends - padded
    step_rows = blk * EXPERT_GROUP
    n_rows = (n_tok * TOP_K + N_EXPERTS * (blk - 1) + step_rows - 1) // step_rows * step_rows
    n_blocks = n_rows // blk
    n_used = (ends[-1] // blk).reshape(1).astype(I32)
    blk_start = jnp.minimum(jnp.arange(n_blocks, dtype=I32), n_used[0] - 1) * blk
    block_e = jnp.minimum(jnp.sum(blk_start[:, None] >= ends[None, :], axis=1), N_EXPERTS - 1).astype(I32)
    slab_all = jnp.concatenate(slabs, axis=0)
    tab_all = jnp.concatenate(tabs, axis=0)[:, :, :N_EXPERTS].astype(I32)
    runs = jnp.concatenate([tab_all[:, 0], tab_all[:, 1], tab_all[:, 2] + starts[None, :]], axis=1)
    tab_flat = jnp.pad(runs, ((0, 0), (0, TAB_WORDS - runs.shape[1]))).reshape(-1)
    loc_flat = (slab_all[:, SLAB_LOC:SLAB_LOC + 8].astype(I32) * ROW_PARTS).reshape(-1)
    packed = lambda a: a.reshape(-1, ROW_PARTS, LANES)
    xs = _dispatch(packed(h2), loc_flat, tab_flat, ends, n_used, None, n_rows, 0)
    if xc is not None:
        xs = _dispatch(packed(h2c), loc_flat, tab_flat, ends, n_used, xs, n_rows, n_lat // MOE_TM)
    ys = _experts(xs, block_e, n_used, w1, w3, w2, layer)
    x_new = _combine(ys, loc_flat, tab_flat, slab, h2, x, mod[5], s1, s3, s2, 0)
    xc_new = None
    if xc is not None:
        xc_new = _combine(ys, loc_flat, tab_flat, slabc, h2c, xc, modc[5], s1, s3, s2, n_lat // MOE_TM)
        if xc_shape is not None:
            xc_new = xc_new.reshape(xc_shape)
    return x_new, xc_new


def kernel(x, c, ctx, c_ctx, ada_w, ada_b, norm1_g, norm2_g, ev_w_in, ev_w_s, ev_b_s, ev_w_out, od_w_in, od_q_norm_g, od_k_norm_g, od_sink, od_conv_w, od_w_out, router_w, router_b, exp_w_gate, exp_w_up, exp_w_down, sh_w_gate, sh_w_up, sh_w_down):
    b, s, d = x.shape
    lc = ctx.shape[1]
    depth = ada_w.shape[0]
    n_groups = ev_w_s.shape[1]
    half = d // 2

    rows = -(-(b + 1) // 8) * 8
    cond = jnp.zeros((rows, d), F32).at[:b].set(c).at[b].set(c_ctx)
    mod_all = _adaln(cond, ada_w, ada_b)

    tables = {
        "chan": _channel_table(LANES, n_groups),
        "pos": {s: _dft_tables(s, LANES), lc: _dft_tables(lc, LANES)},
        "bdq": _head_mean_matrix(N_HEADS * HEAD_DIM),
        "bdk": _head_mean_matrix(N_KV * HEAD_DIM),
        "tile": _tile_matrices(),
    }
    tables["cos"], tables["sin"] = _rope_tables(s)
    tables["cos_c"] = jnp.ones((lc, N_KV * HEAD_DIM), F32)
    tables["sin_c"] = jnp.zeros((lc, N_KV * HEAD_DIM), F32)
    tri = (jnp.arange(MOE_TM)[:, None] > jnp.arange(MOE_TM)[None, :]).astype(BF16)
    upper = (jnp.arange(LANES)[:, None] < jnp.arange(LANES)[None, :]).astype(BF16)

    qw, kw = N_HEADS * HEAD_DIM, N_KV * HEAD_DIM
    perm = jnp.concatenate([jnp.arange(0, qw), jnp.arange(qw + 2 * kw, qw + 2 * kw + 3 * half),
                            jnp.arange(qw, qw + 2 * kw)])

    w1_all, w3_all, w2_all = exp_w_gate.astype(BF16), exp_w_up.astype(BF16), exp_w_down.astype(BF16)
    xc = ctx
    for l in range(depth):
        last = l == depth - 1
        even = l % 2 == 0
        need_ctx = not (last and even)
        pieces = [mod_all[l, :, j * d:(j + 1) * d] for j in range(6)]
        mod = [p[:b].reshape(b, 1, d) for p in pieces]
        modc = [jnp.broadcast_to(p[b].reshape(1, 1, d), (b, 1, d)) for p in pieces]
        g1 = norm1_g[l].reshape(1, d)
        g2 = norm2_g[l].reshape(1, d)
        if even:
            e = l // 2
            bs = jnp.broadcast_to(ev_b_s[e][:, :, None], (n_groups, CHUNK, LANES))
            y, yc = _even_layer(x, xc if (need_ctx and not last) else None, mod, modc, g1,
                                ev_w_in[e].astype(BF16), ev_w_s[e].astype(BF16), bs,
                                ev_w_out[e].astype(BF16), tables)
        else:
            o = l // 2
            qg = (jnp.tile(od_q_norm_g[o], N_HEADS) * (HEAD_DIM ** -0.5)).reshape(1, qw)
            kg = jnp.tile(od_k_norm_g[o], N_KV).reshape(1, kw)
            y, yc = _odd_layer(x, xc, mod, modc, g1, od_w_in[o][:, perm].astype(BF16), qg, kg,
                               od_sink[o], od_conv_w[o], od_w_out[o].astype(BF16), tables, not last)
        x = y
        if not last:
            xc = yc
        rw = jnp.zeros((d, LANES), F32).at[:, :N_EXPERTS].set(router_w[l])
        rw_hi = rw.astype(BF16)
        rw_lo = (rw - rw_hi.astype(F32)).astype(BF16)
        rb = jnp.zeros((1, LANES), F32).at[0, :N_EXPERTS].set(router_b[l])
        x, xc_new = _moe(x, None if last else xc, mod, modc, g2, rw_hi, rw_lo, rb, w1_all, w3_all, w2_all, l,
                         sh_w_gate[l].astype(BF16), sh_w_up[l].astype(BF16), sh_w_down[l].astype(BF16),
                         tri, upper)
        if not last:
            xc = xc_new
    return x
```

```python
import functools
import math

import jax
import jax.numpy as jnp
from jax import lax
from jax.experimental import pallas as pl
from jax.experimental.pallas import tpu as pltpu

F32 = jnp.float32
BF16 = jnp.bfloat16
I32 = jnp.int32
U32 = jnp.uint32

LANES = 128
VMEM_LIMIT = 48 * 2**20

EPS = 1e-6
GRID_W = 64
CHUNK = 128
HEAD_DIM = 64
N_HEADS = 8
N_KV = 2
Q_PER_KV = N_HEADS // N_KV
MIX_TM = 1024
ATT_BLOCK = 128
ATT_ROWS = 64
ROPE_BASE = 10000.0
N_EXPERTS = 64
TOP_K = 6
ROUTED_SCALE = 2.5
EXPERT_BLOCK = 512
EXPERT_GROUP = 2
MOE_TM = 1024
ROW_PARTS = 4
SLAB_IDX, SLAB_LOC, SLAB_GATE = 0, 8, 16
TAB_WORDS = 1024


def _params(*sem):
    return pltpu.CompilerParams(dimension_semantics=sem, vmem_limit_bytes=VMEM_LIMIT)


def _sigmoid(x):
    return 1.0 / (1.0 + jnp.exp(-x))


def _silu(x):
    return x * _sigmoid(x)


def _gelu_tanh(x):
    c = math.sqrt(2.0 / math.pi)
    return x * (0.5 * (1.0 + jnp.tanh(c * (x + 0.044715 * (x * x * x)))))


def _dot(a, b):
    return jnp.dot(a, b, preferred_element_type=F32)


def _unpack_words(w):
    return pltpu.bitcast(w << 16, F32), pltpu.bitcast(w & jnp.uint32(0xFFFF0000), F32)


def _row_chunks(ref, row0=0, n=None):
    total, parts, lanes = ref.shape
    n = total if n is None else n
    flat = ref.reshape(total * parts, lanes)
    return [flat[pl.ds(row0 * parts + c, n, stride=parts), :] for c in range(parts)]


def _unpack_rows(ref, row0=0, n=None):
    halves = [_unpack_words(w) for w in _row_chunks(ref, row0, n)]
    return jnp.concatenate([h[0] for h in halves] + [h[1] for h in halves], axis=-1)


def _pack_rows(ref, val, row0=0):
    total, parts, lanes = ref.shape
    n, half = val.shape[0], val.shape[1] // 2
    bits = pltpu.bitcast(val.astype(BF16).astype(F32), U32)
    words = (bits[:, :half] >> 16) | (bits[:, half:] & jnp.uint32(0xFFFF0000))
    flat = ref.reshape(total * parts, lanes)
    for c in range(parts):
        flat[pl.ds(row0 * parts + c, n, stride=parts), :] = words[:, c * lanes:(c + 1) * lanes]


def _adaln_kernel(c_ref, w_ref, b_ref, o_ref):
    o_ref[...] = _dot(_silu(c_ref[...]), w_ref[...]) + b_ref[...]


def _adaln(cond, ada_w, ada_b):
    n_layers, d, n6 = ada_w.shape
    rows = cond.shape[0]
    tn = 1536
    return pl.pallas_call(
        _adaln_kernel,
        grid=(n_layers, n6 // tn),
        in_specs=[pl.BlockSpec((rows, d), lambda l, j: (0, 0)),
                  pl.BlockSpec((None, d, tn), lambda l, j: (l, 0, j)),
                  pl.BlockSpec((None, 1, tn), lambda l, j: (l, 0, j))],
        out_specs=pl.BlockSpec((None, rows, tn), lambda l, j: (l, 0, j)),
        out_shape=jax.ShapeDtypeStruct((n_layers, rows, n6), F32),
        compiler_params=_params("parallel", "parallel"),
        name="adaln",
    )(cond, ada_w, ada_b.reshape(n_layers, 1, n6))


def _norm_mod(x, g, shift, scale):
    ms = jnp.mean(x * x, axis=-1, keepdims=True)
    h = (x * lax.rsqrt(ms + EPS)) * g
    return h * (1.0 + scale) + shift


def _even_mix_kernel(x_ref, g_ref, sh_ref, sc_ref, w_ref, ws_ref, bs_ref, cs_ref, gm_ref, z_ref, p_ref,
                     *, n_chunks, n_groups):
    h = _norm_mod(x_ref[...], g_ref[...], sh_ref[...], sc_ref[...])
    p_ref[...] = _dot(h.astype(BF16), w_ref[...])
    half = gm_ref.shape[1]
    for c in range(n_chunks):
        rows = slice(c * CHUNK, (c + 1) * CHUNK)
        for g in range(n_groups):
            cols = slice(g * LANES, (g + 1) * LANES)
            ug = _gelu_tanh(p_ref[rows, cols])
            vg = _gelu_tanh(p_ref[rows, half + g * LANES:half + (g + 1) * LANES])
            mu = jnp.mean(vg, axis=-1, keepdims=True)
            dv = vg - mu
            var = jnp.mean(dv * dv, axis=-1, keepdims=True)
            vn = dv * lax.rsqrt(var + 1e-5)
            fg = _dot(ws_ref[g], vn.astype(BF16)) + bs_ref[g]
            gm_ref[rows, cols] = (ug * fg).astype(gm_ref.dtype)
    fz = _dot(p_ref[:, 2 * half:].astype(BF16), cs_ref[...])
    z_ref[0] = fz[:, :half].astype(z_ref.dtype)
    z_ref[1] = fz[:, half:].astype(z_ref.dtype)


def _even_mix(x, g, shift, scale, w_in, ws, bs, cs, tm):
    b, s, d = x.shape
    n3 = w_in.shape[1]
    w = n3 // 3
    n_groups = w // LANES
    kern = functools.partial(_even_mix_kernel, n_chunks=tm // CHUNK, n_groups=n_groups)
    return pl.pallas_call(
        kern,
        grid=(b, s // tm),
        in_specs=[pl.BlockSpec((None, tm, d), lambda bi, i: (bi, i, 0)),
                  pl.BlockSpec((1, d), lambda bi, i: (0, 0)),
                  pl.BlockSpec((None, 1, d), lambda bi, i: (bi, 0, 0)),
                  pl.BlockSpec((None, 1, d), lambda bi, i: (bi, 0, 0)),
                  pl.BlockSpec((d, n3), lambda bi, i: (0, 0)),
                  pl.BlockSpec(ws.shape, lambda bi, i: (0, 0, 0)),
                  pl.BlockSpec(bs.shape, lambda bi, i: (0, 0, 0)),
                  pl.BlockSpec(cs.shape, lambda bi, i: (0, 0))],
        out_specs=[pl.BlockSpec((None, tm, w), lambda bi, i: (bi, i, 0)),
                   pl.BlockSpec((2, tm, w), lambda bi, i: (0, i, bi))],
        out_shape=[jax.ShapeDtypeStruct((b, s, w), BF16),
                   jax.ShapeDtypeStruct((2, s, b * w), BF16)],
        scratch_shapes=[pltpu.VMEM((tm, n3), F32)],
        compiler_params=_params("parallel", "parallel"),
        name="even_mix",
    )(x, g, shift, scale, w_in, ws, bs, cs)


def _mm_kernel(a_ref, b_ref, o_ref, acc_ref):
    k = pl.program_id(2)

    @pl.when(k == 0)
    def _():
        acc_ref[...] = jnp.zeros_like(acc_ref)

    acc_ref[...] += _dot(a_ref[...], b_ref[...])

    @pl.when(k == pl.num_programs(2) - 1)
    def _():
        o_ref[...] = acc_ref[...].astype(o_ref.dtype)


def _matmul(a, b, tm, tn, tk, out_dtype):
    m, kd = a.shape
    n = b.shape[1]
    return pl.pallas_call(
        _mm_kernel,
        grid=(m // tm, n // tn, kd // tk),
        in_specs=[pl.BlockSpec((tm, tk), lambda i, j, k: (i, k)),
                  pl.BlockSpec((tk, tn), lambda i, j, k: (k, j))],
        out_specs=pl.BlockSpec((tm, tn), lambda i, j, k: (i, j)),
        out_shape=jax.ShapeDtypeStruct((m, n), out_dtype),
        scratch_shapes=[pltpu.VMEM((tm, tn), F32)],
        compiler_params=_params("parallel", "parallel", "arbitrary"),
        name="dft_matmul",
    )(a, b)


def _outproj_kernel(a_ref, b_ref, w_ref, x_ref, gate_ref, o_ref):
    ab = jnp.concatenate([a_ref[...], b_ref[...]], axis=-1)
    o_ref[...] = x_ref[...] + gate_ref[...] * _dot(ab, w_ref[...])


def _outproj(a, a_spec, bsrc, b_spec, w, x, gate, tm):
    b, s, d = x.shape
    return pl.pallas_call(
        _outproj_kernel,
        grid=(b, s // tm),
        in_specs=[a_spec, b_spec,
                  pl.BlockSpec(w.shape, lambda bi, i: (0, 0)),
                  pl.BlockSpec((None, tm, d), lambda bi, i: (bi, i, 0)),
                  pl.BlockSpec((None, 1, d), lambda bi, i: (bi, 0, 0))],
        out_specs=pl.BlockSpec((None, tm, d), lambda bi, i: (bi, i, 0)),
        out_shape=jax.ShapeDtypeStruct((b, s, d), F32),
        compiler_params=_params("parallel", "parallel"),
        name="outproj",
    )(a, bsrc, w, x, gate)


def _head_rms(x, bd_ref):
    xx = x * x
    hi = xx.astype(BF16)
    lo = (xx - hi.astype(F32)).astype(BF16)
    ms = _dot(hi, bd_ref[...]) + _dot(lo, bd_ref[...])
    return x * lax.rsqrt(ms + EPS)


def _rope(x, cos, sins):
    width = x.shape[1]
    lane = lax.broadcasted_iota(I32, x.shape, 1)
    first = (lane & 31) < 16
    swapped = jnp.where(first, pltpu.roll(x, width - 16, 1), pltpu.roll(x, 16, 1))
    return x * cos + swapped * sins


HALO = 8


def _odd_in_kernel(x_ref, xp_ref, xn_ref, g_ref, sh_ref, sc_ref, w_ref, cos_ref, sin_ref, qg_ref, kg_ref,
                   bdq_ref, bdk_ref, tile_ref, tile_t_ref, cw_ref, qo_ref, k4_ref, v4_ref, co_ref, p_ref, *, with_q):
    i = pl.program_id(1)
    n_tiles = pl.num_programs(1)
    tm = x_ref.shape[0]
    qw, dc, kw = qo_ref.shape[1], co_ref.shape[1], kg_ref.shape[1]
    g, sh, sc = g_ref[...], sh_ref[...], sc_ref[...]
    p_ref[...] = _dot(_norm_mod(x_ref[...], g, sh, sc).astype(BF16), w_ref[...])
    cos, sin = cos_ref[...], sin_ref[...]
    if with_q:
        reps = qw // kw
        qn = _head_rms(p_ref[:, :qw], bdq_ref) * qg_ref[...]
        qo_ref[...] = _rope(qn, jnp.concatenate([cos] * reps, axis=1),
                            jnp.concatenate([sin] * reps, axis=1)).astype(qo_ref.dtype)
    else:
        qo_ref[...] = jnp.zeros_like(qo_ref)
    k0 = qw + 3 * dc
    kn = _head_rms(p_ref[:, k0:k0 + kw], bdk_ref) * kg_ref[...]
    kr = _rope(kn, cos, sin).astype(BF16)
    vb = p_ref[:, k0 + kw:k0 + 2 * kw].astype(BF16)
    for h in range(N_KV):
        k4_ref[h] = lax.dot_general(tile_t_ref[h], kr, (((1,), (1,)), ((), ())),
                                    preferred_element_type=F32).astype(k4_ref.dtype)
        v4_ref[h] = _dot(vb, tile_ref[h]).astype(v4_ref.dtype)
    halo = _norm_mod(jnp.concatenate([xp_ref[...], xn_ref[...]], axis=0), g, sh, sc).astype(BF16)
    gz = _dot(halo, w_ref[:, qw:qw + dc]) * _dot(halo, w_ref[:, qw + 2 * dc:qw + 3 * dc])
    before = jnp.where(i > 0, gz[HALO - 1:HALO, :], 0.0)
    after = jnp.where(i < n_tiles - 1, gz[HALO:HALO + 1, :], 0.0)
    ridx = lax.broadcasted_iota(I32, (tm, LANES), 0)
    for c in range(dc // LANES):
        cols = slice(c * LANES, (c + 1) * LANES)
        zc = p_ref[:, qw + c * LANES:qw + (c + 1) * LANES] * p_ref[:, qw + 2 * dc + c * LANES:qw + 2 * dc + (c + 1) * LANES]
        zp = jnp.where(ridx == 0, before[:, cols], pltpu.roll(zc, 1, 0))
        zn = jnp.where(ridx == tm - 1, after[:, cols], pltpu.roll(zc, tm - 1, 0))
        y = zp * cw_ref[0:1, cols] + zc * cw_ref[1:2, cols] + zn * cw_ref[2:3, cols]
        co_ref[:, cols] = (p_ref[:, qw + dc + c * LANES:qw + dc + (c + 1) * LANES] * y).astype(co_ref.dtype)


def _odd_in(x, g, shift, scale, w, cos, sin, qg, kg, bdq, bdk, tile, conv_w, tm, with_q):
    b, s, d = x.shape
    n = w.shape[1]
    tile_t = jnp.swapaxes(tile, 1, 2)
    qw = N_HEADS * HEAD_DIM
    kw = N_KV * HEAD_DIM
    rep = Q_PER_KV * HEAD_DIM
    dc = conv_w.shape[1]
    per_tile = tm // HALO
    kern = functools.partial(_odd_in_kernel, with_q=with_q)
    return pl.pallas_call(
        kern,
        grid=(b, s // tm),
        in_specs=[pl.BlockSpec((None, tm, d), lambda bi, i: (bi, i, 0)),
                  pl.BlockSpec((None, HALO, d), lambda bi, i: (bi, jnp.maximum(i * per_tile - 1, 0), 0)),
                  pl.BlockSpec((None, HALO, d), lambda bi, i: (bi, jnp.minimum((i + 1) * per_tile, s // HALO - 1), 0)),
                  pl.BlockSpec((1, d), lambda bi, i: (0, 0)),
                  pl.BlockSpec((None, 1, d), lambda bi, i: (bi, 0, 0)),
                  pl.BlockSpec((None, 1, d), lambda bi, i: (bi, 0, 0)),
                  pl.BlockSpec((d, n), lambda bi, i: (0, 0)),
                  pl.BlockSpec((tm, kw), lambda bi, i: (i, 0)),
                  pl.BlockSpec((tm, kw), lambda bi, i: (i, 0)),
                  pl.BlockSpec((1, qw), lambda bi, i: (0, 0)),
                  pl.BlockSpec((1, kw), lambda bi, i: (0, 0)),
                  pl.BlockSpec(bdq.shape, lambda bi, i: (0, 0)),
                  pl.BlockSpec(bdk.shape, lambda bi, i: (0, 0)),
                  pl.BlockSpec(tile.shape, lambda bi, i: (0, 0, 0)),
                  pl.BlockSpec(tile_t.shape, lambda bi, i: (0, 0, 0)),
                  pl.BlockSpec(conv_w.shape, lambda bi, i: (0, 0))],
        out_specs=[pl.BlockSpec((None, tm, qw), lambda bi, i: (bi, i, 0)),
                   pl.BlockSpec((None, N_KV, rep, tm), lambda bi, i: (bi, 0, 0, i)),
                   pl.BlockSpec((None, N_KV, tm, rep), lambda bi, i: (bi, 0, i, 0)),
                   pl.BlockSpec((None, tm, dc), lambda bi, i: (bi, i, 0))],
        out_shape=[jax.ShapeDtypeStruct((b, s, qw), BF16),
                   jax.ShapeDtypeStruct((b, N_KV, rep, s), BF16),
                   jax.ShapeDtypeStruct((b, N_KV, s, rep), BF16),
                   jax.ShapeDtypeStruct((b, s, dc), BF16)],
        scratch_shapes=[pltpu.VMEM((tm, n), F32)],
        compiler_params=_params("parallel", "parallel"),
        name="odd_in",
    )(x, x, x, g, shift, scale, w, cos, sin, qg, kg, bdq, bdk, tile, tile_t, conv_w)


def _attn_kernel(sink_ref, q_ref, *refs, band, n_blocks):
    o_ref = refs[-1]
    i = pl.program_id(1)
    t = q_ref.shape[0]
    w = Q_PER_KV * HEAD_DIM
    lane = lax.broadcasted_iota(I32, (t, w), 1)
    masks = [(lane >= g * HEAD_DIM) & (lane < (g + 1) * HEAD_DIM) for g in range(Q_PER_KV)]
    bad = None
    if band:
        n_keys = 3 * t + refs[7].shape[1]
        row = lax.broadcasted_iota(I32, (Q_PER_KV * t, n_keys), 0) & (t - 1)
        col = lax.broadcasted_iota(I32, (Q_PER_KV * t, n_keys), 1)
        off_prev = jnp.where(i > 0, 0, 4 * t)
        off_next = jnp.where(i < n_blocks - 1, 0, 4 * t)
        bad_prev = (col < t) & (col < row + off_prev)
        bad_next = (col >= 2 * t) & (col < 3 * t) & (col - 2 * t > row - off_next)
        bad = bad_prev | bad_next
    for h in range(N_KV):
        if band:
            kp, kc_, kn, vp, vc_, vn, kx, vx = refs[:-1]
            kcat = jnp.concatenate([kp[h], kc_[h], kn[h], kx[h]], axis=1)
            vcat = jnp.concatenate([vp[h], vc_[h], vn[h], vx[h]], axis=0)
        else:
            kx, vx = refs[:-1]
            kcat, vcat = kx[h], vx[h]
        q = q_ref[:, h * w:(h + 1) * w]
        q4 = jnp.concatenate([jnp.where(m, q, jnp.zeros_like(q)) for m in masks], axis=0)
        s = _dot(q4, kcat)
        es, invs = [], []
        for c in range(Q_PER_KV * t // ATT_ROWS):
            rows = slice(c * ATT_ROWS, (c + 1) * ATT_ROWS)
            sc = s[rows]
            if band:
                sc = jnp.where(bad[rows], -jnp.inf, sc)
            sink = jnp.full((ATT_ROWS, 1), sink_ref[h * Q_PER_KV + (c * ATT_ROWS) // t], F32)
            m = jnp.maximum(jnp.max(sc, axis=-1, keepdims=True), sink)
            ec = jnp.exp(sc - m)
            invs.append(1.0 / (jnp.sum(ec, axis=-1, keepdims=True) + jnp.exp(sink - m)))
            es.append(ec.astype(BF16))
        e = jnp.concatenate(es, axis=0)
        r = _dot(e, vcat) * jnp.concatenate(invs, axis=0)
        o = jnp.zeros((t, w), F32)
        for g in range(Q_PER_KV):
            o = o + jnp.where(masks[g], r[g * t:(g + 1) * t, :], 0.0)
        o_ref[:, h * w:(h + 1) * w] = o.astype(o_ref.dtype)


def _attention(q, k4, v4, kx4, vx4, sink, band):
    b, s, qw = q.shape
    rep = vx4.shape[-1]
    lc = vx4.shape[2]
    t = ATT_BLOCK
    nb = s // t
    kern = functools.partial(_attn_kernel, band=band, n_blocks=nb)

    def k_spec(off):
        return pl.BlockSpec((None, N_KV, rep, t),
                            lambda bi, i: (bi, 0, 0, jnp.clip(i + off, 0, nb - 1)))

    def v_spec(off):
        return pl.BlockSpec((None, N_KV, t, rep),
                            lambda bi, i: (bi, 0, jnp.clip(i + off, 0, nb - 1), 0))

    in_specs = [pl.BlockSpec(memory_space=pltpu.SMEM),
                pl.BlockSpec((None, t, qw), lambda bi, i: (bi, i, 0))]
    args = [sink, q]
    if band:
        in_specs += [k_spec(-1), k_spec(0), k_spec(1), v_spec(-1), v_spec(0), v_spec(1)]
        args += [k4, k4, k4, v4, v4, v4]
    in_specs += [pl.BlockSpec((None, N_KV, rep, lc), lambda bi, i: (bi, 0, 0, 0)),
                 pl.BlockSpec((None, N_KV, lc, rep), lambda bi, i: (bi, 0, 0, 0))]
    args += [kx4, vx4]
    return pl.pallas_call(
        kern,
        grid=(b, nb),
        in_specs=in_specs,
        out_specs=pl.BlockSpec((None, t, qw), lambda bi, i: (bi, i, 0)),
        out_shape=jax.ShapeDtypeStruct((b, s, qw), BF16),
        compiler_params=_params("parallel", "parallel"),
        name="attention_band" if band else "attention_ctx",
    )(*args)


def _route_kernel(x_ref, g_ref, sh_ref, sc_ref, whi_ref, wlo_ref, rb_ref, cin_ref, tri_ref, upper_ref,
                  h_ref, slab_ref, tab_ref, cnt_ref, carry_ref):
    first = (pl.program_id(0) == 0) & (pl.program_id(1) == 0)

    @pl.when(first)
    def _():
        carry_ref[...] = cin_ref[...]

    h = _norm_mod(x_ref[...], g_ref[...], sh_ref[...], sc_ref[...])
    _pack_rows(h_ref, h)
    hi = h.astype(BF16)
    lo = (h - hi.astype(F32)).astype(BF16)
    logits = _dot(hi, whi_ref[...]) + _dot(lo, whi_ref[...]) + _dot(hi, wlo_ref[...])
    scores = _sigmoid(logits)
    tm, lanes = scores.shape
    lane = lax.broadcasted_iota(I32, (tm, lanes), 1).astype(F32)
    work = jnp.where(lane < N_EXPERTS, scores + rb_ref[...], -jnp.inf)
    hits, idxs, gates = [], [], []
    for _ in range(TOP_K):
        mx = jnp.max(work, axis=-1, keepdims=True)
        idx = jnp.min(jnp.where(work == mx, lane, float(lanes)), axis=-1, keepdims=True)
        hit = lane == idx
        hits.append(hit)
        idxs.append(idx)
        gates.append(jnp.sum(jnp.where(hit, scores, 0.0), axis=-1, keepdims=True))
        work = jnp.where(hit, -jnp.inf, work)
    gsum = gates[0]
    for gk in gates[1:]:
        gsum = gsum + gk
    gscale = ROUTED_SCALE / (gsum + 1e-20)
    onehot = jnp.zeros((tm, lanes), F32)
    for hit in hits:
        onehot = jnp.where(hit, 1.0, onehot)
    cnt = jnp.sum(onehot, axis=0, keepdims=True)
    cnt_hi = jnp.floor(cnt * (1.0 / 256.0))
    cnt_lo = cnt - 256.0 * cnt_hi
    parts = jnp.concatenate([jnp.broadcast_to(cnt_hi, (8, lanes)), jnp.broadcast_to(cnt_lo, (8, lanes))], axis=0)
    sums = _dot(parts.astype(BF16), upper_ref[...])
    seg = 256.0 * sums[0:1] + sums[8:9]
    before = _dot(tri_ref[...], onehot.astype(BF16)) + seg
    slab = jnp.zeros((tm, lanes), F32)
    for k in range(TOP_K):
        loc = jnp.sum(jnp.where(hits[k], before, 0.0), axis=-1, keepdims=True)
        slab = jnp.where(lane == SLAB_IDX + k, idxs[k], slab)
        slab = jnp.where(lane == SLAB_LOC + k, loc, slab)
        slab = jnp.where(lane == SLAB_GATE + k, gates[k] * gscale, slab)
    slab_ref[...] = slab
    row = lax.broadcasted_iota(I32, (8, lanes), 0)
    tab_ref[...] = jnp.where(row == 0, cnt, jnp.where(row == 1, seg, jnp.where(row == 2, carry_ref[...], 0.0)))
    carry_ref[...] = carry_ref[...] + cnt
    cnt_ref[...] = carry_ref[...]


def _route(x, g, shift, scale, whi, wlo, rb, counts_in, tri, upper):
    b, s, d = x.shape
    tm = MOE_TM
    nt = s // tm
    return pl.pallas_call(
        _route_kernel,
        grid=(b, nt),
        in_specs=[pl.BlockSpec((None, tm, d), lambda bi, i: (bi, i, 0)),
                  pl.BlockSpec((1, d), lambda bi, i: (0, 0)),
                  pl.BlockSpec((None, 1, d), lambda bi, i: (bi, 0, 0)),
                  pl.BlockSpec((None, 1, d), lambda bi, i: (bi, 0, 0)),
                  pl.BlockSpec(whi.shape, lambda bi, i: (0, 0)),
                  pl.BlockSpec(wlo.shape, lambda bi, i: (0, 0)),
                  pl.BlockSpec((1, LANES), lambda bi, i: (0, 0)),
                  pl.BlockSpec((1, LANES), lambda bi, i: (0, 0)),
                  pl.BlockSpec((tm, tm), lambda bi, i: (0, 0)),
                  pl.BlockSpec((LANES, LANES), lambda bi, i: (0, 0))],
        out_specs=[pl.BlockSpec((None, tm, ROW_PARTS, LANES), lambda bi, i: (bi, i, 0, 0)),
                   pl.BlockSpec((None, tm, LANES), lambda bi, i: (bi, i, 0)),
                   pl.BlockSpec((None, 8, LANES), lambda bi, i: (bi * nt + i, 0, 0)),
                   pl.BlockSpec((1, LANES), lambda bi, i: (0, 0))],
        out_shape=[jax.ShapeDtypeStruct((b, s, ROW_PARTS, LANES), U32),
                   jax.ShapeDtypeStruct((b, s, LANES), F32),
                   jax.ShapeDtypeStruct((b * nt, 8, LANES), F32),
                   jax.ShapeDtypeStruct((1, LANES), F32)],
        scratch_shapes=[pltpu.VMEM((1, LANES), F32)],
        compiler_params=_params("arbitrary", "arbitrary"),
        name="route",
    )(x, g, shift, scale, whi, wlo, rb, counts_in, tri, upper)


def _rows(ref, row, n):
    return ref.at[pl.ds(row, n)]


def _run_copies(tab_smem, base, stage_ref, far_ref, sem, to_far):
    n_bits = MOE_TM.bit_length()
    common = n_bits - 3

    def per_expert(e, carry):
        n = tab_smem[base + e]
        near = tab_smem[base + N_EXPERTS + e]
        far = tab_smem[base + 2 * N_EXPERTS + e]

        def piece(bit):
            size = 1 << bit

            @pl.when((n & size) != 0)
            def _():
                done = n & (size - 1)
                a, b = _rows(stage_ref, near + done, size), _rows(far_ref, far + done, size)
                (pltpu.make_async_copy(a, b, sem) if to_far else pltpu.make_async_copy(b, a, sem)).start()

        for bit in range(common):
            piece(bit)

        @pl.when(n >= (1 << common))
        def _():
            for bit in range(common, n_bits):
                piece(bit)
        return carry

    lax.fori_loop(0, N_EXPERTS, per_expert, 0)


def _wait_tile(stage_ref, far_ref, sem, to_far):
    n = TOP_K * MOE_TM
    a, b = _rows(stage_ref, 0, n), _rows(far_ref, 0, n)
    (pltpu.make_async_copy(a, b, sem) if to_far else pltpu.make_async_copy(b, a, sem)).wait()


def _dispatch_kernel(ends_ref, nu_ref, h_ref, loc_ref, tab_ref, *rest, zero_fill, tile_base, n_blocks):
    if zero_fill:
        xs_ref, loc_smem, tab_smem, stages_ref, zero_ref, sems, psem = rest
    else:
        _, xs_ref, loc_smem, tab_smem, stages_ref, sems, psem = rest
        zero_ref = None
    tm = MOE_TM
    i = pl.program_id(0)
    n_steps = pl.num_programs(0)
    slot = i % 2
    stage_ref, sem = stages_ref.at[slot], sems.at[slot]

    def table_copies(j):
        return [pltpu.make_async_copy(loc_ref.at[pl.ds((tile_base + j) * tm * 8, tm * 8)], loc_smem, psem),
                pltpu.make_async_copy(tab_ref.at[pl.ds((tile_base + j) * TAB_WORDS, TAB_WORDS)], tab_smem, psem)]

    @pl.when(i == 0)
    def _():
        for cp in table_copies(0):
            cp.start()

    if zero_fill:
        @pl.when(i == 0)
        def _():
            zero_ref[...] = jnp.zeros_like(zero_ref)

            def block_copy(blk):
                return pltpu.make_async_copy(zero_ref, _rows(xs_ref, blk * EXPERT_BLOCK, EXPERT_BLOCK), sem)

            def fill(e, carry, *, start):
                end = ends_ref[e]
                prev = jnp.where(e > 0, ends_ref[jnp.maximum(e - 1, 0)], 0)

                @pl.when(end > prev)
                def _():
                    cp = block_copy(end // EXPERT_BLOCK - 1)
                    cp.start() if start else cp.wait()
                return carry

            def tail(j, carry, *, start):
                cp = block_copy(j)
                cp.start() if start else cp.wait()
                return carry

            lax.fori_loop(0, N_EXPERTS, functools.partial(fill, start=True), 0)
            lax.fori_loop(nu_ref[0], n_blocks, functools.partial(tail, start=True), 0)
            lax.fori_loop(0, N_EXPERTS, functools.partial(fill, start=False), 0)
            lax.fori_loop(nu_ref[0], n_blocks, functools.partial(tail, start=False), 0)

    for cp in table_copies(i):
        cp.wait()

    @pl.when(i >= 2)
    def _():
        _wait_tile(stage_ref, xs_ref, sem, to_far=True)

    flat = stage_ref.reshape(TOP_K * tm * ROW_PARTS, LANES)
    h_flat = h_ref.reshape(tm * ROW_PARTS, LANES)

    def place(t, c):
        row = h_flat[pl.ds(pl.multiple_of(t * ROW_PARTS, ROW_PARTS), ROW_PARTS), :]
        for k in range(TOP_K):
            flat[pl.ds(pl.multiple_of(loc_smem[t * 8 + k], ROW_PARTS), ROW_PARTS), :] = row
        return c

    lax.fori_loop(0, tm, place, 0, unroll=4)
    _run_copies(tab_smem, 0, stage_ref, xs_ref, sem, to_far=True)

    @pl.when(i + 1 < n_steps)
    def _():
        for cp in table_copies(i + 1):
            cp.start()

    @pl.when(i == n_steps - 1)
    def _():
        _wait_tile(stage_ref, xs_ref, sem, to_far=True)

        @pl.when(i >= 1)
        def _():
            _wait_tile(stages_ref.at[1 - slot], xs_ref, sems.at[1 - slot], to_far=True)


def _dispatch(h2, loc_flat, tab_flat, ends, n_used, xs_prev, n_rows, tile_base):
    n = h2.shape[0]
    tm = MOE_TM
    zero_fill = xs_prev is None
    kern = functools.partial(_dispatch_kernel, zero_fill=zero_fill, tile_base=tile_base,
                             n_blocks=n_rows // EXPERT_BLOCK)
    in_specs = [pl.BlockSpec((tm, ROW_PARTS, LANES), lambda i, e, nu: (i, 0, 0)),
                pl.BlockSpec(memory_space=pl.ANY),
                pl.BlockSpec(memory_space=pl.ANY)]
    args = [ends, n_used, h2, loc_flat, tab_flat]
    scratch = [pltpu.SMEM((tm * 8,), I32), pltpu.SMEM((TAB_WORDS,), I32),
               pltpu.VMEM((2, TOP_K * tm, ROW_PARTS, LANES), U32)]
    aliases = {}
    if zero_fill:
        scratch.append(pltpu.VMEM((EXPERT_BLOCK, ROW_PARTS, LANES), U32))
    else:
        in_specs.append(pl.BlockSpec(memory_space=pl.ANY))
        args.append(xs_prev)
        aliases = {5: 0}
    scratch += [pltpu.SemaphoreType.DMA((2,)), pltpu.SemaphoreType.DMA]
    return pl.pallas_call(
        kern,
        grid_spec=pltpu.PrefetchScalarGridSpec(
            num_scalar_prefetch=2,
            grid=(n // tm,),
            in_specs=in_specs,
            out_specs=pl.BlockSpec(memory_space=pl.ANY),
            scratch_shapes=scratch),
        out_shape=jax.ShapeDtypeStruct((n_rows, ROW_PARTS, LANES), U32),
        input_output_aliases=aliases,
        compiler_params=_params("arbitrary"),
        name="dispatch",
    )(*args)


def _expert_kernel(be_ref, nu_ref, x_ref, *refs):
    o_ref = refs[-1]
    blk = EXPERT_BLOCK
    used = pl.program_id(0) * EXPERT_GROUP < nu_ref[0]

    @pl.when(used)
    def _():
        for j in range(EXPERT_GROUP):
            w1_ref, w3_ref, w2_ref = refs[3 * j:3 * j + 3]
            x = _unpack_rows(x_ref, j * blk, blk).astype(BF16)
            a = _silu(_dot(x, w1_ref[...])) * _dot(x, w3_ref[...])
            _pack_rows(o_ref, _dot(a.astype(BF16), w2_ref[...]), j * blk)

    @pl.when(jnp.logical_not(used))
    def _():
        o_ref[...] = jnp.zeros_like(o_ref)


def _experts(xs, block_e, n_used, w1, w3, w2, layer):
    d, de = w1.shape[2:]
    rows = EXPERT_BLOCK * EXPERT_GROUP
    n_steps = xs.shape[0] // rows

    def row_map(i, be, nu):
        return (jnp.minimum(i, (nu[0] + EXPERT_GROUP - 1) // EXPERT_GROUP - 1), 0, 0)

    w_specs = []
    for j in range(EXPERT_GROUP):
        w_map = lambda i, be, nu, j=j: (layer, be[i * EXPERT_GROUP + j], 0, 0)
        w_specs += [pl.BlockSpec((None, None, d, de), w_map),
                    pl.BlockSpec((None, None, d, de), w_map),
                    pl.BlockSpec((None, None, de, d), w_map)]

    return pl.pallas_call(
        _expert_kernel,
        grid_spec=pltpu.PrefetchScalarGridSpec(
            num_scalar_prefetch=2,
            grid=(n_steps,),
            in_specs=[pl.BlockSpec((rows, ROW_PARTS, LANES), row_map)] + w_specs,
            out_specs=pl.BlockSpec((rows, ROW_PARTS, LANES), lambda i, be, nu: (i, 0, 0))),
        out_shape=jax.ShapeDtypeStruct(xs.shape, U32),
        compiler_params=_params("arbitrary"),
        name="experts",
    )(block_e, n_used, xs, *([w1, w3, w2] * EXPERT_GROUP))


def _combine_kernel(loc_ref, tab_ref, ys_ref, slab_ref, h_ref, x_ref, gate_ref, s1_ref, s3_ref, s2_ref,
                    o_ref, loc_smem, tab_smem, gk_ref, stage_ref, lo_ref, hi_ref, sem, psem, tsem, *, tile_base):
    tm = MOE_TM
    words = tm * 8
    step = pl.program_id(0) * pl.num_programs(1) + pl.program_id(1)
    n_steps = pl.num_programs(0) * pl.num_programs(1)

    def loc_copy(j):
        return pltpu.make_async_copy(loc_ref.at[pl.ds((tile_base + j) * words, words)], loc_smem, psem)

    def tab_copy(j):
        return pltpu.make_async_copy(tab_ref.at[pl.ds((tile_base + j) * TAB_WORDS, TAB_WORDS)], tab_smem, tsem)

    def gather(j):
        tab_copy(j).wait()
        _run_copies(tab_smem, 0, stage_ref, ys_ref, sem, to_far=False)
        loc_copy(j).start()

    @pl.when(step == 0)
    def _():
        tab_copy(0).start()
        gather(0)

    hb = _unpack_rows(h_ref).astype(BF16)
    shared = _dot((_silu(_dot(hb, s1_ref[...])) * _dot(hb, s3_ref[...])).astype(BF16), s2_ref[...])
    slab = slab_ref[...]
    for k in range(TOP_K):
        gk_ref[k] = jnp.broadcast_to(slab[:, SLAB_GATE + k:SLAB_GATE + k + 1], (tm, LANES))
    _wait_tile(stage_ref, ys_ref, sem, to_far=False)

    @pl.when(step + 1 < n_steps)
    def _():
        tab_copy(step + 1).start()
    loc_copy(step).wait()

    flat = stage_ref.reshape(TOP_K * tm * ROW_PARTS, LANES)

    def mix(t, c):
        lo = hi = None
        for k in range(TOP_K):
            g = gk_ref[k, pl.ds(t, ROW_PARTS, stride=0), :]
            at = pl.multiple_of(loc_smem[t * 8 + k], ROW_PARTS)
            wl, wh = _unpack_words(flat[pl.ds(at, ROW_PARTS), :])
            lo = g * wl if lo is None else lo + g * wl
            hi = g * wh if hi is None else hi + g * wh
        lo_ref[t] = lo
        hi_ref[t] = hi
        return c

    lax.fori_loop(0, tm, mix, 0, unroll=4)

    @pl.when(step + 1 < n_steps)
    def _():
        gather(step + 1)

    routed =jnp.concatenate(_row_chunks(lo_ref) + _row_chunks(hi_ref), axis=-1)
    o_ref[...] = x_ref[...] + gate_ref[...] * (routed + shared)


def _combine(ys, loc_flat, tab_flat, slab, h2, x, gate, s1, s3, s2, tile_base):
    b, s, d = x.shape
    tm = MOE_TM
    kern = functools.partial(_combine_kernel, tile_base=tile_base)
    return pl.pallas_call(
        kern,
        grid=(b, s // tm),
        in_specs=[pl.BlockSpec(memory_space=pl.ANY),
                  pl.BlockSpec(memory_space=pl.ANY),
                  pl.BlockSpec(memory_space=pl.ANY),
                  pl.BlockSpec((None, tm, LANES), lambda bi, i: (bi, i, 0)),
                  pl.BlockSpec((None, tm, ROW_PARTS, LANES), lambda bi, i: (bi, i, 0, 0)),
                  pl.BlockSpec((None, tm, d), lambda bi, i: (bi, i, 0)),
                  pl.BlockSpec((None, 1, d), lambda bi, i: (bi, 0, 0)),
                  pl.BlockSpec(s1.shape, lambda bi, i: (0, 0)),
                  pl.BlockSpec(s3.shape, lambda bi, i: (0, 0)),
                  pl.BlockSpec(s2.shape, lambda bi, i: (0, 0))],
        out_specs=pl.BlockSpec((None, tm, d), lambda bi, i: (bi, i, 0)),
        out_shape=jax.ShapeDtypeStruct((b, s, d), F32),
        scratch_shapes=[pltpu.SMEM((tm * 8,), I32),
                        pltpu.SMEM((TAB_WORDS,), I32),
                        pltpu.VMEM((TOP_K, tm, LANES), F32),
                        pltpu.VMEM((TOP_K * tm, ROW_PARTS, LANES), U32),
                        pltpu.VMEM((tm, ROW_PARTS, LANES), F32),
                        pltpu.VMEM((tm, ROW_PARTS, LANES), F32),
                        pltpu.SemaphoreType.DMA,
                        pltpu.SemaphoreType.DMA,
                        pltpu.SemaphoreType.DMA],
        compiler_params=_params("arbitrary", "arbitrary"),
        name="combine",
    )(loc_flat, tab_flat, ys, slab, h2, x, gate, s1, s3, s2)


def _dft_tables(length, n_chan):
    scale = 1.0 / math.sqrt(length * n_chan)
    side = 1
    while side * side < length:
        side *= 2
    outer = length // side
    k = jnp.arange(length, dtype=I32)[:, None]
    a_idx = (k * jnp.arange(outer, dtype=I32)[None, :]) % outer
    b_idx = (k * jnp.arange(side, dtype=I32)[None, :]) % length
    ang_a = a_idx.astype(F32) * (2.0 * math.pi / outer)
    ang_b = b_idx.astype(F32) * (2.0 * math.pi / length)
    ca, sa = jnp.cos(ang_a)[:, :, None], jnp.sin(ang_a)[:, :, None]
    cb, sb = jnp.cos(ang_b)[:, None, :], jnp.sin(ang_b)[:, None, :]
    cos_t = (ca * cb - sa * sb).reshape(length, length)
    sin_t = (sa * cb + ca * sb).reshape(length, length)
    table = (jnp.concatenate([cos_t, -sin_t], axis=1) * scale).astype(BF16)
    return table


def _channel_table(n_chan, n_groups):
    m = jnp.arange(n_chan, dtype=I32)
    ang = ((m[:, None] * m[None, :]) % n_chan).astype(F32) * (2.0 * math.pi / n_chan)
    eye = jnp.eye(n_groups, dtype=F32)
    return jnp.concatenate([jnp.kron(eye, jnp.cos(ang)), jnp.kron(eye, jnp.sin(ang))], axis=1).astype(BF16)


def _rope_tables(n_tok):
    rows = n_tok // GRID_W
    axis_dim = HEAD_DIM // 2
    r = jnp.repeat(jnp.arange(rows, dtype=F32), GRID_W)
    col = jnp.tile(jnp.arange(GRID_W, dtype=F32), rows)
    inv = ROPE_BASE ** (-jnp.arange(0, axis_dim, 2, dtype=F32) / axis_dim)
    ar, ac = r[:, None] * inv, col[:, None] * inv
    cos = jnp.concatenate([jnp.cos(ar), jnp.cos(ar), jnp.cos(ac), jnp.cos(ac)], axis=1)
    sin = jnp.concatenate([-jnp.sin(ar), jnp.sin(ar), -jnp.sin(ac), jnp.sin(ac)], axis=1)
    return jnp.tile(cos, (1, N_KV)), jnp.tile(sin, (1, N_KV))


def _head_mean_matrix(width):
    h = jnp.arange(width) // HEAD_DIM
    return ((h[:, None] == h[None, :]).astype(F32) / HEAD_DIM).astype(BF16)


def _tile_matrices():
    src = jnp.arange(N_KV * HEAD_DIM)
    dst = jnp.arange(Q_PER_KV * HEAD_DIM)
    mats = [((src[:, None] // HEAD_DIM == h) & (src[:, None] % HEAD_DIM == dst[None, :] % HEAD_DIM))
            for h in range(N_KV)]
    return jnp.stack(mats).astype(BF16)


def _even_layer(x, xc, mod, modc, norm_g, w_in, ws, bs, w_out, tables):
    b, s, d = x.shape
    w = w_in.shape[1] // 3
    outs = []
    for stream, m, tm in ((x, mod, MIX_TM), (xc, modc, 256)):
        if stream is None:
            outs.append(None)
            continue
        length = stream.shape[1]
        tm = min(tm, length)
        gm, z = _even_mix(stream, norm_g, m[0], m[1], w_in, ws, bs, tables["chan"], tm)
        table = tables["pos"][length]
        y = _matmul(table, z.reshape(2 * length, b * w),
                    min(1024, length), min(1024, b * w), min(2048, 2 * length), BF16)
        outs.append(_outproj(
            gm, pl.BlockSpec((None, tm, w), lambda bi, i: (bi, i, 0)),
            y, pl.BlockSpec((tm, w), lambda bi, i: (i, bi)),
            w_out, stream, m[2], tm))
    return outs


def _odd_layer(x, xc, mod, modc, norm_g, w_in, qg, kg, sink, conv_w, w_out, tables, ctx_out):
    b, s, d = x.shape
    lc = xc.shape[1]
    half = N_HEADS * HEAD_DIM
    tm, tmc = MIX_TM, min(256, lc)
    prep = functools.partial(_odd_in, g=norm_g, w=w_in, qg=qg, kg=kg, bdq=tables["bdq"], bdk=tables["bdk"],
                             tile=tables["tile"], conv_w=conv_w)
    q, k4, v4, conv = prep(x, shift=mod[0], scale=mod[1], cos=tables["cos"], sin=tables["sin"], tm=tm, with_q=True)
    qc, kc4, vc4, convc = prep(xc, shift=modc[0], scale=modc[1], cos=tables["cos_c"], sin=tables["sin_c"],
                               tm=tmc, with_q=ctx_out)
    att = _attention(q, k4, v4, kc4, vc4, sink, band=True)
    spec = lambda t: pl.BlockSpec((None, t, half), lambda bi, i: (bi, i, 0))
    y = _outproj(att, spec(tm), conv, spec(tm), w_out, x, mod[2], tm)
    yc = None
    if ctx_out:
        attc = _attention(qc, None, None, kc4, vc4, sink, band=False)
        yc = _outproj(attc, spec(tmc), convc, spec(tmc), w_out, xc, modc[2], tmc)
    return y, yc


def _moe(x, xc, mod, modc, norm_g, rw_hi, rw_lo, rb, w1, w3, w2, layer, s1, s3, s2, tri, upper):
    b, s, d = x.shape
    n_lat = b * s
    xc_shape = None
    if xc is not None and xc.shape[1] % MOE_TM:
        xc_shape = xc.shape
        xc = xc.reshape(-1, MOE_TM, d)
        modc = [m[:xc.shape[0]] for m in modc]
    counts0 = jnp.zeros((1, LANES), F32)
    h2, slab, tab, counts = _route(x, norm_g, mod[3], mod[4], rw_hi, rw_lo, rb, counts0, tri, upper)
    slabs, tabs = [slab.reshape(n_lat, LANES)], [tab]
    n_tok = n_lat
    if xc is not None:
        h2c, slabc, tabc, counts = _route(xc, norm_g, modc[3], modc[4], rw_hi, rw_lo, rb, counts, tri, upper)
        slabs.append(slabc.reshape(-1, LANES))
        tabs.append(tabc)
        n_tok += slabs[1].shape[0]
    cnt = counts[0, :N_EXPERTS].astype(I32)
    blk = EXPERT_BLOCK
    padded = (cnt + blk - 1) // blk * blk
    ends = jnp.cumsum(padded).astype(I32)
    starts = ends - padded
    step_rows = blk * EXPERT_GROUP
    n_rows = (n_tok * TOP_K + N_EXPERTS * (blk - 1) + step_rows - 1) // step_rows * step_rows
    n_blocks = n_rows // blk
    n_used = (ends[-1] // blk).reshape(1).astype(I32)
    blk_start = jnp.minimum(jnp.arange(n_blocks, dtype=I32), n_used[0] - 1) * blk
    block_e = jnp.minimum(jnp.sum(blk_start[:, None] >= ends[None, :], axis=1), N_EXPERTS - 1).astype(I32)
    slab_all = jnp.concatenate(slabs, axis=0)
    tab_all = jnp.concatenate(tabs, axis=0)[:, :, :N_EXPERTS].astype(I32)
    runs = jnp.concatenate([tab_all[:, 0], tab_all[:, 1], tab_all[:, 2] + starts[None, :]], axis=1)
    tab_flat = jnp.pad(runs, ((0, 0), (0, TAB_WORDS - runs.shape[1]))).reshape(-1)
    loc_flat = (slab_all[:, SLAB_LOC:SLAB_LOC + 8].astype(I32) * ROW_PARTS).reshape(-1)
    packed = lambda a: a.reshape(-1, ROW_PARTS, LANES)
    xs = _dispatch(packed(h2), loc_flat, tab_flat, ends, n_used, None, n_rows, 0)
    if xc is not None:
        xs = _dispatch(packed(h2c), loc_flat, tab_flat, ends, n_used, xs, n_rows, n_lat // MOE_TM)
    ys = _experts(xs, block_e, n_used, w1, w3, w2, layer)
    x_new = _combine(ys, loc_flat, tab_flat, slab, h2, x, mod[5], s1, s3, s2, 0)
    xc_new = None
    if xc is not None:
        xc_new = _combine(ys, loc_flat, tab_flat, slabc, h2c, xc, modc[5], s1, s3, s2, n_lat // MOE_TM)
        if xc_shape is not None:
            xc_new = xc_new.reshape(xc_shape)
    return x_new, xc_new


def kernel(x, c, ctx, c_ctx, ada_w, ada_b, norm1_g, norm2_g, ev_w_in, ev_w_s, ev_b_s, ev_w_out, od_w_in, od_q_norm_g, od_k_norm_g, od_sink, od_conv_w, od_w_out, router_w, router_b, exp_w_gate, exp_w_up, exp_w_down, sh_w_gate, sh_w_up, sh_w_down):
    b, s, d = x.shape
    lc = ctx.shape[1]
    depth = ada_w.shape[0]
    n_groups = ev_w_s.shape[1]
    half = d // 2

    rows = -(-(b + 1) // 8) * 8
    cond = jnp.zeros((rows, d), F32).at[:b].set(c).at[b].set(c_ctx)
    mod_all = _adaln(cond, ada_w, ada_b)

    tables = {
        "chan": _channel_table(LANES, n_groups),
        "pos": {s: _dft_tables(s, LANES), lc: _dft_tables(lc, LANES)},
        "bdq": _head_mean_matrix(N_HEADS * HEAD_DIM),
        "bdk": _head_mean_matrix(N_KV * HEAD_DIM),
        "tile": _tile_matrices(),
    }
    tables["cos"], tables["sin"] = _rope_tables(s)
    tables["cos_c"] = jnp.ones((lc, N_KV * HEAD_DIM), F32)
    tables["sin_c"] = jnp.zeros((lc, N_KV * HEAD_DIM), F32)
    tri = (jnp.arange(MOE_TM)[:, None] > jnp.arange(MOE_TM)[None, :]).astype(BF16)
    upper = (jnp.arange(LANES)[:, None] < jnp.arange(LANES)[None, :]).astype(BF16)

    qw, kw = N_HEADS * HEAD_DIM, N_KV * HEAD_DIM
    perm = jnp.concatenate([jnp.arange(0, qw), jnp.arange(qw + 2 * kw, qw + 2 * kw + 3 * half),
                            jnp.arange(qw, qw + 2 * kw)])

    w1_all, w3_all, w2_all = exp_w_gate.astype(BF16), exp_w_up.astype(BF16), exp_w_down.astype(BF16)
    xc = ctx
    for l in range(depth):
        last = l == depth - 1
        even = l % 2 == 0
        need_ctx = not (last and even)
        pieces = [mod_all[l, :, j * d:(j + 1) * d] for j in range(6)]
        mod = [p[:b].reshape(b, 1, d) for p in pieces]
        modc = [jnp.broadcast_to(p[b].reshape(1, 1, d), (b, 1, d)) for p in pieces]
        g1 = norm1_g[l].reshape(1, d)
        g2 = norm2_g[l].reshape(1, d)
        if even:
            e = l // 2
            bs = jnp.broadcast_to(ev_b_s[e][:, :, None], (n_groups, CHUNK, LANES))
            y, yc = _even_layer(x, xc if (need_ctx and not last) else None, mod, modc, g1,
                                ev_w_in[e].astype(BF16), ev_w_s[e].astype(BF16), bs,
                                ev_w_out[e].astype(BF16), tables)
        else:
            o = l // 2
            qg = (jnp.tile(od_q_norm_g[o], N_HEADS) * (HEAD_DIM ** -0.5)).reshape(1, qw)
            kg = jnp.tile(od_k_norm_g[o], N_KV).reshape(1, kw)
            y, yc = _odd_layer(x, xc, mod, modc, g1, od_w_in[o][:, perm].astype(BF16), qg, kg,
                               od_sink[o], od_conv_w[o], od_w_out[o].astype(BF16), tables, not last)
        x = y
        if not last:
            xc = yc
        rw = jnp.zeros((d, LANES), F32).at[:, :N_EXPERTS].set(router_w[l])
        rw_hi = rw.astype(BF16)
        rw_lo = (rw - rw_hi.astype(F32)).astype(BF16)
        rb = jnp.zeros((1, LANES), F32).at[0, :N_EXPERTS].set(router_b[l])
        x, xc_new = _moe(x, None if last else xc, mod, modc, g2, rw_hi, rw_lo, rb, w1_all, w3_all, w2_all, l,
                         sh_w_gate[l].astype(BF16), sh_w_up[l].astype(BF16), sh_w_down[l].astype(BF16),
                         tri, upper)
        if not last:
            xc = xc_new
    return x
```

```python
import functools
import math

import jax
import jax.numpy as jnp
from jax import lax
from jax.experimental import pallas as pl
from jax.experimental.pallas import tpu as pltpu

F32 = jnp.float32
BF16 = jnp.bfloat16
I32 = jnp.int32
U32 = jnp.uint32

LANES = 128
VMEM_LIMIT = 48 * 2**20

EPS = 1e-6
GRID_W = 64
CHUNK = 128
HEAD_DIM = 64
N_HEADS = 8
N_KV = 2
Q_PER_KV = N_HEADS // N_KV
MIX_TM = 1024
ATT_BLOCK = 128
ATT_ROWS = 64
ROPE_BASE = 10000.0
N_EXPERTS = 64
TOP_K = 6
ROUTED_SCALE = 2.5
EXPERT_BLOCK = 512
EXPERT_GROUP = 2
MOE_TM = 1024
ROW_PARTS = 4
SLAB_IDX, SLAB_LOC, SLAB_GATE = 0, 8, 16
TAB_WORDS = 1024


def _params(*sem):
    return pltpu.CompilerParams(dimension_semantics=sem, vmem_limit_bytes=VMEM_LIMIT)


def _sigmoid(x):
    return 1.0 / (1.0 + jnp.exp(-x))


def _silu(x):
    return x * _sigmoid(x)


def _gelu_tanh(x):
    c = math.sqrt(2.0 / math.pi)
    return x * (0.5 * (1.0 + jnp.tanh(c * (x + 0.044715 * (x * x * x)))))


def _dot(a, b):
    return jnp.dot(a, b, preferred_element_type=F32)


def _unpack_words(w):
    return pltpu.bitcast(w << 16, F32), pltpu.bitcast(w & jnp.uint32(0xFFFF0000), F32)


def _row_chunks(ref, row0=0, n=None):
    total, parts, lanes = ref.shape
    n = total if n is None else n
    flat = ref.reshape(total * parts, lanes)
    return [flat[pl.ds(row0 * parts + c, n, stride=parts), :] for c in range(parts)]


def _unpack_rows(ref, row0=0, n=None):
    halves = [_unpack_words(w) for w in _row_chunks(ref, row0, n)]
    return jnp.concatenate([h[0] for h in halves] + [h[1] for h in halves], axis=-1)


def _pack_rows(ref, val, row0=0):
    total, parts, lanes = ref.shape
    n, half = val.shape[0], val.shape[1] // 2
    bits = pltpu.bitcast(val.astype(BF16).astype(F32), U32)
    words = (bits[:, :half] >> 16) | (bits[:, half:] & jnp.uint32(0xFFFF0000))
    flat = ref.reshape(total * parts, lanes)
    for c in range(parts):
        flat[pl.ds(row0 * parts + c, n, stride=parts), :] = words[:, c * lanes:(c + 1) * lanes]


def _adaln_kernel(c_ref, w_ref, b_ref, o_ref):
    o_ref[...] = _dot(_silu(c_ref[...]), w_ref[...]) + b_ref[...]


def _adaln(cond, ada_w, ada_b):
    n_layers, d, n6 = ada_w.shape
    rows = cond.shape[0]
    tn = 1536
    return pl.pallas_call(
        _adaln_kernel,
        grid=(n_layers, n6 // tn),
        in_specs=[pl.BlockSpec((rows, d), lambda l, j: (0, 0)),
                  pl.BlockSpec((None, d, tn), lambda l, j: (l, 0, j)),
                  pl.BlockSpec((None, 1, tn), lambda l, j: (l, 0, j))],
        out_specs=pl.BlockSpec((None, rows, tn), lambda l, j: (l, 0, j)),
        out_shape=jax.ShapeDtypeStruct((n_layers, rows, n6), F32),
        compiler_params=_params("parallel", "parallel"),
        name="adaln",
    )(cond, ada_w, ada_b.reshape(n_layers, 1, n6))


def _norm_mod(x, g, shift, scale):
    ms = jnp.mean(x * x, axis=-1, keepdims=True)
    h = (x * lax.rsqrt(ms + EPS)) * g
    return h * (1.0 + scale) + shift


def _even_mix_kernel(x_ref, g_ref, sh_ref, sc_ref, w_ref, ws_ref, bs_ref, cs_ref, gm_ref, z_ref, p_ref,
                     *, n_chunks, n_groups):
    h = _norm_mod(x_ref[...], g_ref[...], sh_ref[...], sc_ref[...])
    p_ref[...] = _dot(h.astype(BF16), w_ref[...])
    half = gm_ref.shape[1]
    for c in range(n_chunks):
        rows = slice(c * CHUNK, (c + 1) * CHUNK)
        for g in range(n_groups):
            cols = slice(g * LANES, (g + 1) * LANES)
            ug = _gelu_tanh(p_ref[rows, cols])
            vg = _gelu_tanh(p_ref[rows, half + g * LANES:half + (g + 1) * LANES])
            mu = jnp.mean(vg, axis=-1, keepdims=True)
            dv = vg - mu
            var = jnp.mean(dv * dv, axis=-1, keepdims=True)
            vn = dv * lax.rsqrt(var + 1e-5)
            fg = _dot(ws_ref[g], vn.astype(BF16)) + bs_ref[g]
            gm_ref[rows, cols] = (ug * fg).astype(gm_ref.dtype)
    fz = _dot(p_ref[:, 2 * half:].astype(BF16), cs_ref[...])
    z_ref[0] = fz[:, :half].astype(z_ref.dtype)
    z_ref[1] = fz[:, half:].astype(z_ref.dtype)


def _even_mix(x, g, shift, scale, w_in, ws, bs, cs, tm):
    b, s, d = x.shape
    n3 = w_in.shape[1]
    w = n3 // 3
    n_groups = w // LANES
    kern = functools.partial(_even_mix_kernel, n_chunks=tm // CHUNK, n_groups=n_groups)
    return pl.pallas_call(
        kern,
        grid=(b, s // tm),
        in_specs=[pl.BlockSpec((None, tm, d), lambda bi, i: (bi, i, 0)),
                  pl.BlockSpec((1, d), lambda bi, i: (0, 0)),
                  pl.BlockSpec((None, 1, d), lambda bi, i: (bi, 0, 0)),
                  pl.BlockSpec((None, 1, d), lambda bi, i: (bi, 0, 0)),
                  pl.BlockSpec((d, n3), lambda bi, i: (0, 0)),
                  pl.BlockSpec(ws.shape, lambda bi, i: (0, 0, 0)),
                  pl.BlockSpec(bs.shape, lambda bi, i: (0, 0, 0)),
                  pl.BlockSpec(cs.shape, lambda bi, i: (0, 0))],
        out_specs=[pl.BlockSpec((None, tm, w), lambda bi, i: (bi, i, 0)),
                   pl.BlockSpec((2, tm, w), lambda bi, i: (0, i, bi))],
        out_shape=[jax.ShapeDtypeStruct((b, s, w), BF16),
                   jax.ShapeDtypeStruct((2, s, b * w), BF16)],
        scratch_shapes=[pltpu.VMEM((tm, n3), F32)],
        compiler_params=_params("parallel", "parallel"),
        name="even_mix",
    )(x, g, shift, scale, w_in, ws, bs, cs)


def _mm_kernel(a_ref, b_ref, o_ref, acc_ref):
    k = pl.program_id(2)

    @pl.when(k == 0)
    def _():
        acc_ref[...] = jnp.zeros_like(acc_ref)

    acc_ref[...] += _dot(a_ref[...], b_ref[...])

    @pl.when(k == pl.num_programs(2) - 1)
    def _():
        o_ref[...] = acc_ref[...].astype(o_ref.dtype)


def _matmul(a, b, tm, tn, tk, out_dtype):
    m, kd = a.shape
    n = b.shape[1]
    return pl.pallas_call(
        _mm_kernel,
        grid=(m // tm, n // tn, kd // tk),
        in_specs=[pl.BlockSpec((tm, tk), lambda i, j, k: (i, k)),
                  pl.BlockSpec((tk, tn), lambda i, j, k: (k, j))],
        out_specs=pl.BlockSpec((tm, tn), lambda i, j, k: (i, j)),
        out_shape=jax.ShapeDtypeStruct((m, n), out_dtype),
        scratch_shapes=[pltpu.VMEM((tm, tn), F32)],
        compiler_params=_params("parallel", "parallel", "arbitrary"),
        name="dft_matmul",
    )(a, b)


def _outproj_kernel(a_ref, b_ref, w_ref, x_ref, gate_ref, o_ref):
    ab = jnp.concatenate([a_ref[...], b_ref[...]], axis=-1)
    o_ref[...] = x_ref[...] + gate_ref[...] * _dot(ab, w_ref[...])


def _outproj(a, a_spec, bsrc, b_spec, w, x, gate, tm):
    b, s, d = x.shape
    return pl.pallas_call(
        _outproj_kernel,
        grid=(b, s // tm),
        in_specs=[a_spec, b_spec,
                  pl.BlockSpec(w.shape, lambda bi, i: (0, 0)),
                  pl.BlockSpec((None, tm, d), lambda bi, i: (bi, i, 0)),
                  pl.BlockSpec((None, 1, d), lambda bi, i: (bi, 0, 0))],
        out_specs=pl.BlockSpec((None, tm, d), lambda bi, i: (bi, i, 0)),
        out_shape=jax.ShapeDtypeStruct((b, s, d), F32),
        compiler_params=_params("parallel", "parallel"),
        name="outproj",
    )(a, bsrc, w, x, gate)


def _head_rms(x, bd_ref):
    xx = x * x
    hi = xx.astype(BF16)
    lo = (xx - hi.astype(F32)).astype(BF16)
    ms = _dot(hi, bd_ref[...]) + _dot(lo, bd_ref[...])
    return x * lax.rsqrt(ms + EPS)


def _rope(x, cos, sins):
    width = x.shape[1]
    lane = lax.broadcasted_iota(I32, x.shape, 1)
    first = (lane & 31) < 16
    swapped = jnp.where(first, pltpu.roll(x, width - 16, 1), pltpu.roll(x, 16, 1))
    return x * cos + swapped * sins


HALO = 8


def _odd_in_kernel(x_ref, xp_ref, xn_ref, g_ref, sh_ref, sc_ref, w_ref, cos_ref, sin_ref, qg_ref, kg_ref,
                   bdq_ref, bdk_ref, tile_ref, tile_t_ref, cw_ref, qo_ref, k4_ref, v4_ref, co_ref, p_ref, *, with_q):
    i = pl.program_id(1)
    n_tiles = pl.num_programs(1)
    tm = x_ref.shape[0]
    qw, dc, kw = qo_ref.shape[1], co_ref.shape[1], kg_ref.shape[1]
    g, sh, sc = g_ref[...], sh_ref[...], sc_ref[...]
    p_ref[...] = _dot(_norm_mod(x_ref[...], g, sh, sc).astype(BF16), w_ref[...])
    cos, sin = cos_ref[...], sin_ref[...]
    if with_q:
        reps = qw // kw
        qn = _head_rms(p_ref[:, :qw], bdq_ref) * qg_ref[...]
        qo_ref[...] = _rope(qn, jnp.concatenate([cos] * reps, axis=1),
                            jnp.concatenate([sin] * reps, axis=1)).astype(qo_ref.dtype)
    else:
        qo_ref[...] = jnp.zeros_like(qo_ref)
    k0 = qw + 3 * dc
    kn = _head_rms(p_ref[:, k0:k0 + kw], bdk_ref) * kg_ref[...]
    kr = _rope(kn, cos, sin).astype(BF16)
    vb = p_ref[:, k0 + kw:k0 + 2 * kw].astype(BF16)
    for h in range(N_KV):
        k4_ref[h] = lax.dot_general(tile_t_ref[h], kr, (((1,), (1,)), ((), ())),
                                    preferred_element_type=F32).astype(k4_ref.dtype)
        v4_ref[h] = _dot(vb, tile_ref[h]).astype(v4_ref.dtype)
    halo = _norm_mod(jnp.concatenate([xp_ref[...], xn_ref[...]], axis=0), g, sh, sc).astype(BF16)
    gz = _dot(halo, w_ref[:, qw:qw + dc]) * _dot(halo, w_ref[:, qw + 2 * dc:qw + 3 * dc])
    before = jnp.where(i > 0, gz[HALO - 1:HALO, :], 0.0)
    after = jnp.where(i < n_tiles - 1, gz[HALO:HALO + 1, :], 0.0)
    ridx = lax.broadcasted_iota(I32, (tm, LANES), 0)
    for c in range(dc // LANES):
        cols = slice(c * LANES, (c + 1) * LANES)
        zc = p_ref[:, qw + c * LANES:qw + (c + 1) * LANES] * p_ref[:, qw + 2 * dc + c * LANES:qw + 2 * dc + (c + 1) * LANES]
        zp = jnp.where(ridx == 0, before[:, cols], pltpu.roll(zc, 1, 0))
        zn = jnp.where(ridx == tm - 1, after[:, cols], pltpu.roll(zc, tm - 1, 0))
        y = zp * cw_ref[0:1, cols] + zc * cw_ref[1:2, cols] + zn * cw_ref[2:3, cols]
        co_ref[:, cols] = (p_ref[:, qw + dc + c * LANES:qw + dc + (c + 1) * LANES] * y).astype(co_ref.dtype)


def _odd_in(x, g, shift, scale, w, cos, sin, qg, kg, bdq, bdk, tile, conv_w, tm, with_q):
    b, s, d = x.shape
    n = w.shape[1]
    tile_t = jnp.swapaxes(tile, 1, 2)
    qw = N_HEADS * HEAD_DIM
    kw = N_KV * HEAD_DIM
    rep = Q_PER_KV * HEAD_DIM
    dc = conv_w.shape[1]
    per_tile = tm // HALO
    kern = functools.partial(_odd_in_kernel, with_q=with_q)
    return pl.pallas_call(
        kern,
        grid=(b, s // tm),
        in_specs=[pl.BlockSpec((None, tm, d), lambda bi, i: (bi, i, 0)),
                  pl.BlockSpec((None, HALO, d), lambda bi, i: (bi, jnp.maximum(i * per_tile - 1, 0), 0)),
                  pl.BlockSpec((None, HALO, d), lambda bi, i: (bi, jnp.minimum((i + 1) * per_tile, s // HALO - 1), 0)),
                  pl.BlockSpec((1, d), lambda bi, i: (0, 0)),
                  pl.BlockSpec((None, 1, d), lambda bi, i: (bi, 0, 0)),
                  pl.BlockSpec((None, 1, d), lambda bi, i: (bi, 0, 0)),
                  pl.BlockSpec((d, n), lambda bi, i: (0, 0)),
                  pl.BlockSpec((tm, kw), lambda bi, i: (i, 0)),
                  pl.BlockSpec((tm, kw), lambda bi, i: (i, 0)),
                  pl.BlockSpec((1, qw), lambda bi, i: (0, 0)),
                  pl.BlockSpec((1, kw), lambda bi, i: (0, 0)),
                  pl.BlockSpec(bdq.shape, lambda bi, i: (0, 0)),
                  pl.BlockSpec(bdk.shape, lambda bi, i: (0, 0)),
                  pl.BlockSpec(tile.shape, lambda bi, i: (0, 0, 0)),
                  pl.BlockSpec(tile_t.shape, lambda bi, i: (0, 0, 0)),
                  pl.BlockSpec(conv_w.shape, lambda bi, i: (0, 0))],
        out_specs=[pl.BlockSpec((None, tm, qw), lambda bi, i: (bi, i, 0)),
                   pl.BlockSpec((None, N_KV, rep, tm), lambda bi, i: (bi, 0, 0, i)),
                   pl.BlockSpec((None, N_KV, tm, rep), lambda bi, i: (bi, 0, i, 0)),
                   pl.BlockSpec((None, tm, dc), lambda bi, i: (bi, i, 0))],
        out_shape=[jax.ShapeDtypeStruct((b, s, qw), BF16),
                   jax.ShapeDtypeStruct((b, N_KV, rep, s), BF16),
                   jax.ShapeDtypeStruct((b, N_KV, s, rep), BF16),
                   jax.ShapeDtypeStruct((b, s, dc), BF16)],
        scratch_shapes=[pltpu.VMEM((tm, n), F32)],
        compiler_params=_params("parallel", "parallel"),
        name="odd_in",
    )(x, x, x, g, shift, scale, w, cos, sin, qg, kg, bdq, bdk, tile, tile_t, conv_w)


def _attn_kernel(sink_ref, q_ref, *refs, band, n_blocks):
    o_ref = refs[-1]
    i = pl.program_id(1)
    t = q_ref.shape[0]
    w = Q_PER_KV * HEAD_DIM
    lane = lax.broadcasted_iota(I32, (t, w), 1)
    masks = [(lane >= g * HEAD_DIM) & (lane < (g + 1) * HEAD_DIM) for g in range(Q_PER_KV)]
    bad = None
    if band:
        n_keys = 3 * t + refs[7].shape[1]
        row = lax.broadcasted_iota(I32, (Q_PER_KV * t, n_keys), 0) & (t - 1)
        col = lax.broadcasted_iota(I32, (Q_PER_KV * t, n_keys), 1)
        off_prev = jnp.where(i > 0, 0, 4 * t)
        off_next = jnp.where(i < n_blocks - 1, 0, 4 * t)
        bad_prev = (col < t) & (col < row + off_prev)
        bad_next = (col >= 2 * t) & (col < 3 * t) & (col - 2 * t > row - off_next)
        bad = bad_prev | bad_next
    for h in range(N_KV):
        if band:
            kp, kc_, kn, vp, vc_, vn, kx, vx = refs[:-1]
            kcat = jnp.concatenate([kp[h], kc_[h], kn[h], kx[h]], axis=1)
            vcat = jnp.concatenate([vp[h], vc_[h], vn[h], vx[h]], axis=0)
        else:
            kx, vx = refs[:-1]
            kcat, vcat = kx[h], vx[h]
        q = q_ref[:, h * w:(h + 1) * w]
        q4 = jnp.concatenate([jnp.where(m, q, jnp.zeros_like(q)) for m in masks], axis=0)
        s = _dot(q4, kcat)
        es, invs = [], []
        for c in range(Q_PER_KV * t // ATT_ROWS):
            rows = slice(c * ATT_ROWS, (c + 1) * ATT_ROWS)
            sc = s[rows]
            if band:
                sc = jnp.where(bad[rows], -jnp.inf, sc)
            sink = jnp.full((ATT_ROWS, 1), sink_ref[h * Q_PER_KV + (c * ATT_ROWS) // t], F32)
            m = jnp.maximum(jnp.max(sc, axis=-1, keepdims=True), sink)
            ec = jnp.exp(sc - m)
            invs.append(1.0 / (jnp.sum(ec, axis=-1, keepdims=True) + jnp.exp(sink - m)))
            es.append(ec.astype(BF16))
        e = jnp.concatenate(es, axis=0)
        r = _dot(e, vcat) * jnp.concatenate(invs, axis=0)
        o = jnp.zeros((t, w), F32)
        for g in range(Q_PER_KV):
            o = o + jnp.where(masks[g], r[g * t:(g + 1) * t, :], 0.0)
        o_ref[:, h * w:(h + 1) * w] = o.astype(o_ref.dtype)


def _attention(q, k4, v4, kx4, vx4, sink, band):
    b, s, qw = q.shape
    rep = vx4.shape[-1]
    lc = vx4.shape[2]
    t = ATT_BLOCK
    nb = s // t
    kern = functools.partial(_attn_kernel, band=band, n_blocks=nb)

    def k_spec(off):
        return pl.BlockSpec((None, N_KV, rep, t),
                            lambda bi, i: (bi, 0, 0, jnp.clip(i + off, 0, nb - 1)))

    def v_spec(off):
        return pl.BlockSpec((None, N_KV, t, rep),
                            lambda bi, i: (bi, 0, jnp.clip(i + off, 0, nb - 1), 0))

    in_specs = [pl.BlockSpec(memory_space=pltpu.SMEM),
                pl.BlockSpec((None, t, qw), lambda bi, i: (bi, i, 0))]
    args = [sink, q]
    if band:
        in_specs += [k_spec(-1), k_spec(0), k_spec(1), v_spec(-1), v_spec(0), v_spec(1)]
        args += [k4, k4, k4, v4, v4, v4]
    in_specs += [pl.BlockSpec((None, N_KV, rep, lc), lambda bi, i: (bi, 0, 0, 0)),
                 pl.BlockSpec((None, N_KV, lc, rep), lambda bi, i: (bi, 0, 0, 0))]
    args += [kx4, vx4]
    return pl.pallas_call(
        kern,
        grid=(b, nb),
        in_specs=in_specs,
        out_specs=pl.BlockSpec((None, t, qw), lambda bi, i: (bi, i, 0)),
        out_shape=jax.ShapeDtypeStruct((b, s, qw), BF16),
        compiler_params=_params("parallel", "parallel"),
        name="attention_band" if band else "attention_ctx",
    )(*args)


def _route_kernel(x_ref, g_ref, sh_ref, sc_ref, whi_ref, wlo_ref, rb_ref, cin_ref, tri_ref, upper_ref,
                  h_ref, slab_ref, tab_ref, cnt_ref, carry_ref):
    first = (pl.program_id(0) == 0) & (pl.program_id(1) == 0)

    @pl.when(first)
    def _():
        carry_ref[...] = cin_ref[...]

    h = _norm_mod(x_ref[...], g_ref[...], sh_ref[...], sc_ref[...])
    _pack_rows(h_ref, h)
    hi = h.astype(BF16)
    lo = (h - hi.astype(F32)).astype(BF16)
    logits = _dot(hi, whi_ref[...]) + _dot(lo, whi_ref[...]) + _dot(hi, wlo_ref[...])
    scores = _sigmoid(logits)
    tm, lanes = scores.shape
    lane = lax.broadcasted_iota(I32, (tm, lanes), 1).astype(F32)
    work = jnp.where(lane < N_EXPERTS, scores + rb_ref[...], -jnp.inf)
    hits, idxs, gates = [], [], []
    for _ in range(TOP_K):
        mx = jnp.max(work, axis=-1, keepdims=True)
        idx = jnp.min(jnp.where(work == mx, lane, float(lanes)), axis=-1, keepdims=True)
        hit = lane == idx
        hits.append(hit)
        idxs.append(idx)
        gates.append(jnp.sum(jnp.where(hit, scores, 0.0), axis=-1, keepdims=True))
        work = jnp.where(hit, -jnp.inf, work)
    gsum = gates[0]
    for gk in gates[1:]:
        gsum = gsum + gk
    gscale = ROUTED_SCALE / (gsum + 1e-20)
    onehot = jnp.zeros((tm, lanes), F32)
    for hit in hits:
        onehot = jnp.where(hit, 1.0, onehot)
    cnt = jnp.sum(onehot, axis=0, keepdims=True)
    cnt_hi = jnp.floor(cnt * (1.0 / 256.0))
    cnt_lo = cnt - 256.0 * cnt_hi
    parts = jnp.concatenate([jnp.broadcast_to(cnt_hi, (8, lanes)), jnp.broadcast_to(cnt_lo, (8, lanes))], axis=0)
    sums = _dot(parts.astype(BF16), upper_ref[...])
    seg = 256.0 * sums[0:1] + sums[8:9]
    before = _dot(tri_ref[...], onehot.astype(BF16)) + seg
    slab = jnp.zeros((tm, lanes), F32)
    for k in range(TOP_K):
        loc = jnp.sum(jnp.where(hits[k], before, 0.0), axis=-1, keepdims=True)
        slab = jnp.where(lane == SLAB_IDX + k, idxs[k], slab)
        slab = jnp.where(lane == SLAB_LOC + k, loc, slab)
        slab = jnp.where(lane == SLAB_GATE + k, gates[k] * gscale, slab)
    slab_ref[...] = slab
    row = lax.broadcasted_iota(I32, (8, lanes), 0)
    tab_ref[...] = jnp.where(row == 0, cnt, jnp.where(row == 1, seg, jnp.where(row == 2, carry_ref[...], 0.0)))
    carry_ref[...] = carry_ref[...] + cnt
    cnt_ref[...] = carry_ref[...]


def _route(x, g, shift, scale, whi, wlo, rb, counts_in, tri, upper):
    b, s, d = x.shape
    tm = MOE_TM
    nt = s // tm
    return pl.pallas_call(
        _route_kernel,
        grid=(b, nt),
        in_specs=[pl.BlockSpec((None, tm, d), lambda bi, i: (bi, i, 0)),
                  pl.BlockSpec((1, d), lambda bi, i: (0, 0)),
                  pl.BlockSpec((None, 1, d), lambda bi, i: (bi, 0, 0)),
                  pl.BlockSpec((None, 1, d), lambda bi, i: (bi, 0, 0)),
                  pl.BlockSpec(whi.shape, lambda bi, i: (0, 0)),
                  pl.BlockSpec(wlo.shape, lambda bi, i: (0, 0)),
                  pl.BlockSpec((1, LANES), lambda bi, i: (0, 0)),
                  pl.BlockSpec((1, LANES), lambda bi, i: (0, 0)),
                  pl.BlockSpec((tm, tm), lambda bi, i: (0, 0)),
                  pl.BlockSpec((LANES, LANES), lambda bi, i: (0, 0))],
        out_specs=[pl.BlockSpec((None, tm, ROW_PARTS, LANES), lambda bi, i: (bi, i, 0, 0)),
                   pl.BlockSpec((None, tm, LANES), lambda bi, i: (bi, i, 0)),
                   pl.BlockSpec((None, 8, LANES), lambda bi, i: (bi * nt + i, 0, 0)),
                   pl.BlockSpec((1, LANES), lambda bi, i: (0, 0))],
        out_shape=[jax.ShapeDtypeStruct((b, s, ROW_PARTS, LANES), U32),
                   jax.ShapeDtypeStruct((b, s, LANES), F32),
                   jax.ShapeDtypeStruct((b * nt, 8, LANES), F32),
                   jax.ShapeDtypeStruct((1, LANES), F32)],
        scratch_shapes=[pltpu.VMEM((1, LANES), F32)],
        compiler_params=_params("arbitrary", "arbitrary"),
        name="route",
    )(x, g, shift, scale, whi, wlo, rb, counts_in, tri, upper)


def _rows(ref, row, n):
    return ref.at[pl.ds(row, n)]


def _run_copies(tab_smem, base, stage_ref, far_ref, sem, to_far):
    n_bits = MOE_TM.bit_length()
    common = n_bits - 3

    def per_expert(e, carry):
        n = tab_smem[base + e]
        near = tab_smem[base + N_EXPERTS + e]
        far = tab_smem[base + 2 * N_EXPERTS + e]

        def piece(bit):
            size = 1 << bit

            @pl.when((n & size) != 0)
            def _():
                done = n & (size - 1)
                a, b = _rows(stage_ref, near + done, size), _rows(far_ref, far + done, size)
                (pltpu.make_async_copy(a, b, sem) if to_far else pltpu.make_async_copy(b, a, sem)).start()

        for bit in range(common):
            piece(bit)

        @pl.when(n >= (1 << common))
        def _():
            for bit in range(common, n_bits):
                piece(bit)
        return carry

    lax.fori_loop(0, N_EXPERTS, per_expert, 0)


def _loc_copies(loc_ref, tile, loc_smems, sem):
    return [pltpu.make_async_copy(loc_ref.at[pl.ds((tile * 8 + k) * MOE_TM, MOE_TM)], loc_smems[k], sem)
            for k in range(TOP_K)]


def _wait_tile(stage_ref, far_ref, sem, to_far):
    n = TOP_K * MOE_TM
    a, b = _rows(stage_ref, 0, n), _rows(far_ref, 0, n)
    (pltpu.make_async_copy(a, b, sem) if to_far else pltpu.make_async_copy(b, a, sem)).wait()


def _dispatch_kernel(ends_ref, nu_ref, h_ref, loc_ref, tab_ref, *rest, zero_fill, tile_base, n_blocks):
    if zero_fill:
        xs_ref, rest = rest[0], rest[1:]
    else:
        xs_ref, rest = rest[1], rest[2:]
    loc_smems, rest = rest[:TOP_K], rest[TOP_K:]
    if zero_fill:
        tab_smem, stages_ref, zero_ref, sems, psem = rest
    else:
        tab_smem, stages_ref, sems, psem = rest
        zero_ref = None
    tm = MOE_TM
    i = pl.program_id(0)
    n_steps = pl.num_programs(0)
    slot = i % 2
    stage_ref, sem = stages_ref.at[slot], sems.at[slot]

    def table_copies(j):
        return _loc_copies(loc_ref, tile_base + j, loc_smems, psem) + [
            pltpu.make_async_copy(tab_ref.at[pl.ds((tile_base + j) * TAB_WORDS, TAB_WORDS)], tab_smem, psem)]

    @pl.when(i == 0)
    def _():
        for cp in table_copies(0):
            cp.start()

    if zero_fill:
        @pl.when(i == 0)
        def _():
            zero_ref[...] = jnp.zeros_like(zero_ref)

            def block_copy(blk):
                return pltpu.make_async_copy(zero_ref, _rows(xs_ref, blk * EXPERT_BLOCK, EXPERT_BLOCK), sem)

            def fill(e, carry, *, start):
                end = ends_ref[e]
                prev = jnp.where(e > 0, ends_ref[jnp.maximum(e - 1, 0)], 0)

                @pl.when(end > prev)
                def _():
                    cp = block_copy(end // EXPERT_BLOCK - 1)
                    cp.start() if start else cp.wait()
                return carry

            def tail(j, carry, *, start):
                cp = block_copy(j)
                cp.start() if start else cp.wait()
                return carry

            lax.fori_loop(0, N_EXPERTS, functools.partial(fill, start=True), 0)
            lax.fori_loop(nu_ref[0], n_blocks, functools.partial(tail, start=True), 0)
            lax.fori_loop(0, N_EXPERTS, functools.partial(fill, start=False), 0)
            lax.fori_loop(nu_ref[0], n_blocks, functools.partial(tail, start=False), 0)

    for cp in table_copies(i):
        cp.wait()

    @pl.when(i >= 2)
    def _():
        _wait_tile(stage_ref, xs_ref, sem, to_far=True)

    flat = stage_ref.reshape(TOP_K * tm * ROW_PARTS, LANES)
    h_flat = h_ref.reshape(tm * ROW_PARTS, LANES)

    def place(t, c):
        row = h_flat[pl.ds(pl.multiple_of(t * ROW_PARTS, ROW_PARTS), ROW_PARTS), :]
        for k in range(TOP_K):
            flat[pl.ds(pl.multiple_of(loc_smems[k][t], ROW_PARTS), ROW_PARTS), :] = row
        return c

    lax.fori_loop(0, tm, place, 0, unroll=4)
    _run_copies(tab_smem, 0, stage_ref, xs_ref, sem, to_far=True)

    @pl.when(i + 1 < n_steps)
    def _():
        for cp in table_copies(i + 1):
            cp.start()

    @pl.when(i == n_steps - 1)
    def _():
        _wait_tile(stage_ref, xs_ref, sem, to_far=True)

        @pl.when(i >= 1)
        def _():
            _wait_tile(stages_ref.at[1 - slot], xs_ref, sems.at[1 - slot], to_far=True)


def _dispatch(h2, loc_flat, tab_flat, ends, n_used, xs_prev, n_rows, tile_base):
    n = h2.shape[0]
    tm = MOE_TM
    zero_fill = xs_prev is None
    kern = functools.partial(_dispatch_kernel, zero_fill=zero_fill, tile_base=tile_base,
                             n_blocks=n_rows // EXPERT_BLOCK)
    in_specs = [pl.BlockSpec((tm, ROW_PARTS, LANES), lambda i, e, nu: (i, 0, 0)),
                pl.BlockSpec(memory_space=pl.ANY),
                pl.BlockSpec(memory_space=pl.ANY)]
    args = [ends, n_used, h2, loc_flat, tab_flat]
    scratch = [pltpu.SMEM((tm,), I32)] * TOP_K + [pltpu.SMEM((TAB_WORDS,), I32),
               pltpu.VMEM((2, TOP_K * tm, ROW_PARTS, LANES), U32)]
    aliases = {}
    if zero_fill:
        scratch.append(pltpu.VMEM((EXPERT_BLOCK, ROW_PARTS, LANES), U32))
    else:
        in_specs.append(pl.BlockSpec(memory_space=pl.ANY))
        args.append(xs_prev)
        aliases = {5: 0}
    scratch += [pltpu.SemaphoreType.DMA((2,)), pltpu.SemaphoreType.DMA]
    return pl.pallas_call(
        kern,
        grid_spec=pltpu.PrefetchScalarGridSpec(
            num_scalar_prefetch=2,
            grid=(n // tm,),
            in_specs=in_specs,
            out_specs=pl.BlockSpec(memory_space=pl.ANY),
            scratch_shapes=scratch),
        out_shape=jax.ShapeDtypeStruct((n_rows, ROW_PARTS, LANES), U32),
        input_output_aliases=aliases,
        compiler_params=_params("arbitrary"),
        name="dispatch",
    )(*args)


def _expert_kernel(be_ref, nu_ref, x_ref, *refs):
    o_ref = refs[-1]
    blk = EXPERT_BLOCK
    used = pl.program_id(0) * EXPERT_GROUP < nu_ref[0]

    @pl.when(used)
    def _():
        for j in range(EXPERT_GROUP):
            w1_ref, w3_ref, w2_ref = refs[3 * j:3 * j + 3]
            x = _unpack_rows(x_ref, j * blk, blk).astype(BF16)
            a = _silu(_dot(x, w1_ref[...])) * _dot(x, w3_ref[...])
            _pack_rows(o_ref, _dot(a.astype(BF16), w2_ref[...]), j * blk)

    @pl.when(jnp.logical_not(used))
    def _():
        o_ref[...] = jnp.zeros_like(o_ref)


def _experts(xs, block_e, n_used, w1, w3, w2, layer):
    d, de = w1.shape[2:]
    rows = EXPERT_BLOCK * EXPERT_GROUP
    n_steps = xs.shape[0] // rows

    def row_map(i, be, nu):
        return (jnp.minimum(i, (nu[0] + EXPERT_GROUP - 1) // EXPERT_GROUP - 1), 0, 0)

    w_specs = []
    for j in range(EXPERT_GROUP):
        w_map = lambda i, be, nu, j=j: (layer, be[i * EXPERT_GROUP + j], 0, 0)
        w_specs += [pl.BlockSpec((None, None, d, de), w_map),
                    pl.BlockSpec((None, None, d, de), w_map),
                    pl.BlockSpec((None, None, de, d), w_map)]

    return pl.pallas_call(
        _expert_kernel,
        grid_spec=pltpu.PrefetchScalarGridSpec(
            num_scalar_prefetch=2,
            grid=(n_steps,),
            in_specs=[pl.BlockSpec((rows, ROW_PARTS, LANES), row_map)] + w_specs,
            out_specs=pl.BlockSpec((rows, ROW_PARTS, LANES), lambda i, be, nu: (i, 0, 0))),
        out_shape=jax.ShapeDtypeStruct(xs.shape, U32),
        compiler_params=_params("arbitrary"),
        name="experts",
    )(block_e, n_used, xs, *([w1, w3, w2] * EXPERT_GROUP))


def _combine_kernel(loc_ref, tab_ref, ys_ref, slab_ref, h_ref, x_ref, gate_ref, s1_ref, s3_ref, s2_ref,
                    o_ref, *scratch, tile_base):
    loc_smems = scratch[:TOP_K]
    tab_smem, gk_ref, stage_ref, lo_ref, hi_ref, sem, psem, tsem = scratch[TOP_K:]
    tm = MOE_TM
    step = pl.program_id(0) * pl.num_programs(1) + pl.program_id(1)
    n_steps = pl.num_programs(0) * pl.num_programs(1)

    def loc_copies(j):
        return _loc_copies(loc_ref, tile_base + j, loc_smems, psem)

    def tab_copy(j):
        return pltpu.make_async_copy(tab_ref.at[pl.ds((tile_base + j) * TAB_WORDS, TAB_WORDS)], tab_smem, tsem)

    def gather(j):
        tab_copy(j).wait()
        _run_copies(tab_smem, 0, stage_ref, ys_ref, sem, to_far=False)
        for cp in loc_copies(j):
            cp.start()

    @pl.when(step == 0)
    def _():
        tab_copy(0).start()
        gather(0)

    hb = _unpack_rows(h_ref).astype(BF16)
    shared = _dot((_silu(_dot(hb, s1_ref[...])) * _dot(hb, s3_ref[...])).astype(BF16), s2_ref[...])
    slab = slab_ref[...]
    for k in range(TOP_K):
        gk_ref[k] = jnp.broadcast_to(slab[:, SLAB_GATE + k:SLAB_GATE + k + 1], (tm, LANES))
    _wait_tile(stage_ref, ys_ref, sem, to_far=False)

    @pl.when(step + 1 < n_steps)
    def _():
        tab_copy(step + 1).start()
    for cp in loc_copies(step):
        cp.wait()

    flat = stage_ref.reshape(TOP_K * tm * ROW_PARTS, LANES)

    def tree_sum(terms):
        while len(terms) > 1:
            terms = [a + b for a, b in zip(terms[::2], terms[1::2])] + ([terms[-1]] if len(terms) % 2 else [])
        return terms[0]

    def mix(t, c):
        los, his = [], []
        for k in range(TOP_K):
            g = gk_ref[k, pl.ds(t, ROW_PARTS, stride=0), :]
            at = pl.multiple_of(loc_smems[k][t], ROW_PARTS)
            wl, wh = _unpack_words(flat[pl.ds(at, ROW_PARTS), :])
            los.append(g * wl)
            his.append(g * wh)
        lo_ref[t] = tree_sum(los)
        hi_ref[t] = tree_sum(his)
        return c

    lax.fori_loop(0, tm, mix, 0, unroll=8)

    @pl.when(step + 1 < n_steps)
    def _():
        gather(step + 1)

    routed =jnp.concatenate(_row_chunks(lo_ref) + _row_chunks(hi_ref), axis=-1)
    o_ref[...] = x_ref[...] + gate_ref[...] * (routed + shared)


def _combine(ys, loc_flat, tab_flat, slab, h2, x, gate, s1, s3, s2, tile_base):
    b, s, d = x.shape
    tm = MOE_TM
    kern = functools.partial(_combine_kernel, tile_base=tile_base)
    return pl.pallas_call(
        kern,
        grid=(b, s // tm),
        in_specs=[pl.BlockSpec(memory_space=pl.ANY),
                  pl.BlockSpec(memory_space=pl.ANY),
                  pl.BlockSpec(memory_space=pl.ANY),
                  pl.BlockSpec((None, tm, LANES), lambda bi, i: (bi, i, 0)),
                  pl.BlockSpec((None, tm, ROW_PARTS, LANES), lambda bi, i: (bi, i, 0, 0)),
                  pl.BlockSpec((None, tm, d), lambda bi, i: (bi, i, 0)),
                  pl.BlockSpec((None, 1, d), lambda bi, i: (bi, 0, 0)),
                  pl.BlockSpec(s1.shape, lambda bi, i: (0, 0)),
                  pl.BlockSpec(s3.shape, lambda bi, i: (0, 0)),
                  pl.BlockSpec(s2.shape, lambda bi, i: (0, 0))],
        out_specs=pl.BlockSpec((None, tm, d), lambda bi, i: (bi, i, 0)),
        out_shape=jax.ShapeDtypeStruct((b, s, d), F32),
        scratch_shapes=[pltpu.SMEM((tm,), I32)] * TOP_K + [
                        pltpu.SMEM((TAB_WORDS,), I32),
                        pltpu.VMEM((TOP_K, tm, LANES), F32),
                        pltpu.VMEM((TOP_K * tm, ROW_PARTS, LANES), U32),
                        pltpu.VMEM((tm, ROW_PARTS, LANES), F32),
                        pltpu.VMEM((tm, ROW_PARTS, LANES), F32),
                        pltpu.SemaphoreType.DMA,
                        pltpu.SemaphoreType.DMA,
                        pltpu.SemaphoreType.DMA],
        compiler_params=_params("arbitrary", "arbitrary"),
        name="combine",
    )(loc_flat, tab_flat, ys, slab, h2, x, gate, s1, s3, s2)


def _dft_tables(length, n_chan):
    scale = 1.0 / math.sqrt(length * n_chan)
    side = 1
    while side * side < length:
        side *= 2
    outer = length // side
    k = jnp.arange(length, dtype=I32)[:, None]
    a_idx = (k * jnp.arange(outer, dtype=I32)[None, :]) % outer
    b_idx = (k * jnp.arange(side, dtype=I32)[None, :]) % length
    ang_a = a_idx.astype(F32) * (2.0 * math.pi / outer)
    ang_b = b_idx.astype(F32) * (2.0 * math.pi / length)
    ca, sa = jnp.cos(ang_a)[:, :, None], jnp.sin(ang_a)[:, :, None]
    cb, sb = jnp.cos(ang_b)[:, None, :], jnp.sin(ang_b)[:, None, :]
    cos_t = (ca * cb - sa * sb).reshape(length, length)
    sin_t = (sa * cb + ca * sb).reshape(length, length)
    table = (jnp.concatenate([cos_t, -sin_t], axis=1) * scale).astype(BF16)
    return table


def _channel_table(n_chan, n_groups):
    m = jnp.arange(n_chan, dtype=I32)
    ang = ((m[:, None] * m[None, :]) % n_chan).astype(F32) * (2.0 * math.pi / n_chan)
    eye = jnp.eye(n_groups, dtype=F32)
    return jnp.concatenate([jnp.kron(eye, jnp.cos(ang)), jnp.kron(eye, jnp.sin(ang))], axis=1).astype(BF16)


def _rope_tables(n_tok):
    rows = n_tok // GRID_W
    axis_dim = HEAD_DIM // 2
    r = jnp.repeat(jnp.arange(rows, dtype=F32), GRID_W)
    col = jnp.tile(jnp.arange(GRID_W, dtype=F32), rows)
    inv = ROPE_BASE ** (-jnp.arange(0, axis_dim, 2, dtype=F32) / axis_dim)
    ar, ac = r[:, None] * inv, col[:, None] * inv
    cos = jnp.concatenate([jnp.cos(ar), jnp.cos(ar), jnp.cos(ac), jnp.cos(ac)], axis=1)
    sin = jnp.concatenate([-jnp.sin(ar), jnp.sin(ar), -jnp.sin(ac), jnp.sin(ac)], axis=1)
    return jnp.tile(cos, (1, N_KV)), jnp.tile(sin, (1, N_KV))


def _head_mean_matrix(width):
    h = jnp.arange(width) // HEAD_DIM
    return ((h[:, None] == h[None, :]).astype(F32) / HEAD_DIM).astype(BF16)


def _tile_matrices():
    src = jnp.arange(N_KV * HEAD_DIM)
    dst = jnp.arange(Q_PER_KV * HEAD_DIM)
    mats = [((src[:, None] // HEAD_DIM == h) & (src[:, None] % HEAD_DIM == dst[None, :] % HEAD_DIM))
            for h in range(N_KV)]
    return jnp.stack(mats).astype(BF16)


def _even_layer(x, xc, mod, modc, norm_g, w_in, ws, bs, w_out, tables):
    b, s, d = x.shape
    w = w_in.shape[1] // 3
    outs = []
    for stream, m, tm in ((x, mod, MIX_TM), (xc, modc, 256)):
        if stream is None:
            outs.append(None)
            continue
        length = stream.shape[1]
        tm = min(tm, length)
        gm, z = _even_mix(stream, norm_g, m[0], m[1], w_in, ws, bs, tables["chan"], tm)
        table = tables["pos"][length]
        y = _matmul(table, z.reshape(2 * length, b * w),
                    min(1024, length), min(1024, b * w), min(2048, 2 * length), BF16)
        outs.append(_outproj(
            gm, pl.BlockSpec((None, tm, w), lambda bi, i: (bi, i, 0)),
            y, pl.BlockSpec((tm, w), lambda bi, i: (i, bi)),
            w_out, stream, m[2], tm))
    return outs


def _odd_layer(x, xc, mod, modc, norm_g, w_in, qg, kg, sink, conv_w, w_out, tables, ctx_out):
    b, s, d = x.shape
    lc = xc.shape[1]
    half = N_HEADS * HEAD_DIM
    tm, tmc = MIX_TM, min(256, lc)
    prep = functools.partial(_odd_in, g=norm_g, w=w_in, qg=qg, kg=kg, bdq=tables["bdq"], bdk=tables["bdk"],
                             tile=tables["tile"], conv_w=conv_w)
    q, k4, v4, conv = prep(x, shift=mod[0], scale=mod[1], cos=tables["cos"], sin=tables["sin"], tm=tm, with_q=True)
    qc, kc4, vc4, convc = prep(xc, shift=modc[0], scale=modc[1], cos=tables["cos_c"], sin=tables["sin_c"],
                               tm=tmc, with_q=ctx_out)
    att = _attention(q, k4, v4, kc4, vc4, sink, band=True)
    spec = lambda t: pl.BlockSpec((None, t, half), lambda bi, i: (bi, i, 0))
    y = _outproj(att, spec(tm), conv, spec(tm), w_out, x, mod[2], tm)
    yc = None
    if ctx_out:
        attc = _attention(qc, None, None, kc4, vc4, sink, band=False)
        yc = _outproj(attc, spec(tmc), convc, spec(tmc), w_out, xc, modc[2], tmc)
    return y, yc


def _moe(x, xc, mod, modc, norm_g, rw_hi, rw_lo, rb, w1, w3, w2, layer, s1, s3, s2, tri, upper):
    b, s, d = x.shape
    n_lat = b * s
    xc_shape = None
    if xc is not None and xc.shape[1] % MOE_TM:
        xc_shape = xc.shape
        xc = xc.reshape(-1, MOE_TM, d)
        modc = [m[:xc.shape[0]] for m in modc]
    counts0 = jnp.zeros((1, LANES), F32)
    h2, slab, tab, counts = _route(x, norm_g, mod[3], mod[4], rw_hi, rw_lo, rb, counts0, tri, upper)
    slabs, tabs = [slab.reshape(n_lat, LANES)], [tab]
    n_tok = n_lat
    if xc is not None:
        h2c, slabc, tabc, counts = _route(xc, norm_g, modc[3], modc[4], rw_hi, rw_lo, rb, counts, tri, upper)
        slabs.append(slabc.reshape(-1, LANES))
        tabs.append(tabc)
        n_tok += slabs[1].shape[0]
    cnt = counts[0, :N_EXPERTS].astype(I32)
    blk = EXPERT_BLOCK
    padded = (cnt + blk - 1) // blk * blk
    ends = jnp.cumsum(padded).astype(I32)
    starts = ends - padded
    step_rows = blk * EXPERT_GROUP
    n_rows = (n_tok * TOP_K + N_EXPERTS * (blk - 1) + step_rows - 1) // step_rows * step_rows
    n_blocks = n_rows // blk
    n_used = (ends[-1] // blk).reshape(1).astype(I32)
    blk_start = jnp.minimum(jnp.arange(n_blocks, dtype=I32), n_used[0] - 1) * blk
    block_e = jnp.minimum(jnp.sum(blk_start[:, None] >= ends[None, :], axis=1), N_EXPERTS - 1).astype(I32)
    slab_all = jnp.concatenate(slabs, axis=0)
    tab_all = jnp.concatenate(tabs, axis=0)[:, :, :N_EXPERTS].astype(I32)
    runs = jnp.concatenate([tab_all[:, 0], tab_all[:, 1], tab_all[:, 2] + starts[None, :]], axis=1)
    tab_flat = jnp.pad(runs, ((0, 0), (0, TAB_WORDS - runs.shape[1]))).reshape(-1)
    loc = (slab_all[:, SLAB_LOC:SLAB_LOC + 8].astype(I32) * ROW_PARTS).reshape(-1, MOE_TM, 8)
    loc_flat = jnp.swapaxes(loc, 1, 2).reshape(-1)
    packed = lambda a: a.reshape(-1, ROW_PARTS, LANES)
    xs = _dispatch(packed(h2), loc_flat, tab_flat, ends, n_used, None, n_rows, 0)
    if xc is not None:
        xs = _dispatch(packed(h2c), loc_flat, tab_flat, ends, n_used, xs, n_rows, n_lat // MOE_TM)
    ys = _experts(xs, block_e, n_used, w1, w3, w2, layer)
    x_new = _combine(ys, loc_flat, tab_flat, slab, h2, x, mod[5], s1, s3, s2, 0)
    xc_new = None
    if xc is not None:
        xc_new = _combine(ys, loc_flat, tab_flat, slabc, h2c, xc, modc[5], s1, s3, s2, n_lat // MOE_TM)
        if xc_shape is not None:
            xc_new = xc_new.reshape(xc_shape)
    return x_new, xc_new


def kernel(x, c, ctx, c_ctx, ada_w, ada_b, norm1_g, norm2_g, ev_w_in, ev_w_s, ev_b_s, ev_w_out, od_w_in, od_q_norm_g, od_k_norm_g, od_sink, od_conv_w, od_w_out, router_w, router_b, exp_w_gate, exp_w_up, exp_w_down, sh_w_gate, sh_w_up, sh_w_down):
    b, s, d = x.shape
    lc = ctx.shape[1]
    depth = ada_w.shape[0]
    n_groups = ev_w_s.shape[1]
    half = d // 2

    rows = -(-(b + 1) // 8) * 8
    cond = jnp.zeros((rows, d), F32).at[:b].set(c).at[b].set(c_ctx)
    mod_all = _adaln(cond, ada_w, ada_b)

    tables = {
        "chan": _channel_table(LANES, n_groups),
        "pos": {s: _dft_tables(s, LANES), lc: _dft_tables(lc, LANES)},
        "bdq": _head_mean_matrix(N_HEADS * HEAD_DIM),
        "bdk": _head_mean_matrix(N_KV * HEAD_DIM),
        "tile": _tile_matrices(),
    }
    tables["cos"], tables["sin"] = _rope_tables(s)
    tables["cos_c"] = jnp.ones((lc, N_KV * HEAD_DIM), F32)
    tables["sin_c"] = jnp.zeros((lc, N_KV * HEAD_DIM), F32)
    tri = (jnp.arange(MOE_TM)[:, None] > jnp.arange(MOE_TM)[None, :]).astype(BF16)
    upper = (jnp.arange(LANES)[:, None] < jnp.arange(LANES)[None, :]).astype(BF16)

    qw, kw = N_HEADS * HEAD_DIM, N_KV * HEAD_DIM
    perm = jnp.concatenate([jnp.arange(0, qw), jnp.arange(qw + 2 * kw, qw + 2 * kw + 3 * half),
                            jnp.arange(qw, qw + 2 * kw)])

    w1_all, w3_all, w2_all = exp_w_gate.astype(BF16), exp_w_up.astype(BF16), exp_w_down.astype(BF16)
    xc = ctx
    for l in range(depth):
        last = l == depth - 1
        even = l % 2 == 0
        need_ctx = not (last and even)
        pieces = [mod_all[l, :, j * d:(j + 1) * d] for j in range(6)]
        mod = [p[:b].reshape(b, 1, d) for p in pieces]
        modc = [jnp.broadcast_to(p[b].reshape(1, 1, d), (b, 1, d)) for p in pieces]
        g1 = norm1_g[l].reshape(1, d)
        g2 = norm2_g[l].reshape(1, d)
        if even:
            e = l // 2
            bs = jnp.broadcast_to(ev_b_s[e][:, :, None], (n_groups, CHUNK, LANES))
            y, yc = _even_layer(x, xc if (need_ctx and not last) else None, mod, modc, g1,
                                ev_w_in[e].astype(BF16), ev_w_s[e].astype(BF16), bs,
                                ev_w_out[e].astype(BF16), tables)
        else:
            o = l // 2
            qg = (jnp.tile(od_q_norm_g[o], N_HEADS) * (HEAD_DIM ** -0.5)).reshape(1, qw)
            kg = jnp.tile(od_k_norm_g[o], N_KV).reshape(1, kw)
            y, yc = _odd_layer(x, xc, mod, modc, g1, od_w_in[o][:, perm].astype(BF16), qg, kg,
                               od_sink[o], od_conv_w[o], od_w_out[o].astype(BF16), tables, not last)
        x = y
        if not last:
            xc = yc
        rw = jnp.zeros((d, LANES), F32).at[:, :N_EXPERTS].set(router_w[l])
        rw_hi = rw.astype(BF16)
        rw_lo = (rw - rw_hi.astype(F32)).astype(BF16)
        rb = jnp.zeros((1, LANES), F32).at[0, :N_EXPERTS].set(router_b[l])
        x, xc_new = _moe(x, None if last else xc, mod, modc, g2, rw_hi, rw_lo, rb, w1_all, w3_all, w2_all, l,
                         sh_w_gate[l].astype(BF16), sh_w_up[l].astype(BF16), sh_w_down[l].astype(BF16),
                         tri, upper)
        if not last:
            xc = xc_new
    return x
```

```python
import functools
import math

import jax
import jax.numpy as jnp
from jax import lax
from jax.experimental import pallas as pl
from jax.experimental.pallas import tpu as pltpu

F32 = jnp.float32
BF16 = jnp.bfloat16
I32 = jnp.int32
U32 = jnp.uint32

LANES = 128
VMEM_LIMIT = 48 * 2**20

EPS = 1e-6
GRID_W = 64
CHUNK = 128
HEAD_DIM = 64
N_HEADS = 8
N_KV = 2
Q_PER_KV = N_HEADS // N_KV
MIX_TM = 1024
ATT_BLOCK = 128
ATT_ROWS = 64
ATT_GROUP = 4
ROPE_BASE = 10000.0
N_EXPERTS = 64
TOP_K = 6
ROUTED_SCALE = 2.5
EXPERT_BLOCK = 512
EXPERT_GROUP = 2
MOE_TM = 1024
ROW_PARTS = 4
SLAB_IDX, SLAB_LOC, SLAB_GATE = 0, 8, 16
TAB_WORDS = 1024


def _params(*sem):
    return pltpu.CompilerParams(dimension_semantics=sem, vmem_limit_bytes=VMEM_LIMIT)


def _sigmoid(x):
    return 1.0 / (1.0 + jnp.exp(-x))


def _silu(x):
    return x * _sigmoid(x)


def _gelu_tanh(x):
    c = math.sqrt(2.0 / math.pi)
    return x * (0.5 * (1.0 + jnp.tanh(c * (x + 0.044715 * (x * x * x)))))


def _dot(a, b):
    return jnp.dot(a, b, preferred_element_type=F32)


def _unpack_words(w):
    return pltpu.bitcast(w << 16, F32), pltpu.bitcast(w & jnp.uint32(0xFFFF0000), F32)


def _row_chunks(ref, row0=0, n=None):
    total, parts, lanes = ref.shape
    n = total if n is None else n
    flat = ref.reshape(total * parts, lanes)
    return [flat[pl.ds(row0 * parts + c, n, stride=parts), :] for c in range(parts)]


def _unpack_rows(ref, row0=0, n=None):
    halves = [_unpack_words(w) for w in _row_chunks(ref, row0, n)]
    return jnp.concatenate([h[0] for h in halves] + [h[1] for h in halves], axis=-1)


def _pack_rows(ref, val, row0=0):
    total, parts, lanes = ref.shape
    n, half = val.shape[0], val.shape[1] // 2
    bits = pltpu.bitcast(val.astype(BF16).astype(F32), U32)
    words = (bits[:, :half] >> 16) | (bits[:, half:] & jnp.uint32(0xFFFF0000))
    flat = ref.reshape(total * parts, lanes)
    for c in range(parts):
        flat[pl.ds(row0 * parts + c, n, stride=parts), :] = words[:, c * lanes:(c + 1) * lanes]


def _adaln_kernel(c_ref, w_ref, b_ref, o_ref):
    o_ref[...] = _dot(_silu(c_ref[...]), w_ref[...]) + b_ref[...]


def _adaln(cond, ada_w, ada_b):
    n_layers, d, n6 = ada_w.shape
    rows = cond.shape[0]
    tn = 1536
    return pl.pallas_call(
        _adaln_kernel,
        grid=(n_layers, n6 // tn),
        in_specs=[pl.BlockSpec((rows, d), lambda l, j: (0, 0)),
                  pl.BlockSpec((None, d, tn), lambda l, j: (l, 0, j)),
                  pl.BlockSpec((None, 1, tn), lambda l, j: (l, 0, j))],
        out_specs=pl.BlockSpec((None, rows, tn), lambda l, j: (l, 0, j)),
        out_shape=jax.ShapeDtypeStruct((n_layers, rows, n6), F32),
        compiler_params=_params("parallel", "parallel"),
        name="adaln",
    )(cond, ada_w, ada_b.reshape(n_layers, 1, n6))


def _norm_mod(x, g, shift, scale):
    ms = jnp.mean(x * x, axis=-1, keepdims=True)
    h = (x * lax.rsqrt(ms + EPS)) * g
    return h * (1.0 + scale) + shift


def _even_mix_kernel(x_ref, g_ref, sh_ref, sc_ref, w_ref, ws_ref, bs_ref, cs_ref, gm_ref, z_ref, p_ref,
                     *, n_chunks, n_groups):
    h = _norm_mod(x_ref[...], g_ref[...], sh_ref[...], sc_ref[...])
    p_ref[...] = _dot(h.astype(BF16), w_ref[...])
    half = gm_ref.shape[1]
    for c in range(n_chunks):
        rows = slice(c * CHUNK, (c + 1) * CHUNK)
        for g in range(n_groups):
            cols = slice(g * LANES, (g + 1) * LANES)
            ug = _gelu_tanh(p_ref[rows, cols])
            vg = _gelu_tanh(p_ref[rows, half + g * LANES:half + (g + 1) * LANES])
            mu = jnp.mean(vg, axis=-1, keepdims=True)
            dv = vg - mu
            var = jnp.mean(dv * dv, axis=-1, keepdims=True)
            vn = dv * lax.rsqrt(var + 1e-5)
            fg = _dot(ws_ref[g], vn.astype(BF16)) + bs_ref[g]
            gm_ref[rows, cols] = (ug * fg).astype(gm_ref.dtype)
    fz = _dot(p_ref[:, 2 * half:].astype(BF16), cs_ref[...])
    z_ref[0] = fz[:, :half].astype(z_ref.dtype)
    z_ref[1] = fz[:, half:].astype(z_ref.dtype)


def _even_mix(x, g, shift, scale, w_in, ws, bs, cs, tm):
    b, s, d = x.shape
    n3 = w_in.shape[1]
    w = n3 // 3
    n_groups = w // LANES
    kern = functools.partial(_even_mix_kernel, n_chunks=tm // CHUNK, n_groups=n_groups)
    return pl.pallas_call(
        kern,
        grid=(b, s // tm),
        in_specs=[pl.BlockSpec((None, tm, d), lambda bi, i: (bi, i, 0)),
                  pl.BlockSpec((1, d), lambda bi, i: (0, 0)),
                  pl.BlockSpec((None, 1, d), lambda bi, i: (bi, 0, 0)),
                  pl.BlockSpec((None, 1, d), lambda bi, i: (bi, 0, 0)),
                  pl.BlockSpec((d, n3), lambda bi, i: (0, 0)),
                  pl.BlockSpec(ws.shape, lambda bi, i: (0, 0, 0)),
                  pl.BlockSpec(bs.shape, lambda bi, i: (0, 0, 0)),
                  pl.BlockSpec(cs.shape, lambda bi, i: (0, 0))],
        out_specs=[pl.BlockSpec((None, tm, w), lambda bi, i: (bi, i, 0)),
                   pl.BlockSpec((2, tm, w), lambda bi, i: (0, i, bi))],
        out_shape=[jax.ShapeDtypeStruct((b, s, w), BF16),
                   jax.ShapeDtypeStruct((2, s, b * w), BF16)],
        scratch_shapes=[pltpu.VMEM((tm, n3), F32)],
        compiler_params=_params("parallel", "parallel"),
        name="even_mix",
    )(x, g, shift, scale, w_in, ws, bs, cs)


def _mm_kernel(a_ref, b_ref, o_ref, acc_ref):
    k = pl.program_id(2)

    @pl.when(k == 0)
    def _():
        acc_ref[...] = jnp.zeros_like(acc_ref)

    acc_ref[...] += _dot(a_ref[...], b_ref[...])

    @pl.when(k == pl.num_programs(2) - 1)
    def _():
        o_ref[...] = acc_ref[...].astype(o_ref.dtype)


def _matmul(a, b, tm, tn, tk, out_dtype):
    m, kd = a.shape
    n = b.shape[1]
    return pl.pallas_call(
        _mm_kernel,
        grid=(m // tm, n // tn, kd // tk),
        in_specs=[pl.BlockSpec((tm, tk), lambda i, j, k: (i, k)),
                  pl.BlockSpec((tk, tn), lambda i, j, k: (k, j))],
        out_specs=pl.BlockSpec((tm, tn), lambda i, j, k: (i, j)),
        out_shape=jax.ShapeDtypeStruct((m, n), out_dtype),
        scratch_shapes=[pltpu.VMEM((tm, tn), F32)],
        compiler_params=_params("parallel", "parallel", "arbitrary"),
        name="dft_matmul",
    )(a, b)


def _outproj_kernel(a_ref, b_ref, w_ref, x_ref, gate_ref, o_ref):
    ab = jnp.concatenate([a_ref[...], b_ref[...]], axis=-1)
    o_ref[...] = x_ref[...] + gate_ref[...] * _dot(ab, w_ref[...])


def _outproj(a, a_spec, bsrc, b_spec, w, x, gate, tm):
    b, s, d = x.shape
    return pl.pallas_call(
        _outproj_kernel,
        grid=(b, s // tm),
        in_specs=[a_spec, b_spec,
                  pl.BlockSpec(w.shape, lambda bi, i: (0, 0)),
                  pl.BlockSpec((None, tm, d), lambda bi, i: (bi, i, 0)),
                  pl.BlockSpec((None, 1, d), lambda bi, i: (bi, 0, 0))],
        out_specs=pl.BlockSpec((None, tm, d), lambda bi, i: (bi, i, 0)),
        out_shape=jax.ShapeDtypeStruct((b, s, d), F32),
        compiler_params=_params("parallel", "parallel"),
        name="outproj",
    )(a, bsrc, w, x, gate)


def _head_rms(x, bd_ref):
    xx = x * x
    hi = xx.astype(BF16)
    lo = (xx - hi.astype(F32)).astype(BF16)
    ms = _dot(hi, bd_ref[...]) + _dot(lo, bd_ref[...])
    return x * lax.rsqrt(ms + EPS)


def _rope(x, cos, sins):
    width = x.shape[1]
    lane = lax.broadcasted_iota(I32, x.shape, 1)
    first = (lane & 31) < 16
    swapped = jnp.where(first, pltpu.roll(x, width - 16, 1), pltpu.roll(x, 16, 1))
    return x * cos + swapped * sins


HALO = 8


def _odd_in_kernel(x_ref, xp_ref, xn_ref, g_ref, sh_ref, sc_ref, w_ref, cos_ref, sin_ref, qg_ref, kg_ref,
                   bdq_ref, bdk_ref, tile_ref, tile_t_ref, cw_ref, qo_ref, k4_ref, v4_ref, co_ref, p_ref, *, with_q):
    i = pl.program_id(1)
    n_tiles = pl.num_programs(1)
    tm = x_ref.shape[0]
    qw, dc, kw = qo_ref.shape[1], co_ref.shape[1], kg_ref.shape[1]
    g, sh, sc = g_ref[...], sh_ref[...], sc_ref[...]
    p_ref[...] = _dot(_norm_mod(x_ref[...], g, sh, sc).astype(BF16), w_ref[...])
    cos, sin = cos_ref[...], sin_ref[...]
    if with_q:
        reps = qw // kw
        qn = _head_rms(p_ref[:, :qw], bdq_ref) * qg_ref[...]
        qo_ref[...] = _rope(qn, jnp.concatenate([cos] * reps, axis=1),
                            jnp.concatenate([sin] * reps, axis=1)).astype(qo_ref.dtype)
    else:
        qo_ref[...] = jnp.zeros_like(qo_ref)
    k0 = qw + 3 * dc
    kn = _head_rms(p_ref[:, k0:k0 + kw], bdk_ref) * kg_ref[...]
    kr = _rope(kn, cos, sin).astype(BF16)
    vb = p_ref[:, k0 + kw:k0 + 2 * kw].astype(BF16)
    for h in range(N_KV):
        k4_ref[h] = lax.dot_general(tile_t_ref[h], kr, (((1,), (1,)), ((), ())),
                                    preferred_element_type=F32).astype(k4_ref.dtype)
        v4_ref[h] = _dot(vb, tile_ref[h]).astype(v4_ref.dtype)
    halo = _norm_mod(jnp.concatenate([xp_ref[...], xn_ref[...]], axis=0), g, sh, sc).astype(BF16)
    gz = _dot(halo, w_ref[:, qw:qw + dc]) * _dot(halo, w_ref[:, qw + 2 * dc:qw + 3 * dc])
    before = jnp.where(i > 0, gz[HALO - 1:HALO, :], 0.0)
    after = jnp.where(i < n_tiles - 1, gz[HALO:HALO + 1, :], 0.0)
    ridx = lax.broadcasted_iota(I32, (tm, LANES), 0)
    for c in range(dc // LANES):
        cols = slice(c * LANES, (c + 1) * LANES)
        zc = p_ref[:, qw + c * LANES:qw + (c + 1) * LANES] * p_ref[:, qw + 2 * dc + c * LANES:qw + 2 * dc + (c + 1) * LANES]
        zp = jnp.where(ridx == 0, before[:, cols], pltpu.roll(zc, 1, 0))
        zn = jnp.where(ridx == tm - 1, after[:, cols], pltpu.roll(zc, tm - 1, 0))
        y = zp * cw_ref[0:1, cols] + zc * cw_ref[1:2, cols] + zn * cw_ref[2:3, cols]
        co_ref[:, cols] = (p_ref[:, qw + dc + c * LANES:qw + dc + (c + 1) * LANES] * y).astype(co_ref.dtype)


def _odd_in(x, g, shift, scale, w, cos, sin, qg, kg, bdq, bdk, tile, conv_w, tm, with_q):
    b, s, d = x.shape
    n = w.shape[1]
    tile_t = jnp.swapaxes(tile, 1, 2)
    qw = N_HEADS * HEAD_DIM
    kw = N_KV * HEAD_DIM
    rep = Q_PER_KV * HEAD_DIM
    dc = conv_w.shape[1]
    per_tile = tm // HALO
    kern = functools.partial(_odd_in_kernel, with_q=with_q)
    return pl.pallas_call(
        kern,
        grid=(b, s // tm),
        in_specs=[pl.BlockSpec((None, tm, d), lambda bi, i: (bi, i, 0)),
                  pl.BlockSpec((None, HALO, d), lambda bi, i: (bi, jnp.maximum(i * per_tile - 1, 0), 0)),
                  pl.BlockSpec((None, HALO, d), lambda bi, i: (bi, jnp.minimum((i + 1) * per_tile, s // HALO - 1), 0)),
                  pl.BlockSpec((1, d), lambda bi, i: (0, 0)),
                  pl.BlockSpec((None, 1, d), lambda bi, i: (bi, 0, 0)),
                  pl.BlockSpec((None, 1, d), lambda bi, i: (bi, 0, 0)),
                  pl.BlockSpec((d, n), lambda bi, i: (0, 0)),
                  pl.BlockSpec((tm, kw), lambda bi, i: (i, 0)),
                  pl.BlockSpec((tm, kw), lambda bi, i: (i, 0)),
                  pl.BlockSpec((1, qw), lambda bi, i: (0, 0)),
                  pl.BlockSpec((1, kw), lambda bi, i: (0, 0)),
                  pl.BlockSpec(bdq.shape, lambda bi, i: (0, 0)),
                  pl.BlockSpec(bdk.shape, lambda bi, i: (0, 0)),
                  pl.BlockSpec(tile.shape, lambda bi, i: (0, 0, 0)),
                  pl.BlockSpec(tile_t.shape, lambda bi, i: (0, 0, 0)),
                  pl.BlockSpec(conv_w.shape, lambda bi, i: (0, 0))],
        out_specs=[pl.BlockSpec((None, tm, qw), lambda bi, i: (bi, i, 0)),
                   pl.BlockSpec((None, N_KV, rep, tm), lambda bi, i: (bi, 0, 0, i)),
                   pl.BlockSpec((None, N_KV, tm, rep), lambda bi, i: (bi, 0, i, 0)),
                   pl.BlockSpec((None, tm, dc), lambda bi, i: (bi, i, 0))],
        out_shape=[jax.ShapeDtypeStruct((b, s, qw), BF16),
                   jax.ShapeDtypeStruct((b, N_KV, rep, s), BF16),
                   jax.ShapeDtypeStruct((b, N_KV, s, rep), BF16),
                   jax.ShapeDtypeStruct((b, s, dc), BF16)],
        scratch_shapes=[pltpu.VMEM((tm, n), F32)],
        compiler_params=_params("parallel", "parallel"),
        name="odd_in",
    )(x, x, x, g, shift, scale, w, cos, sin, qg, kg, bdq, bdk, tile, tile_t, conv_w)


def _attn_kernel(sink_ref, q_ref, *refs, band, n_blocks, group):
    o_ref = refs[-1]
    i = pl.program_id(1)
    t = ATT_BLOCK
    w = Q_PER_KV * HEAD_DIM
    lane = lax.broadcasted_iota(I32, (t, w), 1)
    masks = [(lane >= g * HEAD_DIM) & (lane < (g + 1) * HEAD_DIM) for g in range(Q_PER_KV)]
    for j in range(group):
        blk = i * group + j
        bad = None
        if band:
            kp, km, kn, vp, vm, vn, kx, vx = refs[:-1]
            n_keys = 3 * t + vx.shape[1]
            row = lax.broadcasted_iota(I32, (Q_PER_KV * t, n_keys), 0) & (t - 1)
            col = lax.broadcasted_iota(I32, (Q_PER_KV * t, n_keys), 1)
            off_prev = jnp.where(blk > 0, 0, 4 * t)
            off_next = jnp.where(blk < n_blocks - 1, 0, 4 * t)
            bad_prev = (col < t) & (col < row + off_prev)
            bad_next = (col >= 2 * t) & (col < 3 * t) & (col - 2 * t > row - off_next)
            bad = bad_prev | bad_next
        for h in range(N_KV):
            if band:
                k_band = [kp[h] if j == 0 else km[h, :, (j - 1) * t:j * t], km[h, :, j * t:(j + 1) * t],
                          kn[h] if j == group - 1 else km[h, :, (j + 1) * t:(j + 2) * t]]
                v_band = [vp[h] if j == 0 else vm[h, (j - 1) * t:j * t, :], vm[h, j * t:(j + 1) * t, :],
                          vn[h] if j == group - 1 else vm[h, (j + 1) * t:(j + 2) * t, :]]
                kcat = jnp.concatenate(k_band + [kx[h]], axis=1)
                vcat = jnp.concatenate(v_band + [vx[h]], axis=0)
            else:
                kx, vx = refs[:-1]
                kcat, vcat = kx[h], vx[h]
            q = q_ref[j * t:(j + 1) * t, h * w:(h + 1) * w]
            q4 = jnp.concatenate([jnp.where(m, q, jnp.zeros_like(q)) for m in masks], axis=0)
            s = _dot(q4, kcat)
            es, invs = [], []
            for c in range(Q_PER_KV * t // ATT_ROWS):
                rows = slice(c * ATT_ROWS, (c + 1) * ATT_ROWS)
                sc = s[rows]
                if band:
                    sc = jnp.where(bad[rows], -jnp.inf, sc)
                sink = jnp.full((ATT_ROWS, 1), sink_ref[h * Q_PER_KV + (c * ATT_ROWS) // t], F32)
                m = jnp.maximum(jnp.max(sc, axis=-1, keepdims=True), sink)
                ec = jnp.exp(sc - m)
                invs.append(1.0 / (jnp.sum(ec, axis=-1, keepdims=True) + jnp.exp(sink - m)))
                es.append(ec.astype(BF16))
            e = jnp.concatenate(es, axis=0)
            r = _dot(e, vcat) * jnp.concatenate(invs, axis=0)
            o = jnp.zeros((t, w), F32)
            for g in range(Q_PER_KV):
                o = o + jnp.where(masks[g], r[g * t:(g + 1) * t, :], 0.0)
            o_ref[j * t:(j + 1) * t, h * w:(h + 1) * w] = o.astype(o_ref.dtype)


def _attention(q, k4, v4, kx4, vx4, sink, band):
    b, s, qw = q.shape
    rep = vx4.shape[-1]
    lc = vx4.shape[2]
    t = ATT_BLOCK
    nb = s // t
    group = ATT_GROUP if band and nb % ATT_GROUP == 0 else 1
    kern = functools.partial(_attn_kernel, band=band, n_blocks=nb, group=group)

    def edge(off):
        return lambda i: jnp.clip(i * group + off, 0, nb - 1)

    def k_spec(width, blk):
        return pl.BlockSpec((None, N_KV, rep, width), lambda bi, i: (bi, 0, 0, blk(i)))

    def v_spec(width, blk):
        return pl.BlockSpec((None, N_KV, width, rep), lambda bi, i: (bi, 0, blk(i), 0))

    in_specs = [pl.BlockSpec(memory_space=pltpu.SMEM),
                pl.BlockSpec((None, group * t, qw), lambda bi, i: (bi, i, 0))]
    args = [sink, q]
    if band:
        own = lambda i: i
        in_specs += [k_spec(t, edge(-1)), k_spec(group * t, own), k_spec(t, edge(group)),
                     v_spec(t, edge(-1)), v_spec(group * t, own), v_spec(t, edge(group))]
        args += [k4, k4, k4, v4, v4, v4]
    in_specs += [pl.BlockSpec((None, N_KV, rep, lc), lambda bi, i: (bi, 0, 0, 0)),
                 pl.BlockSpec((None, N_KV, lc, rep), lambda bi, i: (bi, 0, 0, 0))]
    args += [kx4, vx4]
    return pl.pallas_call(
        kern,
        grid=(b, nb // group),
        in_specs=in_specs,
        out_specs=pl.BlockSpec((None, group * t, qw), lambda bi, i: (bi, i, 0)),
        out_shape=jax.ShapeDtypeStruct((b, s, qw), BF16),
        compiler_params=_params("parallel", "parallel"),
        name="attention_band" if band else "attention_ctx",
    )(*args)


def _route_kernel(x_ref, g_ref, sh_ref, sc_ref, whi_ref, wlo_ref, rb_ref, cin_ref, tri_ref, upper_ref,
                  h_ref, slab_ref, tab_ref, cnt_ref, carry_ref):
    first = (pl.program_id(0) == 0) & (pl.program_id(1) == 0)

    @pl.when(first)
    def _():
        carry_ref[...] = cin_ref[...]

    h = _norm_mod(x_ref[...], g_ref[...], sh_ref[...], sc_ref[...])
    _pack_rows(h_ref, h)
    hi = h.astype(BF16)
    lo = (h - hi.astype(F32)).astype(BF16)
    logits = _dot(hi, whi_ref[...]) + _dot(lo, whi_ref[...]) + _dot(hi, wlo_ref[...])
    scores = _sigmoid(logits)
    tm, lanes = scores.shape
    lane = lax.broadcasted_iota(I32, (tm, lanes), 1).astype(F32)
    work = jnp.where(lane < N_EXPERTS, scores + rb_ref[...], -jnp.inf)
    hits, idxs, gates = [], [], []
    for _ in range(TOP_K):
        mx = jnp.max(work, axis=-1, keepdims=True)
        idx = jnp.min(jnp.where(work == mx, lane, float(lanes)), axis=-1, keepdims=True)
        hit = lane == idx
        hits.append(hit)
        idxs.append(idx)
        gates.append(jnp.sum(jnp.where(hit, scores, 0.0), axis=-1, keepdims=True))
        work = jnp.where(hit, -jnp.inf, work)
    gsum = gates[0]
    for gk in gates[1:]:
        gsum = gsum + gk
    gscale = ROUTED_SCALE / (gsum + 1e-20)
    onehot = jnp.zeros((tm, lanes), F32)
    for hit in hits:
        onehot = jnp.where(hit, 1.0, onehot)
    cnt = jnp.sum(onehot, axis=0, keepdims=True)
    cnt_hi = jnp.floor(cnt * (1.0 / 256.0))
    cnt_lo = cnt - 256.0 * cnt_hi
    parts = jnp.concatenate([jnp.broadcast_to(cnt_hi, (8, lanes)), jnp.broadcast_to(cnt_lo, (8, lanes))], axis=0)
    sums = _dot(parts.astype(BF16), upper_ref[...])
    seg = 256.0 * sums[0:1] + sums[8:9]
    before = _dot(tri_ref[...], onehot.astype(BF16)) + seg
    slab = jnp.zeros((tm, lanes), F32)
    for k in range(TOP_K):
        loc = jnp.sum(jnp.where(hits[k], before, 0.0), axis=-1, keepdims=True)
        slab = jnp.where(lane == SLAB_IDX + k, idxs[k], slab)
        slab = jnp.where(lane == SLAB_LOC + k, loc, slab)
        slab = jnp.where(lane == SLAB_GATE + k, gates[k] * gscale, slab)
    slab_ref[...] = slab
    row = lax.broadcasted_iota(I32, (8, lanes), 0)
    tab_ref[...] = jnp.where(row == 0, cnt, jnp.where(row == 1, seg, jnp.where(row == 2, carry_ref[...], 0.0)))
    carry_ref[...] = carry_ref[...] + cnt
    cnt_ref[...] = carry_ref[...]


def _route(x, g, shift, scale, whi, wlo, rb, counts_in, tri, upper):
    b, s, d = x.shape
    tm = MOE_TM
    nt = s // tm
    return pl.pallas_call(
        _route_kernel,
        grid=(b, nt),
        in_specs=[pl.BlockSpec((None, tm, d), lambda bi, i: (bi, i, 0)),
                  pl.BlockSpec((1, d), lambda bi, i: (0, 0)),
                  pl.BlockSpec((None, 1, d), lambda bi, i: (bi, 0, 0)),
                  pl.BlockSpec((None, 1, d), lambda bi, i: (bi, 0, 0)),
                  pl.BlockSpec(whi.shape, lambda bi, i: (0, 0)),
                  pl.BlockSpec(wlo.shape, lambda bi, i: (0, 0)),
                  pl.BlockSpec((1, LANES), lambda bi, i: (0, 0)),
                  pl.BlockSpec((1, LANES), lambda bi, i: (0, 0)),
                  pl.BlockSpec((tm, tm), lambda bi, i: (0, 0)),
                  pl.BlockSpec((LANES, LANES), lambda bi, i: (0, 0))],
        out_specs=[pl.BlockSpec((None, tm, ROW_PARTS, LANES), lambda bi, i: (bi, i, 0, 0)),
                   pl.BlockSpec((None, tm, LANES), lambda bi, i: (bi, i, 0)),
                   pl.BlockSpec((None, 8, LANES), lambda bi, i: (bi * nt + i, 0, 0)),
                   pl.BlockSpec((1, LANES), lambda bi, i: (0, 0))],
        out_shape=[jax.ShapeDtypeStruct((b, s, ROW_PARTS, LANES), U32),
                   jax.ShapeDtypeStruct((b, s, LANES), F32),
                   jax.ShapeDtypeStruct((b * nt, 8, LANES), F32),
                   jax.ShapeDtypeStruct((1, LANES), F32)],
        scratch_shapes=[pltpu.VMEM((1, LANES), F32)],
        compiler_params=_params("arbitrary", "arbitrary"),
        name="route",
    )(x, g, shift, scale, whi, wlo, rb, counts_in, tri, upper)


def _rows(ref, row, n):
    return ref.at[pl.ds(row, n)]


def _run_copies(tab_smem, base, stage_ref, far_ref, sem, to_far):
    n_bits = MOE_TM.bit_length()
    common = n_bits - 3

    def per_expert(e, carry):
        n = tab_smem[base + e]
        near = tab_smem[base + N_EXPERTS + e]
        far = tab_smem[base + 2 * N_EXPERTS + e]

        def piece(bit):
            size = 1 << bit

            @pl.when((n & size) != 0)
            def _():
                done = n & (size - 1)
                a, b = _rows(stage_ref, near + done, size), _rows(far_ref, far + done, size)
                (pltpu.make_async_copy(a, b, sem) if to_far else pltpu.make_async_copy(b, a, sem)).start()

        for bit in range(common):
            piece(bit)

        @pl.when(n >= (1 << common))
        def _():
            for bit in range(common, n_bits):
                piece(bit)
        return carry

    lax.fori_loop(0, N_EXPERTS, per_expert, 0)


def _loc_copies(loc_ref, tile, loc_smems, sem):
    return [pltpu.make_async_copy(loc_ref.at[pl.ds((tile * 8 + k) * MOE_TM, MOE_TM)], loc_smems[k], sem)
            for k in range(TOP_K)]


def _wait_tile(stage_ref, far_ref, sem, to_far):
    n = TOP_K * MOE_TM
    a, b = _rows(stage_ref, 0, n), _rows(far_ref, 0, n)
    (pltpu.make_async_copy(a, b, sem) if to_far else pltpu.make_async_copy(b, a, sem)).wait()


def _dispatch_kernel(ends_ref, nu_ref, h_ref, loc_ref, tab_ref, *rest, zero_fill, tile_base, n_blocks):
    if zero_fill:
        xs_ref, rest = rest[0], rest[1:]
    else:
        xs_ref, rest = rest[1], rest[2:]
    loc_smems, rest = rest[:TOP_K], rest[TOP_K:]
    if zero_fill:
        tab_smem, stages_ref, zero_ref, sems, psem = rest
    else:
        tab_smem, stages_ref, sems, psem = rest
        zero_ref = None
    tm = MOE_TM
    i = pl.program_id(0)
    n_steps = pl.num_programs(0)
    slot = i % 2
    stage_ref, sem = stages_ref.at[slot], sems.at[slot]

    def table_copies(j):
        return _loc_copies(loc_ref, tile_base + j, loc_smems, psem) + [
            pltpu.make_async_copy(tab_ref.at[pl.ds((tile_base + j) * TAB_WORDS, TAB_WORDS)], tab_smem, psem)]

    @pl.when(i == 0)
    def _():
        for cp in table_copies(0):
            cp.start()

    if zero_fill:
        @pl.when(i == 0)
        def _():
            zero_ref[...] = jnp.zeros_like(zero_ref)

            def block_copy(blk):
                return pltpu.make_async_copy(zero_ref, _rows(xs_ref, blk * EXPERT_BLOCK, EXPERT_BLOCK), sem)

            def fill(e, carry, *, start):
                end = ends_ref[e]
                prev = jnp.where(e > 0, ends_ref[jnp.maximum(e - 1, 0)], 0)

                @pl.when(end > prev)
                def _():
                    cp = block_copy(end // EXPERT_BLOCK - 1)
                    cp.start() if start else cp.wait()
                return carry

            def tail(j, carry, *, start):
                cp = block_copy(j)
                cp.start() if start else cp.wait()
                return carry

            lax.fori_loop(0, N_EXPERTS, functools.partial(fill, start=True), 0)
            lax.fori_loop(nu_ref[0], n_blocks, functools.partial(tail, start=True), 0)
            lax.fori_loop(0, N_EXPERTS, functools.partial(fill, start=False), 0)
            lax.fori_loop(nu_ref[0], n_blocks, functools.partial(tail, start=False), 0)

    for cp in table_copies(i):
        cp.wait()

    @pl.when(i >= 2)
    def _():
        _wait_tile(stage_ref, xs_ref, sem, to_far=True)

    flat = stage_ref.reshape(TOP_K * tm * ROW_PARTS, LANES)
    h_flat = h_ref.reshape(tm * ROW_PARTS, LANES)

    def place(t, c):
        row = h_flat[pl.ds(pl.multiple_of(t * ROW_PARTS, ROW_PARTS), ROW_PARTS), :]
        for k in range(TOP_K):
            flat[pl.ds(pl.multiple_of(loc_smems[k][t], ROW_PARTS), ROW_PARTS), :] = row
        return c

    lax.fori_loop(0, tm, place, 0, unroll=4)
    _run_copies(tab_smem, 0, stage_ref, xs_ref, sem, to_far=True)

    @pl.when(i + 1 < n_steps)
    def _():
        for cp in table_copies(i + 1):
            cp.start()

    @pl.when(i == n_steps - 1)
    def _():
        _wait_tile(stage_ref, xs_ref, sem, to_far=True)

        @pl.when(i >= 1)
        def _():
            _wait_tile(stages_ref.at[1 - slot], xs_ref, sems.at[1 - slot], to_far=True)


def _dispatch(h2, loc_flat, tab_flat, ends, n_used, xs_prev, n_rows, tile_base):
    n = h2.shape[0]
    tm = MOE_TM
    zero_fill = xs_prev is None
    kern = functools.partial(_dispatch_kernel, zero_fill=zero_fill, tile_base=tile_base,
                             n_blocks=n_rows // EXPERT_BLOCK)
    in_specs = [pl.BlockSpec((tm, ROW_PARTS, LANES), lambda i, e, nu: (i, 0, 0)),
                pl.BlockSpec(memory_space=pl.ANY),
                pl.BlockSpec(memory_space=pl.ANY)]
    args = [ends, n_used, h2, loc_flat, tab_flat]
    scratch = [pltpu.SMEM((tm,), I32)] * TOP_K + [pltpu.SMEM((TAB_WORDS,), I32),
               pltpu.VMEM((2, TOP_K * tm, ROW_PARTS, LANES), U32)]
    aliases = {}
    if zero_fill:
        scratch.append(pltpu.VMEM((EXPERT_BLOCK, ROW_PARTS, LANES), U32))
    else:
        in_specs.append(pl.BlockSpec(memory_space=pl.ANY))
        args.append(xs_prev)
        aliases = {5: 0}
    scratch += [pltpu.SemaphoreType.DMA((2,)), pltpu.SemaphoreType.DMA]
    return pl.pallas_call(
        kern,
        grid_spec=pltpu.PrefetchScalarGridSpec(
            num_scalar_prefetch=2,
            grid=(n // tm,),
            in_specs=in_specs,
            out_specs=pl.BlockSpec(memory_space=pl.ANY),
            scratch_shapes=scratch),
        out_shape=jax.ShapeDtypeStruct((n_rows, ROW_PARTS, LANES), U32),
        input_output_aliases=aliases,
        compiler_params=_params("arbitrary"),
        name="dispatch",
    )(*args)


def _expert_kernel(be_ref, nu_ref, x_ref, *refs):
    o_ref = refs[-1]
    blk = EXPERT_BLOCK
    used = pl.program_id(0) * EXPERT_GROUP < nu_ref[0]

    @pl.when(used)
    def _():
        for j in range(EXPERT_GROUP):
            w1_ref, w3_ref, w2_ref = refs[3 * j:3 * j + 3]
            x = _unpack_rows(x_ref, j * blk, blk).astype(BF16)
            a = _silu(_dot(x, w1_ref[...])) * _dot(x, w3_ref[...])
            _pack_rows(o_ref, _dot(a.astype(BF16), w2_ref[...]), j * blk)

    @pl.when(jnp.logical_not(used))
    def _():
        o_ref[...] = jnp.zeros_like(o_ref)


def _experts(xs, block_e, n_used, w1, w3, w2, layer):
    d, de = w1.shape[2:]
    rows = EXPERT_BLOCK * EXPERT_GROUP
    n_steps = xs.shape[0] // rows

    def row_map(i, be, nu):
        return (jnp.minimum(i, (nu[0] + EXPERT_GROUP - 1) // EXPERT_GROUP - 1), 0, 0)

    w_specs = []
    for j in range(EXPERT_GROUP):
        w_map = lambda i, be, nu, j=j: (layer, be[i * EXPERT_GROUP + j], 0, 0)
        w_specs += [pl.BlockSpec((None, None, d, de), w_map),
                    pl.BlockSpec((None, None, d, de), w_map),
                    pl.BlockSpec((None, None, de, d), w_map)]

    return pl.pallas_call(
        _expert_kernel,
        grid_spec=pltpu.PrefetchScalarGridSpec(
            num_scalar_prefetch=2,
            grid=(n_steps,),
            in_specs=[pl.BlockSpec((rows, ROW_PARTS, LANES), row_map)] + w_specs,
            out_specs=pl.BlockSpec((rows, ROW_PARTS, LANES), lambda i, be, nu: (i, 0, 0))),
        out_shape=jax.ShapeDtypeStruct(xs.shape, U32),
        compiler_params=_params("arbitrary"),
        name="experts",
    )(block_e, n_used, xs, *([w1, w3, w2] * EXPERT_GROUP))


def _combine_kernel(loc_ref, tab_ref, ys_ref, slab_ref, h_ref, x_ref, gate_ref, s1_ref, s3_ref, s2_ref,
                    o_ref, *scratch, tile_base):
    loc_smems = scratch[:TOP_K]
    tab_smem, gk_ref, stage_ref, lo_ref, hi_ref, sem, psem, tsem = scratch[TOP_K:]
    tm = MOE_TM
    step = pl.program_id(0) * pl.num_programs(1) + pl.program_id(1)
    n_steps = pl.num_programs(0) * pl.num_programs(1)

    def loc_copies(j):
        return _loc_copies(loc_ref, tile_base + j, loc_smems, psem)

    def tab_copy(j):
        return pltpu.make_async_copy(tab_ref.at[pl.ds((tile_base + j) * TAB_WORDS, TAB_WORDS)], tab_smem, tsem)

    def gather(j):
        tab_copy(j).wait()
        _run_copies(tab_smem, 0, stage_ref, ys_ref, sem, to_far=False)
        for cp in loc_copies(j):
            cp.start()

    @pl.when(step == 0)
    def _():
        tab_copy(0).start()
        gather(0)

    hb = _unpack_rows(h_ref).astype(BF16)
    shared = _dot((_silu(_dot(hb, s1_ref[...])) * _dot(hb, s3_ref[...])).astype(BF16), s2_ref[...])
    slab = slab_ref[...]
    for k in range(TOP_K):
        gk_ref[k] = jnp.broadcast_to(slab[:, SLAB_GATE + k:SLAB_GATE + k + 1], (tm, LANES))
    _wait_tile(stage_ref, ys_ref, sem, to_far=False)

    @pl.when(step + 1 < n_steps)
    def _():
        tab_copy(step + 1).start()
    for cp in loc_copies(step):
        cp.wait()

    flat = stage_ref.reshape(TOP_K * tm * ROW_PARTS, LANES)

    def tree_sum(terms):
        while len(terms) > 1:
            terms = [a + b for a, b in zip(terms[::2], terms[1::2])] + ([terms[-1]] if len(terms) % 2 else [])
        return terms[0]

    def mix(t, c):
        los, his = [], []
        for k in range(TOP_K):
            g = gk_ref[k, pl.ds(t, ROW_PARTS, stride=0), :]
            at = pl.multiple_of(loc_smems[k][t], ROW_PARTS)
            wl, wh = _unpack_words(flat[pl.ds(at, ROW_PARTS), :])
            los.append(g * wl)
            his.append(g * wh)
        lo_ref[t] = tree_sum(los)
        hi_ref[t] = tree_sum(his)
        return c

    lax.fori_loop(0, tm, mix, 0, unroll=8)

    @pl.when(step + 1 < n_steps)
    def _():
        gather(step + 1)

    routed =jnp.concatenate(_row_chunks(lo_ref) + _row_chunks(hi_ref), axis=-1)
    o_ref[...] = x_ref[...] + gate_ref[...] * (routed + shared)


def _combine(ys, loc_flat, tab_flat, slab, h2, x, gate, s1, s3, s2, tile_base):
    b, s, d = x.shape
    tm = MOE_TM
    kern = functools.partial(_combine_kernel, tile_base=tile_base)
    return pl.pallas_call(
        kern,
        grid=(b, s // tm),
        in_specs=[pl.BlockSpec(memory_space=pl.ANY),
                  pl.BlockSpec(memory_space=pl.ANY),
                  pl.BlockSpec(memory_space=pl.ANY),
                  pl.BlockSpec((None, tm, LANES), lambda bi, i: (bi, i, 0)),
                  pl.BlockSpec((None, tm, ROW_PARTS, LANES), lambda bi, i: (bi, i, 0, 0)),
                  pl.BlockSpec((None, tm, d), lambda bi, i: (bi, i, 0)),
                  pl.BlockSpec((None, 1, d), lambda bi, i: (bi, 0, 0)),
                  pl.BlockSpec(s1.shape, lambda bi, i: (0, 0)),
                  pl.BlockSpec(s3.shape, lambda bi, i: (0, 0)),
                  pl.BlockSpec(s2.shape, lambda bi, i: (0, 0))],
        out_specs=pl.BlockSpec((None, tm, d), lambda bi, i: (bi, i, 0)),
        out_shape=jax.ShapeDtypeStruct((b, s, d), F32),
        scratch_shapes=[pltpu.SMEM((tm,), I32)] * TOP_K + [
                        pltpu.SMEM((TAB_WORDS,), I32),
                        pltpu.VMEM((TOP_K, tm, LANES), F32),
                        pltpu.VMEM((TOP_K * tm, ROW_PARTS, LANES), U32),
                        pltpu.VMEM((tm, ROW_PARTS, LANES), F32),
                        pltpu.VMEM((tm, ROW_PARTS, LANES), F32),
                        pltpu.SemaphoreType.DMA,
                        pltpu.SemaphoreType.DMA,
                        pltpu.SemaphoreType.DMA],
        compiler_params=_params("arbitrary", "arbitrary"),
        name="combine",
    )(loc_flat, tab_flat, ys, slab, h2, x, gate, s1, s3, s2)


def _dft_tables(length, n_chan):
    scale = 1.0 / math.sqrt(length * n_chan)
    side = 1
    while side * side < length:
        side *= 2
    outer = length // side
    k = jnp.arange(length, dtype=I32)[:, None]
    a_idx = (k * jnp.arange(outer, dtype=I32)[None, :]) % outer
    b_idx = (k * jnp.arange(side, dtype=I32)[None, :]) % length
    ang_a = a_idx.astype(F32) * (2.0 * math.pi / outer)
    ang_b = b_idx.astype(F32) * (2.0 * math.pi / length)
    ca, sa = jnp.cos(ang_a)[:, :, None], jnp.sin(ang_a)[:, :, None]
    cb, sb = jnp.cos(ang_b)[:, None, :], jnp.sin(ang_b)[:, None, :]
    cos_t = (ca * cb - sa * sb).reshape(length, length)
    sin_t = (sa * cb + ca * sb).reshape(length, length)
    table = (jnp.concatenate([cos_t, -sin_t], axis=1) * scale).astype(BF16)
    return table


def _channel_table(n_chan, n_groups):
    m = jnp.arange(n_chan, dtype=I32)
    ang = ((m[:, None] * m[None, :]) % n_chan).astype(F32) * (2.0 * math.pi / n_chan)
    eye = jnp.eye(n_groups, dtype=F32)
    return jnp.concatenate([jnp.kron(eye, jnp.cos(ang)), jnp.kron(eye, jnp.sin(ang))], axis=1).astype(BF16)


def _rope_tables(n_tok):
    rows = n_tok // GRID_W
    axis_dim = HEAD_DIM // 2
    r = jnp.repeat(jnp.arange(rows, dtype=F32), GRID_W)
    col = jnp.tile(jnp.arange(GRID_W, dtype=F32), rows)
    inv = ROPE_BASE ** (-jnp.arange(0, axis_dim, 2, dtype=F32) / axis_dim)
    ar, ac = r[:, None] * inv, col[:, None] * inv
    cos = jnp.concatenate([jnp.cos(ar), jnp.cos(ar), jnp.cos(ac), jnp.cos(ac)], axis=1)
    sin = jnp.concatenate([-jnp.sin(ar), jnp.sin(ar), -jnp.sin(ac), jnp.sin(ac)], axis=1)
    return jnp.tile(cos, (1, N_KV)), jnp.tile(sin, (1, N_KV))


def _head_mean_matrix(width):
    h = jnp.arange(width) // HEAD_DIM
    return ((h[:, None] == h[None, :]).astype(F32) / HEAD_DIM).astype(BF16)


def _tile_matrices():
    src = jnp.arange(N_KV * HEAD_DIM)
    dst = jnp.arange(Q_PER_KV * HEAD_DIM)
    mats = [((src[:, None] // HEAD_DIM == h) & (src[:, None] % HEAD_DIM == dst[None, :] % HEAD_DIM))
            for h in range(N_KV)]
    return jnp.stack(mats).astype(BF16)


def _even_layer(x, xc, mod, modc, norm_g, w_in, ws, bs, w_out, tables):
    b, s, d = x.shape
    w = w_in.shape[1] // 3
    outs = []
    for stream, m, tm in ((x, mod, MIX_TM), (xc, modc, 256)):
        if stream is None:
            outs.append(None)
            continue
        length = stream.shape[1]
        tm = min(tm, length)
        gm, z = _even_mix(stream, norm_g, m[0], m[1], w_in, ws, bs, tables["chan"], tm)
        table = tables["pos"][length]
        y = _matmul(table, z.reshape(2 * length, b * w),
                    min(1024, length), min(1024, b * w), min(2048, 2 * length), BF16)
        outs.append(_outproj(
            gm, pl.BlockSpec((None, tm, w), lambda bi, i: (bi, i, 0)),
            y, pl.BlockSpec((tm, w), lambda bi, i: (i, bi)),
            w_out, stream, m[2], tm))
    return outs


def _odd_layer(x, xc, mod, modc, norm_g, w_in, qg, kg, sink, conv_w, w_out, tables, ctx_out):
    b, s, d = x.shape
    lc = xc.shape[1]
    half = N_HEADS * HEAD_DIM
    tm, tmc = MIX_TM, min(256, lc)
    prep = functools.partial(_odd_in, g=norm_g, w=w_in, qg=qg, kg=kg, bdq=tables["bdq"], bdk=tables["bdk"],
                             tile=tables["tile"], conv_w=conv_w)
    q, k4, v4, conv = prep(x, shift=mod[0], scale=mod[1], cos=tables["cos"], sin=tables["sin"], tm=tm, with_q=True)
    qc, kc4, vc4, convc = prep(xc, shift=modc[0], scale=modc[1], cos=tables["cos_c"], sin=tables["sin_c"],
                               tm=tmc, with_q=ctx_out)
    att = _attention(q, k4, v4, kc4, vc4, sink, band=True)
    spec = lambda t: pl.BlockSpec((None, t, half), lambda bi, i: (bi, i, 0))
    y = _outproj(att, spec(tm), conv, spec(tm), w_out, x, mod[2], tm)
    yc = None
    if ctx_out:
        attc = _attention(qc, None, None, kc4, vc4, sink, band=False)
        yc = _outproj(attc, spec(tmc), convc, spec(tmc), w_out, xc, modc[2], tmc)
    return y, yc


def _moe(x, xc, mod, modc, norm_g, rw_hi, rw_lo, rb, w1, w3, w2, layer, s1, s3, s2, tri, upper):
    b, s, d = x.shape
    n_lat = b * s
    xc_shape = None
    if xc is not None and xc.shape[1] % MOE_TM:
        xc_shape = xc.shape
        xc = xc.reshape(-1, MOE_TM, d)
        modc = [m[:xc.shape[0]] for m in modc]
    counts0 = jnp.zeros((1, LANES), F32)
    h2, slab, tab, counts = _route(x, norm_g, mod[3], mod[4], rw_hi, rw_lo, rb, counts0, tri, upper)
    slabs, tabs = [slab.reshape(n_lat, LANES)], [tab]
    n_tok = n_lat
    if xc is not None:
        h2c, slabc, tabc, counts = _route(xc, norm_g, modc[3], modc[4], rw_hi, rw_lo, rb, counts, tri, upper)
        slabs.append(slabc.reshape(-1, LANES))
        tabs.append(tabc)
        n_tok += slabs[1].shape[0]
    cnt = counts[0, :N_EXPERTS].astype(I32)
    blk = EXPERT_BLOCK
    padded = (cnt + blk - 1) // blk * blk
    ends = jnp.cumsum(padded).astype(I32)
    starts = ends - padded
    step_rows = blk * EXPERT_GROUP
    n_rows = (n_tok * TOP_K + N_EXPERTS * (blk - 1) + step_rows - 1) // step_rows * step_rows
    n_blocks = n_rows // blk
    n_used = (ends[-1] // blk).reshape(1).astype(I32)
    blk_start = jnp.minimum(jnp.arange(n_blocks, dtype=I32), n_used[0] - 1) * blk
    block_e = jnp.minimum(jnp.sum(blk_start[:, None] >= ends[None, :], axis=1), N_EXPERTS - 1).astype(I32)
    slab_all = jnp.concatenate(slabs, axis=0)
    tab_all = jnp.concatenate(tabs, axis=0)[:, :, :N_EXPERTS].astype(I32)
    runs = jnp.concatenate([tab_all[:, 0], tab_all[:, 1], tab_all[:, 2] + starts[None, :]], axis=1)
    tab_flat = jnp.pad(runs, ((0, 0), (0, TAB_WORDS - runs.shape[1]))).reshape(-1)
    loc = (slab_all[:, SLAB_LOC:SLAB_LOC + 8].astype(I32) * ROW_PARTS).reshape(-1, MOE_TM, 8)
    loc_flat = jnp.swapaxes(loc, 1, 2).reshape(-1)
    packed = lambda a: a.reshape(-1, ROW_PARTS, LANES)
    xs = _dispatch(packed(h2), loc_flat, tab_flat, ends, n_used, None, n_rows, 0)
    if xc is not None:
        xs = _dispatch(packed(h2c), loc_flat, tab_flat, ends, n_used, xs, n_rows, n_lat // MOE_TM)
    ys = _experts(xs, block_e, n_used, w1, w3, w2, layer)
    x_new = _combine(ys, loc_flat, tab_flat, slab, h2, x, mod[5], s1, s3, s2, 0)
    xc_new = None
    if xc is not None:
        xc_new = _combine(ys, loc_flat, tab_flat, slabc, h2c, xc, modc[5], s1, s3, s2, n_lat // MOE_TM)
        if xc_shape is not None:
            xc_new = xc_new.reshape(xc_shape)
    return x_new, xc_new


def kernel(x, c, ctx, c_ctx, ada_w, ada_b, norm1_g, norm2_g, ev_w_in, ev_w_s, ev_b_s, ev_w_out, od_w_in, od_q_norm_g, od_k_norm_g, od_sink, od_conv_w, od_w_out, router_w, router_b, exp_w_gate, exp_w_up, exp_w_down, sh_w_gate, sh_w_up, sh_w_down):
    b, s, d = x.shape
    lc = ctx.shape[1]
    depth = ada_w.shape[0]
    n_groups = ev_w_s.shape[1]
    half = d // 2

    rows = -(-(b + 1) // 8) * 8
    cond = jnp.zeros((rows, d), F32).at[:b].set(c).at[b].set(c_ctx)
    mod_all = _adaln(cond, ada_w, ada_b)

    tables = {
        "chan": _channel_table(LANES, n_groups),
        "pos": {s: _dft_tables(s, LANES), lc: _dft_tables(lc, LANES)},
        "bdq": _head_mean_matrix(N_HEADS * HEAD_DIM),
        "bdk": _head_mean_matrix(N_KV * HEAD_DIM),
        "tile": _tile_matrices(),
    }
    tables["cos"], tables["sin"] = _rope_tables(s)
    tables["cos_c"] = jnp.ones((lc, N_KV * HEAD_DIM), F32)
    tables["sin_c"] = jnp.zeros((lc, N_KV * HEAD_DIM), F32)
    tri = (jnp.arange(MOE_TM)[:, None] > jnp.arange(MOE_TM)[None, :]).astype(BF16)
    upper = (jnp.arange(LANES)[:, None] < jnp.arange(LANES)[None, :]).astype(BF16)

    qw, kw = N_HEADS * HEAD_DIM, N_KV * HEAD_DIM
    perm = jnp.concatenate([jnp.arange(0, qw), jnp.arange(qw + 2 * kw, qw + 2 * kw + 3 * half),
                            jnp.arange(qw, qw + 2 * kw)])

    w1_all, w3_all, w2_all = exp_w_gate.astype(BF16), exp_w_up.astype(BF16), exp_w_down.astype(BF16)
    xc = ctx
    for l in range(depth):
        last = l == depth - 1
        even = l % 2 == 0
        need_ctx = not (last and even)
        pieces = [mod_all[l, :, j * d:(j + 1) * d] for j in range(6)]
        mod = [p[:b].reshape(b, 1, d) for p in pieces]
        modc = [jnp.broadcast_to(p[b].reshape(1, 1, d), (b, 1, d)) for p in pieces]
        g1 = norm1_g[l].reshape(1, d)
        g2 = norm2_g[l].reshape(1, d)
        if even:
            e = l // 2
            bs = jnp.broadcast_to(ev_b_s[e][:, :, None], (n_groups, CHUNK, LANES))
            y, yc = _even_layer(x, xc if (need_ctx and not last) else None, mod, modc, g1,
                                ev_w_in[e].astype(BF16), ev_w_s[e].astype(BF16), bs,
                                ev_w_out[e].astype(BF16), tables)
        else:
            o = l // 2
            qg = (jnp.tile(od_q_norm_g[o], N_HEADS) * (HEAD_DIM ** -0.5)).reshape(1, qw)
            kg = jnp.tile(od_k_norm_g[o], N_KV).reshape(1, kw)
            y, yc = _odd_layer(x, xc, mod, modc, g1, od_w_in[o][:, perm].astype(BF16), qg, kg,
                               od_sink[o], od_conv_w[o], od_w_out[o].astype(BF16), tables, not last)
        x = y
        if not last:
            xc = yc
        rw = jnp.zeros((d, LANES), F32).at[:, :N_EXPERTS].set(router_w[l])
        rw_hi = rw.astype(BF16)
        rw_lo = (rw - rw_hi.astype(F32)).astype(BF16)
        rb = jnp.zeros((1, LANES), F32).at[0, :N_EXPERTS].set(router_b[l])
        x, xc_new = _moe(x, None if last else xc, mod, modc, g2, rw_hi, rw_lo, rb, w1_all, w3_all, w2_all, l,
                         sh_w_gate[l].astype(BF16), sh_w_up[l].astype(BF16), sh_w_down[l].astype(BF16),
                         tri, upper)
        if not last:
            xc = xc_new
    return x
```

```python
import functools
import math

import jax
import jax.numpy as jnp
from jax import lax
from jax.experimental import pallas as pl
from jax.experimental.pallas import tpu as pltpu

F32 = jnp.float32
BF16 = jnp.bfloat16
I32 = jnp.int32
U32 = jnp.uint32

LANES = 128
VMEM_LIMIT = 48 * 2**20

EPS = 1e-6
GRID_W = 64
CHUNK = 128
HEAD_DIM = 64
N_HEADS = 8
N_KV = 2
Q_PER_KV = N_HEADS // N_KV
MIX_TM = 1024
ATT_BLOCK = 128
ATT_ROWS = 64
ATT_GROUP = 4
ROPE_BASE = 10000.0
N_EXPERTS = 64
TOP_K = 6
ROUTED_SCALE = 2.5
EXPERT_BLOCK = 512
EXPERT_GROUP = 2
MOE_TM = 1024
ROW_PARTS = 4
SLAB_IDX, SLAB_LOC, SLAB_GATE = 0, 8, 16
TAB_WORDS = 1024


def _params(*sem):
    return pltpu.CompilerParams(dimension_semantics=sem, vmem_limit_bytes=VMEM_LIMIT)


def _sigmoid(x):
    return 1.0 / (1.0 + jnp.exp(-x))


def _silu(x):
    return x * _sigmoid(x)


def _gelu_tanh(x):
    c = math.sqrt(2.0 / math.pi)
    return x * (0.5 * (1.0 + jnp.tanh(c * (x + 0.044715 * (x * x * x)))))


def _dot(a, b):
    return jnp.dot(a, b, preferred_element_type=F32)


def _unpack_words(w):
    return pltpu.bitcast(w << 16, F32), pltpu.bitcast(w & jnp.uint32(0xFFFF0000), F32)


def _row_chunks(ref, row0=0, n=None):
    total, parts, lanes = ref.shape
    n = total if n is None else n
    flat = ref.reshape(total * parts, lanes)
    return [flat[pl.ds(row0 * parts + c, n, stride=parts), :] for c in range(parts)]


def _unpack_rows(ref, row0=0, n=None):
    halves = [_unpack_words(w) for w in _row_chunks(ref, row0, n)]
    return jnp.concatenate([h[0] for h in halves] + [h[1] for h in halves], axis=-1)


def _pack_rows(ref, val, row0=0):
    total, parts, lanes = ref.shape
    n, half = val.shape[0], val.shape[1] // 2
    bits = pltpu.bitcast(val.astype(BF16).astype(F32), U32)
    words = (bits[:, :half] >> 16) | (bits[:, half:] & jnp.uint32(0xFFFF0000))
    flat = ref.reshape(total * parts, lanes)
    for c in range(parts):
        flat[pl.ds(row0 * parts + c, n, stride=parts), :] = words[:, c * lanes:(c + 1) * lanes]


def _adaln_kernel(c_ref, w_ref, b_ref, o_ref):
    o_ref[...] = _dot(_silu(c_ref[...]), w_ref[...]) + b_ref[...]


def _adaln(cond, ada_w, ada_b):
    n_layers, d, n6 = ada_w.shape
    rows = cond.shape[0]
    tn = 1536
    return pl.pallas_call(
        _adaln_kernel,
        grid=(n_layers, n6 // tn),
        in_specs=[pl.BlockSpec((rows, d), lambda l, j: (0, 0)),
                  pl.BlockSpec((None, d, tn), lambda l, j: (l, 0, j)),
                  pl.BlockSpec((None, 1, tn), lambda l, j: (l, 0, j))],
        out_specs=pl.BlockSpec((None, rows, tn), lambda l, j: (l, 0, j)),
        out_shape=jax.ShapeDtypeStruct((n_layers, rows, n6), F32),
        compiler_params=_params("parallel", "parallel"),
        name="adaln",
    )(cond, ada_w, ada_b.reshape(n_layers, 1, n6))


def _norm_mod(x, g, shift, scale):
    ms = jnp.mean(x * x, axis=-1, keepdims=True)
    h = (x * lax.rsqrt(ms + EPS)) * g
    return h * (1.0 + scale) + shift


def _even_mix_kernel(x_ref, g_ref, sh_ref, sc_ref, w_ref, ws_ref, bs_ref, cs_ref, gm_ref, z_ref, p_ref,
                     *, n_chunks, n_groups):
    h = _norm_mod(x_ref[...], g_ref[...], sh_ref[...], sc_ref[...])
    p_ref[...] = _dot(h.astype(BF16), w_ref[...])
    half = gm_ref.shape[1]
    for c in range(n_chunks):
        rows = slice(c * CHUNK, (c + 1) * CHUNK)
        for g in range(n_groups):
            cols = slice(g * LANES, (g + 1) * LANES)
            ug = _gelu_tanh(p_ref[rows, cols])
            vg = _gelu_tanh(p_ref[rows, half + g * LANES:half + (g + 1) * LANES])
            mu = jnp.mean(vg, axis=-1, keepdims=True)
            dv = vg - mu
            var = jnp.mean(dv * dv, axis=-1, keepdims=True)
            vn = dv * lax.rsqrt(var + 1e-5)
            fg = _dot(ws_ref[g], vn.astype(BF16)) + bs_ref[g]
            gm_ref[rows, cols] = (ug * fg).astype(gm_ref.dtype)
    fz = _dot(p_ref[:, 2 * half:].astype(BF16), cs_ref[...])
    z_ref[0] = fz[:, :half].astype(z_ref.dtype)
    z_ref[1] = fz[:, half:].astype(z_ref.dtype)


def _even_mix(x, g, shift, scale, w_in, ws, bs, cs, tm):
    b, s, d = x.shape
    n3 = w_in.shape[1]
    w = n3 // 3
    n_groups = w // LANES
    kern = functools.partial(_even_mix_kernel, n_chunks=tm // CHUNK, n_groups=n_groups)
    return pl.pallas_call(
        kern,
        grid=(b, s // tm),
        in_specs=[pl.BlockSpec((None, tm, d), lambda bi, i: (bi, i, 0)),
                  pl.BlockSpec((1, d), lambda bi, i: (0, 0)),
                  pl.BlockSpec((None, 1, d), lambda bi, i: (bi, 0, 0)),
                  pl.BlockSpec((None, 1, d), lambda bi, i: (bi, 0, 0)),
                  pl.BlockSpec((d, n3), lambda bi, i: (0, 0)),
                  pl.BlockSpec(ws.shape, lambda bi, i: (0, 0, 0)),
                  pl.BlockSpec(bs.shape, lambda bi, i: (0, 0, 0)),
                  pl.BlockSpec(cs.shape, lambda bi, i: (0, 0))],
        out_specs=[pl.BlockSpec((None, tm, w), lambda bi, i: (bi, i, 0)),
                   pl.BlockSpec((2, tm, w), lambda bi, i: (0, i, bi))],
        out_shape=[jax.ShapeDtypeStruct((b, s, w), BF16),
                   jax.ShapeDtypeStruct((2, s, b * w), BF16)],
        scratch_shapes=[pltpu.VMEM((tm, n3), F32)],
        compiler_params=_params("parallel", "parallel"),
        name="even_mix",
    )(x, g, shift, scale, w_in, ws, bs, cs)


def _mm_kernel(a_ref, b_ref, o_ref, acc_ref):
    k = pl.program_id(2)

    @pl.when(k == 0)
    def _():
        acc_ref[...] = jnp.zeros_like(acc_ref)

    acc_ref[...] += _dot(a_ref[...], b_ref[...])

    @pl.when(k == pl.num_programs(2) - 1)
    def _():
        o_ref[...] = acc_ref[...].astype(o_ref.dtype)


def _matmul(a, b, tm, tn, tk, out_dtype):
    m, kd = a.shape
    n = b.shape[1]
    return pl.pallas_call(
        _mm_kernel,
        grid=(m // tm, n // tn, kd // tk),
        in_specs=[pl.BlockSpec((tm, tk), lambda i, j, k: (i, k)),
                  pl.BlockSpec((tk, tn), lambda i, j, k: (k, j))],
        out_specs=pl.BlockSpec((tm, tn), lambda i, j, k: (i, j)),
        out_shape=jax.ShapeDtypeStruct((m, n), out_dtype),
        scratch_shapes=[pltpu.VMEM((tm, tn), F32)],
        compiler_params=_params("parallel", "parallel", "arbitrary"),
        name="dft_matmul",
    )(a, b)


def _outproj_kernel(a_ref, b_ref, w_ref, x_ref, gate_ref, o_ref):
    ab = jnp.concatenate([a_ref[...], b_ref[...]], axis=-1)
    o_ref[...] = x_ref[...] + gate_ref[...] * _dot(ab, w_ref[...])


def _outproj(a, a_spec, bsrc, b_spec, w, x, gate, tm):
    b, s, d = x.shape
    return pl.pallas_call(
        _outproj_kernel,
        grid=(b, s // tm),
        in_specs=[a_spec, b_spec,
                  pl.BlockSpec(w.shape, lambda bi, i: (0, 0)),
                  pl.BlockSpec((None, tm, d), lambda bi, i: (bi, i, 0)),
                  pl.BlockSpec((None, 1, d), lambda bi, i: (bi, 0, 0))],
        out_specs=pl.BlockSpec((None, tm, d), lambda bi, i: (bi, i, 0)),
        out_shape=jax.ShapeDtypeStruct((b, s, d), F32),
        compiler_params=_params("parallel", "parallel"),
        name="outproj",
    )(a, bsrc, w, x, gate)


def _head_rms(x, bd_ref):
    xx = x * x
    hi = xx.astype(BF16)
    lo = (xx - hi.astype(F32)).astype(BF16)
    ms = _dot(hi, bd_ref[...]) + _dot(lo, bd_ref[...])
    return x * lax.rsqrt(ms + EPS)


def _rope(x, cos, sins):
    width = x.shape[1]
    lane = lax.broadcasted_iota(I32, x.shape, 1)
    first = (lane & 31) < 16
    swapped = jnp.where(first, pltpu.roll(x, width - 16, 1), pltpu.roll(x, 16, 1))
    return x * cos + swapped * sins


HALO = 8


def _odd_in_kernel(x_ref, xp_ref, xn_ref, g_ref, sh_ref, sc_ref, w_ref, cos_ref, sin_ref, qg_ref, kg_ref,
                   bdq_ref, bdk_ref, tile_ref, tile_t_ref, cw_ref, qo_ref, k4_ref, v4_ref, co_ref, p_ref, *, with_q):
    i = pl.program_id(1)
    n_tiles = pl.num_programs(1)
    tm = x_ref.shape[0]
    qw, dc, kw = qo_ref.shape[1], co_ref.shape[1], kg_ref.shape[1]
    g, sh, sc = g_ref[...], sh_ref[...], sc_ref[...]
    p_ref[...] = _dot(_norm_mod(x_ref[...], g, sh, sc).astype(BF16), w_ref[...])
    cos, sin = cos_ref[...], sin_ref[...]
    if with_q:
        reps = qw // kw
        qn = _head_rms(p_ref[:, :qw], bdq_ref) * qg_ref[...]
        qo_ref[...] = _rope(qn, jnp.concatenate([cos] * reps, axis=1),
                            jnp.concatenate([sin] * reps, axis=1)).astype(qo_ref.dtype)
    else:
        qo_ref[...] = jnp.zeros_like(qo_ref)
    k0 = qw + 3 * dc
    kn = _head_rms(p_ref[:, k0:k0 + kw], bdk_ref) * kg_ref[...]
    kr = _rope(kn, cos, sin).astype(BF16)
    vb = p_ref[:, k0 + kw:k0 + 2 * kw].astype(BF16)
    for h in range(N_KV):
        k4_ref[h] = lax.dot_general(tile_t_ref[h], kr, (((1,), (1,)), ((), ())),
                                    preferred_element_type=F32).astype(k4_ref.dtype)
        v4_ref[h] = _dot(vb, tile_ref[h]).astype(v4_ref.dtype)
    halo = _norm_mod(jnp.concatenate([xp_ref[...], xn_ref[...]], axis=0), g, sh, sc).astype(BF16)
    gz = _dot(halo, w_ref[:, qw:qw + dc]) * _dot(halo, w_ref[:, qw + 2 * dc:qw + 3 * dc])
    before = jnp.where(i > 0, gz[HALO - 1:HALO, :], 0.0)
    after = jnp.where(i < n_tiles - 1, gz[HALO:HALO + 1, :], 0.0)
    ridx = lax.broadcasted_iota(I32, (tm, LANES), 0)
    for c in range(dc // LANES):
        cols = slice(c * LANES, (c + 1) * LANES)
        zc = p_ref[:, qw + c * LANES:qw + (c + 1) * LANES] * p_ref[:, qw + 2 * dc + c * LANES:qw + 2 * dc + (c + 1) * LANES]
        zp = jnp.where(ridx == 0, before[:, cols], pltpu.roll(zc, 1, 0))
        zn = jnp.where(ridx == tm - 1, after[:, cols], pltpu.roll(zc, tm - 1, 0))
        y = zp * cw_ref[0:1, cols] + zc * cw_ref[1:2, cols] + zn * cw_ref[2:3, cols]
        co_ref[:, cols] = (p_ref[:, qw + dc + c * LANES:qw + dc + (c + 1) * LANES] * y).astype(co_ref.dtype)


def _odd_in(x, g, shift, scale, w, cos, sin, qg, kg, bdq, bdk, tile, conv_w, tm, with_q):
    b, s, d = x.shape
    n = w.shape[1]
    tile_t = jnp.swapaxes(tile, 1, 2)
    qw = N_HEADS * HEAD_DIM
    kw = N_KV * HEAD_DIM
    rep = Q_PER_KV * HEAD_DIM
    dc = conv_w.shape[1]
    per_tile = tm // HALO
    kern = functools.partial(_odd_in_kernel, with_q=with_q)
    return pl.pallas_call(
        kern,
        grid=(b, s // tm),
        in_specs=[pl.BlockSpec((None, tm, d), lambda bi, i: (bi, i, 0)),
                  pl.BlockSpec((None, HALO, d), lambda bi, i: (bi, jnp.maximum(i * per_tile - 1, 0), 0)),
                  pl.BlockSpec((None, HALO, d), lambda bi, i: (bi, jnp.minimum((i + 1) * per_tile, s // HALO - 1), 0)),
                  pl.BlockSpec((1, d), lambda bi, i: (0, 0)),
                  pl.BlockSpec((None, 1, d), lambda bi, i: (bi, 0, 0)),
                  pl.BlockSpec((None, 1, d), lambda bi, i: (bi, 0, 0)),
                  pl.BlockSpec((d, n), lambda bi, i: (0, 0)),
                  pl.BlockSpec((tm, kw), lambda bi, i: (i, 0)),
                  pl.BlockSpec((tm, kw), lambda bi, i: (i, 0)),
                  pl.BlockSpec((1, qw), lambda bi, i: (0, 0)),
                  pl.BlockSpec((1, kw), lambda bi, i: (0, 0)),
                  pl.BlockSpec(bdq.shape, lambda bi, i: (0, 0)),
                  pl.BlockSpec(bdk.shape, lambda bi, i: (0, 0)),
                  pl.BlockSpec(tile.shape, lambda bi, i: (0, 0, 0)),
                  pl.BlockSpec(tile_t.shape, lambda bi, i: (0, 0, 0)),
                  pl.BlockSpec(conv_w.shape, lambda bi, i: (0, 0))],
        out_specs=[pl.BlockSpec((None, tm, qw), lambda bi, i: (bi, i, 0)),
                   pl.BlockSpec((None, N_KV, rep, tm), lambda bi, i: (bi, 0, 0, i)),
                   pl.BlockSpec((None, N_KV, tm, rep), lambda bi, i: (bi, 0, i, 0)),
                   pl.BlockSpec((None, tm, dc), lambda bi, i: (bi, i, 0))],
        out_shape=[jax.ShapeDtypeStruct((b, s, qw), BF16),
                   jax.ShapeDtypeStruct((b, N_KV, rep, s), BF16),
                   jax.ShapeDtypeStruct((b, N_KV, s, rep), BF16),
                   jax.ShapeDtypeStruct((b, s, dc), BF16)],
        scratch_shapes=[pltpu.VMEM((tm, n), F32)],
        compiler_params=_params("parallel", "parallel"),
        name="odd_in",
    )(x, x, x, g, shift, scale, w, cos, sin, qg, kg, bdq, bdk, tile, tile_t, conv_w)


def _attn_kernel(sink_ref, q_ref, *refs, band, n_blocks, group):
    o_ref = refs[-1]
    i = pl.program_id(1)
    t = ATT_BLOCK
    w = Q_PER_KV * HEAD_DIM
    lane = lax.broadcasted_iota(I32, (t, w), 1)
    masks = [(lane >= g * HEAD_DIM) & (lane < (g + 1) * HEAD_DIM) for g in range(Q_PER_KV)]
    for j in range(group):
        blk = i * group + j
        bad = None
        if band:
            kp, km, kn, vp, vm, vn, kx, vx = refs[:-1]
            n_keys = 3 * t + vx.shape[1]
            row = lax.broadcasted_iota(I32, (Q_PER_KV * t, n_keys), 0) & (t - 1)
            col = lax.broadcasted_iota(I32, (Q_PER_KV * t, n_keys), 1)
            off_prev = jnp.where(blk > 0, 0, 4 * t)
            off_next = jnp.where(blk < n_blocks - 1, 0, 4 * t)
            bad_prev = (col < t) & (col < row + off_prev)
            bad_next = (col >= 2 * t) & (col < 3 * t) & (col - 2 * t > row - off_next)
            bad = bad_prev | bad_next
        for h in range(N_KV):
            if band:
                k_band = [kp[h] if j == 0 else km[h, :, (j - 1) * t:j * t], km[h, :, j * t:(j + 1) * t],
                          kn[h] if j == group - 1 else km[h, :, (j + 1) * t:(j + 2) * t]]
                v_band = [vp[h] if j == 0 else vm[h, (j - 1) * t:j * t, :], vm[h, j * t:(j + 1) * t, :],
                          vn[h] if j == group - 1 else vm[h, (j + 1) * t:(j + 2) * t, :]]
                kcat = jnp.concatenate(k_band + [kx[h]], axis=1)
                vcat = jnp.concatenate(v_band + [vx[h]], axis=0)
            else:
                kx, vx = refs[:-1]
                kcat, vcat = kx[h], vx[h]
            q = q_ref[j * t:(j + 1) * t, h * w:(h + 1) * w]
            q4 = jnp.concatenate([jnp.where(m, q, jnp.zeros_like(q)) for m in masks], axis=0)
            s = _dot(q4, kcat)
            es, invs = [], []
            for c in range(Q_PER_KV * t // ATT_ROWS):
                rows = slice(c * ATT_ROWS, (c + 1) * ATT_ROWS)
                sc = s[rows]
                if band:
                    sc = jnp.where(bad[rows], -jnp.inf, sc)
                sink = jnp.full((ATT_ROWS, 1), sink_ref[h * Q_PER_KV + (c * ATT_ROWS) // t], F32)
                m = jnp.maximum(jnp.max(sc, axis=-1, keepdims=True), sink)
                ec = jnp.exp(sc - m)
                invs.append(1.0 / (jnp.sum(ec, axis=-1, keepdims=True) + jnp.exp(sink - m)))
                es.append(ec.astype(BF16))
            e = jnp.concatenate(es, axis=0)
            r = _dot(e, vcat) * jnp.concatenate(invs, axis=0)
            o = jnp.zeros((t, w), F32)
            for g in range(Q_PER_KV):
                o = o + jnp.where(masks[g], r[g * t:(g + 1) * t, :], 0.0)
            o_ref[j * t:(j + 1) * t, h * w:(h + 1) * w] = o.astype(o_ref.dtype)


def _attention(q, k4, v4, kx4, vx4, sink, band):
    b, s, qw = q.shape
    rep = vx4.shape[-1]
    lc = vx4.shape[2]
    t = ATT_BLOCK
    nb = s // t
    group = ATT_GROUP if band and nb % ATT_GROUP == 0 else 1
    kern = functools.partial(_attn_kernel, band=band, n_blocks=nb, group=group)

    def edge(off):
        return lambda i: jnp.clip(i * group + off, 0, nb - 1)

    def k_spec(width, blk):
        return pl.BlockSpec((None, N_KV, rep, width), lambda bi, i: (bi, 0, 0, blk(i)))

    def v_spec(width, blk):
        return pl.BlockSpec((None, N_KV, width, rep), lambda bi, i: (bi, 0, blk(i), 0))

    in_specs = [pl.BlockSpec(memory_space=pltpu.SMEM),
                pl.BlockSpec((None, group * t, qw), lambda bi, i: (bi, i, 0))]
    args = [sink, q]
    if band:
        own = lambda i: i
        in_specs += [k_spec(t, edge(-1)), k_spec(group * t, own), k_spec(t, edge(group)),
                     v_spec(t, edge(-1)), v_spec(group * t, own), v_spec(t, edge(group))]
        args += [k4, k4, k4, v4, v4, v4]
    in_specs += [pl.BlockSpec((None, N_KV, rep, lc), lambda bi, i: (bi, 0, 0, 0)),
                 pl.BlockSpec((None, N_KV, lc, rep), lambda bi, i: (bi, 0, 0, 0))]
    args += [kx4, vx4]
    return pl.pallas_call(
        kern,
        grid=(b, nb // group),
        in_specs=in_specs,
        out_specs=pl.BlockSpec((None, group * t, qw), lambda bi, i: (bi, i, 0)),
        out_shape=jax.ShapeDtypeStruct((b, s, qw), BF16),
        compiler_params=_params("parallel", "parallel"),
        name="attention_band" if band else "attention_ctx",
    )(*args)


def _route_kernel(x_ref, g_ref, sh_ref, sc_ref, whi_ref, wlo_ref, rb_ref, cin_ref, tri_ref, upper_ref,
                  h_ref, slab_ref, tab_ref, cnt_ref, carry_ref):
    first = (pl.program_id(0) == 0) & (pl.program_id(1) == 0)

    @pl.when(first)
    def _():
        carry_ref[...] = cin_ref[...]

    h = _norm_mod(x_ref[...], g_ref[...], sh_ref[...], sc_ref[...])
    _pack_rows(h_ref, h)
    hi = h.astype(BF16)
    lo = (h - hi.astype(F32)).astype(BF16)
    logits = _dot(hi, whi_ref[...]) + _dot(lo, whi_ref[...]) + _dot(hi, wlo_ref[...])
    scores = _sigmoid(logits)
    tm, lanes = scores.shape
    lane = lax.broadcasted_iota(I32, (tm, lanes), 1).astype(F32)
    work = jnp.where(lane < N_EXPERTS, scores + rb_ref[...], -jnp.inf)
    hits, idxs, gates = [], [], []
    for _ in range(TOP_K):
        mx = jnp.max(work, axis=-1, keepdims=True)
        idx = jnp.min(jnp.where(work == mx, lane, float(lanes)), axis=-1, keepdims=True)
        hit = lane == idx
        hits.append(hit)
        idxs.append(idx)
        gates.append(jnp.sum(jnp.where(hit, scores, 0.0), axis=-1, keepdims=True))
        work = jnp.where(hit, -jnp.inf, work)
    gsum = gates[0]
    for gk in gates[1:]:
        gsum = gsum + gk
    gscale = ROUTED_SCALE / (gsum + 1e-20)
    onehot = jnp.zeros((tm, lanes), F32)
    for hit in hits:
        onehot = jnp.where(hit, 1.0, onehot)
    cnt = jnp.sum(onehot, axis=0, keepdims=True)
    cnt_hi = jnp.floor(cnt * (1.0 / 256.0))
    cnt_lo = cnt - 256.0 * cnt_hi
    parts = jnp.concatenate([jnp.broadcast_to(cnt_hi, (8, lanes)), jnp.broadcast_to(cnt_lo, (8, lanes))], axis=0)
    sums = _dot(parts.astype(BF16), upper_ref[...])
    seg = 256.0 * sums[0:1] + sums[8:9]
    before = _dot(tri_ref[...], onehot.astype(BF16)) + seg
    slab = jnp.zeros((tm, lanes), F32)
    for k in range(TOP_K):
        loc = jnp.sum(jnp.where(hits[k], before, 0.0), axis=-1, keepdims=True)
        slab = jnp.where(lane == SLAB_IDX + k, idxs[k], slab)
        slab = jnp.where(lane == SLAB_LOC + k, loc, slab)
        slab = jnp.where(lane == SLAB_GATE + k, gates[k] * gscale, slab)
    slab_ref[...] = slab
    row = lax.broadcasted_iota(I32, (8, lanes), 0)
    tab_ref[...] = jnp.where(row == 0, cnt, jnp.where(row == 1, seg, jnp.where(row == 2, carry_ref[...], 0.0)))
    carry_ref[...] = carry_ref[...] + cnt
    cnt_ref[...] = carry_ref[...]


def _route(x, g, shift, scale, whi, wlo, rb, counts_in, tri, upper):
    b, s, d = x.shape
    tm = MOE_TM
    nt = s // tm
    return pl.pallas_call(
        _route_kernel,
        grid=(b, nt),
        in_specs=[pl.BlockSpec((None, tm, d), lambda bi, i: (bi, i, 0)),
                  pl.BlockSpec((1, d), lambda bi, i: (0, 0)),
                  pl.BlockSpec((None, 1, d), lambda bi, i: (bi, 0, 0)),
                  pl.BlockSpec((None, 1, d), lambda bi, i: (bi, 0, 0)),
                  pl.BlockSpec(whi.shape, lambda bi, i: (0, 0)),
                  pl.BlockSpec(wlo.shape, lambda bi, i: (0, 0)),
                  pl.BlockSpec((1, LANES), lambda bi, i: (0, 0)),
                  pl.BlockSpec((1, LANES), lambda bi, i: (0, 0)),
                  pl.BlockSpec((tm, tm), lambda bi, i: (0, 0)),
                  pl.BlockSpec((LANES, LANES), lambda bi, i: (0, 0))],
        out_specs=[pl.BlockSpec((None, tm, ROW_PARTS, LANES), lambda bi, i: (bi, i, 0, 0)),
                   pl.BlockSpec((None, tm, LANES), lambda bi, i: (bi, i, 0)),
                   pl.BlockSpec((None, 8, LANES), lambda bi, i: (bi * nt + i, 0, 0)),
                   pl.BlockSpec((1, LANES), lambda bi, i: (0, 0))],
        out_shape=[jax.ShapeDtypeStruct((b, s, ROW_PARTS, LANES), U32),
                   jax.ShapeDtypeStruct((b, s, LANES), F32),
                   jax.ShapeDtypeStruct((b * nt, 8, LANES), F32),
                   jax.ShapeDtypeStruct((1, LANES), F32)],
        scratch_shapes=[pltpu.VMEM((1, LANES), F32)],
        compiler_params=_params("arbitrary", "arbitrary"),
        name="route",
    )(x, g, shift, scale, whi, wlo, rb, counts_in, tri, upper)


def _rows(ref, row, n):
    return ref.at[pl.ds(row, n)]


def _run_copies(tab_smem, base, stage_ref, far_ref, sem, to_far):
    n_bits = MOE_TM.bit_length()
    common = n_bits - 3

    def per_expert(e, carry):
        n = tab_smem[base + e]
        near = tab_smem[base + N_EXPERTS + e]
        far = tab_smem[base + 2 * N_EXPERTS + e]

        def piece(bit):
            size = 1 << bit

            @pl.when((n & size) != 0)
            def _():
                done = n & (size - 1)
                a, b = _rows(stage_ref, near + done, size), _rows(far_ref, far + done, size)
                (pltpu.make_async_copy(a, b, sem) if to_far else pltpu.make_async_copy(b, a, sem)).start()

        for bit in range(common):
            piece(bit)

        @pl.when(n >= (1 << common))
        def _():
            for bit in range(common, n_bits):
                piece(bit)
        return carry

    lax.fori_loop(0, N_EXPERTS, per_expert, 0)


def _loc_copies(loc_ref, tile, loc_smems, sem):
    return [pltpu.make_async_copy(loc_ref.at[pl.ds((tile * 8 + k) * MOE_TM, MOE_TM)], loc_smems[k], sem)
            for k in range(TOP_K)]


def _wait_tile(stage_ref, far_ref, sem, to_far):
    n = TOP_K * MOE_TM
    a, b = _rows(stage_ref, 0, n), _rows(far_ref, 0, n)
    (pltpu.make_async_copy(a, b, sem) if to_far else pltpu.make_async_copy(b, a, sem)).wait()


def _dispatch_kernel(ends_ref, nu_ref, from_ref, h_ref, loc_ref, tab_ref, *rest, zero_fill, tile_base, n_blocks):
    if zero_fill:
        xs_ref, rest = rest[0], rest[1:]
    else:
        xs_ref, rest = rest[1], rest[2:]
    loc_smems, rest = rest[:TOP_K], rest[TOP_K:]
    if zero_fill:
        tab_smem, stages_ref, zero_ref, sems, psem = rest
    else:
        tab_smem, stages_ref, sems, psem = rest
        zero_ref = None
    tm = MOE_TM
    i = pl.program_id(0)
    n_steps = pl.num_programs(0)
    slot = i % 2
    stage_ref, sem = stages_ref.at[slot], sems.at[slot]

    def table_copies(j):
        return _loc_copies(loc_ref, tile_base + j, loc_smems, psem) + [
            pltpu.make_async_copy(tab_ref.at[pl.ds((tile_base + j) * TAB_WORDS, TAB_WORDS)], tab_smem, psem)]

    @pl.when(i == 0)
    def _():
        for cp in table_copies(0):
            cp.start()

    if zero_fill:
        @pl.when(i == 0)
        def _():
            zero_ref[...] = jnp.zeros_like(zero_ref)

            def block_copy(blk):
                return pltpu.make_async_copy(zero_ref, _rows(xs_ref, blk * EXPERT_BLOCK, EXPERT_BLOCK), sem)

            def tail(j, carry, *, start):
                cp = block_copy(j)
                cp.start() if start else cp.wait()
                return carry

            def fill(e, carry, *, start):
                return lax.fori_loop(from_ref[e], ends_ref[e] // EXPERT_BLOCK,
                                     functools.partial(tail, start=start), carry)

            lax.fori_loop(0, N_EXPERTS, functools.partial(fill, start=True), 0)
            lax.fori_loop(nu_ref[0], n_blocks, functools.partial(tail, start=True), 0)
            lax.fori_loop(0, N_EXPERTS, functools.partial(fill, start=False), 0)
            lax.fori_loop(nu_ref[0], n_blocks, functools.partial(tail, start=False), 0)

    for cp in table_copies(i):
        cp.wait()

    @pl.when(i >= 2)
    def _():
        _wait_tile(stage_ref, xs_ref, sem, to_far=True)

    flat = stage_ref.reshape(TOP_K * tm * ROW_PARTS, LANES)
    h_flat = h_ref.reshape(tm * ROW_PARTS, LANES)

    def place(t, c):
        row = h_flat[pl.ds(pl.multiple_of(t * ROW_PARTS, ROW_PARTS), ROW_PARTS), :]
        for k in range(TOP_K):
            flat[pl.ds(pl.multiple_of(loc_smems[k][t], ROW_PARTS), ROW_PARTS), :] = row
        return c

    lax.fori_loop(0, tm, place, 0, unroll=4)
    _run_copies(tab_smem, 0, stage_ref, xs_ref, sem, to_far=True)

    @pl.when(i + 1 < n_steps)
    def _():
        for cp in table_copies(i + 1):
            cp.start()

    @pl.when(i == n_steps - 1)
    def _():
        _wait_tile(stage_ref, xs_ref, sem, to_far=True)

        @pl.when(i >= 1)
        def _():
            _wait_tile(stages_ref.at[1 - slot], xs_ref, sems.at[1 - slot], to_far=True)


def _dispatch(h2, loc_flat, tab_flat, ends, n_used, clear_from, xs_prev, n_rows, tile_base):
    n = h2.shape[0]
    tm = MOE_TM
    zero_fill = xs_prev is None
    kern = functools.partial(_dispatch_kernel, zero_fill=zero_fill, tile_base=tile_base,
                             n_blocks=n_rows // EXPERT_BLOCK)
    in_specs = [pl.BlockSpec((tm, ROW_PARTS, LANES), lambda i, e, nu, cf: (i, 0, 0)),
                pl.BlockSpec(memory_space=pl.ANY),
                pl.BlockSpec(memory_space=pl.ANY)]
    args = [ends, n_used, clear_from, h2, loc_flat, tab_flat]
    scratch = [pltpu.SMEM((tm,), I32)] * TOP_K + [pltpu.SMEM((TAB_WORDS,), I32),
               pltpu.VMEM((2, TOP_K * tm, ROW_PARTS, LANES), U32)]
    aliases = {}
    if zero_fill:
        scratch.append(pltpu.VMEM((EXPERT_BLOCK, ROW_PARTS, LANES), U32))
    else:
        in_specs.append(pl.BlockSpec(memory_space=pl.ANY))
        args.append(xs_prev)
        aliases = {6: 0}
    scratch += [pltpu.SemaphoreType.DMA((2,)), pltpu.SemaphoreType.DMA]
    return pl.pallas_call(
        kern,
        grid_spec=pltpu.PrefetchScalarGridSpec(
            num_scalar_prefetch=3,
            grid=(n // tm,),
            in_specs=in_specs,
            out_specs=pl.BlockSpec(memory_space=pl.ANY),
            scratch_shapes=scratch),
        out_shape=jax.ShapeDtypeStruct((n_rows, ROW_PARTS, LANES), U32),
        input_output_aliases=aliases,
        compiler_params=_params("arbitrary"),
        name="dispatch",
    )(*args)


def _expert_kernel(be_ref, nu_ref, x_ref, *refs):
    o_ref = refs[-1]
    blk = EXPERT_BLOCK
    used = pl.program_id(0) * EXPERT_GROUP < nu_ref[0]

    @pl.when(used)
    def _():
        for j in range(EXPERT_GROUP):
            w1_ref, w3_ref, w2_ref = refs[3 * j:3 * j + 3]
            x = _unpack_rows(x_ref, j * blk, blk).astype(BF16)
            a = _silu(_dot(x, w1_ref[...])) * _dot(x, w3_ref[...])
            _pack_rows(o_ref, _dot(a.astype(BF16), w2_ref[...]), j * blk)

    @pl.when(jnp.logical_not(used))
    def _():
        o_ref[...] = jnp.zeros_like(o_ref)


def _experts(xs, block_e, n_used, w1, w3, w2, layer):
    d, de = w1.shape[2:]
    rows = EXPERT_BLOCK * EXPERT_GROUP
    n_steps = xs.shape[0] // rows

    def row_map(i, be, nu):
        return (jnp.minimum(i, (nu[0] + EXPERT_GROUP - 1) // EXPERT_GROUP - 1), 0, 0)

    w_specs = []
    for j in range(EXPERT_GROUP):
        w_map = lambda i, be, nu, j=j: (layer, be[i * EXPERT_GROUP + j], 0, 0)
        w_specs += [pl.BlockSpec((None, None, d, de), w_map),
                    pl.BlockSpec((None, None, d, de), w_map),
                    pl.BlockSpec((None, None, de, d), w_map)]

    return pl.pallas_call(
        _expert_kernel,
        grid_spec=pltpu.PrefetchScalarGridSpec(
            num_scalar_prefetch=2,
            grid=(n_steps,),
            in_specs=[pl.BlockSpec((rows, ROW_PARTS, LANES), row_map)] + w_specs,
            out_specs=pl.BlockSpec((rows, ROW_PARTS, LANES), lambda i, be, nu: (i, 0, 0))),
        out_shape=jax.ShapeDtypeStruct(xs.shape, U32),
        compiler_params=_params("arbitrary"),
        name="experts",
    )(block_e, n_used, xs, *([w1, w3, w2] * EXPERT_GROUP))


def _combine_kernel(loc_ref, tab_ref, ys_ref, slab_ref, h_ref, x_ref, gate_ref, s1_ref, s3_ref, s2_ref,
                    o_ref, *scratch, tile_base):
    loc_smems = scratch[:TOP_K]
    tab_smem, gk_ref, stage_ref, lo_ref, hi_ref, sem, psem, tsem = scratch[TOP_K:]
    tm = MOE_TM
    step = pl.program_id(0) * pl.num_programs(1) + pl.program_id(1)
    n_steps = pl.num_programs(0) * pl.num_programs(1)

    def loc_copies(j):
        return _loc_copies(loc_ref, tile_base + j, loc_smems, psem)

    def tab_copy(j):
        return pltpu.make_async_copy(tab_ref.at[pl.ds((tile_base + j) * TAB_WORDS, TAB_WORDS)], tab_smem, tsem)

    def gather(j):
        tab_copy(j).wait()
        _run_copies(tab_smem, 0, stage_ref, ys_ref, sem, to_far=False)
        for cp in loc_copies(j):
            cp.start()

    @pl.when(step == 0)
    def _():
        tab_copy(0).start()
        gather(0)

    hb = _unpack_rows(h_ref).astype(BF16)
    shared = _dot((_silu(_dot(hb, s1_ref[...])) * _dot(hb, s3_ref[...])).astype(BF16), s2_ref[...])
    slab = slab_ref[...]
    for k in range(TOP_K):
        gk_ref[k] = jnp.broadcast_to(slab[:, SLAB_GATE + k:SLAB_GATE + k + 1], (tm, LANES))
    _wait_tile(stage_ref, ys_ref, sem, to_far=False)

    @pl.when(step + 1 < n_steps)
    def _():
        tab_copy(step + 1).start()
    for cp in loc_copies(step):
        cp.wait()

    flat = stage_ref.reshape(TOP_K * tm * ROW_PARTS, LANES)

    def tree_sum(terms):
        while len(terms) > 1:
            terms = [a + b for a, b in zip(terms[::2], terms[1::2])] + ([terms[-1]] if len(terms) % 2 else [])
        return terms[0]

    def mix(t, c):
        los, his = [], []
        for k in range(TOP_K):
            g = gk_ref[k, pl.ds(t, ROW_PARTS, stride=0), :]
            at = pl.multiple_of(loc_smems[k][t], ROW_PARTS)
            wl, wh = _unpack_words(flat[pl.ds(at, ROW_PARTS), :])
            los.append(g * wl)
            his.append(g * wh)
        lo_ref[t] = tree_sum(los)
        hi_ref[t] = tree_sum(his)
        return c

    lax.fori_loop(0, tm, mix, 0, unroll=8)

    @pl.when(step + 1 < n_steps)
    def _():
        gather(step + 1)

    routed =jnp.concatenate(_row_chunks(lo_ref) + _row_chunks(hi_ref), axis=-1)
    o_ref[...] = x_ref[...] + gate_ref[...] * (routed + shared)


def _combine(ys, loc_flat, tab_flat, slab, h2, x, gate, s1, s3, s2, tile_base):
    b, s, d = x.shape
    tm = MOE_TM
    kern = functools.partial(_combine_kernel, tile_base=tile_base)
    return pl.pallas_call(
        kern,
        grid=(b, s // tm),
        in_specs=[pl.BlockSpec(memory_space=pl.ANY),
                  pl.BlockSpec(memory_space=pl.ANY),
                  pl.BlockSpec(memory_space=pl.ANY),
                  pl.BlockSpec((None, tm, LANES), lambda bi, i: (bi, i, 0)),
                  pl.BlockSpec((None, tm, ROW_PARTS, LANES), lambda bi, i: (bi, i, 0, 0)),
                  pl.BlockSpec((None, tm, d), lambda bi, i: (bi, i, 0)),
                  pl.BlockSpec((None, 1, d), lambda bi, i: (bi, 0, 0)),
                  pl.BlockSpec(s1.shape, lambda bi, i: (0, 0)),
                  pl.BlockSpec(s3.shape, lambda bi, i: (0, 0)),
                  pl.BlockSpec(s2.shape, lambda bi, i: (0, 0))],
        out_specs=pl.BlockSpec((None, tm, d), lambda bi, i: (bi, i, 0)),
        out_shape=jax.ShapeDtypeStruct((b, s, d), F32),
        scratch_shapes=[pltpu.SMEM((tm,), I32)] * TOP_K + [
                        pltpu.SMEM((TAB_WORDS,), I32),
                        pltpu.VMEM((TOP_K, tm, LANES), F32),
                        pltpu.VMEM((TOP_K * tm, ROW_PARTS, LANES), U32),
                        pltpu.VMEM((tm, ROW_PARTS, LANES), F32),
                        pltpu.VMEM((tm, ROW_PARTS, LANES), F32),
                        pltpu.SemaphoreType.DMA,
                        pltpu.SemaphoreType.DMA,
                        pltpu.SemaphoreType.DMA],
        compiler_params=_params("arbitrary", "arbitrary"),
        name="combine",
    )(loc_flat, tab_flat, ys, slab, h2, x, gate, s1, s3, s2)


def _dft_tables(length, n_chan):
    scale = 1.0 / math.sqrt(length * n_chan)
    side = 1
    while side * side < length:
        side *= 2
    outer = length // side
    k = jnp.arange(length, dtype=I32)[:, None]
    a_idx = (k * jnp.arange(outer, dtype=I32)[None, :]) % outer
    b_idx = (k * jnp.arange(side, dtype=I32)[None, :]) % length
    ang_a = a_idx.astype(F32) * (2.0 * math.pi / outer)
    ang_b = b_idx.astype(F32) * (2.0 * math.pi / length)
    ca, sa = jnp.cos(ang_a)[:, :, None], jnp.sin(ang_a)[:, :, None]
    cb, sb = jnp.cos(ang_b)[:, None, :], jnp.sin(ang_b)[:, None, :]
    cos_t = (ca * cb - sa * sb).reshape(length, length)
    sin_t = (sa * cb + ca * sb).reshape(length, length)
    table = (jnp.concatenate([cos_t, -sin_t], axis=1) * scale).astype(BF16)
    return table


def _channel_table(n_chan, n_groups):
    m = jnp.arange(n_chan, dtype=I32)
    ang = ((m[:, None] * m[None, :]) % n_chan).astype(F32) * (2.0 * math.pi / n_chan)
    eye = jnp.eye(n_groups, dtype=F32)
    return jnp.concatenate([jnp.kron(eye, jnp.cos(ang)), jnp.kron(eye, jnp.sin(ang))], axis=1).astype(BF16)


def _rope_tables(n_tok):
    rows = n_tok // GRID_W
    axis_dim = HEAD_DIM // 2
    r = jnp.repeat(jnp.arange(rows, dtype=F32), GRID_W)
    col = jnp.tile(jnp.arange(GRID_W, dtype=F32), rows)
    inv = ROPE_BASE ** (-jnp.arange(0, axis_dim, 2, dtype=F32) / axis_dim)
    ar, ac = r[:, None] * inv, col[:, None] * inv
    cos = jnp.concatenate([jnp.cos(ar), jnp.cos(ar), jnp.cos(ac), jnp.cos(ac)], axis=1)
    sin = jnp.concatenate([-jnp.sin(ar), jnp.sin(ar), -jnp.sin(ac), jnp.sin(ac)], axis=1)
    return jnp.tile(cos, (1, N_KV)), jnp.tile(sin, (1, N_KV))


def _head_mean_matrix(width):
    h = jnp.arange(width) // HEAD_DIM
    return ((h[:, None] == h[None, :]).astype(F32) / HEAD_DIM).astype(BF16)


def _tile_matrices():
    src = jnp.arange(N_KV * HEAD_DIM)
    dst = jnp.arange(Q_PER_KV * HEAD_DIM)
    mats = [((src[:, None] // HEAD_DIM == h) & (src[:, None] % HEAD_DIM == dst[None, :] % HEAD_DIM))
            for h in range(N_KV)]
    return jnp.stack(mats).astype(BF16)


def _even_layer(x, xc, mod, modc, norm_g, w_in, ws, bs, w_out, tables):
    b, s, d = x.shape
    w = w_in.shape[1] // 3
    outs = []
    for stream, m, tm in ((x, mod, MIX_TM), (xc, modc, 256)):
        if stream is None:
            outs.append(None)
            continue
        length = stream.shape[1]
        tm = min(tm, length)
        gm, z = _even_mix(stream, norm_g, m[0], m[1], w_in, ws, bs, tables["chan"], tm)
        table = tables["pos"][length]
        y = _matmul(table, z.reshape(2 * length, b * w),
                    min(1024, length), min(1024, b * w), min(2048, 2 * length), BF16)
        outs.append(_outproj(
            gm, pl.BlockSpec((None, tm, w), lambda bi, i: (bi, i, 0)),
            y, pl.BlockSpec((tm, w), lambda bi, i: (i, bi)),
            w_out, stream, m[2], tm))
    return outs


def _odd_layer(x, xc, mod, modc, norm_g, w_in, qg, kg, sink, conv_w, w_out, tables, ctx_out):
    b, s, d = x.shape
    lc = xc.shape[1]
    half = N_HEADS * HEAD_DIM
    tm, tmc = MIX_TM, min(256, lc)
    prep = functools.partial(_odd_in, g=norm_g, w=w_in, qg=qg, kg=kg, bdq=tables["bdq"], bdk=tables["bdk"],
                             tile=tables["tile"], conv_w=conv_w)
    q, k4, v4, conv = prep(x, shift=mod[0], scale=mod[1], cos=tables["cos"], sin=tables["sin"], tm=tm, with_q=True)
    qc, kc4, vc4, convc = prep(xc, shift=modc[0], scale=modc[1], cos=tables["cos_c"], sin=tables["sin_c"],
                               tm=tmc, with_q=ctx_out)
    att = _attention(q, k4, v4, kc4, vc4, sink, band=True)
    spec = lambda t: pl.BlockSpec((None, t, half), lambda bi, i: (bi, i, 0))
    y = _outproj(att, spec(tm), conv, spec(tm), w_out, x, mod[2], tm)
    yc = None
    if ctx_out:
        attc = _attention(qc, None, None, kc4, vc4, sink, band=False)
        yc = _outproj(attc, spec(tmc), convc, spec(tmc), w_out, xc, modc[2], tmc)
    return y, yc


def _moe(x, xc, mod, modc, norm_g, rw_hi, rw_lo, rb, w1, w3, w2, layer, s1, s3, s2, tri, upper):
    b, s, d = x.shape
    n_lat = b * s
    xc_shape = None
    if xc is not None and xc.shape[1] % MOE_TM:
        xc_shape = xc.shape
        xc = xc.reshape(-1, MOE_TM, d)
        modc = [m[:xc.shape[0]] for m in modc]
    counts0 = jnp.zeros((1, LANES), F32)
    h2, slab, tab, counts = _route(x, norm_g, mod[3], mod[4], rw_hi, rw_lo, rb, counts0, tri, upper)
    counts_first = counts
    slabs, tabs = [slab.reshape(n_lat, LANES)], [tab]
    n_tok = n_lat
    if xc is not None:
        h2c, slabc, tabc, counts = _route(xc, norm_g, modc[3], modc[4], rw_hi, rw_lo, rb, counts, tri, upper)
        slabs.append(slabc.reshape(-1, LANES))
        tabs.append(tabc)
        n_tok += slabs[1].shape[0]
    cnt = counts[0, :N_EXPERTS].astype(I32)
    blk = EXPERT_BLOCK
    padded = (cnt + blk - 1) // blk * blk
    ends = jnp.cumsum(padded).astype(I32)
    starts = ends - padded
    step_rows = blk * EXPERT_GROUP
    n_rows = (n_tok * TOP_K + N_EXPERTS * (blk - 1) + step_rows - 1) // step_rows * step_rows
    n_blocks = n_rows // blk
    n_used = (ends[-1] // blk).reshape(1).astype(I32)
    blk_start = jnp.minimum(jnp.arange(n_blocks, dtype=I32), n_used[0] - 1) * blk
    block_e = jnp.minimum(jnp.sum(blk_start[:, None] >= ends[None, :], axis=1), N_EXPERTS - 1).astype(I32)
    slab_all = jnp.concatenate(slabs, axis=0)
    tab_all = jnp.concatenate(tabs, axis=0)[:, :, :N_EXPERTS].astype(I32)
    runs = jnp.concatenate([tab_all[:, 0], tab_all[:, 1], tab_all[:, 2] + starts[None, :]], axis=1)
    tab_flat = jnp.pad(runs, ((0, 0), (0, TAB_WORDS - runs.shape[1]))).reshape(-1)
    loc = (slab_all[:, SLAB_LOC:SLAB_LOC + 8].astype(I32) * ROW_PARTS).reshape(-1, MOE_TM, 8)
    loc_flat = jnp.swapaxes(loc, 1, 2).reshape(-1)
    packed = lambda a: a.reshape(-1, ROW_PARTS, LANES)
    clear_from = (starts + counts_first[0, :N_EXPERTS].astype(I32)) // blk
    xs = _dispatch(packed(h2), loc_flat, tab_flat, ends, n_used, clear_from, None, n_rows, 0)
    if xc is not None:
        xs = _dispatch(packed(h2c), loc_flat, tab_flat, ends, n_used, clear_from, xs, n_rows, n_lat // MOE_TM)
    ys = _experts(xs, block_e, n_used, w1, w3, w2, layer)
    x_new = _combine(ys, loc_flat, tab_flat, slab, h2, x, mod[5], s1, s3, s2, 0)
    xc_new = None
    if xc is not None:
        xc_new = _combine(ys, loc_flat, tab_flat, slabc, h2c, xc, modc[5], s1, s3, s2, n_lat // MOE_TM)
        if xc_shape is not None:
            xc_new = xc_new.reshape(xc_shape)
    return x_new, xc_new


def kernel(x, c, ctx, c_ctx, ada_w, ada_b, norm1_g, norm2_g, ev_w_in, ev_w_s, ev_b_s, ev_w_out, od_w_in, od_q_norm_g, od_k_norm_g, od_sink, od_conv_w, od_w_out, router_w, router_b, exp_w_gate, exp_w_up, exp_w_down, sh_w_gate, sh_w_up, sh_w_down):
    b, s, d = x.shape
    lc = ctx.shape[1]
    depth = ada_w.shape[0]
    n_groups = ev_w_s.shape[1]
    half = d // 2

    rows = -(-(b + 1) // 8) * 8
    cond = jnp.zeros((rows, d), F32).at[:b].set(c).at[b].set(c_ctx)
    mod_all = _adaln(cond, ada_w, ada_b)

    tables = {
        "chan": _channel_table(LANES, n_groups),
        "pos": {s: _dft_tables(s, LANES), lc: _dft_tables(lc, LANES)},
        "bdq": _head_mean_matrix(N_HEADS * HEAD_DIM),
        "bdk": _head_mean_matrix(N_KV * HEAD_DIM),
        "tile": _tile_matrices(),
    }
    tables["cos"], tables["sin"] = _rope_tables(s)
    tables["cos_c"] = jnp.ones((lc, N_KV * HEAD_DIM), F32)
    tables["sin_c"] = jnp.zeros((lc, N_KV * HEAD_DIM), F32)
    tri = (jnp.arange(MOE_TM)[:, None] > jnp.arange(MOE_TM)[None, :]).astype(BF16)
    upper = (jnp.arange(LANES)[:, None] < jnp.arange(LANES)[None, :]).astype(BF16)

    qw, kw = N_HEADS * HEAD_DIM, N_KV * HEAD_DIM
    perm = jnp.concatenate([jnp.arange(0, qw), jnp.arange(qw + 2 * kw, qw + 2 * kw + 3 * half),
                            jnp.arange(qw, qw + 2 * kw)])

    w1_all, w3_all, w2_all = exp_w_gate.astype(BF16), exp_w_up.astype(BF16), exp_w_down.astype(BF16)
    xc = ctx
    for l in range(depth):
        last = l == depth - 1
        even = l % 2 == 0
        need_ctx = not (last and even)
        pieces = [mod_all[l, :, j * d:(j + 1) * d] for j in range(6)]
        mod = [p[:b].reshape(b, 1, d) for p in pieces]
        modc = [jnp.broadcast_to(p[b].reshape(1, 1, d), (b, 1, d)) for p in pieces]
        g1 = norm1_g[l].reshape(1, d)
        g2 = norm2_g[l].reshape(1, d)
        if even:
            e = l // 2
            bs = jnp.broadcast_to(ev_b_s[e][:, :, None], (n_groups, CHUNK, LANES))
            y, yc = _even_layer(x, xc if (need_ctx and not last) else None, mod, modc, g1,
                                ev_w_in[e].astype(BF16), ev_w_s[e].astype(BF16), bs,
                                ev_w_out[e].astype(BF16), tables)
        else:
            o = l // 2
            qg = (jnp.tile(od_q_norm_g[o], N_HEADS) * (HEAD_DIM ** -0.5)).reshape(1, qw)
            kg = jnp.tile(od_k_norm_g[o], N_KV).reshape(1, kw)
            y, yc = _odd_layer(x, xc, mod, modc, g1, od_w_in[o][:, perm].astype(BF16), qg, kg,
                               od_sink[o], od_conv_w[o], od_w_out[o].astype(BF16), tables, not last)
        x = y
        if not last:
            xc = yc
        rw = jnp.zeros((d, LANES), F32).at[:, :N_EXPERTS].set(router_w[l])
        rw_hi = rw.astype(BF16)
        rw_lo = (rw - rw_hi.astype(F32)).astype(BF16)
        rb = jnp.zeros((1, LANES), F32).at[0, :N_EXPERTS].set(router_b[l])
        x, xc_new = _moe(x, None if last else xc, mod, modc, g2, rw_hi, rw_lo, rb, w1_all, w3_all, w2_all, l,
                         sh_w_gate[l].astype(BF16), sh_w_up[l].astype(BF16), sh_w_down[l].astype(BF16),
                         tri, upper)
        if not last:
            xc = xc_new
    return x
```
